```python
import jax, jax.numpy as jnp
from jax import lax
import numpy as np

D_MODEL = 2048
BATCH = 1
SEQ = 16384
DEPTH = 2

POOL_WINDOWS = (2, 4, 8, 16)
POOL_GROUPS = 4
POOL_CH = D_MODEL // 16
POOL_W = POOL_GROUPS * POOL_CH
GMLP_GROUPS = 4
GMLP_CH = D_MODEL // 16
GMLP_W = GMLP_GROUPS * GMLP_CH
GMLP_CHUNK = 128
HEAD_DIM = 64
NSA_HEADS = D_MODEL // 128
NSA_KV_HEADS = 4
NSA_HPG = NSA_HEADS // NSA_KV_HEADS
NSA_Q_W = NSA_HEADS * HEAD_DIM
NSA_KV_W = NSA_KV_HEADS * HEAD_DIM
CMP_BLOCK = 32
CMP_STRIDE = 16
CMP_HIDDEN = 128
SEL_BLOCK = 64
SEL_TOPN = 16
WINDOW = 512
Q_BLOCK = 128
FORCE_BONUS = 1000.0
NEG = -1e30
N_BRANCH = 3
D_FF = 4 * D_MODEL
N_MOD = 6
IN_SPLITS = (POOL_W, 2 * GMLP_W, NSA_Q_W, 6 * NSA_KV_W, 3 * NSA_HEADS, N_BRANCH * D_MODEL)
D_IN = sum(IN_SPLITS)

kernel_name = 'hybrid_pool_gmlp_nsa_block'


def rms_norm(x, g, eps=1e-6):
    xf = x.astype(jnp.float32)
    y = xf * lax.rsqrt(jnp.mean(xf * xf, axis=-1, keepdims=True) + eps)
    return (y * g.astype(jnp.float32)).astype(x.dtype)


def masked_softmax(s, mask):
    s = jnp.where(mask, s, NEG)
    m = jnp.max(s, axis=-1, keepdims=True)
    e = jnp.where(mask, jnp.exp(s - m), 0.0)
    return e / jnp.maximum(jnp.sum(e, axis=-1, keepdims=True), 1e-30)


def pool_mixer(a, w_grp, scale):
    b, s, _ = a.shape
    af = a.astype(jnp.float32).reshape(b, s, POOL_GROUPS, POOL_CH)
    cs = jnp.cumsum(af, axis=1)
    t = jnp.arange(s)
    outs = []
    for gi, w in enumerate(POOL_WINDOWS):
        c_g = cs[:, :, gi]
        lo = jnp.pad(c_g[:, :s - w], ((0, 0), (w, 0), (0, 0)))
        cnt = jnp.minimum(t + 1, w).astype(jnp.float32)[None, :, None]
        outs.append((c_g - lo) / cnt - af[:, :, gi])
    pooled = jnp.stack(outs, axis=2).astype(a.dtype)
    y = jnp.einsum('bsgc,gcd->bsgd', pooled, w_grp)
    return y.reshape(b, s, POOL_W) * scale


def gmlp_mixer(uv, ln_g, ln_b, w_s, b_s):
    b, s, _ = uv.shape
    uv = jax.nn.gelu(uv)
    u, v = jnp.split(uv, 2, axis=-1)
    vf = v.astype(jnp.float32)
    mu = jnp.mean(vf, axis=-1, keepdims=True)
    var = jnp.mean(jnp.square(vf - mu), axis=-1, keepdims=True)
    v = ((vf - mu) * lax.rsqrt(var + 1e-5) * ln_g + ln_b).astype(uv.dtype)
    vc = v.reshape(b, s // GMLP_CHUNK, GMLP_CHUNK, GMLP_GROUPS, GMLP_CH)
    causal = jnp.tril(jnp.ones((GMLP_CHUNK, GMLP_CHUNK), w_s.dtype))
    mixed = jnp.einsum('gij,bnjgc->bnigc', w_s * causal, vc) + b_s.T[None, None, :, :, None]
    return u * mixed.reshape(b, s, GMLP_W)


def compress_kv(kv, pos, w1, b1, w2, b2):
    b, s, g, d = kv.shape
    n_cmp = (s - CMP_BLOCK) // CMP_STRIDE + 1
    idx = CMP_STRIDE * jnp.arange(n_cmp)[:, None] + jnp.arange(CMP_BLOCK)[None, :]
    blk = kv[:, idx] + pos[None, None, :, None, :]
    blk = jnp.swapaxes(blk, 2, 3).reshape(b, n_cmp, g, CMP_BLOCK * d)
    hid = jax.nn.gelu(blk @ w1 + b1)
    return hid @ w2 + b2


def nsa_mixer(q, kc, vc, ks, vs, kw, vw, gate_logits, cmp_pos, cmp_w1, cmp_b1, cmp_w2, cmp_b2):
    b, s = q.shape[:2]
    G, P = NSA_KV_HEADS, NSA_HPG
    q = q.reshape(b, s, G, P, HEAD_DIM)
    gates = jax.nn.sigmoid(gate_logits.astype(jnp.float32)).reshape(b, s, G, P, 3)
    k_cmp = compress_kv(kc.reshape(b, s, G, HEAD_DIM), cmp_pos[0], cmp_w1[0], cmp_b1[0], cmp_w2[0], cmp_b2[0])
    v_cmp = compress_kv(vc.reshape(b, s, G, HEAD_DIM), cmp_pos[1], cmp_w1[1], cmp_b1[1], cmp_w2[1], cmp_b2[1])
    n_cmp = k_cmp.shape[1]
    cmp_start = jnp.arange(n_cmp) * CMP_STRIDE
    cmp_end = cmp_start + CMP_BLOCK - 1
    n_sel = s // SEL_BLOCK
    n_top = min(SEL_TOPN, n_sel)
    sel_start = jnp.arange(n_sel) * SEL_BLOCK
    overlap = ((cmp_start[:, None] < sel_start[None, :] + SEL_BLOCK)
               & (cmp_end[:, None] >= sel_start[None, :])).astype(jnp.float32)
    ks_blk = ks.reshape(b, n_sel, SEL_BLOCK, G, HEAD_DIM).transpose(0, 3, 1, 2, 4)
    vs_blk = vs.reshape(b, n_sel, SEL_BLOCK, G, HEAD_DIM).transpose(0, 3, 1, 2, 4)
    kw_pad = jnp.pad(kw.reshape(b, s, G, HEAD_DIM), ((0, 0), (WINDOW, 0), (0, 0), (0, 0)))
    vw_pad = jnp.pad(vw.reshape(b, s, G, HEAD_DIM), ((0, 0), (WINDOW, 0), (0, 0), (0, 0)))
    scale = HEAD_DIM ** -0.5
    bi = jnp.arange(b)[:, None, None, None]
    gi = jnp.arange(G)[None, :, None, None]
    jj = jnp.arange(n_sel)

    def block(qb):
        t0 = qb * Q_BLOCK
        tpos = t0 + jnp.arange(Q_BLOCK)
        qblk = lax.dynamic_slice_in_dim(q, t0, Q_BLOCK, axis=1)
        gblk = lax.dynamic_slice_in_dim(gates, t0, Q_BLOCK, axis=1)
        s_c = jnp.einsum('bqgpd,bngd->bgpqn', qblk, k_cmp).astype(jnp.float32) * scale
        p_c = masked_softmax(s_c, cmp_end[None, :] <= tpos[:, None])
        o_c = jnp.einsum('bgpqn,bngd->bqgpd', p_c.astype(v_cmp.dtype), v_cmp)
        imp = jnp.einsum('bgqn,nj->bgqj', p_c.sum(axis=2), overlap)
        cur = tpos // SEL_BLOCK
        forced = (jj[None, :] == 0) | (jj[None, :] == cur[:, None]) | (jj[None, :] == cur[:, None] - 1)
        valid = jj[None, :] <= cur[:, None]
        score = jnp.where(valid, imp + FORCE_BONUS * forced.astype(jnp.float32), -1.0)
        _, idx = lax.top_k(score, n_top)
        kg = ks_blk[bi, gi, idx]
        vg = vs_blk[bi, gi, idx]
        keypos = idx[..., None] * SEL_BLOCK + jnp.arange(SEL_BLOCK)
        mask_s = (keypos <= tpos[None, None, :, None, None]).reshape(b, G, 1, Q_BLOCK, n_top * SEL_BLOCK)
        s_s = jnp.einsum('bqgpd,bgqnkd->bgpqnk', qblk, kg).astype(jnp.float32) * scale
        p_s = masked_softmax(s_s.reshape(b, G, P, Q_BLOCK, n_top * SEL_BLOCK), mask_s)
        o_s = jnp.einsum('bgpqm,bgqmd->bqgpd', p_s.astype(vg.dtype),
                         vg.reshape(b, G, Q_BLOCK, n_top * SEL_BLOCK, HEAD_DIM))
        kwb = lax.dynamic_slice_in_dim(kw_pad, t0, Q_BLOCK + WINDOW, axis=1)
        vwb = lax.dynamic_slice_in_dim(vw_pad, t0, Q_BLOCK + WINDOW, axis=1)
        kpos = t0 - WINDOW + jnp.arange(Q_BLOCK + WINDOW)
        rel = tpos[:, None] - kpos[None, :]
        mask_w = (rel >= 0) & (rel < WINDOW) & (kpos[None, :] >= 0)
        s_w = jnp.einsum('bqgpd,bkgd->bgpqk', qblk, kwb).astype(jnp.float32) * scale
        p_w = masked_softmax(s_w, mask_w)
        o_w = jnp.einsum('bgpqk,bkgd->bqgpd', p_w.astype(vwb.dtype), vwb)
        o = gblk[..., 0:1] * o_c + gblk[..., 1:2] * o_s + gblk[..., 2:3] * o_w
        return o.astype(q.dtype)

    out = lax.map(block, jnp.arange(s // Q_BLOCK))
    return jnp.moveaxis(out, 0, 1).reshape(b, s, NSA_Q_W)


def token_mixer(h, w_in, pool_w, pool_scale, gmlp_ln_g, gmlp_ln_b, gmlp_ws, gmlp_bs,
                cmp_pos, cmp_w1, cmp_b1, cmp_w2, cmp_b2, w_br_pool, w_br_gmlp, w_br_nsa, w_out):
    b, s, d = h.shape
    proj = h @ w_in
    cuts = list(np.cumsum(IN_SPLITS)[:-1])
    a_pool, a_gmlp, a_q, a_kv, a_ngate, a_bgate = jnp.split(proj, cuts, axis=-1)
    kc, vc, ks, vs, kw, vw = jnp.split(a_kv, 6, axis=-1)
    y_a = pool_mixer(a_pool, pool_w, pool_scale)
    y_b = gmlp_mixer(a_gmlp, gmlp_ln_g, gmlp_ln_b, gmlp_ws, gmlp_bs)
    y_c = nsa_mixer(a_q, kc, vc, ks, vs, kw, vw, a_ngate, cmp_pos, cmp_w1, cmp_b1, cmp_w2, cmp_b2)
    g = jax.nn.sigmoid(a_bgate.reshape(b, s, N_BRANCH, d))
    merged = (g[:, :, 0] * (y_a @ w_br_pool)
              + g[:, :, 1] * (y_b @ w_br_gmlp)
              + g[:, :, 2] * (y_c @ w_br_nsa))
    return merged @ w_out


def setup_inputs(seed: int = 0) -> dict:
    key = jax.random.key(seed)
    ks = jax.random.split(key, 23)

    def nrm(k, shape, sc):
        return jax.random.normal(k, shape, jnp.float32) * sc

    L = DEPTH
    return {
        'x': nrm(ks[0], (BATCH, SEQ, D_MODEL), 1.0),
        'c': nrm(ks[1], (BATCH, D_MODEL), 1.0),
        'norm_g': 1.0 + nrm(ks[2], (L, 4, D_MODEL), 0.05),
        'w_ada': nrm(ks[3], (L, D_MODEL, N_MOD * D_MODEL), 0.5 * D_MODEL ** -0.5),
        'b_ada': nrm(ks[4], (L, N_MOD * D_MODEL), 0.02),
        'w_in': nrm(ks[5], (L, D_MODEL, D_IN), D_MODEL ** -0.5),
        'pool_w': nrm(ks[6], (L, POOL_GROUPS, POOL_CH, POOL_CH), POOL_CH ** -0.5),
        'pool_scale': 1.0 + nrm(ks[7], (L, POOL_W), 0.1),
        'gmlp_ln_g': 1.0 + nrm(ks[8], (L, GMLP_W), 0.05),
        'gmlp_ln_b': nrm(ks[9], (L, GMLP_W), 0.02),
        'gmlp_ws': nrm(ks[10], (L, GMLP_GROUPS, GMLP_CHUNK, GMLP_CHUNK), 0.5 * GMLP_CHUNK ** -0.5),
        'gmlp_bs': 1.0 + nrm(ks[11], (L, GMLP_GROUPS, GMLP_CHUNK), 0.1),
        'cmp_pos': nrm(ks[12], (L, 2, CMP_BLOCK, HEAD_DIM), 0.1),
        'cmp_w1': nrm(ks[13], (L, 2, CMP_BLOCK * HEAD_DIM, CMP_HIDDEN), (CMP_BLOCK * HEAD_DIM) ** -0.5),
        'cmp_b1': nrm(ks[14], (L, 2, CMP_HIDDEN), 0.02),
        'cmp_w2': nrm(ks[15], (L, 2, CMP_HIDDEN, HEAD_DIM), CMP_HIDDEN ** -0.5),
        'cmp_b2': nrm(ks[16], (L, 2, HEAD_DIM), 0.02),
        'w_br_pool': nrm(ks[17], (L, POOL_W, D_MODEL), POOL_W ** -0.5),
        'w_br_gmlp': nrm(ks[18], (L, GMLP_W, D_MODEL), GMLP_W ** -0.5),
        'w_br_nsa': nrm(ks[19], (L, NSA_Q_W, D_MODEL), NSA_Q_W ** -0.5),
        'w_out': nrm(ks[20], (L, D_MODEL, D_MODEL), D_MODEL ** -0.5),
        'w_ff1': nrm(ks[21], (L, D_MODEL, D_FF), D_MODEL ** -0.5),
        'w_ff2': nrm(ks[22], (L, D_FF, D_MODEL), D_FF ** -0.5),
    }


def reference(x, c, norm_g, w_ada, b_ada, w_in, pool_w, pool_scale, gmlp_ln_g, gmlp_ln_b,
              gmlp_ws, gmlp_bs, cmp_pos, cmp_w1, cmp_b1, cmp_w2, cmp_b2,
              w_br_pool, w_br_gmlp, w_br_nsa, w_out, w_ff1, w_ff2):
    b, s, d = x.shape
    c_act = jax.nn.silu(c)
    for l in range(DEPTH):
        mod = (c_act @ w_ada[l] + b_ada[l]).reshape(b, N_MOD, 1, d)
        shift1, scale1, gate1, shift2, scale2, gate2 = [mod[:, i] for i in range(N_MOD)]
        h = rms_norm(x, norm_g[l, 0]) * (1.0 + scale1) + shift1
        y = token_mixer(h, w_in[l], pool_w[l], pool_scale[l], gmlp_ln_g[l], gmlp_ln_b[l],
                        gmlp_ws[l], gmlp_bs[l], cmp_pos[l], cmp_w1[l], cmp_b1[l], cmp_w2[l],
                        cmp_b2[l], w_br_pool[l], w_br_gmlp[l], w_br_nsa[l], w_out[l])
        x = x + gate1 * rms_norm(y, norm_g[l, 1])
        h = rms_norm(x, norm_g[l, 2]) * (1.0 + scale2) + shift2
        f = jnp.square(jax.nn.relu(h @ w_ff1[l])) @ w_ff2[l]
        x = x + gate2 * rms_norm(f, norm_g[l, 3])
    return x
```

```python
import functools

import jax
import jax.numpy as jnp
from jax import lax
from jax.experimental import pallas as pl
from jax.experimental.pallas import tpu as pltpu

F32 = jnp.float32
BF16 = jnp.bfloat16

D_MODEL = 2048
POOL_WINDOWS = (2, 4, 8, 16)
POOL_HALO = 16
GROUP_CH = 128
MIX_W = 4 * GROUP_CH
GMLP_CHUNK = 128
HEAD_DIM = 64
KV_GROUPS = 4
HEADS_PER_GROUP = 4
Q_W = 16 * HEAD_DIM
KV_W = KV_GROUPS * HEAD_DIM
CMP_BLOCK = 32
CMP_STRIDE = 16
SEL_BLOCK = 64
SEL_TOPN = 16
WINDOW = 512
FORCE_BONUS = 1000.0
NEG = -1e30
N_GATE = 3 * 16
D_FF = 4 * D_MODEL

Q_BLOCK = 128
QL = HEADS_PER_GROUP * Q_BLOCK
SEL_TILE = 256
WIN_TILE = 128

OFF_POOL, OFF_U, OFF_V, OFF_Q, OFF_KV, OFF_BG = 0, 512, 1024, 1536, 2560, 4096
PROJ_W = OFF_BG + 3 * D_MODEL

VMEM_LIMIT = 56 * 1024 * 1024


def _cparams(*sem):
    return pltpu.CompilerParams(dimension_semantics=sem, vmem_limit_bytes=VMEM_LIMIT)


def _dot(a, b):
    return jnp.dot(a, b, preferred_element_type=F32)


def _rms(x, g):
    return x * lax.rsqrt(jnp.mean(x * x, axis=-1, keepdims=True) + 1e-6) * g


def _mod_kernel(c_ref, w_ref, b_ref, o_ref):
    c = c_ref[...]
    act = c * jax.nn.sigmoid(c)
    o_ref[0] = jnp.sum(act * w_ref[0], axis=0, keepdims=True) + b_ref[0]


def _modulation(c, w_ada, b_ada):
    n_layer, d, n_out = w_ada.shape
    tn = 1024
    return pl.pallas_call(
        _mod_kernel,
        grid=(n_layer, n_out // tn),
        in_specs=[
            pl.BlockSpec((d, 1), lambda l, j: (0, 0)),
            pl.BlockSpec((1, d, tn), lambda l, j: (l, 0, j)),
            pl.BlockSpec((1, 1, tn), lambda l, j: (l, 0, j)),
        ],
        out_specs=pl.BlockSpec((1, 1, tn), lambda l, j: (l, 0, j)),
        out_shape=jax.ShapeDtypeStruct((n_layer, 1, n_out), F32),
        compiler_params=_cparams("parallel", "parallel"),
        name="adaln_mod",
    )(c.reshape(d, 1), w_ada, b_ada.reshape(n_layer, 1, n_out))


def _inproj_kernel(x_ref, g_ref, sc_ref, sh_ref, w_ref, wng_ref, o_ref, ng_ref, h_scr):
    @pl.when(pl.program_id(1) == 0)
    def _():
        h = _rms(x_ref[...], g_ref[...]) * (1.0 + sc_ref[...]) + sh_ref[...]
        hb = h.astype(BF16)
        h_scr[...] = hb
        ng_ref[...] = _dot(hb, wng_ref[...])

    o_ref[...] = _dot(h_scr[...], w_ref[...]).astype(o_ref.dtype)


def _in_projection(x, g, scale, shift, w_main, w_ng):
    s, d = x.shape
    tm, tn = min(1024, s), 1024
    row = lambda i, j: (0, 0)
    return pl.pallas_call(
        _inproj_kernel,
        grid=(s // tm, PROJ_W // tn),
        in_specs=[
            pl.BlockSpec((tm, d), lambda i, j: (i, 0)),
            pl.BlockSpec((1, d), row),
            pl.BlockSpec((1, d), row),
            pl.BlockSpec((1, d), row),
            pl.BlockSpec((d, tn), lambda i, j: (0, j)),
            pl.BlockSpec((d, 128), row),
        ],
        out_specs=[
            pl.BlockSpec((tm, tn), lambda i, j: (i, j)),
            pl.BlockSpec((tm, 128), lambda i, j: (i, 0)),
        ],
        out_shape=[
            jax.ShapeDtypeStruct((s, PROJ_W), BF16),
            jax.ShapeDtypeStruct((s, 128), F32),
        ],
        scratch_shapes=[pltpu.VMEM((tm, d), BF16)],
        compiler_params=_cparams("parallel", "arbitrary"),
        name="in_proj",
    )(x, g, scale, shift, w_main, w_ng)


def _mixer_kernel(a_ref, halo_ref, u_ref, v_ref, pw_ref, ps_ref, lg_ref, lb_ref, ws_ref, bs_ref,
                  ya_ref, yb_ref):
    i = pl.program_id(0)
    tm = a_ref.shape[0]
    a = a_ref[...].astype(F32)
    halo = jnp.where(i > 0, halo_ref[...].astype(F32), 0.0)
    ext = jnp.concatenate([halo, a], axis=0)
    p2 = ext[1:] + ext[:-1]
    p4 = p2[2:] + p2[:-2]
    p8 = p4[4:] + p4[:-4]
    p16 = p8[8:] + p8[:-8]
    sums = (p2[15:15 + tm], p4[13:13 + tm], p8[9:9 + tm], p16[1:1 + tm])
    t = i * tm + lax.broadcasted_iota(jnp.int32, (tm, 1), 0)
    for gi, w in enumerate(POOL_WINDOWS):
        cols = slice(gi * GROUP_CH, (gi + 1) * GROUP_CH)
        cnt = jnp.minimum(t + 1, w).astype(F32)
        pooled = sums[gi][:, cols] / cnt - a[:, cols]
        y = _dot(pooled.astype(BF16), pw_ref[gi])
        ya_ref[:, cols] = (y * ps_ref[:, cols]).astype(ya_ref.dtype)

    u = jax.nn.gelu(u_ref[...].astype(F32))
    v = jax.nn.gelu(v_ref[...].astype(F32))
    mu = jnp.mean(v, axis=-1, keepdims=True)
    var = jnp.mean(jnp.square(v - mu), axis=-1, keepdims=True)
    vn = ((v - mu) * lax.rsqrt(var + 1e-5) * lg_ref[...] + lb_ref[...]).astype(BF16)
    r = lax.broadcasted_iota(jnp.int32, (GMLP_CHUNK, GMLP_CHUNK), 0)
    c = lax.broadcasted_iota(jnp.int32, (GMLP_CHUNK, GMLP_CHUNK), 1)
    for gi in range(4):
        cols = slice(gi * GROUP_CH, (gi + 1) * GROUP_CH)
        wsm = jnp.where(r >= c, ws_ref[gi], 0.0).astype(BF16)
        bias = bs_ref[:, gi:gi + 1]
        for ck in range(tm // GMLP_CHUNK):
            rows = slice(ck * GMLP_CHUNK, (ck + 1) * GMLP_CHUNK)
            mixed = _dot(wsm, vn[rows, cols]) + bias
            yb_ref[rows, cols] = (u[rows, cols] * mixed).astype(yb_ref.dtype)


def _mixers(proj, pool_w, pool_scale, ln_g, ln_b, ws, bs_t):
    s = proj.shape[0]
    tm = min(512, s)
    hb = tm // POOL_HALO
    const2 = lambda i: (0, 0)
    const3 = lambda i: (0, 0, 0)
    return pl.pallas_call(
        _mixer_kernel,
        grid=(s // tm,),
        in_specs=[
            pl.BlockSpec((tm, MIX_W), lambda i: (i, OFF_POOL // MIX_W)),
            pl.BlockSpec((POOL_HALO, MIX_W), lambda i: (jnp.maximum(i * hb - 1, 0), OFF_POOL // MIX_W)),
            pl.BlockSpec((tm, MIX_W), lambda i: (i, OFF_U // MIX_W)),
            pl.BlockSpec((tm, MIX_W), lambda i: (i, OFF_V // MIX_W)),
            pl.BlockSpec((4, GROUP_CH, GROUP_CH), const3),
            pl.BlockSpec((1, MIX_W), const2),
            pl.BlockSpec((1, MIX_W), const2),
            pl.BlockSpec((1, MIX_W), const2),
            pl.BlockSpec((4, GMLP_CHUNK, GMLP_CHUNK), const3),
            pl.BlockSpec((GMLP_CHUNK, 4), const2),
        ],
        out_specs=[
            pl.BlockSpec((tm, MIX_W), lambda i: (i, 0)),
            pl.BlockSpec((tm, MIX_W), lambda i: (i, 0)),
        ],
        out_shape=[jax.ShapeDtypeStruct((s, MIX_W), BF16)] * 2,
        compiler_params=_cparams("parallel"),
        name="mixers",
    )(proj, proj, proj, proj, pool_w, pool_scale, ln_g, ln_b, ws, bs_t)


def _compress_kernel(x_ref, pos_ref, w1_ref, b1_ref, w2_ref, b2_ref, o_ref):
    half = CMP_STRIDE * HEAD_DIM
    x = x_ref[0, 0].astype(F32)
    nc = x.shape[0]
    pos = pos_ref[0]
    first = _dot((x + pos[:, :half]).astype(BF16), w1_ref[0, :half, :])
    second = _dot((x + pos[:, half:]).astype(BF16), w1_ref[0, half:, :])
    hid = jax.nn.gelu(first + pltpu.roll(second, nc - 1, 0) + b1_ref[0])
    o_ref[0, 0] = _dot(hid.astype(BF16), w2_ref[0]) + b2_ref[0]


def _compress(xkv, pos, w1, b1, w2, b2):
    _, n_group, nc, width = xkv.shape
    per_kv = lambda a, g: (a, 0, 0)
    return pl.pallas_call(
        _compress_kernel,
        grid=(2, n_group),
        in_specs=[
            pl.BlockSpec((1, 1, nc, width), lambda a, g: (a, g, 0, 0)),
            pl.BlockSpec((1, 1, 2 * width), per_kv),
            pl.BlockSpec((1, 2 * width, 128), per_kv),
            pl.BlockSpec((1, 1, 128), per_kv),
            pl.BlockSpec((1, 128, HEAD_DIM), per_kv),
            pl.BlockSpec((1, 1, HEAD_DIM), per_kv),
        ],
        out_specs=pl.BlockSpec((1, 1, nc, HEAD_DIM), lambda a, g: (a, g, 0, 0)),
        out_shape=jax.ShapeDtypeStruct((2, n_group, nc, HEAD_DIM), F32),
        compiler_params=_cparams("parallel", "parallel"),
        name="compress_kv",
    )(xkv, pos, w1, b1, w2, b2)


def _nsa_kernel(qt_ref, kc_ref, vct_ref, ks_ref, vst_ref, kw_ref, vwt_ref, gl_ref, o_ref,
                ps_scr, sel_scr):
    qb = pl.program_id(1)
    t0 = qb * Q_BLOCK
    qt = qt_ref[0, 0]
    nc = kc_ref.shape[1]
    n_sel = sel_scr.shape[0]
    tq = t0 + lax.broadcasted_iota(jnp.int32, (1, QL), 1) % Q_BLOCK

    s_c = _dot(kc_ref[0], qt)
    blk_end = CMP_STRIDE * lax.broadcasted_iota(jnp.int32, (nc, 1), 0) + (CMP_BLOCK - 1)
    mask_c = blk_end <= tq
    s_c = jnp.where(mask_c, s_c, NEG)
    m_c = jnp.max(s_c, axis=0, keepdims=True)
    e_c = jnp.where(mask_c, jnp.exp(s_c - m_c), 0.0)
    inv_c = 1.0 / jnp.maximum(jnp.sum(e_c, axis=0, keepdims=True), 1e-30)
    p_c = e_c * inv_c
    o_c = _dot(vct_ref[0], p_c.astype(BF16))

    p_grp = (p_c[:, 0:Q_BLOCK] + p_c[:, Q_BLOCK:2 * Q_BLOCK]
             + p_c[:, 2 * Q_BLOCK:3 * Q_BLOCK] + p_c[:, 3 * Q_BLOCK:])
    ps_scr[0:8, :] = jnp.zeros((8, Q_BLOCK), F32)
    ps_scr[8:8 + nc, :] = p_grp
    imp = ps_scr[pl.ds(7, n_sel, stride=4), :]
    for k in range(8, 12):
        imp = imp + ps_scr[pl.ds(k, n_sel, stride=4), :]

    j_idx = lax.broadcasted_iota(jnp.int32, (n_sel, Q_BLOCK), 0)
    cur = (t0 + lax.broadcasted_iota(jnp.int32, (n_sel, Q_BLOCK), 1)) // SEL_BLOCK
    forced = (j_idx == 0) | (j_idx == cur) | (j_idx == cur - 1)
    valid = j_idx <= cur
    score = jnp.where(valid, imp + FORCE_BONUS * forced.astype(F32), -1.0)

    def pick_one(_, carry):
        sc, sel = carry
        best = jnp.max(sc, axis=0, keepdims=True)
        first = jnp.min(jnp.where(sc == best, j_idx, n_sel), axis=0, keepdims=True)
        pick = j_idx == first
        return jnp.where(pick, -2.0, sc), jnp.where(pick, 1.0, sel)

    _, sel = lax.fori_loop(0, SEL_TOPN, pick_one, (score, jnp.zeros((n_sel, Q_BLOCK), F32)))
    sel_scr[...] = jnp.where(valid, sel, 0.0)

    blocks_per_tile = SEL_TILE // SEL_BLOCK

    def sel_step(kt, carry):
        m_prev, l_prev, acc = carry
        k0 = pl.multiple_of(kt * SEL_TILE, SEL_TILE)
        s = _dot(ks_ref[0, pl.ds(k0, SEL_TILE), :], qt)
        rows = [jnp.broadcast_to(sel_scr[pl.ds(kt * blocks_per_tile + b, 1), :], (SEL_BLOCK, Q_BLOCK))
                for b in range(blocks_per_tile)]
        picked = jnp.concatenate(rows, axis=0) > 0.5
        picked = jnp.concatenate([picked] * HEADS_PER_GROUP, axis=1)
        key = k0 + lax.broadcasted_iota(jnp.int32, (SEL_TILE, 1), 0)
        s = jnp.where(picked & (key <= tq), s, NEG)
        m_new = jnp.maximum(m_prev, jnp.max(s, axis=0, keepdims=True))
        alpha = jnp.exp(m_prev - m_new)
        p = jnp.exp(s - m_new)
        l_new = alpha * l_prev + jnp.sum(p, axis=0, keepdims=True)
        acc = alpha * acc + _dot(vst_ref[0, :, pl.ds(k0, SEL_TILE)], p.astype(BF16))
        return m_new, l_new, acc

    init = (jnp.full((1, QL), NEG, F32), jnp.zeros((1, QL), F32), jnp.zeros((HEAD_DIM, QL), F32))
    _, l_s, acc_s = lax.fori_loop(0, t0 // SEL_TILE + 1, sel_step, init)

    n_win = WINDOW // WIN_TILE + 1

    def win_step(w, carry):
        m_prev, l_prev, acc = carry
        k0 = pl.multiple_of(t0 - w * WIN_TILE, WIN_TILE)
        s = _dot(kw_ref[0, pl.ds(k0, WIN_TILE), :], qt)
        rel = tq - (k0 + lax.broadcasted_iota(jnp.int32, (WIN_TILE, 1), 0))
        s = jnp.where((rel >= 0) & (rel < WINDOW), s, NEG)
        m_new = jnp.maximum(m_prev, jnp.max(s, axis=0, keepdims=True))
        alpha = jnp.exp(m_prev - m_new)
        p = jnp.exp(s - m_new)
        l_new = alpha * l_prev + jnp.sum(p, axis=0, keepdims=True)
        acc = alpha * acc + _dot(vwt_ref[0, :, pl.ds(k0, WIN_TILE)], p.astype(BF16))
        return m_new, l_new, acc

    _, l_w, acc_w = lax.fori_loop(0, jnp.minimum(n_win, qb + 1), win_step, init)

    gate = jax.nn.sigmoid(gl_ref[0, 0])
    o = gate[0:1] * o_c + gate[1:2] * (acc_s / l_s) + gate[2:3] * (acc_w / l_w)
    stacked = jnp.concatenate([o[:, p * Q_BLOCK:(p + 1) * Q_BLOCK] for p in range(HEADS_PER_GROUP)], axis=0)
    o_ref[...] = stacked.T.astype(o_ref.dtype)


def _sparse_attention(qt, k_cmp, v_cmp_t, ks, vs_t, kw, vw_t, gate_logits, s):
    n_group, n_qb = qt.shape[:2]
    nc = k_cmp.shape[1]
    n_sel = s // SEL_BLOCK
    per_group3 = lambda g, i: (g, 0, 0)
    per_step = lambda g, i: (g, i, 0, 0)
    return pl.pallas_call(
        _nsa_kernel,
        grid=(n_group, n_qb),
        in_specs=[
            pl.BlockSpec((1, 1, HEAD_DIM, QL), per_step),
            pl.BlockSpec((1, nc, HEAD_DIM), per_group3),
            pl.BlockSpec((1, HEAD_DIM, nc), per_group3),
            pl.BlockSpec((1, s, HEAD_DIM), per_group3),
            pl.BlockSpec((1, HEAD_DIM, s), per_group3),
            pl.BlockSpec((1, s, HEAD_DIM), per_group3),
            pl.BlockSpec((1, HEAD_DIM, s), per_group3),
            pl.BlockSpec((1, 1, 3, QL), per_step),
        ],
        out_specs=pl.BlockSpec((Q_BLOCK, HEADS_PER_GROUP * HEAD_DIM), lambda g, i: (i, g)),
        out_shape=jax.ShapeDtypeStruct((s, Q_W), BF16),
        scratch_shapes=[pltpu.VMEM((8 + nc, Q_BLOCK), F32), pltpu.VMEM((n_sel, Q_BLOCK), F32)],
        compiler_params=_cparams("parallel", "arbitrary"),
        name="sparse_attn",
    )(qt, k_cmp, v_cmp_t, ks, vs_t, kw, vw_t, gate_logits)


def _merge_kernel(x_ref, ya_ref, yb_ref, yc_ref, g0_ref, g1_ref, g2_ref, wp_ref, wg_ref, wn_ref,
                  wo_ref, ng_ref, gate_ref, o_ref):
    merged = jax.nn.sigmoid(g0_ref[...].astype(F32)) * _dot(ya_ref[...], wp_ref[...])
    merged += jax.nn.sigmoid(g1_ref[...].astype(F32)) * _dot(yb_ref[...], wg_ref[...])
    merged += jax.nn.sigmoid(g2_ref[...].astype(F32)) * _dot(yc_ref[...], wn_ref[...])
    y = _dot(merged.astype(BF16), wo_ref[...])
    o_ref[...] = x_ref[...] + gate_ref[...] * _rms(y, ng_ref[...])


def _merge(x, ya, yb, yc, proj, wp, wg, wn, wo, norm_g, gate):
    s, d = x.shape
    tm = min(256, s)
    const = lambda i: (0, 0)
    rows = lambda i: (i, 0)
    whole = lambda a: pl.BlockSpec(a.shape, const, pipeline_mode=pl.Buffered(1))
    bg = OFF_BG // d
    return pl.pallas_call(
        _merge_kernel,
        grid=(s // tm,),
        in_specs=[
            pl.BlockSpec((tm, d), rows),
            pl.BlockSpec((tm, MIX_W), rows),
            pl.BlockSpec((tm, MIX_W), rows),
            pl.BlockSpec((tm, Q_W), rows),
            pl.BlockSpec((tm, d), lambda i: (i, bg)),
            pl.BlockSpec((tm, d), lambda i: (i, bg + 1)),
            pl.BlockSpec((tm, d), lambda i: (i, bg + 2)),
            whole(wp), whole(wg), whole(wn), whole(wo),
            pl.BlockSpec((1, d), const),
            pl.BlockSpec((1, d), const),
        ],
        out_specs=pl.BlockSpec((tm, d), rows),
        out_shape=jax.ShapeDtypeStruct((s, d), F32),
        compiler_params=_cparams("parallel"),
        name="merge_out",
    )(x, ya, yb, yc, proj, proj, proj, wp, wg, wn, wo, norm_g, gate)


def _ffn_kernel(x_ref, gi_ref, sc_ref, sh_ref, w1_ref, w2_ref, go_ref, gate_ref, o_ref, h_scr):
    j = pl.program_id(1)

    @pl.when(j == 0)
    def _():
        h = _rms(x_ref[...], gi_ref[...]) * (1.0 + sc_ref[...]) + sh_ref[...]
        h_scr[...] = h.astype(BF16)
        o_ref[...] = jnp.zeros_like(o_ref)

    a = jnp.square(jnp.maximum(_dot(h_scr[...], w1_ref[...]), 0.0))
    o_ref[...] += _dot(a.astype(BF16), w2_ref[...])

    @pl.when(j == pl.num_programs(1) - 1)
    def _():
        o_ref[...] = x_ref[...] + gate_ref[...] * _rms(o_ref[...], go_ref[...])


def _ffn(x, g_in, scale, shift, w1, w2, g_out, gate):
    s, d = x.shape
    tm, tf = min(1024, s), 512
    const = lambda i, j: (0, 0)
    rows = lambda i, j: (i, 0)
    return pl.pallas_call(
        _ffn_kernel,
        grid=(s // tm, D_FF // tf),
        in_specs=[
            pl.BlockSpec((tm, d), rows, pipeline_mode=pl.Buffered(1)),
            pl.BlockSpec((1, d), const),
            pl.BlockSpec((1, d), const),
            pl.BlockSpec((1, d), const),
            pl.BlockSpec((d, tf), lambda i, j: (0, j)),
            pl.BlockSpec((tf, d), lambda i, j: (j, 0)),
            pl.BlockSpec((1, d), const),
            pl.BlockSpec((1, d), const),
        ],
        out_specs=pl.BlockSpec((tm, d), rows),
        out_shape=jax.ShapeDtypeStruct((s, d), F32),
        scratch_shapes=[pltpu.VMEM((tm, d), BF16)],
        compiler_params=_cparams("parallel", "arbitrary"),
        name="ffn",
    )(x, g_in, scale, shift, w1, w2, g_out, gate)


def _token_mixing(x, mod, norm_g, w_in, pool_w, pool_scale, ln_g, ln_b, ws, bs, cmp_pos, cmp_w1,
                  cmp_b1, cmp_w2, cmp_b2, w_br_pool, w_br_gmlp, w_br_nsa, w_out):
    s, d = x.shape
    n_qb = s // Q_BLOCK
    ng0 = OFF_BG
    w_main = jnp.concatenate([w_in[:, :ng0], w_in[:, ng0 + N_GATE:]], axis=1).astype(BF16)
    w_ng = jnp.pad(w_in[:, ng0:ng0 + N_GATE], ((0, 0), (0, 128 - N_GATE))).astype(BF16)
    row = lambda v: v.reshape(1, -1)

    proj, ngate = _in_projection(x, row(norm_g[0]), row(mod[1]), row(mod[0]), w_main, w_ng)

    ya, yb = _mixers(proj, pool_w.astype(BF16), row(pool_scale), row(ln_g), row(ln_b), ws, bs.T)

    q = proj[:, OFF_Q:OFF_KV] * jnp.asarray(HEAD_DIM ** -0.5, BF16)
    qt = q.reshape(n_qb, Q_BLOCK, KV_GROUPS, HEADS_PER_GROUP, HEAD_DIM).transpose(2, 0, 4, 3, 1)
    qt = qt.reshape(KV_GROUPS, n_qb, HEAD_DIM, QL)
    kv = proj[:, OFF_KV:OFF_BG].reshape(s, 6, KV_GROUPS, HEAD_DIM).transpose(1, 2, 0, 3)
    xkv = kv[0:2].reshape(2, KV_GROUPS, s // CMP_STRIDE, CMP_STRIDE * HEAD_DIM)
    gl = ngate[:, :N_GATE].reshape(n_qb, Q_BLOCK, KV_GROUPS, HEADS_PER_GROUP, 3).transpose(2, 0, 4, 3, 1)
    gl = gl.reshape(KV_GROUPS, n_qb, 3, QL)

    cmp = _compress(xkv, cmp_pos.reshape(2, 1, CMP_BLOCK * HEAD_DIM), cmp_w1.astype(BF16),
                    cmp_b1.reshape(2, 1, -1), cmp_w2.astype(BF16), cmp_b2.reshape(2, 1, -1))
    k_cmp = cmp[0].astype(BF16)
    v_cmp_t = cmp[1].transpose(0, 2, 1).astype(BF16)

    yc = _sparse_attention(qt, k_cmp, v_cmp_t, kv[2], kv[3].transpose(0, 2, 1), kv[4],
                           kv[5].transpose(0, 2, 1), gl, s)

    return _merge(x, ya, yb, yc, proj, w_br_pool.astype(BF16), w_br_gmlp.astype(BF16),
                  w_br_nsa.astype(BF16), w_out.astype(BF16), row(norm_g[1]), row(mod[2]))


def kernel(x, c, norm_g, w_ada, b_ada, w_in, pool_w, pool_scale, gmlp_ln_g, gmlp_ln_b, gmlp_ws, gmlp_bs,
           cmp_pos, cmp_w1, cmp_b1, cmp_w2, cmp_b2, w_br_pool, w_br_gmlp, w_br_nsa, w_out, w_ff1, w_ff2):
    b, s, d = x.shape
    assert b == 1 and d == D_MODEL and s % 1024 == 0
    n_layer = w_ada.shape[0]
    mod_all = _modulation(c, w_ada, b_ada).reshape(n_layer, 6, d)
    xs = x[0]
    row = lambda v: v.reshape(1, -1)
    for l in range(n_layer):
        mod = mod_all[l]
        xs = _token_mixing(xs, mod, norm_g[l], w_in[l], pool_w[l], pool_scale[l], gmlp_ln_g[l],
                           gmlp_ln_b[l], gmlp_ws[l], gmlp_bs[l], cmp_pos[l], cmp_w1[l], cmp_b1[l],
                           cmp_w2[l], cmp_b2[l], w_br_pool[l], w_br_gmlp[l], w_br_nsa[l], w_out[l])
        xs = _ffn(xs, row(norm_g[l, 2]), row(mod[4]), row(mod[3]), w_ff1[l].astype(BF16),
                  w_ff2[l].astype(BF16), row(norm_g[l, 3]), row(mod[5]))
    return xs[None]
```

```python
import functools

import jax
import jax.numpy as jnp
from jax import lax
from jax.experimental import pallas as pl
from jax.experimental.pallas import tpu as pltpu

F32 = jnp.float32
BF16 = jnp.bfloat16

D_MODEL = 2048
POOL_WINDOWS = (2, 4, 8, 16)
POOL_HALO = 16
GROUP_CH = 128
MIX_W = 4 * GROUP_CH
GMLP_CHUNK = 128
HEAD_DIM = 64
KV_GROUPS = 4
HEADS_PER_GROUP = 4
Q_W = 16 * HEAD_DIM
KV_W = KV_GROUPS * HEAD_DIM
CMP_BLOCK = 32
CMP_STRIDE = 16
SEL_BLOCK = 64
SEL_TOPN = 16
WINDOW = 512
FORCE_BONUS = 1000.0
LOG2E = 1.4426950408889634
NEG = -1e30
N_GATE = 3 * 16
D_FF = 4 * D_MODEL

Q_BLOCK = 128
QL = HEADS_PER_GROUP * Q_BLOCK
SEL_TILE = 512
KS_AUG_W = 128

OFF_POOL, OFF_U, OFF_V, OFF_Q, OFF_KV, OFF_BG = 0, 512, 1024, 1536, 2560, 4096
PROJ_W = OFF_BG + 3 * D_MODEL

VMEM_LIMIT = 56 * 1024 * 1024


def _cparams(*sem):
    return pltpu.CompilerParams(dimension_semantics=sem, vmem_limit_bytes=VMEM_LIMIT)


def _dot(a, b):
    return jnp.dot(a, b, preferred_element_type=F32)


def _rms(x, g):
    return x * lax.rsqrt(jnp.mean(x * x, axis=-1, keepdims=True) + 1e-6) * g


def _mod_kernel(c_ref, w_ref, b_ref, o_ref):
    c = c_ref[...]
    act = c * jax.nn.sigmoid(c)
    o_ref[0] = jnp.sum(act * w_ref[0], axis=0, keepdims=True) + b_ref[0]


def _modulation(c, w_ada, b_ada):
    n_layer, d, n_out = w_ada.shape
    tn = 1024
    return pl.pallas_call(
        _mod_kernel,
        grid=(n_layer, n_out // tn),
        in_specs=[
            pl.BlockSpec((d, 1), lambda l, j: (0, 0)),
            pl.BlockSpec((1, d, tn), lambda l, j: (l, 0, j)),
            pl.BlockSpec((1, 1, tn), lambda l, j: (l, 0, j)),
        ],
        out_specs=pl.BlockSpec((1, 1, tn), lambda l, j: (l, 0, j)),
        out_shape=jax.ShapeDtypeStruct((n_layer, 1, n_out), F32),
        compiler_params=_cparams("parallel", "parallel"),
        name="adaln_mod",
    )(c.reshape(d, 1), w_ada, b_ada.reshape(n_layer, 1, n_out))


def _inproj_kernel(x_ref, g_ref, sc_ref, sh_ref, w_ref, wng_ref, cs_ref, o_ref, ng_ref, h_scr):
    @pl.when(pl.program_id(1) == 0)
    def _():
        h = _rms(x_ref[...], g_ref[...]) * (1.0 + sc_ref[...]) + sh_ref[...]
        hb = h.astype(BF16)
        h_scr[...] = hb
        ng_ref[...] = _dot(hb, wng_ref[...])

    o_ref[...] = (_dot(h_scr[...], w_ref[...]) * cs_ref[...]).astype(o_ref.dtype)


def _in_projection(x, g, scale, shift, w_main, w_ng, col_scale):
    s, d = x.shape
    tm, tn = min(1024, s), 1024
    row = lambda i, j: (0, 0)
    return pl.pallas_call(
        _inproj_kernel,
        grid=(s // tm, PROJ_W // tn),
        in_specs=[
            pl.BlockSpec((tm, d), lambda i, j: (i, 0)),
            pl.BlockSpec((1, d), row),
            pl.BlockSpec((1, d), row),
            pl.BlockSpec((1, d), row),
            pl.BlockSpec((d, tn), lambda i, j: (0, j)),
            pl.BlockSpec((d, 128), row),
            pl.BlockSpec((1, tn), lambda i, j: (0, j)),
        ],
        out_specs=[
            pl.BlockSpec((tm, tn), lambda i, j: (i, j)),
            pl.BlockSpec((tm, 128), lambda i, j: (i, 0)),
        ],
        out_shape=[
            jax.ShapeDtypeStruct((s, PROJ_W), BF16),
            jax.ShapeDtypeStruct((s, 128), F32),
        ],
        scratch_shapes=[pltpu.VMEM((tm, d), BF16)],
        compiler_params=_cparams("parallel", "arbitrary"),
        name="in_proj",
    )(x, g, scale, shift, w_main, w_ng, col_scale)


def _mixer_kernel(a_ref, halo_ref, u_ref, v_ref, pw_ref, ps_ref, lg_ref, lb_ref, ws_ref, bs_ref,
                  ya_ref, yb_ref):
    i = pl.program_id(0)
    tm = a_ref.shape[0]
    a = a_ref[...].astype(F32)
    halo = jnp.where(i > 0, halo_ref[...].astype(F32), 0.0)
    ext = jnp.concatenate([halo, a], axis=0)
    p2 = ext[1:] + ext[:-1]
    p4 = p2[2:] + p2[:-2]
    p8 = p4[4:] + p4[:-4]
    p16 = p8[8:] + p8[:-8]
    sums = (p2[15:15 + tm], p4[13:13 + tm], p8[9:9 + tm], p16[1:1 + tm])
    t = i * tm + lax.broadcasted_iota(jnp.int32, (tm, 1), 0)
    for gi, w in enumerate(POOL_WINDOWS):
        cols = slice(gi * GROUP_CH, (gi + 1) * GROUP_CH)
        cnt = jnp.minimum(t + 1, w).astype(F32)
        pooled = sums[gi][:, cols] / cnt - a[:, cols]
        y = _dot(pooled.astype(BF16), pw_ref[gi])
        ya_ref[:, cols] = (y * ps_ref[:, cols]).astype(ya_ref.dtype)

    u = jax.nn.gelu(u_ref[...].astype(F32))
    v = jax.nn.gelu(v_ref[...].astype(F32))
    mu = jnp.mean(v, axis=-1, keepdims=True)
    var = jnp.mean(jnp.square(v - mu), axis=-1, keepdims=True)
    vn = ((v - mu) * lax.rsqrt(var + 1e-5) * lg_ref[...] + lb_ref[...]).astype(BF16)
    r = lax.broadcasted_iota(jnp.int32, (GMLP_CHUNK, GMLP_CHUNK), 0)
    c = lax.broadcasted_iota(jnp.int32, (GMLP_CHUNK, GMLP_CHUNK), 1)
    for gi in range(4):
        cols = slice(gi * GROUP_CH, (gi + 1) * GROUP_CH)
        wsm = jnp.where(r >= c, ws_ref[gi], 0.0).astype(BF16)
        bias = bs_ref[:, gi:gi + 1]
        for ck in range(tm // GMLP_CHUNK):
            rows = slice(ck * GMLP_CHUNK, (ck + 1) * GMLP_CHUNK)
            mixed = _dot(wsm, vn[rows, cols]) + bias
            yb_ref[rows, cols] = (u[rows, cols] * mixed).astype(yb_ref.dtype)


def _mixers(proj, pool_w, pool_scale, ln_g, ln_b, ws, bs_t):
    s = proj.shape[0]
    tm = min(512, s)
    hb = tm // POOL_HALO
    const2 = lambda i: (0, 0)
    const3 = lambda i: (0, 0, 0)
    return pl.pallas_call(
        _mixer_kernel,
        grid=(s // tm,),
        in_specs=[
            pl.BlockSpec((tm, MIX_W), lambda i: (i, OFF_POOL // MIX_W)),
            pl.BlockSpec((POOL_HALO, MIX_W), lambda i: (jnp.maximum(i * hb - 1, 0), OFF_POOL // MIX_W)),
            pl.BlockSpec((tm, MIX_W), lambda i: (i, OFF_U // MIX_W)),
            pl.BlockSpec((tm, MIX_W), lambda i: (i, OFF_V // MIX_W)),
            pl.BlockSpec((4, GROUP_CH, GROUP_CH), const3),
            pl.BlockSpec((1, MIX_W), const2),
            pl.BlockSpec((1, MIX_W), const2),
            pl.BlockSpec((1, MIX_W), const2),
            pl.BlockSpec((4, GMLP_CHUNK, GMLP_CHUNK), const3),
            pl.BlockSpec((GMLP_CHUNK, 4), const2),
        ],
        out_specs=[
            pl.BlockSpec((tm, MIX_W), lambda i: (i, 0)),
            pl.BlockSpec((tm, MIX_W), lambda i: (i, 0)),
        ],
        out_shape=[jax.ShapeDtypeStruct((s, MIX_W), BF16)] * 2,
        compiler_params=_cparams("parallel"),
        name="mixers",
    )(proj, proj, proj, proj, pool_w, pool_scale, ln_g, ln_b, ws, bs_t)


def _compress_kernel(x_ref, pos_ref, w1_ref, b1_ref, w2_ref, b2_ref, o_ref):
    half = CMP_STRIDE * HEAD_DIM
    x = x_ref[0, 0].astype(F32)
    nc = x.shape[0]
    pos = pos_ref[0]
    first = _dot((x + pos[:, :half]).astype(BF16), w1_ref[0, :half, :])
    second = _dot((x + pos[:, half:]).astype(BF16), w1_ref[0, half:, :])
    hid = jax.nn.gelu(first + pltpu.roll(second, nc - 1, 0) + b1_ref[0])
    o_ref[0, 0] = _dot(hid.astype(BF16), w2_ref[0]) + b2_ref[0]


def _compress(xkv, pos, w1, b1, w2, b2):
    _, n_group, nc, width = xkv.shape
    per_kv = lambda a, g: (a, 0, 0)
    return pl.pallas_call(
        _compress_kernel,
        grid=(2, n_group),
        in_specs=[
            pl.BlockSpec((1, 1, nc, width), lambda a, g: (a, g, 0, 0)),
            pl.BlockSpec((1, 1, 2 * width), per_kv),
            pl.BlockSpec((1, 2 * width, 128), per_kv),
            pl.BlockSpec((1, 1, 128), per_kv),
            pl.BlockSpec((1, 128, HEAD_DIM), per_kv),
            pl.BlockSpec((1, 1, HEAD_DIM), per_kv),
        ],
        out_specs=pl.BlockSpec((1, 1, nc, HEAD_DIM), lambda a, g: (a, g, 0, 0)),
        out_shape=jax.ShapeDtypeStruct((2, n_group, nc, HEAD_DIM), F32),
        compiler_params=_cparams("parallel", "parallel"),
        name="compress_kv",
    )(xkv, pos, w1, b1, w2, b2)


def _nsa_kernel(qt_ref, kc_ref, vct_ref, ks_ref, vst_ref, kw_ref, vwt_ref, gl_ref, o_ref,
                ps_scr, bias_scr, qaug_scr, s0_scr, s1_scr):
    qb = pl.program_id(1)
    t0 = qb * Q_BLOCK
    qt = qt_ref[0, 0]
    nc = kc_ref.shape[1]
    n_sel = bias_scr.shape[0]
    tq = t0 + lax.broadcasted_iota(jnp.int32, (1, QL), 1) % Q_BLOCK

    s_c = _dot(kc_ref[0], qt)
    blk_end = CMP_STRIDE * lax.broadcasted_iota(jnp.int32, (nc, 1), 0) + (CMP_BLOCK - 1)
    mask_c = blk_end <= tq
    s_c = jnp.where(mask_c, s_c, NEG)
    m_c = jnp.max(s_c, axis=0, keepdims=True)
    e_c = jnp.where(mask_c, jnp.exp2(s_c - m_c), 0.0)
    inv_c = 1.0 / jnp.maximum(jnp.sum(e_c, axis=0, keepdims=True), 1e-30)
    p_c = e_c * inv_c
    o_c = _dot(vct_ref[0], p_c.astype(BF16))

    p_grp = (p_c[:, 0:Q_BLOCK] + p_c[:, Q_BLOCK:2 * Q_BLOCK]
             + p_c[:, 2 * Q_BLOCK:3 * Q_BLOCK] + p_c[:, 3 * Q_BLOCK:])
    ps_scr[0:8, :] = jnp.zeros((8, Q_BLOCK), F32)
    ps_scr[8:8 + nc, :] = p_grp
    imp = ps_scr[pl.ds(7, n_sel, stride=4), :]
    for k in range(8, 12):
        imp = imp + ps_scr[pl.ds(k, n_sel, stride=4), :]

    j_idx = lax.broadcasted_iota(jnp.int32, (n_sel, Q_BLOCK), 0)
    cur = (t0 + lax.broadcasted_iota(jnp.int32, (n_sel, Q_BLOCK), 1)) // SEL_BLOCK
    forced = (j_idx == 0) | (j_idx == cur) | (j_idx == cur - 1)
    valid = j_idx <= cur
    score = jnp.where(valid, imp + FORCE_BONUS * forced.astype(F32), -1.0)

    def pick_one(_, carry):
        sc, sel = carry
        best = jnp.max(sc, axis=0, keepdims=True)
        first = jnp.min(jnp.where(sc == best, j_idx, n_sel), axis=0, keepdims=True)
        pick = j_idx == first
        return jnp.where(pick, -2.0, sc), jnp.where(pick, 1.0, sel)

    _, sel = lax.fori_loop(0, SEL_TOPN, pick_one, (score, jnp.zeros((n_sel, Q_BLOCK), F32)))
    bias_scr[...] = jnp.where(valid & (sel > 0.5) & (j_idx * SEL_BLOCK < t0), 0.0, NEG)

    def flash_update(s, carry, vt):
        m_prev, l_prev, acc = carry
        m_new = jnp.maximum(m_prev, jnp.max(s, axis=0, keepdims=True))
        alpha = jnp.exp2(m_prev - m_new)
        p = jnp.exp2(s - m_new)
        l_new = alpha * l_prev + jnp.sum(p, axis=0, keepdims=True)
        return m_new, l_new, alpha * acc + _dot(vt, p.astype(BF16))

    t0a = pl.multiple_of(t0, Q_BLOCK)
    s_d = _dot(ks_ref[0, pl.ds(t0a, Q_BLOCK), 0:HEAD_DIM], qt)
    s_d = jnp.where(t0 + lax.broadcasted_iota(jnp.int32, (Q_BLOCK, 1), 0) <= tq, s_d, NEG)
    init = (jnp.full((1, QL), NEG, F32), jnp.zeros((1, QL), F32), jnp.zeros((HEAD_DIM, QL), F32))
    carry = flash_update(s_d, init, vst_ref[0, :, pl.ds(t0a, Q_BLOCK)])

    blocks_per_tile = SEL_TILE // SEL_BLOCK
    last_tile = ks_ref.shape[1] // SEL_TILE - 1
    qaug_scr[0:HEAD_DIM, :] = qt
    qaug_scr[HEAD_DIM + 16:, :] = jnp.zeros((qaug_scr.shape[0] - HEAD_DIM - 16, QL), BF16)

    def sel_scores(kt, dst):
        kt = jnp.minimum(kt, last_tile)
        b8 = bias_scr[pl.ds(pl.multiple_of(kt * blocks_per_tile, blocks_per_tile), blocks_per_tile), :]
        b16 = jnp.concatenate([b8, jnp.zeros((16 - blocks_per_tile, Q_BLOCK), F32)], axis=0)
        qaug_scr[HEAD_DIM:HEAD_DIM + 16, :] = jnp.concatenate([b16] * HEADS_PER_GROUP, axis=1).astype(BF16)
        k0 = pl.multiple_of(kt * SEL_TILE, SEL_TILE)
        dst[...] = _dot(ks_ref[0, pl.ds(k0, SEL_TILE), :], qaug_scr[...])

    def half_step(kt, cur, nxt, carry):
        sel_scores(kt + 1, nxt)
        k0 = pl.multiple_of(jnp.minimum(kt, last_tile) * SEL_TILE, SEL_TILE)
        return flash_update(cur[...], carry, vst_ref[0, :, pl.ds(k0, SEL_TILE)])

    def pair_step(i, carry):
        carry = half_step(2 * i, s0_scr, s1_scr, carry)
        return half_step(2 * i + 1, s1_scr, s0_scr, carry)

    n_main = (t0 + SEL_TILE - 1) // SEL_TILE
    sel_scores(0, s0_scr)
    _, l_s, acc_s = lax.fori_loop(0, (n_main + 1) // 2, pair_step, carry)

    span = WINDOW + Q_BLOCK
    w0 = pl.multiple_of(jnp.maximum(t0 - WINDOW, 0), Q_BLOCK)
    s_w = _dot(kw_ref[0, pl.ds(w0, span), :], qt)
    rel = tq - (w0 + lax.broadcasted_iota(jnp.int32, (span, 1), 0))
    s_w = jnp.where((rel >= 0) & (rel < WINDOW), s_w, NEG)
    p_w = jnp.exp2(s_w - jnp.max(s_w, axis=0, keepdims=True))
    l_w = jnp.sum(p_w, axis=0, keepdims=True)
    acc_w = _dot(vwt_ref[0, :, pl.ds(w0, span)], p_w.astype(BF16))

    gate = jax.nn.sigmoid(gl_ref[0, 0])
    o = gate[0:1] * o_c + gate[1:2] * (acc_s / l_s) + gate[2:3] * (acc_w / l_w)
    stacked = jnp.concatenate([o[:, p * Q_BLOCK:(p + 1) * Q_BLOCK] for p in range(HEADS_PER_GROUP)], axis=0)
    o_ref[...] = stacked.T.astype(o_ref.dtype)


def _sparse_attention(qt, k_cmp, v_cmp_t, ks, vs_t, kw, vw_t, gate_logits, s):
    n_group, n_qb = qt.shape[:2]
    nc = k_cmp.shape[1]
    n_sel = s // SEL_BLOCK
    per_group3 = lambda g, i: (g, 0, 0)
    per_step = lambda g, i: (g, i, 0, 0)
    return pl.pallas_call(
        _nsa_kernel,
        grid=(n_group, n_qb),
        in_specs=[
            pl.BlockSpec((1, 1, HEAD_DIM, QL), per_step),
            pl.BlockSpec((1, nc, HEAD_DIM), per_group3),
            pl.BlockSpec((1, HEAD_DIM, nc), per_group3),
            pl.BlockSpec((1, s, KS_AUG_W), per_group3),
            pl.BlockSpec((1, HEAD_DIM, s), per_group3),
            pl.BlockSpec((1, s, HEAD_DIM), per_group3),
            pl.BlockSpec((1, HEAD_DIM, s), per_group3),
            pl.BlockSpec((1, 1, 3, QL), per_step),
        ],
        out_specs=pl.BlockSpec((Q_BLOCK, HEADS_PER_GROUP * HEAD_DIM), lambda g, i: (i, g)),
        out_shape=jax.ShapeDtypeStruct((s, Q_W), BF16),
        scratch_shapes=[pltpu.VMEM((8 + nc, Q_BLOCK), F32), pltpu.VMEM((n_sel, Q_BLOCK), F32),
                        pltpu.VMEM((KS_AUG_W, QL), BF16), pltpu.VMEM((SEL_TILE, QL), F32),
                        pltpu.VMEM((SEL_TILE, QL), F32)],
        compiler_params=_cparams("parallel", "arbitrary"),
        name="sparse_attn",
    )(qt, k_cmp, v_cmp_t, ks, vs_t, kw, vw_t, gate_logits)


def _merge_kernel(x_ref, ya_ref, yb_ref, yc_ref, g0_ref, g1_ref, g2_ref, wp_ref, wg_ref, wn_ref,
                  wo_ref, ng_ref, gate_ref, o_ref):
    merged = jax.nn.sigmoid(g0_ref[...].astype(F32)) * _dot(ya_ref[...], wp_ref[...])
    merged += jax.nn.sigmoid(g1_ref[...].astype(F32)) * _dot(yb_ref[...], wg_ref[...])
    merged += jax.nn.sigmoid(g2_ref[...].astype(F32)) * _dot(yc_ref[...], wn_ref[...])
    y = _dot(merged.astype(BF16), wo_ref[...])
    o_ref[...] = x_ref[...] + gate_ref[...] * _rms(y, ng_ref[...])


def _merge(x, ya, yb, yc, proj, wp, wg, wn, wo, norm_g, gate):
    s, d = x.shape
    tm = min(256, s)
    const = lambda i: (0, 0)
    rows = lambda i: (i, 0)
    whole = lambda a: pl.BlockSpec(a.shape, const, pipeline_mode=pl.Buffered(1))
    bg = OFF_BG // d
    return pl.pallas_call(
        _merge_kernel,
        grid=(s // tm,),
        in_specs=[
            pl.BlockSpec((tm, d), rows),
            pl.BlockSpec((tm, MIX_W), rows),
            pl.BlockSpec((tm, MIX_W), rows),
            pl.BlockSpec((tm, Q_W), rows),
            pl.BlockSpec((tm, d), lambda i: (i, bg)),
            pl.BlockSpec((tm, d), lambda i: (i, bg + 1)),
            pl.BlockSpec((tm, d), lambda i: (i, bg + 2)),
            whole(wp), whole(wg), whole(wn), whole(wo),
            pl.BlockSpec((1, d), const),
            pl.BlockSpec((1, d), const),
        ],
        out_specs=pl.BlockSpec((tm, d), rows),
        out_shape=jax.ShapeDtypeStruct((s, d), F32),
        compiler_params=_cparams("parallel"),
        name="merge_out",
    )(x, ya, yb, yc, proj, proj, proj, wp, wg, wn, wo, norm_g, gate)


def _ffn_kernel(x_ref, gi_ref, sc_ref, sh_ref, w1_ref, w2_ref, go_ref, gate_ref, o_ref, h_scr):
    j = pl.program_id(1)

    @pl.when(j == 0)
    def _():
        h = _rms(x_ref[...], gi_ref[...]) * (1.0 + sc_ref[...]) + sh_ref[...]
        h_scr[...] = h.astype(BF16)
        o_ref[...] = jnp.zeros_like(o_ref)

    a = jnp.square(jnp.maximum(_dot(h_scr[...], w1_ref[...]), 0.0))
    o_ref[...] += _dot(a.astype(BF16), w2_ref[...])

    @pl.when(j == pl.num_programs(1) - 1)
    def _():
        o_ref[...] = x_ref[...] + gate_ref[...] * _rms(o_ref[...], go_ref[...])


def _ffn(x, g_in, scale, shift, w1, w2, g_out, gate):
    s, d = x.shape
    tm, tf = min(1024, s), 512
    const = lambda i, j: (0, 0)
    rows = lambda i, j: (i, 0)
    return pl.pallas_call(
        _ffn_kernel,
        grid=(s // tm, D_FF // tf),
        in_specs=[
            pl.BlockSpec((tm, d), rows, pipeline_mode=pl.Buffered(1)),
            pl.BlockSpec((1, d), const),
            pl.BlockSpec((1, d), const),
            pl.BlockSpec((1, d), const),
            pl.BlockSpec((d, tf), lambda i, j: (0, j)),
            pl.BlockSpec((tf, d), lambda i, j: (j, 0)),
            pl.BlockSpec((1, d), const),
            pl.BlockSpec((1, d), const),
        ],
        out_specs=pl.BlockSpec((tm, d), rows),
        out_shape=jax.ShapeDtypeStruct((s, d), F32),
        scratch_shapes=[pltpu.VMEM((tm, d), BF16)],
        compiler_params=_cparams("parallel", "arbitrary"),
        name="ffn",
    )(x, g_in, scale, shift, w1, w2, g_out, gate)


def _token_mixing(x, mod, norm_g, w_in, pool_w, pool_scale, ln_g, ln_b, ws, bs, cmp_pos, cmp_w1,
                  cmp_b1, cmp_w2, cmp_b2, w_br_pool, w_br_gmlp, w_br_nsa, w_out):
    s, d = x.shape
    n_qb = s // Q_BLOCK
    ng0 = OFF_BG
    w_main = jnp.concatenate([w_in[:, :ng0], w_in[:, ng0 + N_GATE:]], axis=1).astype(BF16)
    w_ng = jnp.pad(w_in[:, ng0:ng0 + N_GATE], ((0, 0), (0, 128 - N_GATE))).astype(BF16)
    row = lambda v: v.reshape(1, -1)

    col_scale = jnp.ones((1, PROJ_W), F32).at[:, OFF_Q:OFF_KV].set(HEAD_DIM ** -0.5 * LOG2E)
    proj, ngate = _in_projection(x, row(norm_g[0]), row(mod[1]), row(mod[0]), w_main, w_ng, col_scale)

    ya, yb = _mixers(proj, pool_w.astype(BF16), row(pool_scale), row(ln_g), row(ln_b), ws, bs.T)

    qt = proj[:, OFF_Q:OFF_KV].reshape(n_qb, Q_BLOCK, KV_GROUPS, HEADS_PER_GROUP, HEAD_DIM).transpose(2, 0, 4, 3, 1)
    qt = qt.reshape(KV_GROUPS, n_qb, HEAD_DIM, QL)
    kv = proj[:, OFF_KV:OFF_BG].reshape(s, 6, KV_GROUPS, HEAD_DIM).transpose(1, 2, 0, 3)
    xkv = kv[0:2].reshape(2, KV_GROUPS, s // CMP_STRIDE, CMP_STRIDE * HEAD_DIM)
    gl = ngate[:, :N_GATE].reshape(n_qb, Q_BLOCK, KV_GROUPS, HEADS_PER_GROUP, 3).transpose(2, 0, 4, 3, 1)
    gl = gl.reshape(KV_GROUPS, n_qb, 3, QL)

    cmp = _compress(xkv, cmp_pos.reshape(2, 1, CMP_BLOCK * HEAD_DIM), cmp_w1.astype(BF16),
                    cmp_b1.reshape(2, 1, -1), cmp_w2.astype(BF16), cmp_b2.reshape(2, 1, -1))
    k_cmp = cmp[0].astype(BF16)
    v_cmp_t = cmp[1].transpose(0, 2, 1).astype(BF16)

    blk = (jnp.arange(s) // SEL_BLOCK) % (SEL_TILE // SEL_BLOCK)
    onehot = (blk[:, None] == jnp.arange(KS_AUG_W - HEAD_DIM)[None, :]).astype(BF16)
    ks_aug = jnp.concatenate([kv[2], jnp.broadcast_to(onehot, (KV_GROUPS,) + onehot.shape)], axis=-1)
    yc = _sparse_attention(qt, k_cmp, v_cmp_t, ks_aug, kv[3].transpose(0, 2, 1), kv[4],
                           kv[5].transpose(0, 2, 1), gl, s)

    return _merge(x, ya, yb, yc, proj, w_br_pool.astype(BF16), w_br_gmlp.astype(BF16),
                  w_br_nsa.astype(BF16), w_out.astype(BF16), row(norm_g[1]), row(mod[2]))


def kernel(x, c, norm_g, w_ada, b_ada, w_in, pool_w, pool_scale, gmlp_ln_g, gmlp_ln_b, gmlp_ws, gmlp_bs,
           cmp_pos, cmp_w1, cmp_b1, cmp_w2, cmp_b2, w_br_pool, w_br_gmlp, w_br_nsa, w_out, w_ff1, w_ff2):
    b, s, d = x.shape
    assert b == 1 and d == D_MODEL and s % 1024 == 0
    n_layer = w_ada.shape[0]
    mod_all = _modulation(c, w_ada, b_ada).reshape(n_layer, 6, d)
    xs = x[0]
    row = lambda v: v.reshape(1, -1)
    for l in range(n_layer):
        mod = mod_all[l]
        xs = _token_mixing(xs, mod, norm_g[l], w_in[l], pool_w[l], pool_scale[l], gmlp_ln_g[l],
                           gmlp_ln_b[l], gmlp_ws[l], gmlp_bs[l], cmp_pos[l], cmp_w1[l], cmp_b1[l],
                           cmp_w2[l], cmp_b2[l], w_br_pool[l], w_br_gmlp[l], w_br_nsa[l], w_out[l])
        xs = _ffn(xs, row(norm_g[l, 2]), row(mod[4]), row(mod[3]), w_ff1[l].astype(BF16),
                  w_ff2[l].astype(BF16), row(norm_g[l, 3]), row(mod[5]))
    return xs[None]
```

```python
import functools

import jax
import jax.numpy as jnp
from jax import lax
from jax.experimental import pallas as pl
from jax.experimental.pallas import tpu as pltpu

F32 = jnp.float32
BF16 = jnp.bfloat16

D_MODEL = 2048
POOL_WINDOWS = (2, 4, 8, 16)
POOL_HALO = 16
GROUP_CH = 128
MIX_W = 4 * GROUP_CH
GMLP_CHUNK = 128
HEAD_DIM = 64
KV_GROUPS = 4
HEADS_PER_GROUP = 4
Q_W = 16 * HEAD_DIM
KV_W = KV_GROUPS * HEAD_DIM
CMP_BLOCK = 32
CMP_STRIDE = 16
SEL_BLOCK = 64
SEL_TOPN = 16
WINDOW = 512
FORCE_BONUS = 1000.0
LOG2E = 1.4426950408889634
NEG = -1e30
N_GATE = 3 * 16
D_FF = 4 * D_MODEL

Q_BLOCK = 128
QL = HEADS_PER_GROUP * Q_BLOCK
SEL_TILE = 512
KS_AUG_W = 128
CMP_AUG_W = 256
V_AUG_ROWS = HEAD_DIM + 16
P_CHUNK = 64

OFF_POOL, OFF_U, OFF_V, OFF_Q, OFF_KV, OFF_BG = 0, 512, 1024, 1536, 2560, 4096
PROJ_W = OFF_BG + 3 * D_MODEL

VMEM_LIMIT = 56 * 1024 * 1024


def _cparams(*sem):
    return pltpu.CompilerParams(dimension_semantics=sem, vmem_limit_bytes=VMEM_LIMIT)


def _dot(a, b):
    return jnp.dot(a, b, preferred_element_type=F32)


def _rms(x, g):
    return x * lax.rsqrt(jnp.mean(x * x, axis=-1, keepdims=True) + 1e-6) * g


def _mod_kernel(c_ref, w_ref, b_ref, o_ref):
    c = c_ref[...]
    act = c * jax.nn.sigmoid(c)
    o_ref[0] = jnp.sum(act * w_ref[0], axis=0, keepdims=True) + b_ref[0]


def _modulation(c, w_ada, b_ada):
    n_layer, d, n_out = w_ada.shape
    tn = 1024
    return pl.pallas_call(
        _mod_kernel,
        grid=(n_layer, n_out // tn),
        in_specs=[
            pl.BlockSpec((d, 1), lambda l, j: (0, 0)),
            pl.BlockSpec((1, d, tn), lambda l, j: (l, 0, j)),
            pl.BlockSpec((1, 1, tn), lambda l, j: (l, 0, j)),
        ],
        out_specs=pl.BlockSpec((1, 1, tn), lambda l, j: (l, 0, j)),
        out_shape=jax.ShapeDtypeStruct((n_layer, 1, n_out), F32),
        compiler_params=_cparams("parallel", "parallel"),
        name="adaln_mod",
    )(c.reshape(d, 1), w_ada, b_ada.reshape(n_layer, 1, n_out))


def _inproj_kernel(x_ref, g_ref, sc_ref, sh_ref, w_ref, wng_ref, cs_ref, o_ref, ng_ref, h_scr):
    @pl.when(pl.program_id(1) == 0)
    def _():
        h = _rms(x_ref[...], g_ref[...]) * (1.0 + sc_ref[...]) + sh_ref[...]
        hb = h.astype(BF16)
        h_scr[...] = hb
        ng_ref[...] = _dot(hb, wng_ref[...])

    o_ref[...] = (_dot(h_scr[...], w_ref[...]) * cs_ref[...]).astype(o_ref.dtype)


def _in_projection(x, g, scale, shift, w_main, w_ng, col_scale):
    s, d = x.shape
    tm, tn = min(1024, s), 1024
    row = lambda i, j: (0, 0)
    return pl.pallas_call(
        _inproj_kernel,
        grid=(s // tm, PROJ_W // tn),
        in_specs=[
            pl.BlockSpec((tm, d), lambda i, j: (i, 0)),
            pl.BlockSpec((1, d), row),
            pl.BlockSpec((1, d), row),
            pl.BlockSpec((1, d), row),
            pl.BlockSpec((d, tn), lambda i, j: (0, j)),
            pl.BlockSpec((d, 128), row),
            pl.BlockSpec((1, tn), lambda i, j: (0, j)),
        ],
        out_specs=[
            pl.BlockSpec((tm, tn), lambda i, j: (i, j)),
            pl.BlockSpec((tm, 128), lambda i, j: (i, 0)),
        ],
        out_shape=[
            jax.ShapeDtypeStruct((s, PROJ_W), BF16),
            jax.ShapeDtypeStruct((s, 128), F32),
        ],
        scratch_shapes=[pltpu.VMEM((tm, d), BF16)],
        compiler_params=_cparams("parallel", "arbitrary"),
        name="in_proj",
    )(x, g, scale, shift, w_main, w_ng, col_scale)


def _mixer_kernel(a_ref, halo_ref, u_ref, v_ref, pw_ref, ps_ref, lg_ref, lb_ref, ws_ref, bs_ref,
                  ya_ref, yb_ref):
    i = pl.program_id(0)
    tm = a_ref.shape[0]
    a = a_ref[...].astype(F32)
    halo = jnp.where(i > 0, halo_ref[...].astype(F32), 0.0)
    ext = jnp.concatenate([halo, a], axis=0)
    p2 = ext[1:] + ext[:-1]
    p4 = p2[2:] + p2[:-2]
    p8 = p4[4:] + p4[:-4]
    p16 = p8[8:] + p8[:-8]
    sums = (p2[15:15 + tm], p4[13:13 + tm], p8[9:9 + tm], p16[1:1 + tm])
    t = i * tm + lax.broadcasted_iota(jnp.int32, (tm, 1), 0)
    for gi, w in enumerate(POOL_WINDOWS):
        cols = slice(gi * GROUP_CH, (gi + 1) * GROUP_CH)
        cnt = jnp.minimum(t + 1, w).astype(F32)
        pooled = sums[gi][:, cols] / cnt - a[:, cols]
        y = _dot(pooled.astype(BF16), pw_ref[gi])
        ya_ref[:, cols] = (y * ps_ref[:, cols]).astype(ya_ref.dtype)

    u = jax.nn.gelu(u_ref[...].astype(F32))
    v = jax.nn.gelu(v_ref[...].astype(F32))
    mu = jnp.mean(v, axis=-1, keepdims=True)
    var = jnp.mean(jnp.square(v - mu), axis=-1, keepdims=True)
    vn = ((v - mu) * lax.rsqrt(var + 1e-5) * lg_ref[...] + lb_ref[...]).astype(BF16)
    r = lax.broadcasted_iota(jnp.int32, (GMLP_CHUNK, GMLP_CHUNK), 0)
    c = lax.broadcasted_iota(jnp.int32, (GMLP_CHUNK, GMLP_CHUNK), 1)
    for gi in range(4):
        cols = slice(gi * GROUP_CH, (gi + 1) * GROUP_CH)
        wsm = jnp.where(r >= c, ws_ref[gi], 0.0).astype(BF16)
        bias = bs_ref[:, gi:gi + 1]
        for ck in range(tm // GMLP_CHUNK):
            rows = slice(ck * GMLP_CHUNK, (ck + 1) * GMLP_CHUNK)
            mixed = _dot(wsm, vn[rows, cols]) + bias
            yb_ref[rows, cols] = (u[rows, cols] * mixed).astype(yb_ref.dtype)


def _mixers(proj, pool_w, pool_scale, ln_g, ln_b, ws, bs_t):
    s = proj.shape[0]
    tm = min(512, s)
    hb = tm // POOL_HALO
    const2 = lambda i: (0, 0)
    const3 = lambda i: (0, 0, 0)
    return pl.pallas_call(
        _mixer_kernel,
        grid=(s // tm,),
        in_specs=[
            pl.BlockSpec((tm, MIX_W), lambda i: (i, OFF_POOL // MIX_W)),
            pl.BlockSpec((POOL_HALO, MIX_W), lambda i: (jnp.maximum(i * hb - 1, 0), OFF_POOL // MIX_W)),
            pl.BlockSpec((tm, MIX_W), lambda i: (i, OFF_U // MIX_W)),
            pl.BlockSpec((tm, MIX_W), lambda i: (i, OFF_V // MIX_W)),
            pl.BlockSpec((4, GROUP_CH, GROUP_CH), const3),
            pl.BlockSpec((1, MIX_W), const2),
            pl.BlockSpec((1, MIX_W), const2),
            pl.BlockSpec((1, MIX_W), const2),
            pl.BlockSpec((4, GMLP_CHUNK, GMLP_CHUNK), const3),
            pl.BlockSpec((GMLP_CHUNK, 4), const2),
        ],
        out_specs=[
            pl.BlockSpec((tm, MIX_W), lambda i: (i, 0)),
            pl.BlockSpec((tm, MIX_W), lambda i: (i, 0)),
        ],
        out_shape=[jax.ShapeDtypeStruct((s, MIX_W), BF16)] * 2,
        compiler_params=_cparams("parallel"),
        name="mixers",
    )(proj, proj, proj, proj, pool_w, pool_scale, ln_g, ln_b, ws, bs_t)


def _compress_kernel(x_ref, pos_ref, w1_ref, b1_ref, w2_ref, b2_ref, o_ref):
    half = CMP_STRIDE * HEAD_DIM
    x = x_ref[0, 0].astype(F32)
    nc = x.shape[0]
    pos = pos_ref[0]
    first = _dot((x + pos[:, :half]).astype(BF16), w1_ref[0, :half, :])
    second = _dot((x + pos[:, half:]).astype(BF16), w1_ref[0, half:, :])
    hid = jax.nn.gelu(first + pltpu.roll(second, nc - 1, 0) + b1_ref[0])
    o_ref[0, 0] = _dot(hid.astype(BF16), w2_ref[0]) + b2_ref[0]


def _compress(xkv, pos, w1, b1, w2, b2):
    _, n_group, nc, width = xkv.shape
    per_kv = lambda a, g: (a, 0, 0)
    return pl.pallas_call(
        _compress_kernel,
        grid=(2, n_group),
        in_specs=[
            pl.BlockSpec((1, 1, nc, width), lambda a, g: (a, g, 0, 0)),
            pl.BlockSpec((1, 1, 2 * width), per_kv),
            pl.BlockSpec((1, 2 * width, 128), per_kv),
            pl.BlockSpec((1, 1, 128), per_kv),
            pl.BlockSpec((1, 128, HEAD_DIM), per_kv),
            pl.BlockSpec((1, 1, HEAD_DIM), per_kv),
        ],
        out_specs=pl.BlockSpec((1, 1, nc, HEAD_DIM), lambda a, g: (a, g, 0, 0)),
        out_shape=jax.ShapeDtypeStruct((2, n_group, nc, HEAD_DIM), F32),
        compiler_params=_cparams("parallel", "parallel"),
        name="compress_kv",
    )(xkv, pos, w1, b1, w2, b2)


def _nsa_kernel(qt_ref, kc_ref, vct_ref, ks_ref, vst_ref, kw_ref, vwt_ref, gl_ref, o_ref,
                ps_scr, bias_scr, qc_scr, qw_scr, qs_scr, sc_scr, s0_scr, s1_scr, p0_scr, p1_scr):
    qb = pl.program_id(1)
    t0 = qb * Q_BLOCK
    t0a = pl.multiple_of(t0, Q_BLOCK)
    qt = qt_ref[0, 0]
    nc = kc_ref.shape[1]
    n_sel = bias_scr.shape[0]
    tq = t0 + lax.broadcasted_iota(jnp.int32, (1, QL), 1) % Q_BLOCK
    q_rel = tq - t0

    def bias_rows(cond, rows):
        return jnp.where(cond, NEG, 0.0).astype(BF16) + jnp.zeros((rows, QL), BF16)

    def flash_update(s, carry, vt):
        m_prev, acc = carry
        m_new = jnp.maximum(m_prev, jnp.max(s, axis=0, keepdims=True))
        p = jnp.exp2((s - m_new).astype(BF16))
        return m_new, jnp.exp2(m_prev - m_new) * acc + _dot(vt, p)

    init = (jnp.full((1, QL), NEG, F32), jnp.zeros((V_AUG_ROWS, QL), F32))

    n_grp = nc // 8
    grp = lax.broadcasted_iota(jnp.int32, (qc_scr.shape[0] - HEAD_DIM, 1), 0)
    qc_scr[0:HEAD_DIM, :] = qt
    qc_scr[HEAD_DIM:, :] = bias_rows((grp > qb) & (grp < n_grp), qc_scr.shape[0] - HEAD_DIM)
    sc_scr[...] = _dot(kc_ref[0], qc_scr[...])
    r0 = pl.multiple_of(jnp.maximum(8 * qb - 8, 0), 8)
    band_end = CMP_STRIDE * (r0 + lax.broadcasted_iota(jnp.int32, (16, 1), 0)) + (CMP_BLOCK - 1)
    sc_scr[pl.ds(r0, 16), :] = jnp.where(band_end <= tq, sc_scr[pl.ds(r0, 16), :], NEG)
    s_c = sc_scr[...]
    m_c = jnp.maximum(jnp.max(s_c, axis=0, keepdims=True), 0.1 * NEG)
    e_c = jnp.exp2(s_c - m_c)
    o_c = _dot(vct_ref[0], e_c.astype(BF16))
    inv_c = 1.0 / jnp.maximum(o_c[HEAD_DIM:HEAD_DIM + 1], 1e-30)
    o_c = o_c[:HEAD_DIM] * inv_c

    p_grp = None
    for h in range(HEADS_PER_GROUP):
        lanes = slice(h * Q_BLOCK, (h + 1) * Q_BLOCK)
        p_h = e_c[:, lanes] * inv_c[:, lanes]
        p_grp = p_h if p_grp is None else p_grp + p_h
    ps_scr[0:8, :] = jnp.zeros((8, Q_BLOCK), F32)
    ps_scr[8:8 + nc, :] = p_grp
    imp = ps_scr[pl.ds(7, n_sel, stride=4), :]
    for k in range(8, 12):
        imp = imp + ps_scr[pl.ds(k, n_sel, stride=4), :]

    span = WINDOW + Q_BLOCK
    rho = lax.broadcasted_iota(jnp.int32, (16, 1), 0)
    qw_scr[0:HEAD_DIM, :] = qt
    qw_scr[HEAD_DIM:HEAD_DIM + 16, :] = bias_rows((rho < 8) & (((rho - qb) & 7) < WINDOW // Q_BLOCK - qb), 16)
    qw_scr[HEAD_DIM + 16:, :] = jnp.zeros((qw_scr.shape[0] - HEAD_DIM - 16, QL), BF16)
    s_w = _dot(kw_ref[0, pl.ds(t0a, span), :], qw_scr[...])
    i_rel = lax.broadcasted_iota(jnp.int32, (Q_BLOCK, 1), 0)
    s_w = jnp.concatenate([jnp.where(i_rel > q_rel, s_w[:Q_BLOCK], NEG), s_w[Q_BLOCK:WINDOW],
                           jnp.where(i_rel <= q_rel, s_w[WINDOW:], NEG)], axis=0)
    _, acc_w = flash_update(s_w, init, vwt_ref[0, :, pl.ds(t0a, span)])

    j_idx = lax.broadcasted_iota(jnp.int32, (n_sel, Q_BLOCK), 0)
    cur = (t0 + lax.broadcasted_iota(jnp.int32, (n_sel, Q_BLOCK), 1)) // SEL_BLOCK
    forced = (j_idx == 0) | (j_idx == cur) | (j_idx == cur - 1)
    valid = j_idx <= cur
    score = jnp.where(forced, -2.0, jnp.where(valid, imp, -1.0))
    sel = forced.astype(F32)
    for _ in range(SEL_TOPN - 3):
        best = jnp.max(score, axis=0, keepdims=True)
        first = jnp.min(jnp.where(score == best, j_idx, n_sel), axis=0, keepdims=True)
        pick = j_idx == first
        score = jnp.where(pick, -2.0, score)
        sel = jnp.where(pick, 1.0, sel)
    bias_scr[...] = jnp.where(valid & (sel > 0.5) & (j_idx * SEL_BLOCK < t0), 0.0, NEG)

    s_d = _dot(ks_ref[0, pl.ds(t0a, Q_BLOCK), 0:HEAD_DIM], qt)
    m_d, acc_d = flash_update(jnp.where(i_rel <= q_rel, s_d, NEG), init, vst_ref[0, :, pl.ds(t0a, Q_BLOCK)])

    blocks_per_tile = SEL_TILE // SEL_BLOCK
    last_tile = ks_ref.shape[1] // SEL_TILE - 1
    qs_scr[0:HEAD_DIM, :] = qt
    qs_scr[HEAD_DIM + 16:, :] = jnp.zeros((qs_scr.shape[0] - HEAD_DIM - 16, QL), BF16)

    def tile_start(kt):
        return pl.multiple_of(jnp.clip(kt, 0, last_tile) * SEL_TILE, SEL_TILE)

    def sel_scores(kt, dst):
        b0 = pl.multiple_of(jnp.minimum(kt, last_tile) * blocks_per_tile, blocks_per_tile)
        b16 = jnp.concatenate([bias_scr[pl.ds(b0, blocks_per_tile), :],
                               jnp.zeros((16 - blocks_per_tile, Q_BLOCK), F32)], axis=0)
        qs_scr[HEAD_DIM:HEAD_DIM + 16, :] = jnp.concatenate([b16] * HEADS_PER_GROUP, axis=1).astype(BF16)
        dst[...] = _dot(ks_ref[0, pl.ds(tile_start(kt), SEL_TILE), :], qs_scr[...])

    def half_step(kt, s_cur, s_nxt, p_cur, p_prev, carry):
        m_prev, alpha_prev, acc = carry
        acc = alpha_prev * acc + _dot(vst_ref[0, :, pl.ds(tile_start(kt - 1), SEL_TILE)], p_prev[...])
        m8 = jnp.max(s_cur[...].reshape(SEL_TILE // 8, 8, QL), axis=0)
        m_new = jnp.maximum(m_prev, jnp.max(m8, axis=0, keepdims=True))
        for c in range(SEL_TILE // P_CHUNK):
            rows = slice(c * P_CHUNK, (c + 1) * P_CHUNK)
            p_cur[rows, :] = jnp.exp2((s_cur[rows, :] - m_new).astype(BF16))
        sel_scores(kt + 1, s_nxt)
        return m_new, jnp.exp2(m_prev - m_new), acc

    def pair_step(i, carry):
        carry = half_step(2 * i, s0_scr, s1_scr, p0_scr, p1_scr, carry)
        return half_step(2 * i + 1, s1_scr, s0_scr, p1_scr, p0_scr, carry)

    n_pairs = ((t0 + SEL_TILE - 1) // SEL_TILE + 1) // 2
    sel_scores(0, s0_scr)
    p1_scr[...] = jnp.zeros(p1_scr.shape, BF16)
    carry = (m_d, jnp.ones((1, QL), F32), acc_d)
    _, alpha_last, acc_s = lax.fori_loop(0, n_pairs, pair_step, carry)
    acc_s = alpha_last * acc_s + _dot(vst_ref[0, :, pl.ds(tile_start(2 * n_pairs - 1), SEL_TILE)], p1_scr[...])

    gate = jax.nn.sigmoid(gl_ref[0, 0])
    o = (gate[0:1] * o_c + gate[1:2] * (acc_s[:HEAD_DIM] / acc_s[HEAD_DIM:HEAD_DIM + 1])
         + gate[2:3] * (acc_w[:HEAD_DIM] / acc_w[HEAD_DIM:HEAD_DIM + 1]))
    stacked = jnp.concatenate([o[:, p * Q_BLOCK:(p + 1) * Q_BLOCK] for p in range(HEADS_PER_GROUP)], axis=0)
    o_ref[...] = stacked.T.astype(o_ref.dtype)


def _sparse_attention(qt, kc_aug, vc_aug_t, ks_aug, vs_aug_t, kw_aug, vw_aug_t, gate_logits, s):
    n_group, n_qb = qt.shape[:2]
    nc = kc_aug.shape[1]
    n_sel = s // SEL_BLOCK
    assert HEAD_DIM + nc // 8 <= CMP_AUG_W
    per_group3 = lambda g, i: (g, 0, 0)
    per_step = lambda g, i: (g, i, 0, 0)
    tile_f32 = pltpu.VMEM((SEL_TILE, QL), F32)
    tile_bf16 = pltpu.VMEM((SEL_TILE, QL), BF16)
    return pl.pallas_call(
        _nsa_kernel,
        grid=(n_group, n_qb),
        in_specs=[
            pl.BlockSpec((1, 1, HEAD_DIM, QL), per_step),
            pl.BlockSpec((1, nc, CMP_AUG_W), per_group3),
            pl.BlockSpec((1, V_AUG_ROWS, nc), per_group3),
            pl.BlockSpec((1, s, KS_AUG_W), per_group3),
            pl.BlockSpec((1, V_AUG_ROWS, s), per_group3),
            pl.BlockSpec((1, s + WINDOW, KS_AUG_W), per_group3),
            pl.BlockSpec((1, V_AUG_ROWS, s + WINDOW), per_group3),
            pl.BlockSpec((1, 1, 3, QL), per_step),
        ],
        out_specs=pl.BlockSpec((Q_BLOCK, HEADS_PER_GROUP * HEAD_DIM), lambda g, i: (i, g)),
        out_shape=jax.ShapeDtypeStruct((s, Q_W), BF16),
        scratch_shapes=[pltpu.VMEM((8 + nc, Q_BLOCK), F32), pltpu.VMEM((n_sel, Q_BLOCK), F32),
                        pltpu.VMEM((CMP_AUG_W, QL), BF16), pltpu.VMEM((KS_AUG_W, QL), BF16),
                        pltpu.VMEM((KS_AUG_W, QL), BF16), pltpu.VMEM((nc, QL), F32),
                        tile_f32, tile_f32, tile_bf16, tile_bf16],
        compiler_params=_cparams("parallel", "arbitrary"),
        name="sparse_attn",
    )(qt, kc_aug, vc_aug_t, ks_aug, vs_aug_t, kw_aug, vw_aug_t, gate_logits)


def _merge_kernel(x_ref, ya_ref, yb_ref, yc_ref, g0_ref, g1_ref, g2_ref, wp_ref, wg_ref, wn_ref,
                  wo_ref, ng_ref, gate_ref, o_ref):
    merged = jax.nn.sigmoid(g0_ref[...].astype(F32)) * _dot(ya_ref[...], wp_ref[...])
    merged += jax.nn.sigmoid(g1_ref[...].astype(F32)) * _dot(yb_ref[...], wg_ref[...])
    merged += jax.nn.sigmoid(g2_ref[...].astype(F32)) * _dot(yc_ref[...], wn_ref[...])
    y = _dot(merged.astype(BF16), wo_ref[...])
    o_ref[...] = x_ref[...] + gate_ref[...] * _rms(y, ng_ref[...])


def _merge(x, ya, yb, yc, proj, wp, wg, wn, wo, norm_g, gate):
    s, d = x.shape
    tm = min(256, s)
    const = lambda i: (0, 0)
    rows = lambda i: (i, 0)
    whole = lambda a: pl.BlockSpec(a.shape, const, pipeline_mode=pl.Buffered(1))
    bg = OFF_BG // d
    return pl.pallas_call(
        _merge_kernel,
        grid=(s // tm,),
        in_specs=[
            pl.BlockSpec((tm, d), rows),
            pl.BlockSpec((tm, MIX_W), rows),
            pl.BlockSpec((tm, MIX_W), rows),
            pl.BlockSpec((tm, Q_W), rows),
            pl.BlockSpec((tm, d), lambda i: (i, bg)),
            pl.BlockSpec((tm, d), lambda i: (i, bg + 1)),
            pl.BlockSpec((tm, d), lambda i: (i, bg + 2)),
            whole(wp), whole(wg), whole(wn), whole(wo),
            pl.BlockSpec((1, d), const),
            pl.BlockSpec((1, d), const),
        ],
        out_specs=pl.BlockSpec((tm, d), rows),
        out_shape=jax.ShapeDtypeStruct((s, d), F32),
        compiler_params=_cparams("parallel"),
        name="merge_out",
    )(x, ya, yb, yc, proj, proj, proj, wp, wg, wn, wo, norm_g, gate)


def _ffn_kernel(x_ref, gi_ref, sc_ref, sh_ref, w1_ref, w2_ref, go_ref, gate_ref, o_ref, h_scr):
    j = pl.program_id(1)

    @pl.when(j == 0)
    def _():
        h = _rms(x_ref[...], gi_ref[...]) * (1.0 + sc_ref[...]) + sh_ref[...]
        h_scr[...] = h.astype(BF16)
        o_ref[...] = jnp.zeros_like(o_ref)

    a = jnp.square(jnp.maximum(_dot(h_scr[...], w1_ref[...]), 0.0))
    o_ref[...] += _dot(a.astype(BF16), w2_ref[...])

    @pl.when(j == pl.num_programs(1) - 1)
    def _():
        o_ref[...] = x_ref[...] + gate_ref[...] * _rms(o_ref[...], go_ref[...])


def _ffn(x, g_in, scale, shift, w1, w2, g_out, gate):
    s, d = x.shape
    tm, tf = min(1024, s), 512
    const = lambda i, j: (0, 0)
    rows = lambda i, j: (i, 0)
    return pl.pallas_call(
        _ffn_kernel,
        grid=(s // tm, D_FF // tf),
        in_specs=[
            pl.BlockSpec((tm, d), rows, pipeline_mode=pl.Buffered(1)),
            pl.BlockSpec((1, d), const),
            pl.BlockSpec((1, d), const),
            pl.BlockSpec((1, d), const),
            pl.BlockSpec((d, tf), lambda i, j: (0, j)),
            pl.BlockSpec((tf, d), lambda i, j: (j, 0)),
            pl.BlockSpec((1, d), const),
            pl.BlockSpec((1, d), const),
        ],
        out_specs=pl.BlockSpec((tm, d), rows),
        out_shape=jax.ShapeDtypeStruct((s, d), F32),
        scratch_shapes=[pltpu.VMEM((tm, d), BF16)],
        compiler_params=_cparams("parallel", "arbitrary"),
        name="ffn",
    )(x, g_in, scale, shift, w1, w2, g_out, gate)


def _token_mixing(x, mod, norm_g, w_in, pool_w, pool_scale, ln_g, ln_b, ws, bs, cmp_pos, cmp_w1,
                  cmp_b1, cmp_w2, cmp_b2, w_br_pool, w_br_gmlp, w_br_nsa, w_out):
    s, d = x.shape
    n_qb = s // Q_BLOCK
    ng0 = OFF_BG
    w_main = jnp.concatenate([w_in[:, :ng0], w_in[:, ng0 + N_GATE:]], axis=1).astype(BF16)
    w_ng = jnp.pad(w_in[:, ng0:ng0 + N_GATE], ((0, 0), (0, 128 - N_GATE))).astype(BF16)
    row = lambda v: v.reshape(1, -1)

    col_scale = jnp.ones((1, PROJ_W), F32).at[:, OFF_Q:OFF_KV].set(HEAD_DIM ** -0.5 * LOG2E)
    proj, ngate = _in_projection(x, row(norm_g[0]), row(mod[1]), row(mod[0]), w_main, w_ng, col_scale)

    ya, yb = _mixers(proj, pool_w.astype(BF16), row(pool_scale), row(ln_g), row(ln_b), ws, bs.T)

    qt = proj[:, OFF_Q:OFF_KV].reshape(n_qb, Q_BLOCK, KV_GROUPS, HEADS_PER_GROUP, HEAD_DIM).transpose(2, 0, 4, 3, 1)
    qt = qt.reshape(KV_GROUPS, n_qb, HEAD_DIM, QL)
    kv = proj[:, OFF_KV:OFF_BG].reshape(s, 6, KV_GROUPS, HEAD_DIM).transpose(1, 2, 0, 3)
    xkv = kv[0:2].reshape(2, KV_GROUPS, s // CMP_STRIDE, CMP_STRIDE * HEAD_DIM)
    gl = ngate[:, :N_GATE].reshape(n_qb, Q_BLOCK, KV_GROUPS, HEADS_PER_GROUP, 3).transpose(2, 0, 4, 3, 1)
    gl = gl.reshape(KV_GROUPS, n_qb, 3, QL)

    cmp = _compress(xkv, cmp_pos.reshape(2, 1, CMP_BLOCK * HEAD_DIM), cmp_w1.astype(BF16),
                    cmp_b1.reshape(2, 1, -1), cmp_w2.astype(BF16), cmp_b2.reshape(2, 1, -1))
    def with_onehot(k, idx, width):
        onehot = (idx[:, None] == jnp.arange(width - HEAD_DIM)[None, :]).astype(BF16)
        return jnp.concatenate([k, jnp.broadcast_to(onehot, k.shape[:-1] + onehot.shape[-1:])], axis=-1)

    def with_ones_row(vt):
        extra = jnp.zeros(vt.shape[:1] + (V_AUG_ROWS - HEAD_DIM,) + vt.shape[2:], BF16).at[:, 0].set(1.0)
        return jnp.concatenate([vt, extra], axis=1)

    nc = s // CMP_STRIDE
    kc_aug = with_onehot(cmp[0].astype(BF16), jnp.arange(nc) // 8, CMP_AUG_W)
    vc_aug_t = with_ones_row(cmp[1].transpose(0, 2, 1).astype(BF16))
    ks_aug = with_onehot(kv[2], (jnp.arange(s) // SEL_BLOCK) % (SEL_TILE // SEL_BLOCK), KS_AUG_W)
    vs_aug_t = with_ones_row(kv[3].transpose(0, 2, 1))
    front = ((0, 0), (WINDOW, 0), (0, 0))
    kw_aug = with_onehot(jnp.pad(kv[4], front), (jnp.arange(s + WINDOW) // Q_BLOCK) % 8, KS_AUG_W)
    vw_aug_t = with_ones_row(jnp.pad(kv[5], front).transpose(0, 2, 1))
    yc = _sparse_attention(qt, kc_aug, vc_aug_t, ks_aug, vs_aug_t, kw_aug, vw_aug_t, gl, s)

    return _merge(x, ya, yb, yc, proj, w_br_pool.astype(BF16), w_br_gmlp.astype(BF16),
                  w_br_nsa.astype(BF16), w_out.astype(BF16), row(norm_g[1]), row(mod[2]))


def kernel(x, c, norm_g, w_ada, b_ada, w_in, pool_w, pool_scale, gmlp_ln_g, gmlp_ln_b, gmlp_ws, gmlp_bs,
           cmp_pos, cmp_w1, cmp_b1, cmp_w2, cmp_b2, w_br_pool, w_br_gmlp, w_br_nsa, w_out, w_ff1, w_ff2):
    b, s, d = x.shape
    assert b == 1 and d == D_MODEL and s % 1024 == 0
    n_layer = w_ada.shape[0]
    mod_all = _modulation(c, w_ada, b_ada).reshape(n_layer, 6, d)
    xs = x[0]
    row = lambda v: v.reshape(1, -1)
    for l in range(n_layer):
        mod = mod_all[l]
        xs = _token_mixing(xs, mod, norm_g[l], w_in[l], pool_w[l], pool_scale[l], gmlp_ln_g[l],
                           gmlp_ln_b[l], gmlp_ws[l], gmlp_bs[l], cmp_pos[l], cmp_w1[l], cmp_b1[l],
                           cmp_w2[l], cmp_b2[l], w_br_pool[l], w_br_gmlp[l], w_br_nsa[l], w_out[l])
        xs = _ffn(xs, row(norm_g[l, 2]), row(mod[4]), row(mod[3]), w_ff1[l].astype(BF16),
                  w_ff2[l].astype(BF16), row(norm_g[l, 3]), row(mod[5]))
    return xs[None]
```

```python
import functools

import jax
import jax.numpy as jnp
from jax import lax
from jax.experimental import pallas as pl
from jax.experimental.pallas import tpu as pltpu

F32 = jnp.float32
BF16 = jnp.bfloat16

D_MODEL = 2048
POOL_WINDOWS = (2, 4, 8, 16)
POOL_HALO = 16
GROUP_CH = 128
MIX_W = 4 * GROUP_CH
GMLP_CHUNK = 128
HEAD_DIM = 64
KV_GROUPS = 4
HEADS_PER_GROUP = 4
Q_W = 16 * HEAD_DIM
KV_W = KV_GROUPS * HEAD_DIM
CMP_BLOCK = 32
CMP_STRIDE = 16
SEL_BLOCK = 64
SEL_TOPN = 16
WINDOW = 512
FORCE_BONUS = 1000.0
LOG2E = 1.4426950408889634
NEG = -1e30
N_GATE = 3 * 16
D_FF = 4 * D_MODEL

Q_BLOCK = 256
QL = HEADS_PER_GROUP * Q_BLOCK
CMP_PER_QB = Q_BLOCK // CMP_STRIDE
WIN_CHUNK = 128
SEL_TILE = 512
KS_AUG_W = 128
CMP_AUG_W = 256
V_AUG_ROWS = HEAD_DIM + 16
P_CHUNK = 64
QK_PART = 256

OFF_POOL, OFF_U, OFF_V, OFF_Q, OFF_KV, OFF_BG = 0, 512, 1024, 1536, 2560, 4096
PROJ_W = OFF_BG + 3 * D_MODEL

VMEM_LIMIT = 56 * 1024 * 1024


def _cparams(*sem):
    return pltpu.CompilerParams(dimension_semantics=sem, vmem_limit_bytes=VMEM_LIMIT)


def _dot(a, b):
    return jnp.dot(a, b, preferred_element_type=F32)


def _rms(x, g):
    return x * lax.rsqrt(jnp.mean(x * x, axis=-1, keepdims=True) + 1e-6) * g


def _mod_kernel(c_ref, w_ref, b_ref, o_ref):
    c = c_ref[...]
    act = c * jax.nn.sigmoid(c)
    o_ref[0] = jnp.sum(act * w_ref[0], axis=0, keepdims=True) + b_ref[0]


def _modulation(c, w_ada, b_ada):
    n_layer, d, n_out = w_ada.shape
    tn = 1024
    return pl.pallas_call(
        _mod_kernel,
        grid=(n_layer, n_out // tn),
        in_specs=[
            pl.BlockSpec((d, 1), lambda l, j: (0, 0)),
            pl.BlockSpec((1, d, tn), lambda l, j: (l, 0, j)),
            pl.BlockSpec((1, 1, tn), lambda l, j: (l, 0, j)),
        ],
        out_specs=pl.BlockSpec((1, 1, tn), lambda l, j: (l, 0, j)),
        out_shape=jax.ShapeDtypeStruct((n_layer, 1, n_out), F32),
        compiler_params=_cparams("parallel", "parallel"),
        name="adaln_mod",
    )(c.reshape(d, 1), w_ada, b_ada.reshape(n_layer, 1, n_out))


def _inproj_kernel(x_ref, g_ref, sc_ref, sh_ref, w_ref, wng_ref, cs_ref, o_ref, ng_ref, h_scr):
    @pl.when(pl.program_id(1) == 0)
    def _():
        h = _rms(x_ref[...], g_ref[...]) * (1.0 + sc_ref[...]) + sh_ref[...]
        hb = h.astype(BF16)
        h_scr[...] = hb
        ng_ref[...] = _dot(hb, wng_ref[...])

    o_ref[...] = (_dot(h_scr[...], w_ref[...]) * cs_ref[...]).astype(o_ref.dtype)


def _in_projection(x, g, scale, shift, w_main, w_ng, col_scale):
    s, d = x.shape
    tm, tn = min(1024, s), 1024
    row = lambda i, j: (0, 0)
    return pl.pallas_call(
        _inproj_kernel,
        grid=(s // tm, PROJ_W // tn),
        in_specs=[
            pl.BlockSpec((tm, d), lambda i, j: (i, 0)),
            pl.BlockSpec((1, d), row),
            pl.BlockSpec((1, d), row),
            pl.BlockSpec((1, d), row),
            pl.BlockSpec((d, tn), lambda i, j: (0, j)),
            pl.BlockSpec((d, 128), row),
            pl.BlockSpec((1, tn), lambda i, j: (0, j)),
        ],
        out_specs=[
            pl.BlockSpec((tm, tn), lambda i, j: (i, j)),
            pl.BlockSpec((tm, 128), lambda i, j: (i, 0)),
        ],
        out_shape=[
            jax.ShapeDtypeStruct((s, PROJ_W), BF16),
            jax.ShapeDtypeStruct((s, 128), F32),
        ],
        scratch_shapes=[pltpu.VMEM((tm, d), BF16)],
        compiler_params=_cparams("parallel", "arbitrary"),
        name="in_proj",
    )(x, g, scale, shift, w_main, w_ng, col_scale)


def _mixer_kernel(a_ref, halo_ref, u_ref, v_ref, pw_ref, ps_ref, lg_ref, lb_ref, ws_ref, bs_ref,
                  ya_ref, yb_ref):
    i = pl.program_id(0)
    tm = a_ref.shape[0]
    a = a_ref[...].astype(F32)
    halo = jnp.where(i > 0, halo_ref[...].astype(F32), 0.0)
    ext = jnp.concatenate([halo, a], axis=0)
    p2 = ext[1:] + ext[:-1]
    p4 = p2[2:] + p2[:-2]
    p8 = p4[4:] + p4[:-4]
    p16 = p8[8:] + p8[:-8]
    sums = (p2[15:15 + tm], p4[13:13 + tm], p8[9:9 + tm], p16[1:1 + tm])
    t = i * tm + lax.broadcasted_iota(jnp.int32, (tm, 1), 0)
    for gi, w in enumerate(POOL_WINDOWS):
        cols = slice(gi * GROUP_CH, (gi + 1) * GROUP_CH)
        cnt = jnp.minimum(t + 1, w).astype(F32)
        pooled = sums[gi][:, cols] / cnt - a[:, cols]
        y = _dot(pooled.astype(BF16), pw_ref[gi])
        ya_ref[:, cols] = (y * ps_ref[:, cols]).astype(ya_ref.dtype)

    u = jax.nn.gelu(u_ref[...].astype(F32))
    v = jax.nn.gelu(v_ref[...].astype(F32))
    mu = jnp.mean(v, axis=-1, keepdims=True)
    var = jnp.mean(jnp.square(v - mu), axis=-1, keepdims=True)
    vn = ((v - mu) * lax.rsqrt(var + 1e-5) * lg_ref[...] + lb_ref[...]).astype(BF16)
    r = lax.broadcasted_iota(jnp.int32, (GMLP_CHUNK, GMLP_CHUNK), 0)
    c = lax.broadcasted_iota(jnp.int32, (GMLP_CHUNK, GMLP_CHUNK), 1)
    for gi in range(4):
        cols = slice(gi * GROUP_CH, (gi + 1) * GROUP_CH)
        wsm = jnp.where(r >= c, ws_ref[gi], 0.0).astype(BF16)
        bias = bs_ref[:, gi:gi + 1]
        for ck in range(tm // GMLP_CHUNK):
            rows = slice(ck * GMLP_CHUNK, (ck + 1) * GMLP_CHUNK)
            mixed = _dot(wsm, vn[rows, cols]) + bias
            yb_ref[rows, cols] = (u[rows, cols] * mixed).astype(yb_ref.dtype)


def _mixers(proj, pool_w, pool_scale, ln_g, ln_b, ws, bs_t):
    s = proj.shape[0]
    tm = min(512, s)
    hb = tm // POOL_HALO
    const2 = lambda i: (0, 0)
    const3 = lambda i: (0, 0, 0)
    return pl.pallas_call(
        _mixer_kernel,
        grid=(s // tm,),
        in_specs=[
            pl.BlockSpec((tm, MIX_W), lambda i: (i, OFF_POOL // MIX_W)),
            pl.BlockSpec((POOL_HALO, MIX_W), lambda i: (jnp.maximum(i * hb - 1, 0), OFF_POOL // MIX_W)),
            pl.BlockSpec((tm, MIX_W), lambda i: (i, OFF_U // MIX_W)),
            pl.BlockSpec((tm, MIX_W), lambda i: (i, OFF_V // MIX_W)),
            pl.BlockSpec((4, GROUP_CH, GROUP_CH), const3),
            pl.BlockSpec((1, MIX_W), const2),
            pl.BlockSpec((1, MIX_W), const2),
            pl.BlockSpec((1, MIX_W), const2),
            pl.BlockSpec((4, GMLP_CHUNK, GMLP_CHUNK), const3),
            pl.BlockSpec((GMLP_CHUNK, 4), const2),
        ],
        out_specs=[
            pl.BlockSpec((tm, MIX_W), lambda i: (i, 0)),
            pl.BlockSpec((tm, MIX_W), lambda i: (i, 0)),
        ],
        out_shape=[jax.ShapeDtypeStruct((s, MIX_W), BF16)] * 2,
        compiler_params=_cparams("parallel"),
        name="mixers",
    )(proj, proj, proj, proj, pool_w, pool_scale, ln_g, ln_b, ws, bs_t)


def _compress_kernel(x_ref, pos_ref, w1_ref, b1_ref, w2_ref, b2_ref, o_ref):
    half = CMP_STRIDE * HEAD_DIM
    x = x_ref[0, 0].astype(F32)
    nc = x.shape[0]
    pos = pos_ref[0]
    first = _dot((x + pos[:, :half]).astype(BF16), w1_ref[0, :half, :])
    second = _dot((x + pos[:, half:]).astype(BF16), w1_ref[0, half:, :])
    hid = jax.nn.gelu(first + pltpu.roll(second, nc - 1, 0) + b1_ref[0])
    o_ref[0, 0] = _dot(hid.astype(BF16), w2_ref[0]) + b2_ref[0]


def _compress(xkv, pos, w1, b1, w2, b2):
    _, n_group, nc, width = xkv.shape
    per_kv = lambda a, g: (a, 0, 0)
    return pl.pallas_call(
        _compress_kernel,
        grid=(2, n_group),
        in_specs=[
            pl.BlockSpec((1, 1, nc, width), lambda a, g: (a, g, 0, 0)),
            pl.BlockSpec((1, 1, 2 * width), per_kv),
            pl.BlockSpec((1, 2 * width, 128), per_kv),
            pl.BlockSpec((1, 1, 128), per_kv),
            pl.BlockSpec((1, 128, HEAD_DIM), per_kv),
            pl.BlockSpec((1, 1, HEAD_DIM), per_kv),
        ],
        out_specs=pl.BlockSpec((1, 1, nc, HEAD_DIM), lambda a, g: (a, g, 0, 0)),
        out_shape=jax.ShapeDtypeStruct((2, n_group, nc, HEAD_DIM), F32),
        compiler_params=_cparams("parallel", "parallel"),
        name="compress_kv",
    )(xkv, pos, w1, b1, w2, b2)


def _nsa_kernel(qt_ref, kc_ref, vct_ref, ks_ref, vst_ref, kw_ref, vwt_ref, gl_ref, o_ref,
                ps_scr, bias_scr, diag_bias_scr, qc_scr, qw_scr, qs_scr, sc_scr, s0_scr, s1_scr, p0_scr, p1_scr):
    qb = pl.program_id(1)
    t0 = qb * Q_BLOCK
    t0a = pl.multiple_of(t0, Q_BLOCK)
    qt = qt_ref[0, 0]
    nc = kc_ref.shape[1]
    n_sel = bias_scr.shape[0]
    q_rel = lax.broadcasted_iota(jnp.int32, (1, QL), 1) % Q_BLOCK
    tq = t0 + q_rel

    def bias_rows(cond, rows):
        return jnp.where(cond, NEG, 0.0).astype(BF16) + jnp.zeros((rows, QL), BF16)

    def flash_update(s, carry, vt):
        m_prev, acc = carry
        m_new = jnp.maximum(m_prev, jnp.max(s, axis=0, keepdims=True))
        p = jnp.exp2((s - m_new).astype(BF16))
        return m_new, jnp.exp2(m_prev - m_new) * acc + _dot(vt, p)

    init = (jnp.full((1, QL), NEG, F32), jnp.zeros((V_AUG_ROWS, QL), F32))

    grp = lax.broadcasted_iota(jnp.int32, (qc_scr.shape[0] - HEAD_DIM, 1), 0)
    visible = CMP_PER_QB * (qb + 1)
    qc_scr[0:HEAD_DIM, :] = qt
    qc_scr[HEAD_DIM:, :] = bias_rows((8 * grp >= visible) & (grp < nc // 8), qc_scr.shape[0] - HEAD_DIM)
    band = CMP_PER_QB + 8
    r0 = pl.multiple_of(jnp.maximum(visible - band, 0), 8)
    band_end = CMP_STRIDE * (r0 + lax.broadcasted_iota(jnp.int32, (band, 1), 0)) + (CMP_BLOCK - 1)

    def compressed(rows):
        def run():
            sc_scr[0:rows, :] = _dot(kc_ref[0, 0:rows, :], qc_scr[...])
            sc_scr[pl.ds(r0, band), :] = jnp.where(band_end <= tq, sc_scr[pl.ds(r0, band), :], NEG)
            s_c = sc_scr[0:rows, :]
            m_c = jnp.maximum(jnp.max(s_c, axis=0, keepdims=True), 0.1 * NEG)
            e_c = jnp.exp2(s_c - m_c)
            o_aug = _dot(vct_ref[0, :, 0:rows], e_c.astype(BF16))
            inv_c = 1.0 / jnp.maximum(o_aug[HEAD_DIM:HEAD_DIM + 1], 1e-30)
            p_grp = None
            for h in range(HEADS_PER_GROUP):
                lanes = slice(h * Q_BLOCK, (h + 1) * Q_BLOCK)
                p_h = e_c[:, lanes] * inv_c[:, lanes]
                p_grp = p_h if p_grp is None else p_grp + p_h
            for c in range(Q_BLOCK // 128):
                ps_scr[c, 8:8 + rows, :] = p_grp[:, c * 128:(c + 1) * 128]
                if rows < nc:
                    ps_scr[c, 8 + rows:, :] = jnp.zeros((nc - rows, 128), F32)
            return o_aug[:HEAD_DIM] * inv_c
        return run

    ps_scr[:, 0:8, :] = jnp.zeros((Q_BLOCK // 128, 8, 128), F32)
    quarter = nc // 4
    o_c = lax.cond(visible <= 2 * quarter,
                   lambda: lax.cond(visible <= quarter, compressed(quarter), compressed(2 * quarter)),
                   lambda: lax.cond(visible <= 3 * quarter, compressed(3 * quarter), compressed(nc)))
    imp = jnp.concatenate([sum(ps_scr[c, pl.ds(k, n_sel, stride=4), :] for k in range(7, 12))
                           for c in range(Q_BLOCK // 128)], axis=1)

    n_tri = Q_BLOCK // WIN_CHUNK
    n_chunk = WINDOW // WIN_CHUNK + n_tri
    span = n_chunk * WIN_CHUNK
    rho = lax.broadcasted_iota(jnp.int32, (16, 1), 0)
    first_real = WINDOW // WIN_CHUNK - n_tri * qb
    qw_scr[0:HEAD_DIM, :] = qt
    qw_scr[HEAD_DIM:HEAD_DIM + 16, :] = bias_rows((rho < 8) & (((rho - n_tri * qb) & 7) < first_real), 16)
    qw_scr[HEAD_DIM + 16:, :] = jnp.zeros((qw_scr.shape[0] - HEAD_DIM - 16, QL), BF16)
    s_w = _dot(kw_ref[0, pl.ds(t0a, span), :], qw_scr[...])
    i_rel = lax.broadcasted_iota(jnp.int32, (Q_BLOCK, 1), 0)
    s_w = jnp.concatenate([jnp.where(i_rel > q_rel, s_w[:Q_BLOCK], NEG), s_w[Q_BLOCK:WINDOW],
                           jnp.where(i_rel <= q_rel, s_w[WINDOW:], NEG)], axis=0)
    _, acc_w = flash_update(s_w, init, vwt_ref[0, :, pl.ds(t0a, span)])

    j_idx = lax.broadcasted_iota(jnp.int32, (n_sel, Q_BLOCK), 0)
    cur = (t0 + lax.broadcasted_iota(jnp.int32, (n_sel, Q_BLOCK), 1)) // SEL_BLOCK
    forced = (j_idx == 0) | (j_idx == cur) | (j_idx == cur - 1)
    valid = j_idx <= cur
    score = jnp.where(forced, -2.0, jnp.where(valid, imp, -1.0))
    n_pick = SEL_TOPN - 3

    fast = score
    for _ in range(n_pick):
        fast = jnp.where(fast == jnp.max(fast, axis=0, keepdims=True), -2.0, fast)
    retired = (fast < -1.5) & valid & jnp.logical_not(forced)
    most_retired = jnp.max(jnp.sum(retired.astype(F32), axis=0, keepdims=True))

    def ranked_with_ties():
        def pick_one(_, sc):
            best = jnp.max(sc, axis=0, keepdims=True)
            first = jnp.min(jnp.where(sc == best, j_idx, n_sel), axis=0, keepdims=True)
            return jnp.where(j_idx == first, -2.0, sc)
        return lax.fori_loop(0, n_pick, pick_one, score)

    ranked = lax.cond(most_retired > n_pick, ranked_with_ties, lambda: fast)
    chosen = valid & (ranked < -1.5)
    diag_bias_scr[...] = jnp.where(chosen, 0.0, NEG)
    bias_scr[...] = jnp.where(chosen & (j_idx * SEL_BLOCK < t0), 0.0, NEG)

    blocks_per_tile = SEL_TILE // SEL_BLOCK
    last_tile = ks_ref.shape[1] // SEL_TILE - 1
    qs_scr[0:HEAD_DIM, :] = qt
    qs_scr[HEAD_DIM + 16:, :] = jnp.zeros((qs_scr.shape[0] - HEAD_DIM - 16, QL), BF16)

    def tile_start(kt):
        return pl.multiple_of(jnp.clip(kt, 0, last_tile) * SEL_TILE, SEL_TILE)

    def set_bias_rows(table, kt):
        b0 = pl.multiple_of(jnp.minimum(kt, last_tile) * blocks_per_tile, blocks_per_tile)
        b16 = jnp.concatenate([table[pl.ds(b0, blocks_per_tile), :],
                               jnp.zeros((16 - blocks_per_tile, Q_BLOCK), F32)], axis=0)
        qs_scr[HEAD_DIM:HEAD_DIM + 16, :] = jnp.concatenate([b16] * HEADS_PER_GROUP, axis=1).astype(BF16)

    set_bias_rows(diag_bias_scr, t0 // SEL_TILE)
    s_d = _dot(ks_ref[0, pl.ds(t0a, Q_BLOCK), :], qs_scr[...])
    m_d, acc_d = flash_update(jnp.where(i_rel <= q_rel, s_d, NEG), init, vst_ref[0, :, pl.ds(t0a, Q_BLOCK)])

    def sel_scores(kt, dst):
        set_bias_rows(bias_scr, kt)
        dst[...] = _dot(ks_ref[0, pl.ds(tile_start(kt), SEL_TILE), :], qs_scr[...])

    def half_step(kt, s_cur, s_nxt, p_cur, p_prev, carry):
        m_prev, alpha_prev, acc = carry
        acc = alpha_prev * acc + _dot(vst_ref[0, :, pl.ds(tile_start(kt - 1), SEL_TILE)], p_prev[...])
        m8 = jnp.max(s_cur[...].reshape(SEL_TILE // 8, 8, QL), axis=0)
        m_new = jnp.maximum(m_prev, jnp.max(m8, axis=0, keepdims=True))
        set_bias_rows(bias_scr, kt + 1)
        k_nxt = tile_start(kt + 1)
        for h in range(SEL_TILE // QK_PART):
            for c in range(QK_PART // P_CHUNK):
                rows = slice(h * QK_PART + c * P_CHUNK, h * QK_PART + (c + 1) * P_CHUNK)
                p_cur[rows, :] = jnp.exp2((s_cur[rows, :] - m_new).astype(BF16))
            part = slice(h * QK_PART, (h + 1) * QK_PART)
            s_nxt[part, :] = _dot(ks_ref[0, pl.ds(k_nxt + h * QK_PART, QK_PART), :], qs_scr[...])
        return m_new, jnp.exp2(m_prev - m_new), acc

    def pair_step(i, carry):
        carry = half_step(2 * i, s0_scr, s1_scr, p0_scr, p1_scr, carry)
        return half_step(2 * i + 1, s1_scr, s0_scr, p1_scr, p0_scr, carry)

    n_pairs = ((t0 + SEL_TILE - 1) // SEL_TILE + 1) // 2
    sel_scores(0, s0_scr)
    p1_scr[...] = jnp.zeros(p1_scr.shape, BF16)
    carry = (m_d, jnp.ones((1, QL), F32), acc_d)
    _, alpha_last, acc_s = lax.fori_loop(0, n_pairs, pair_step, carry)
    acc_s = alpha_last * acc_s + _dot(vst_ref[0, :, pl.ds(tile_start(2 * n_pairs - 1), SEL_TILE)], p1_scr[...])

    gate = jax.nn.sigmoid(gl_ref[0, 0])
    o = (gate[0:1] * o_c + gate[1:2] * (acc_s[:HEAD_DIM] / acc_s[HEAD_DIM:HEAD_DIM + 1])
         + gate[2:3] * (acc_w[:HEAD_DIM] / acc_w[HEAD_DIM:HEAD_DIM + 1]))
    stacked = jnp.concatenate([o[:, p * Q_BLOCK:(p + 1) * Q_BLOCK] for p in range(HEADS_PER_GROUP)], axis=0)
    o_ref[...] = stacked.T.astype(o_ref.dtype)


def _sparse_attention(qt, kc_aug, vc_aug_t, ks_aug, vs_aug_t, kw_aug, vw_aug_t, gate_logits, s):
    n_group, n_qb = qt.shape[:2]
    nc = kc_aug.shape[1]
    n_sel = s // SEL_BLOCK
    assert HEAD_DIM + nc // 8 <= CMP_AUG_W and nc // 4 >= CMP_PER_QB + 8 and WINDOW // WIN_CHUNK + Q_BLOCK // WIN_CHUNK <= 8
    per_step = lambda g, i: (g, i, 0, 0)
    per_group = lambda shape: pl.BlockSpec((1,) + shape, lambda g, i: (g, 0, 0), pipeline_mode=pl.Buffered(1))
    tile_f32 = pltpu.VMEM((SEL_TILE, QL), F32)
    tile_bf16 = pltpu.VMEM((SEL_TILE, QL), BF16)
    sel_table = pltpu.VMEM((n_sel, Q_BLOCK), F32)
    return pl.pallas_call(
        _nsa_kernel,
        grid=(n_group, n_qb),
        in_specs=[
            pl.BlockSpec((1, 1, HEAD_DIM, QL), per_step),
            per_group((nc, CMP_AUG_W)),
            per_group((V_AUG_ROWS, nc)),
            per_group((s, KS_AUG_W)),
            per_group((V_AUG_ROWS, s)),
            per_group((s + WINDOW, KS_AUG_W)),
            per_group((V_AUG_ROWS, s + WINDOW)),
            pl.BlockSpec((1, 1, 3, QL), per_step),
        ],
        out_specs=pl.BlockSpec((Q_BLOCK, HEADS_PER_GROUP * HEAD_DIM), lambda g, i: (i, g)),
        out_shape=jax.ShapeDtypeStruct((s, Q_W), BF16),
        scratch_shapes=[pltpu.VMEM((Q_BLOCK // 128, 8 + nc, 128), F32), sel_table, sel_table,
                        pltpu.VMEM((CMP_AUG_W, QL), BF16), pltpu.VMEM((KS_AUG_W, QL), BF16),
                        pltpu.VMEM((KS_AUG_W, QL), BF16), pltpu.VMEM((nc, QL), F32),
                        tile_f32, tile_f32, tile_bf16, tile_bf16],
        compiler_params=_cparams("parallel", "arbitrary"),
        name="sparse_attn",
    )(qt, kc_aug, vc_aug_t, ks_aug, vs_aug_t, kw_aug, vw_aug_t, gate_logits)


def _merge_kernel(x_ref, ya_ref, yb_ref, yc_ref, g0_ref, g1_ref, g2_ref, wp_ref, wg_ref, wn_ref,
                  wo_ref, ng_ref, gate_ref, o_ref):
    merged = jax.nn.sigmoid(g0_ref[...].astype(F32)) * _dot(ya_ref[...], wp_ref[...])
    merged += jax.nn.sigmoid(g1_ref[...].astype(F32)) * _dot(yb_ref[...], wg_ref[...])
    merged += jax.nn.sigmoid(g2_ref[...].astype(F32)) * _dot(yc_ref[...], wn_ref[...])
    y = _dot(merged.astype(BF16), wo_ref[...])
    o_ref[...] = x_ref[...] + gate_ref[...] * _rms(y, ng_ref[...])


def _merge(x, ya, yb, yc, proj, wp, wg, wn, wo, norm_g, gate):
    s, d = x.shape
    tm = min(256, s)
    const = lambda i: (0, 0)
    rows = lambda i: (i, 0)
    whole = lambda a: pl.BlockSpec(a.shape, const, pipeline_mode=pl.Buffered(1))
    bg = OFF_BG // d
    return pl.pallas_call(
        _merge_kernel,
        grid=(s // tm,),
        in_specs=[
            pl.BlockSpec((tm, d), rows),
            pl.BlockSpec((tm, MIX_W), rows),
            pl.BlockSpec((tm, MIX_W), rows),
            pl.BlockSpec((tm, Q_W), rows),
            pl.BlockSpec((tm, d), lambda i: (i, bg)),
            pl.BlockSpec((tm, d), lambda i: (i, bg + 1)),
            pl.BlockSpec((tm, d), lambda i: (i, bg + 2)),
            whole(wp), whole(wg), whole(wn), whole(wo),
            pl.BlockSpec((1, d), const),
            pl.BlockSpec((1, d), const),
        ],
        out_specs=pl.BlockSpec((tm, d), rows),
        out_shape=jax.ShapeDtypeStruct((s, d), F32),
        compiler_params=_cparams("parallel"),
        name="merge_out",
    )(x, ya, yb, yc, proj, proj, proj, wp, wg, wn, wo, norm_g, gate)


def _ffn_kernel(x_ref, gi_ref, sc_ref, sh_ref, w1_ref, w2_ref, go_ref, gate_ref, o_ref, h_scr):
    j = pl.program_id(1)

    @pl.when(j == 0)
    def _():
        h = _rms(x_ref[...], gi_ref[...]) * (1.0 + sc_ref[...]) + sh_ref[...]
        h_scr[...] = h.astype(BF16)
        o_ref[...] = jnp.zeros_like(o_ref)

    a = jnp.square(jnp.maximum(_dot(h_scr[...], w1_ref[...]), 0.0))
    o_ref[...] += _dot(a.astype(BF16), w2_ref[...])

    @pl.when(j == pl.num_programs(1) - 1)
    def _():
        o_ref[...] = x_ref[...] + gate_ref[...] * _rms(o_ref[...], go_ref[...])


def _ffn(x, g_in, scale, shift, w1, w2, g_out, gate):
    s, d = x.shape
    tm, tf = min(1024, s), 512
    const = lambda i, j: (0, 0)
    rows = lambda i, j: (i, 0)
    return pl.pallas_call(
        _ffn_kernel,
        grid=(s // tm, D_FF // tf),
        in_specs=[
            pl.BlockSpec((tm, d), rows, pipeline_mode=pl.Buffered(1)),
            pl.BlockSpec((1, d), const),
            pl.BlockSpec((1, d), const),
            pl.BlockSpec((1, d), const),
            pl.BlockSpec((d, tf), lambda i, j: (0, j)),
            pl.BlockSpec((tf, d), lambda i, j: (j, 0)),
            pl.BlockSpec((1, d), const),
            pl.BlockSpec((1, d), const),
        ],
        out_specs=pl.BlockSpec((tm, d), rows),
        out_shape=jax.ShapeDtypeStruct((s, d), F32),
        scratch_shapes=[pltpu.VMEM((tm, d), BF16)],
        compiler_params=_cparams("parallel", "arbitrary"),
        name="ffn",
    )(x, g_in, scale, shift, w1, w2, g_out, gate)


def _token_mixing(x, mod, norm_g, w_in, pool_w, pool_scale, ln_g, ln_b, ws, bs, cmp_pos, cmp_w1,
                  cmp_b1, cmp_w2, cmp_b2, w_br_pool, w_br_gmlp, w_br_nsa, w_out):
    s, d = x.shape
    n_qb = s // Q_BLOCK
    ng0 = OFF_BG
    w_main = jnp.concatenate([w_in[:, :ng0], w_in[:, ng0 + N_GATE:]], axis=1).astype(BF16)
    w_ng = jnp.pad(w_in[:, ng0:ng0 + N_GATE], ((0, 0), (0, 128 - N_GATE))).astype(BF16)
    row = lambda v: v.reshape(1, -1)

    col_scale = jnp.ones((1, PROJ_W), F32).at[:, OFF_Q:OFF_KV].set(HEAD_DIM ** -0.5 * LOG2E)
    proj, ngate = _in_projection(x, row(norm_g[0]), row(mod[1]), row(mod[0]), w_main, w_ng, col_scale)

    ya, yb = _mixers(proj, pool_w.astype(BF16), row(pool_scale), row(ln_g), row(ln_b), ws, bs.T)

    qt = proj[:, OFF_Q:OFF_KV].reshape(n_qb, Q_BLOCK, KV_GROUPS, HEADS_PER_GROUP, HEAD_DIM).transpose(2, 0, 4, 3, 1)
    qt = qt.reshape(KV_GROUPS, n_qb, HEAD_DIM, QL)
    kv = proj[:, OFF_KV:OFF_BG].reshape(s, 6, KV_GROUPS, HEAD_DIM).transpose(1, 2, 0, 3)
    xkv = kv[0:2].reshape(2, KV_GROUPS, s // CMP_STRIDE, CMP_STRIDE * HEAD_DIM)
    gl = ngate[:, :N_GATE].reshape(n_qb, Q_BLOCK, KV_GROUPS, HEADS_PER_GROUP, 3).transpose(2, 0, 4, 3, 1)
    gl = gl.reshape(KV_GROUPS, n_qb, 3, QL)

    cmp = _compress(xkv, cmp_pos.reshape(2, 1, CMP_BLOCK * HEAD_DIM), cmp_w1.astype(BF16),
                    cmp_b1.reshape(2, 1, -1), cmp_w2.astype(BF16), cmp_b2.reshape(2, 1, -1))
    def with_onehot(k, idx, width):
        onehot = (idx[:, None] == jnp.arange(width - HEAD_DIM)[None, :]).astype(BF16)
        return jnp.concatenate([k, jnp.broadcast_to(onehot, k.shape[:-1] + onehot.shape[-1:])], axis=-1)

    def with_ones_row(vt):
        extra = jnp.zeros(vt.shape[:1] + (V_AUG_ROWS - HEAD_DIM,) + vt.shape[2:], BF16).at[:, 0].set(1.0)
        return jnp.concatenate([vt, extra], axis=1)

    nc = s // CMP_STRIDE
    kc_aug = with_onehot(cmp[0].astype(BF16), jnp.arange(nc) // 8, CMP_AUG_W)
    vc_aug_t = with_ones_row(cmp[1].transpose(0, 2, 1).astype(BF16))
    ks_aug = with_onehot(kv[2], (jnp.arange(s) // SEL_BLOCK) % (SEL_TILE // SEL_BLOCK), KS_AUG_W)
    vs_aug_t = with_ones_row(kv[3].transpose(0, 2, 1))
    front = ((0, 0), (WINDOW, 0), (0, 0))
    kw_aug = with_onehot(jnp.pad(kv[4], front), (jnp.arange(s + WINDOW) // WIN_CHUNK) % 8, KS_AUG_W)
    vw_aug_t = with_ones_row(jnp.pad(kv[5], front).transpose(0, 2, 1))
    yc = _sparse_attention(qt, kc_aug, vc_aug_t, ks_aug, vs_aug_t, kw_aug, vw_aug_t, gl, s)

    return _merge(x, ya, yb, yc, proj, w_br_pool.astype(BF16), w_br_gmlp.astype(BF16),
                  w_br_nsa.astype(BF16), w_out.astype(BF16), row(norm_g[1]), row(mod[2]))


def kernel(x, c, norm_g, w_ada, b_ada, w_in, pool_w, pool_scale, gmlp_ln_g, gmlp_ln_b, gmlp_ws, gmlp_bs,
           cmp_pos, cmp_w1, cmp_b1, cmp_w2, cmp_b2, w_br_pool, w_br_gmlp, w_br_nsa, w_out, w_ff1, w_ff2):
    b, s, d = x.shape
    assert b == 1 and d == D_MODEL and s % 1024 == 0
    n_layer = w_ada.shape[0]
    mod_all = _modulation(c, w_ada, b_ada).reshape(n_layer, 6, d)
    xs = x[0]
    row = lambda v: v.reshape(1, -1)
    for l in range(n_layer):
        mod = mod_all[l]
        xs = _token_mixing(xs, mod, norm_g[l], w_in[l], pool_w[l], pool_scale[l], gmlp_ln_g[l],
                           gmlp_ln_b[l], gmlp_ws[l], gmlp_bs[l], cmp_pos[l], cmp_w1[l], cmp_b1[l],
                           cmp_w2[l], cmp_b2[l], w_br_pool[l], w_br_gmlp[l], w_br_nsa[l], w_out[l])
        xs = _ffn(xs, row(norm_g[l, 2]), row(mod[4]), row(mod[3]), w_ff1[l].astype(BF16),
                  w_ff2[l].astype(BF16), row(norm_g[l, 3]), row(mod[5]))
    return xs[None]
```

```python
import functools

import jax
import jax.numpy as jnp
from jax import lax
from jax.experimental import pallas as pl
from jax.experimental.pallas import tpu as pltpu

F32 = jnp.float32
BF16 = jnp.bfloat16

D_MODEL = 2048
POOL_WINDOWS = (2, 4, 8, 16)
POOL_HALO = 16
GROUP_CH = 128
MIX_W = 4 * GROUP_CH
GMLP_CHUNK = 128
HEAD_DIM = 64
KV_GROUPS = 4
HEADS_PER_GROUP = 4
Q_W = 16 * HEAD_DIM
KV_W = KV_GROUPS * HEAD_DIM
CMP_BLOCK = 32
CMP_STRIDE = 16
SEL_BLOCK = 64
SEL_TOPN = 16
WINDOW = 512
FORCE_BONUS = 1000.0
LOG2E = 1.4426950408889634
NEG = -1e30
N_GATE = 3 * 16
D_FF = 4 * D_MODEL

Q_BLOCK = 256
QL = HEADS_PER_GROUP * Q_BLOCK
CMP_PER_QB = Q_BLOCK // CMP_STRIDE
WIN_CHUNK = 128
PAD_TILES = WINDOW // Q_BLOCK
SEL_TILE = 512
KS_AUG_W = 128
CMP_AUG_W = 256
V_AUG_ROWS = HEAD_DIM + 16
P_CHUNK = 16

OFF_POOL, OFF_U, OFF_V, OFF_Q, OFF_KV, OFF_BG = 0, 512, 1024, 1536, 2560, 4096
PROJ_W = OFF_BG + 3 * D_MODEL

VMEM_LIMIT = 56 * 1024 * 1024


def _cparams(*sem):
    return pltpu.CompilerParams(dimension_semantics=sem, vmem_limit_bytes=VMEM_LIMIT)


def _dot(a, b):
    return jnp.dot(a, b, preferred_element_type=F32)


def _rms(x, g):
    return x * lax.rsqrt(jnp.mean(x * x, axis=-1, keepdims=True) + 1e-6) * g


def _mod_kernel(c_ref, w_ref, b_ref, o_ref):
    c = c_ref[...]
    act = c * jax.nn.sigmoid(c)
    o_ref[0] = jnp.sum(act * w_ref[0], axis=0, keepdims=True) + b_ref[0]


def _modulation(c, w_ada, b_ada):
    n_layer, d, n_out = w_ada.shape
    tn = 1024
    return pl.pallas_call(
        _mod_kernel,
        grid=(n_layer, n_out // tn),
        in_specs=[
            pl.BlockSpec((d, 1), lambda l, j: (0, 0)),
            pl.BlockSpec((1, d, tn), lambda l, j: (l, 0, j)),
            pl.BlockSpec((1, 1, tn), lambda l, j: (l, 0, j)),
        ],
        out_specs=pl.BlockSpec((1, 1, tn), lambda l, j: (l, 0, j)),
        out_shape=jax.ShapeDtypeStruct((n_layer, 1, n_out), F32),
        compiler_params=_cparams("parallel", "parallel"),
        name="adaln_mod",
    )(c.reshape(d, 1), w_ada, b_ada.reshape(n_layer, 1, n_out))


def _inproj_kernel(x_ref, g_ref, sc_ref, sh_ref, w_ref, wng_ref, cs_ref, o_ref, ng_ref, h_scr):
    @pl.when(pl.program_id(1) == 0)
    def _():
        h = _rms(x_ref[...], g_ref[...]) * (1.0 + sc_ref[...]) + sh_ref[...]
        hb = h.astype(BF16)
        h_scr[...] = hb
        ng_ref[...] = _dot(hb, wng_ref[...])

    o_ref[...] = (_dot(h_scr[...], w_ref[...]) * cs_ref[...]).astype(o_ref.dtype)


def _in_projection(x, g, scale, shift, w_main, w_ng, col_scale):
    s, d = x.shape
    tm, tn = min(1024, s), 1024
    row = lambda i, j: (0, 0)
    return pl.pallas_call(
        _inproj_kernel,
        grid=(s // tm, PROJ_W // tn),
        in_specs=[
            pl.BlockSpec((tm, d), lambda i, j: (i, 0)),
            pl.BlockSpec((1, d), row),
            pl.BlockSpec((1, d), row),
            pl.BlockSpec((1, d), row),
            pl.BlockSpec((d, tn), lambda i, j: (0, j)),
            pl.BlockSpec((d, 128), row),
            pl.BlockSpec((1, tn), lambda i, j: (0, j)),
        ],
        out_specs=[
            pl.BlockSpec((tm, tn), lambda i, j: (i, j)),
            pl.BlockSpec((tm, 128), lambda i, j: (i, 0)),
        ],
        out_shape=[
            jax.ShapeDtypeStruct((s, PROJ_W), BF16),
            jax.ShapeDtypeStruct((s, 128), F32),
        ],
        scratch_shapes=[pltpu.VMEM((tm, d), BF16)],
        compiler_params=_cparams("parallel", "arbitrary"),
        name="in_proj",
    )(x, g, scale, shift, w_main, w_ng, col_scale)


def _mixer_kernel(a_ref, halo_ref, u_ref, v_ref, pw_ref, ps_ref, lg_ref, lb_ref, ws_ref, bs_ref,
                  ya_ref, yb_ref):
    i = pl.program_id(0)
    tm = a_ref.shape[0]
    a = a_ref[...].astype(F32)
    halo = jnp.where(i > 0, halo_ref[...].astype(F32), 0.0)
    ext = jnp.concatenate([halo, a], axis=0)
    p2 = ext[1:] + ext[:-1]
    p4 = p2[2:] + p2[:-2]
    p8 = p4[4:] + p4[:-4]
    p16 = p8[8:] + p8[:-8]
    sums = (p2[15:15 + tm], p4[13:13 + tm], p8[9:9 + tm], p16[1:1 + tm])
    t = i * tm + lax.broadcasted_iota(jnp.int32, (tm, 1), 0)
    for gi, w in enumerate(POOL_WINDOWS):
        cols = slice(gi * GROUP_CH, (gi + 1) * GROUP_CH)
        cnt = jnp.minimum(t + 1, w).astype(F32)
        pooled = sums[gi][:, cols] / cnt - a[:, cols]
        y = _dot(pooled.astype(BF16), pw_ref[gi])
        ya_ref[:, cols] = (y * ps_ref[:, cols]).astype(ya_ref.dtype)

    u = jax.nn.gelu(u_ref[...].astype(F32))
    v = jax.nn.gelu(v_ref[...].astype(F32))
    mu = jnp.mean(v, axis=-1, keepdims=True)
    var = jnp.mean(jnp.square(v - mu), axis=-1, keepdims=True)
    vn = ((v - mu) * lax.rsqrt(var + 1e-5) * lg_ref[...] + lb_ref[...]).astype(BF16)
    r = lax.broadcasted_iota(jnp.int32, (GMLP_CHUNK, GMLP_CHUNK), 0)
    c = lax.broadcasted_iota(jnp.int32, (GMLP_CHUNK, GMLP_CHUNK), 1)
    for gi in range(4):
        cols = slice(gi * GROUP_CH, (gi + 1) * GROUP_CH)
        wsm = jnp.where(r >= c, ws_ref[gi], 0.0).astype(BF16)
        bias = bs_ref[:, gi:gi + 1]
        for ck in range(tm // GMLP_CHUNK):
            rows = slice(ck * GMLP_CHUNK, (ck + 1) * GMLP_CHUNK)
            mixed = _dot(wsm, vn[rows, cols]) + bias
            yb_ref[rows, cols] = (u[rows, cols] * mixed).astype(yb_ref.dtype)


def _mixers(proj, pool_w, pool_scale, ln_g, ln_b, ws, bs_t):
    s = proj.shape[0]
    tm = min(512, s)
    hb = tm // POOL_HALO
    const2 = lambda i: (0, 0)
    const3 = lambda i: (0, 0, 0)
    return pl.pallas_call(
        _mixer_kernel,
        grid=(s // tm,),
        in_specs=[
            pl.BlockSpec((tm, MIX_W), lambda i: (i, OFF_POOL // MIX_W)),
            pl.BlockSpec((POOL_HALO, MIX_W), lambda i: (jnp.maximum(i * hb - 1, 0), OFF_POOL // MIX_W)),
            pl.BlockSpec((tm, MIX_W), lambda i: (i, OFF_U // MIX_W)),
            pl.BlockSpec((tm, MIX_W), lambda i: (i, OFF_V // MIX_W)),
            pl.BlockSpec((4, GROUP_CH, GROUP_CH), const3),
            pl.BlockSpec((1, MIX_W), const2),
            pl.BlockSpec((1, MIX_W), const2),
            pl.BlockSpec((1, MIX_W), const2),
            pl.BlockSpec((4, GMLP_CHUNK, GMLP_CHUNK), const3),
            pl.BlockSpec((GMLP_CHUNK, 4), const2),
        ],
        out_specs=[
            pl.BlockSpec((tm, MIX_W), lambda i: (i, 0)),
            pl.BlockSpec((tm, MIX_W), lambda i: (i, 0)),
        ],
        out_shape=[jax.ShapeDtypeStruct((s, MIX_W), BF16)] * 2,
        compiler_params=_cparams("parallel"),
        name="mixers",
    )(proj, proj, proj, proj, pool_w, pool_scale, ln_g, ln_b, ws, bs_t)


def _compress_kernel(x_ref, pos_ref, w1_ref, b1_ref, w2_ref, b2_ref, o_ref):
    half = CMP_STRIDE * HEAD_DIM
    x = x_ref[0, 0].astype(F32)
    nc = x.shape[0]
    pos = pos_ref[0]
    first = _dot((x + pos[:, :half]).astype(BF16), w1_ref[0, :half, :])
    second = _dot((x + pos[:, half:]).astype(BF16), w1_ref[0, half:, :])
    hid = jax.nn.gelu(first + pltpu.roll(second, nc - 1, 0) + b1_ref[0])
    o_ref[0, 0] = _dot(hid.astype(BF16), w2_ref[0]) + b2_ref[0]


def _compress(xkv, pos, w1, b1, w2, b2):
    _, n_group, nc, width = xkv.shape
    per_kv = lambda a, g: (a, 0, 0)
    return pl.pallas_call(
        _compress_kernel,
        grid=(2, n_group),
        in_specs=[
            pl.BlockSpec((1, 1, nc, width), lambda a, g: (a, g, 0, 0)),
            pl.BlockSpec((1, 1, 2 * width), per_kv),
            pl.BlockSpec((1, 2 * width, 128), per_kv),
            pl.BlockSpec((1, 1, 128), per_kv),
            pl.BlockSpec((1, 128, HEAD_DIM), per_kv),
            pl.BlockSpec((1, 1, HEAD_DIM), per_kv),
        ],
        out_specs=pl.BlockSpec((1, 1, nc, HEAD_DIM), lambda a, g: (a, g, 0, 0)),
        out_shape=jax.ShapeDtypeStruct((2, n_group, nc, HEAD_DIM), F32),
        compiler_params=_cparams("parallel", "parallel"),
        name="compress_kv",
    )(xkv, pos, w1, b1, w2, b2)


def _operand_kernel(q_lo_ref, q_hi_ref, sel_ref, win_ref, qt_ref, ks_ref, vst_ref, kw_ref, vwt_ref):
    i = pl.program_id(0)
    n_tiles = pl.num_programs(0) - PAD_TILES
    rows = q_lo_ref.shape[0]
    lane = lax.broadcasted_iota(jnp.int32, (rows, 128), 1)
    pos = i * rows + lax.broadcasted_iota(jnp.int32, (rows, 128), 0)
    ones_rows = (lax.broadcasted_iota(jnp.int32, (V_AUG_ROWS - HEAD_DIM, rows), 0) == 0).astype(BF16)

    def key_pair(slab, idx):
        onehot = (lane - HEAD_DIM == idx).astype(F32)
        left = jnp.where(lane < HEAD_DIM, slab, onehot)
        right = jnp.where(lane < HEAD_DIM, pltpu.roll(slab, HEAD_DIM, 1), onehot)
        return left.astype(BF16), right.astype(BF16)

    def transposed_pair(slab):
        t = slab.T.astype(BF16)
        return t[:HEAD_DIM], t[HEAD_DIM:]

    def write_kv(src, keep, idx, k_ref, vt_ref):
        for pair in range(KV_GROUPS // 2):
            lanes = slice(pair * 128, (pair + 1) * 128)
            k_slab = jnp.where(keep, src[:, lanes].astype(F32), 0.0)
            v_slab = jnp.where(keep, src[:, KV_W + pair * 128:KV_W + (pair + 1) * 128].astype(F32), 0.0)
            for g, k, vt in zip((2 * pair, 2 * pair + 1), key_pair(k_slab, idx), transposed_pair(v_slab)):
                k_ref[g] = k
                vt_ref[g, 0:HEAD_DIM, :] = vt
                vt_ref[g, HEAD_DIM:, :] = ones_rows

    @pl.when(i < n_tiles)
    def _():
        for half, ref in enumerate((q_lo_ref, q_hi_ref)):
            for gg in range(2):
                for pp in range(HEADS_PER_GROUP // 2):
                    lanes = slice(gg * 256 + pp * 128, gg * 256 + (pp + 1) * 128)
                    for h, t in zip((2 * pp, 2 * pp + 1), transposed_pair(ref[:, lanes].astype(F32))):
                        qt_ref[2 * half + gg, 0, :, h * Q_BLOCK:(h + 1) * Q_BLOCK] = t
        write_kv(sel_ref, True, (pos // SEL_BLOCK) % (SEL_TILE // SEL_BLOCK), ks_ref, vst_ref)

    write_kv(win_ref, i >= PAD_TILES, (pos // WIN_CHUNK) % 8, kw_ref, vwt_ref)


def _attention_operands(proj):
    s = proj.shape[0]
    n_tiles = s // Q_BLOCK
    blk = MIX_W
    real = lambda i: jnp.minimum(i, n_tiles - 1)
    src = lambda c: pl.BlockSpec((Q_BLOCK, blk), lambda i: (real(i), c))
    return pl.pallas_call(
        _operand_kernel,
        grid=(n_tiles + PAD_TILES,),
        in_specs=[src(OFF_Q // blk), src(OFF_Q // blk + 1), src(OFF_KV // blk + 1),
                  pl.BlockSpec((Q_BLOCK, blk), lambda i: (jnp.maximum(i - PAD_TILES, 0), OFF_KV // blk + 2))],
        out_specs=[
            pl.BlockSpec((KV_GROUPS, 1, HEAD_DIM, QL), lambda i: (0, real(i), 0, 0)),
            pl.BlockSpec((KV_GROUPS, Q_BLOCK, KS_AUG_W), lambda i: (0, real(i), 0)),
            pl.BlockSpec((KV_GROUPS, V_AUG_ROWS, Q_BLOCK), lambda i: (0, 0, real(i))),
            pl.BlockSpec((KV_GROUPS, Q_BLOCK, KS_AUG_W), lambda i: (0, i, 0)),
            pl.BlockSpec((KV_GROUPS, V_AUG_ROWS, Q_BLOCK), lambda i: (0, 0, i)),
        ],
        out_shape=[
            jax.ShapeDtypeStruct((KV_GROUPS, n_tiles, HEAD_DIM, QL), BF16),
            jax.ShapeDtypeStruct((KV_GROUPS, s, KS_AUG_W), BF16),
            jax.ShapeDtypeStruct((KV_GROUPS, V_AUG_ROWS, s), BF16),
            jax.ShapeDtypeStruct((KV_GROUPS, s + WINDOW, KS_AUG_W), BF16),
            jax.ShapeDtypeStruct((KV_GROUPS, V_AUG_ROWS, s + WINDOW), BF16),
        ],
        compiler_params=_cparams("arbitrary"),
        name="attn_operands",
    )(proj, proj, proj, proj)


def _nsa_kernel(qt_ref, kc_ref, vct_ref, ks_ref, vst_ref, kw_ref, vwt_ref, gl_ref, o_ref,
                ps_scr, bias_scr, diag_bias_scr, qc_scr, qw_scr, qs_scr, sc_scr, s0_scr, s1_scr, p0_scr, p1_scr):
    qb = pl.program_id(1)
    t0 = qb * Q_BLOCK
    t0a = pl.multiple_of(t0, Q_BLOCK)
    qt = qt_ref[0, 0]
    nc = kc_ref.shape[1]
    n_sel = bias_scr.shape[0]
    q_rel = lax.broadcasted_iota(jnp.int32, (1, QL), 1) % Q_BLOCK
    tq = t0 + q_rel

    def bias_rows(cond, rows):
        return jnp.where(cond, NEG, 0.0).astype(BF16) + jnp.zeros((rows, QL), BF16)

    def flash_update(s, carry, vt):
        m_prev, acc = carry
        m_new = jnp.maximum(m_prev, jnp.max(s, axis=0, keepdims=True))
        p = jnp.exp2((s - m_new).astype(BF16))
        return m_new, jnp.exp2(m_prev - m_new) * acc + _dot(vt, p)

    init = (jnp.full((1, QL), NEG, F32), jnp.zeros((V_AUG_ROWS, QL), F32))

    grp = lax.broadcasted_iota(jnp.int32, (qc_scr.shape[0] - HEAD_DIM, 1), 0)
    visible = CMP_PER_QB * (qb + 1)
    qc_scr[0:HEAD_DIM, :] = qt
    qc_scr[HEAD_DIM:, :] = bias_rows((8 * grp >= visible) & (grp < nc // 8), qc_scr.shape[0] - HEAD_DIM)
    band = CMP_PER_QB + 8
    r0 = pl.multiple_of(jnp.maximum(visible - band, 0), 8)
    band_end = CMP_STRIDE * (r0 + lax.broadcasted_iota(jnp.int32, (band, 1), 0)) + (CMP_BLOCK - 1)

    def compressed(rows):
        def run():
            sc_scr[0:rows, :] = _dot(kc_ref[0, 0:rows, :], qc_scr[...])
            sc_scr[pl.ds(r0, band), :] = jnp.where(band_end <= tq, sc_scr[pl.ds(r0, band), :], NEG)
            s_c = sc_scr[0:rows, :]
            m_c = jnp.maximum(jnp.max(s_c, axis=0, keepdims=True), 0.1 * NEG)
            e_c = jnp.exp2(s_c - m_c)
            o_aug = _dot(vct_ref[0, :, 0:rows], e_c.astype(BF16))
            inv_c = 1.0 / jnp.maximum(o_aug[HEAD_DIM:HEAD_DIM + 1], 1e-30)
            p_grp = None
            for h in range(HEADS_PER_GROUP):
                lanes = slice(h * Q_BLOCK, (h + 1) * Q_BLOCK)
                p_h = e_c[:, lanes] * inv_c[:, lanes]
                p_grp = p_h if p_grp is None else p_grp + p_h
            for c in range(Q_BLOCK // 128):
                ps_scr[c, 8:8 + rows, :] = p_grp[:, c * 128:(c + 1) * 128]
                if rows < nc:
                    ps_scr[c, 8 + rows:, :] = jnp.zeros((nc - rows, 128), F32)
            return o_aug[:HEAD_DIM] * inv_c
        return run

    ps_scr[:, 0:8, :] = jnp.zeros((Q_BLOCK // 128, 8, 128), F32)
    quarter = nc // 4
    o_c = lax.cond(visible <= 2 * quarter,
                   lambda: lax.cond(visible <= quarter, compressed(quarter), compressed(2 * quarter)),
                   lambda: lax.cond(visible <= 3 * quarter, compressed(3 * quarter), compressed(nc)))
    imp = jnp.concatenate([sum(ps_scr[c, pl.ds(k, n_sel, stride=4), :] for k in range(7, 12))
                           for c in range(Q_BLOCK // 128)], axis=1)

    n_tri = Q_BLOCK // WIN_CHUNK
    n_chunk = WINDOW // WIN_CHUNK + n_tri
    span = n_chunk * WIN_CHUNK
    rho = lax.broadcasted_iota(jnp.int32, (16, 1), 0)
    first_real = WINDOW // WIN_CHUNK - n_tri * qb
    qw_scr[0:HEAD_DIM, :] = qt
    qw_scr[HEAD_DIM:HEAD_DIM + 16, :] = bias_rows((rho < 8) & (((rho - n_tri * qb) & 7) < first_real), 16)
    qw_scr[HEAD_DIM + 16:, :] = jnp.zeros((qw_scr.shape[0] - HEAD_DIM - 16, QL), BF16)
    s_w = _dot(kw_ref[0, pl.ds(t0a, span), :], qw_scr[...])
    i_rel = lax.broadcasted_iota(jnp.int32, (Q_BLOCK, 1), 0)
    s_w = jnp.concatenate([jnp.where(i_rel > q_rel, s_w[:Q_BLOCK], NEG), s_w[Q_BLOCK:WINDOW],
                           jnp.where(i_rel <= q_rel, s_w[WINDOW:], NEG)], axis=0)
    _, acc_w = flash_update(s_w, init, vwt_ref[0, :, pl.ds(t0a, span)])

    j_idx = lax.broadcasted_iota(jnp.int32, (n_sel, Q_BLOCK), 0)
    cur = (t0 + lax.broadcasted_iota(jnp.int32, (n_sel, Q_BLOCK), 1)) // SEL_BLOCK
    forced = (j_idx == 0) | (j_idx == cur) | (j_idx == cur - 1)
    valid = j_idx <= cur
    score = jnp.where(forced, -2.0, jnp.where(valid, imp, -1.0))
    n_pick = SEL_TOPN - 3

    fast = score
    for _ in range(n_pick):
        fast = jnp.where(fast == jnp.max(fast, axis=0, keepdims=True), -2.0, fast)
    retired = (fast < -1.5) & valid & jnp.logical_not(forced)
    most_retired = jnp.max(jnp.sum(retired.astype(F32), axis=0, keepdims=True))

    def ranked_with_ties():
        def pick_one(_, sc):
            best = jnp.max(sc, axis=0, keepdims=True)
            first = jnp.min(jnp.where(sc == best, j_idx, n_sel), axis=0, keepdims=True)
            return jnp.where(j_idx == first, -2.0, sc)
        return lax.fori_loop(0, n_pick, pick_one, score)

    ranked = lax.cond(most_retired > n_pick, ranked_with_ties, lambda: fast)
    chosen = valid & (ranked < -1.5)
    diag_bias_scr[...] = jnp.where(chosen, 0.0, NEG)
    bias_scr[...] = jnp.where(chosen & (j_idx * SEL_BLOCK < t0), 0.0, NEG)

    blocks_per_tile = SEL_TILE // SEL_BLOCK
    last_tile = ks_ref.shape[1] // SEL_TILE - 1
    qs_scr[0:HEAD_DIM, :] = qt
    qs_scr[HEAD_DIM + 16:, :] = jnp.zeros((qs_scr.shape[0] - HEAD_DIM - 16, QL), BF16)

    def tile_start(kt):
        return pl.multiple_of(jnp.clip(kt, 0, last_tile) * SEL_TILE, SEL_TILE)

    def set_bias_rows(table, kt):
        b0 = pl.multiple_of(jnp.minimum(kt, last_tile) * blocks_per_tile, blocks_per_tile)
        b16 = jnp.concatenate([table[pl.ds(b0, blocks_per_tile), :],
                               jnp.zeros((16 - blocks_per_tile, Q_BLOCK), F32)], axis=0)
        qs_scr[HEAD_DIM:HEAD_DIM + 16, :] = jnp.concatenate([b16] * HEADS_PER_GROUP, axis=1).astype(BF16)

    set_bias_rows(diag_bias_scr, t0 // SEL_TILE)
    s_d = _dot(ks_ref[0, pl.ds(t0a, Q_BLOCK), :], qs_scr[...])
    m_d, acc_d = flash_update(jnp.where(i_rel <= q_rel, s_d, NEG), init, vst_ref[0, :, pl.ds(t0a, Q_BLOCK)])

    def sel_scores(kt, dst):
        set_bias_rows(bias_scr, kt)
        dst[...] = _dot(ks_ref[0, pl.ds(tile_start(kt), SEL_TILE), :], qs_scr[...])

    def half_step(kt, s_cur, s_nxt, p_cur, p_prev, carry):
        m_prev, alpha_prev, acc = carry
        m8 = jnp.max(s_cur[...].reshape(SEL_TILE // 8, 8, QL), axis=0)
        m_new = jnp.maximum(m_prev, jnp.max(m8, axis=0, keepdims=True))
        sel_scores(kt + 1, s_nxt)
        for c in range(SEL_TILE // P_CHUNK):
            rows = slice(c * P_CHUNK, (c + 1) * P_CHUNK)
            p_cur[rows, :] = jnp.exp2((s_cur[rows, :] - m_new).astype(BF16))
        acc = alpha_prev * acc + _dot(vst_ref[0, :, pl.ds(tile_start(kt - 1), SEL_TILE)], p_prev[...])
        return m_new, jnp.exp2(m_prev - m_new), acc

    def pair_step(i, carry):
        carry = half_step(2 * i, s0_scr, s1_scr, p0_scr, p1_scr, carry)
        return half_step(2 * i + 1, s1_scr, s0_scr, p1_scr, p0_scr, carry)

    n_pairs = ((t0 + SEL_TILE - 1) // SEL_TILE + 1) // 2
    sel_scores(0, s0_scr)
    p1_scr[...] = jnp.zeros(p1_scr.shape, BF16)
    carry = (m_d, jnp.ones((1, QL), F32), acc_d)
    _, alpha_last, acc_s = lax.fori_loop(0, n_pairs, pair_step, carry)
    acc_s = alpha_last * acc_s + _dot(vst_ref[0, :, pl.ds(tile_start(2 * n_pairs - 1), SEL_TILE)], p1_scr[...])

    gate = jax.nn.sigmoid(gl_ref[0, 0])
    o = (gate[0:1] * o_c + gate[1:2] * (acc_s[:HEAD_DIM] / acc_s[HEAD_DIM:HEAD_DIM + 1])
         + gate[2:3] * (acc_w[:HEAD_DIM] / acc_w[HEAD_DIM:HEAD_DIM + 1]))
    stacked = jnp.concatenate([o[:, p * Q_BLOCK:(p + 1) * Q_BLOCK] for p in range(HEADS_PER_GROUP)], axis=0)
    o_ref[...] = stacked.T.astype(o_ref.dtype)


def _sparse_attention(qt, kc_aug, vc_aug_t, ks_aug, vs_aug_t, kw_aug, vw_aug_t, gate_logits, s):
    n_group, n_qb = qt.shape[:2]
    nc = kc_aug.shape[1]
    n_sel = s // SEL_BLOCK
    assert HEAD_DIM + nc // 8 <= CMP_AUG_W and nc // 4 >= CMP_PER_QB + 8 and WINDOW // WIN_CHUNK + Q_BLOCK // WIN_CHUNK <= 8
    per_step = lambda g, i: (g, i, 0, 0)
    per_group = lambda shape: pl.BlockSpec((1,) + shape, lambda g, i: (g, 0, 0), pipeline_mode=pl.Buffered(1))
    tile_f32 = pltpu.VMEM((SEL_TILE, QL), F32)
    tile_bf16 = pltpu.VMEM((SEL_TILE, QL), BF16)
    sel_table = pltpu.VMEM((n_sel, Q_BLOCK), F32)
    return pl.pallas_call(
        _nsa_kernel,
        grid=(n_group, n_qb),
        in_specs=[
            pl.BlockSpec((1, 1, HEAD_DIM, QL), per_step),
            per_group((nc, CMP_AUG_W)),
            per_group((V_AUG_ROWS, nc)),
            per_group((s, KS_AUG_W)),
            per_group((V_AUG_ROWS, s)),
            per_group((s + WINDOW, KS_AUG_W)),
            per_group((V_AUG_ROWS, s + WINDOW)),
            pl.BlockSpec((1, 1, 3, QL), per_step),
        ],
        out_specs=pl.BlockSpec((Q_BLOCK, HEADS_PER_GROUP * HEAD_DIM), lambda g, i: (i, g)),
        out_shape=jax.ShapeDtypeStruct((s, Q_W), BF16),
        scratch_shapes=[pltpu.VMEM((Q_BLOCK // 128, 8 + nc, 128), F32), sel_table, sel_table,
                        pltpu.VMEM((CMP_AUG_W, QL), BF16), pltpu.VMEM((KS_AUG_W, QL), BF16),
                        pltpu.VMEM((KS_AUG_W, QL), BF16), pltpu.VMEM((nc, QL), F32),
                        tile_f32, tile_f32, tile_bf16, tile_bf16],
        compiler_params=_cparams("parallel", "arbitrary"),
        name="sparse_attn",
    )(qt, kc_aug, vc_aug_t, ks_aug, vs_aug_t, kw_aug, vw_aug_t, gate_logits)


def _merge_kernel(x_ref, ya_ref, yb_ref, yc_ref, g0_ref, g1_ref, g2_ref, wp_ref, wg_ref, wn_ref,
                  wo_ref, ng_ref, gate_ref, o_ref):
    merged = jax.nn.sigmoid(g0_ref[...].astype(F32)) * _dot(ya_ref[...], wp_ref[...])
    merged += jax.nn.sigmoid(g1_ref[...].astype(F32)) * _dot(yb_ref[...], wg_ref[...])
    merged += jax.nn.sigmoid(g2_ref[...].astype(F32)) * _dot(yc_ref[...], wn_ref[...])
    y = _dot(merged.astype(BF16), wo_ref[...])
    o_ref[...] = x_ref[...] + gate_ref[...] * _rms(y, ng_ref[...])


def _merge(x, ya, yb, yc, proj, wp, wg, wn, wo, norm_g, gate):
    s, d = x.shape
    tm = min(256, s)
    const = lambda i: (0, 0)
    rows = lambda i: (i, 0)
    whole = lambda a: pl.BlockSpec(a.shape, const, pipeline_mode=pl.Buffered(1))
    bg = OFF_BG // d
    return pl.pallas_call(
        _merge_kernel,
        grid=(s // tm,),
        in_specs=[
            pl.BlockSpec((tm, d), rows),
            pl.BlockSpec((tm, MIX_W), rows),
            pl.BlockSpec((tm, MIX_W), rows),
            pl.BlockSpec((tm, Q_W), rows),
            pl.BlockSpec((tm, d), lambda i: (i, bg)),
            pl.BlockSpec((tm, d), lambda i: (i, bg + 1)),
            pl.BlockSpec((tm, d), lambda i: (i, bg + 2)),
            whole(wp), whole(wg), whole(wn), whole(wo),
            pl.BlockSpec((1, d), const),
            pl.BlockSpec((1, d), const),
        ],
        out_specs=pl.BlockSpec((tm, d), rows),
        out_shape=jax.ShapeDtypeStruct((s, d), F32),
        compiler_params=_cparams("parallel"),
        name="merge_out",
    )(x, ya, yb, yc, proj, proj, proj, wp, wg, wn, wo, norm_g, gate)


def _ffn_kernel(x_ref, gi_ref, sc_ref, sh_ref, w1_ref, w2_ref, go_ref, gate_ref, o_ref, h_scr):
    j = pl.program_id(1)

    @pl.when(j == 0)
    def _():
        h = _rms(x_ref[...], gi_ref[...]) * (1.0 + sc_ref[...]) + sh_ref[...]
        h_scr[...] = h.astype(BF16)
        o_ref[...] = jnp.zeros_like(o_ref)

    a = jnp.square(jnp.maximum(_dot(h_scr[...], w1_ref[...]), 0.0))
    o_ref[...] += _dot(a.astype(BF16), w2_ref[...])

    @pl.when(j == pl.num_programs(1) - 1)
    def _():
        o_ref[...] = x_ref[...] + gate_ref[...] * _rms(o_ref[...], go_ref[...])


def _ffn(x, g_in, scale, shift, w1, w2, g_out, gate):
    s, d = x.shape
    tm, tf = min(1024, s), 512
    const = lambda i, j: (0, 0)
    rows = lambda i, j: (i, 0)
    return pl.pallas_call(
        _ffn_kernel,
        grid=(s // tm, D_FF // tf),
        in_specs=[
            pl.BlockSpec((tm, d), rows, pipeline_mode=pl.Buffered(1)),
            pl.BlockSpec((1, d), const),
            pl.BlockSpec((1, d), const),
            pl.BlockSpec((1, d), const),
            pl.BlockSpec((d, tf), lambda i, j: (0, j)),
            pl.BlockSpec((tf, d), lambda i, j: (j, 0)),
            pl.BlockSpec((1, d), const),
            pl.BlockSpec((1, d), const),
        ],
        out_specs=pl.BlockSpec((tm, d), rows),
        out_shape=jax.ShapeDtypeStruct((s, d), F32),
        scratch_shapes=[pltpu.VMEM((tm, d), BF16)],
        compiler_params=_cparams("parallel", "arbitrary"),
        name="ffn",
    )(x, g_in, scale, shift, w1, w2, g_out, gate)


def _token_mixing(x, mod, norm_g, w_in, pool_w, pool_scale, ln_g, ln_b, ws, bs, cmp_pos, cmp_w1,
                  cmp_b1, cmp_w2, cmp_b2, w_br_pool, w_br_gmlp, w_br_nsa, w_out):
    s, d = x.shape
    n_qb = s // Q_BLOCK
    ng0 = OFF_BG
    w_main = jnp.concatenate([w_in[:, :ng0], w_in[:, ng0 + N_GATE:]], axis=1).astype(BF16)
    w_ng = jnp.pad(w_in[:, ng0:ng0 + N_GATE], ((0, 0), (0, 128 - N_GATE))).astype(BF16)
    row = lambda v: v.reshape(1, -1)

    col_scale = jnp.ones((1, PROJ_W), F32).at[:, OFF_Q:OFF_KV].set(HEAD_DIM ** -0.5 * LOG2E)
    proj, ngate = _in_projection(x, row(norm_g[0]), row(mod[1]), row(mod[0]), w_main, w_ng, col_scale)

    ya, yb = _mixers(proj, pool_w.astype(BF16), row(pool_scale), row(ln_g), row(ln_b), ws, bs.T)

    qt, ks_aug, vs_aug_t, kw_aug, vw_aug_t = _attention_operands(proj)
    kv_cmp = proj[:, OFF_KV:OFF_KV + 2 * KV_W].reshape(s, 2, KV_GROUPS, HEAD_DIM).transpose(1, 2, 0, 3)
    xkv = kv_cmp.reshape(2, KV_GROUPS, s // CMP_STRIDE, CMP_STRIDE * HEAD_DIM)
    gl = ngate[:, :N_GATE].reshape(n_qb, Q_BLOCK, KV_GROUPS, HEADS_PER_GROUP, 3).transpose(2, 0, 4, 3, 1)
    gl = gl.reshape(KV_GROUPS, n_qb, 3, QL)

    cmp = _compress(xkv, cmp_pos.reshape(2, 1, CMP_BLOCK * HEAD_DIM), cmp_w1.astype(BF16),
                    cmp_b1.reshape(2, 1, -1), cmp_w2.astype(BF16), cmp_b2.reshape(2, 1, -1))
    def with_onehot(k, idx, width):
        onehot = (idx[:, None] == jnp.arange(width - HEAD_DIM)[None, :]).astype(BF16)
        return jnp.concatenate([k, jnp.broadcast_to(onehot, k.shape[:-1] + onehot.shape[-1:])], axis=-1)

    def with_ones_row(vt):
        extra = jnp.zeros(vt.shape[:1] + (V_AUG_ROWS - HEAD_DIM,) + vt.shape[2:], BF16).at[:, 0].set(1.0)
        return jnp.concatenate([vt, extra], axis=1)

    nc = s // CMP_STRIDE
    kc_aug = with_onehot(cmp[0].astype(BF16), jnp.arange(nc) // 8, CMP_AUG_W)
    vc_aug_t = with_ones_row(cmp[1].transpose(0, 2, 1).astype(BF16))
    yc = _sparse_attention(qt, kc_aug, vc_aug_t, ks_aug, vs_aug_t, kw_aug, vw_aug_t, gl, s)

    return _merge(x, ya, yb, yc, proj, w_br_pool.astype(BF16), w_br_gmlp.astype(BF16),
                  w_br_nsa.astype(BF16), w_out.astype(BF16), row(norm_g[1]), row(mod[2]))


def kernel(x, c, norm_g, w_ada, b_ada, w_in, pool_w, pool_scale, gmlp_ln_g, gmlp_ln_b, gmlp_ws, gmlp_bs,
           cmp_pos, cmp_w1, cmp_b1, cmp_w2, cmp_b2, w_br_pool, w_br_gmlp, w_br_nsa, w_out, w_ff1, w_ff2):
    b, s, d = x.shape
    assert b == 1 and d == D_MODEL and s % 1024 == 0
    n_layer = w_ada.shape[0]
    mod_all = _modulation(c, w_ada, b_ada).reshape(n_layer, 6, d)
    xs = x[0]
    row = lambda v: v.reshape(1, -1)
    for l in range(n_layer):
        mod = mod_all[l]
        xs = _token_mixing(xs, mod, norm_g[l], w_in[l], pool_w[l], pool_scale[l], gmlp_ln_g[l],
                           gmlp_ln_b[l], gmlp_ws[l], gmlp_bs[l], cmp_pos[l], cmp_w1[l], cmp_b1[l],
                           cmp_w2[l], cmp_b2[l], w_br_pool[l], w_br_gmlp[l], w_br_nsa[l], w_out[l])
        xs = _ffn(xs, row(norm_g[l, 2]), row(mod[4]), row(mod[3]), w_ff1[l].astype(BF16),
                  w_ff2[l].astype(BF16), row(norm_g[l, 3]), row(mod[5]))
    return xs[None]
```

```python
import functools

import jax
import jax.numpy as jnp
from jax import lax
from jax.experimental import pallas as pl
from jax.experimental.pallas import tpu as pltpu

F32 = jnp.float32
BF16 = jnp.bfloat16

D_MODEL = 2048
POOL_WINDOWS = (2, 4, 8, 16)
POOL_HALO = 16
GROUP_CH = 128
MIX_W = 4 * GROUP_CH
GMLP_CHUNK = 128
HEAD_DIM = 64
KV_GROUPS = 4
HEADS_PER_GROUP = 4
Q_W = 16 * HEAD_DIM
KV_W = KV_GROUPS * HEAD_DIM
CMP_BLOCK = 32
CMP_STRIDE = 16
SEL_BLOCK = 64
SEL_TOPN = 16
WINDOW = 512
FORCE_BONUS = 1000.0
LOG2E = 1.4426950408889634
NEG = -1e30
N_GATE = 3 * 16
D_FF = 4 * D_MODEL

Q_BLOCK = 256
QL = HEADS_PER_GROUP * Q_BLOCK
CMP_PER_QB = Q_BLOCK // CMP_STRIDE
WIN_CHUNK = 128
PAD_TILES = WINDOW // Q_BLOCK
SEL_TILE = 512
KS_AUG_W = 128
CMP_AUG_W = 256
V_AUG_ROWS = HEAD_DIM + 16
P_CHUNK = 64
QK_PART = 256

OFF_POOL, OFF_U, OFF_V, OFF_Q, OFF_KV, OFF_BG = 0, 512, 1024, 1536, 2560, 4096
PROJ_W = OFF_BG + 3 * D_MODEL

VMEM_LIMIT = 56 * 1024 * 1024


def _cparams(*sem):
    return pltpu.CompilerParams(dimension_semantics=sem, vmem_limit_bytes=VMEM_LIMIT)


def _dot(a, b):
    return jnp.dot(a, b, preferred_element_type=F32)


def _rms(x, g):
    return x * lax.rsqrt(jnp.mean(x * x, axis=-1, keepdims=True) + 1e-6) * g


def _mod_kernel(c_ref, w_ref, b_ref, o_ref):
    c = c_ref[...]
    act = c * jax.nn.sigmoid(c)
    o_ref[0] = jnp.sum(act * w_ref[0], axis=0, keepdims=True) + b_ref[0]


def _modulation(c, w_ada, b_ada):
    n_layer, d, n_out = w_ada.shape
    tn = 1024
    return pl.pallas_call(
        _mod_kernel,
        grid=(n_layer, n_out // tn),
        in_specs=[
            pl.BlockSpec((d, 1), lambda l, j: (0, 0)),
            pl.BlockSpec((1, d, tn), lambda l, j: (l, 0, j)),
            pl.BlockSpec((1, 1, tn), lambda l, j: (l, 0, j)),
        ],
        out_specs=pl.BlockSpec((1, 1, tn), lambda l, j: (l, 0, j)),
        out_shape=jax.ShapeDtypeStruct((n_layer, 1, n_out), F32),
        compiler_params=_cparams("parallel", "parallel"),
        name="adaln_mod",
    )(c.reshape(d, 1), w_ada, b_ada.reshape(n_layer, 1, n_out))


def _inproj_kernel(x_ref, g_ref, sc_ref, sh_ref, w_ref, wng_ref, cs_ref, o_ref, ng_ref, h_scr):
    @pl.when(pl.program_id(1) == 0)
    def _():
        h = _rms(x_ref[...], g_ref[...]) * (1.0 + sc_ref[...]) + sh_ref[...]
        hb = h.astype(BF16)
        h_scr[...] = hb
        ng_ref[...] = _dot(hb, wng_ref[...])

    o_ref[...] = (_dot(h_scr[...], w_ref[...]) * cs_ref[...]).astype(o_ref.dtype)


def _in_projection(x, g, scale, shift, w_main, w_ng, col_scale):
    s, d = x.shape
    tm, tn = min(1024, s), 1024
    row = lambda i, j: (0, 0)
    return pl.pallas_call(
        _inproj_kernel,
        grid=(s // tm, PROJ_W // tn),
        in_specs=[
            pl.BlockSpec((tm, d), lambda i, j: (i, 0)),
            pl.BlockSpec((1, d), row),
            pl.BlockSpec((1, d), row),
            pl.BlockSpec((1, d), row),
            pl.BlockSpec((d, tn), lambda i, j: (0, j)),
            pl.BlockSpec((d, 128), row),
            pl.BlockSpec((1, tn), lambda i, j: (0, j)),
        ],
        out_specs=[
            pl.BlockSpec((tm, tn), lambda i, j: (i, j)),
            pl.BlockSpec((tm, 128), lambda i, j: (i, 0)),
        ],
        out_shape=[
            jax.ShapeDtypeStruct((s, PROJ_W), BF16),
            jax.ShapeDtypeStruct((s, 128), F32),
        ],
        scratch_shapes=[pltpu.VMEM((tm, d), BF16)],
        compiler_params=_cparams("parallel", "arbitrary"),
        name="in_proj",
    )(x, g, scale, shift, w_main, w_ng, col_scale)


def _mixer_kernel(a_ref, halo_ref, u_ref, v_ref, pw_ref, ps_ref, lg_ref, lb_ref, ws_ref, bs_ref,
                  ya_ref, yb_ref):
    i = pl.program_id(0)
    tm = a_ref.shape[0]
    a = a_ref[...].astype(F32)
    halo = jnp.where(i > 0, halo_ref[...].astype(F32), 0.0)
    ext = jnp.concatenate([halo, a], axis=0)
    p2 = ext[1:] + ext[:-1]
    p4 = p2[2:] + p2[:-2]
    p8 = p4[4:] + p4[:-4]
    p16 = p8[8:] + p8[:-8]
    sums = (p2[15:15 + tm], p4[13:13 + tm], p8[9:9 + tm], p16[1:1 + tm])
    t = i * tm + lax.broadcasted_iota(jnp.int32, (tm, 1), 0)
    for gi, w in enumerate(POOL_WINDOWS):
        cols = slice(gi * GROUP_CH, (gi + 1) * GROUP_CH)
        cnt = jnp.minimum(t + 1, w).astype(F32)
        pooled = sums[gi][:, cols] / cnt - a[:, cols]
        y = _dot(pooled.astype(BF16), pw_ref[gi])
        ya_ref[:, cols] = (y * ps_ref[:, cols]).astype(ya_ref.dtype)

    u = jax.nn.gelu(u_ref[...].astype(F32))
    v = jax.nn.gelu(v_ref[...].astype(F32))
    mu = jnp.mean(v, axis=-1, keepdims=True)
    var = jnp.mean(jnp.square(v - mu), axis=-1, keepdims=True)
    vn = ((v - mu) * lax.rsqrt(var + 1e-5) * lg_ref[...] + lb_ref[...]).astype(BF16)
    r = lax.broadcasted_iota(jnp.int32, (GMLP_CHUNK, GMLP_CHUNK), 0)
    c = lax.broadcasted_iota(jnp.int32, (GMLP_CHUNK, GMLP_CHUNK), 1)
    for gi in range(4):
        cols = slice(gi * GROUP_CH, (gi + 1) * GROUP_CH)
        wsm = jnp.where(r >= c, ws_ref[gi], 0.0).astype(BF16)
        bias = bs_ref[:, gi:gi + 1]
        for ck in range(tm // GMLP_CHUNK):
            rows = slice(ck * GMLP_CHUNK, (ck + 1) * GMLP_CHUNK)
            mixed = _dot(wsm, vn[rows, cols]) + bias
            yb_ref[rows, cols] = (u[rows, cols] * mixed).astype(yb_ref.dtype)


def _mixers(proj, pool_w, pool_scale, ln_g, ln_b, ws, bs_t):
    s = proj.shape[0]
    tm = min(512, s)
    hb = tm // POOL_HALO
    const2 = lambda i: (0, 0)
    const3 = lambda i: (0, 0, 0)
    return pl.pallas_call(
        _mixer_kernel,
        grid=(s // tm,),
        in_specs=[
            pl.BlockSpec((tm, MIX_W), lambda i: (i, OFF_POOL // MIX_W)),
            pl.BlockSpec((POOL_HALO, MIX_W), lambda i: (jnp.maximum(i * hb - 1, 0), OFF_POOL // MIX_W)),
            pl.BlockSpec((tm, MIX_W), lambda i: (i, OFF_U // MIX_W)),
            pl.BlockSpec((tm, MIX_W), lambda i: (i, OFF_V // MIX_W)),
            pl.BlockSpec((4, GROUP_CH, GROUP_CH), const3),
            pl.BlockSpec((1, MIX_W), const2),
            pl.BlockSpec((1, MIX_W), const2),
            pl.BlockSpec((1, MIX_W), const2),
            pl.BlockSpec((4, GMLP_CHUNK, GMLP_CHUNK), const3),
            pl.BlockSpec((GMLP_CHUNK, 4), const2),
        ],
        out_specs=[
            pl.BlockSpec((tm, MIX_W), lambda i: (i, 0)),
            pl.BlockSpec((tm, MIX_W), lambda i: (i, 0)),
        ],
        out_shape=[jax.ShapeDtypeStruct((s, MIX_W), BF16)] * 2,
        compiler_params=_cparams("parallel"),
        name="mixers",
    )(proj, proj, proj, proj, pool_w, pool_scale, ln_g, ln_b, ws, bs_t)


def _compress_kernel(x_ref, pos_ref, w1_ref, b1_ref, w2_ref, b2_ref, o_ref):
    half = CMP_STRIDE * HEAD_DIM
    x = x_ref[0, 0].astype(F32)
    nc = x.shape[0]
    pos = pos_ref[0]
    first = _dot((x + pos[:, :half]).astype(BF16), w1_ref[0, :half, :])
    second = _dot((x + pos[:, half:]).astype(BF16), w1_ref[0, half:, :])
    hid = jax.nn.gelu(first + pltpu.roll(second, nc - 1, 0) + b1_ref[0])
    o_ref[0, 0] = _dot(hid.astype(BF16), w2_ref[0]) + b2_ref[0]


def _compress(xkv, pos, w1, b1, w2, b2):
    _, n_group, nc, width = xkv.shape
    per_kv = lambda a, g: (a, 0, 0)
    return pl.pallas_call(
        _compress_kernel,
        grid=(2, n_group),
        in_specs=[
            pl.BlockSpec((1, 1, nc, width), lambda a, g: (a, g, 0, 0)),
            pl.BlockSpec((1, 1, 2 * width), per_kv),
            pl.BlockSpec((1, 2 * width, 128), per_kv),
            pl.BlockSpec((1, 1, 128), per_kv),
            pl.BlockSpec((1, 128, HEAD_DIM), per_kv),
            pl.BlockSpec((1, 1, HEAD_DIM), per_kv),
        ],
        out_specs=pl.BlockSpec((1, 1, nc, HEAD_DIM), lambda a, g: (a, g, 0, 0)),
        out_shape=jax.ShapeDtypeStruct((2, n_group, nc, HEAD_DIM), F32),
        compiler_params=_cparams("parallel", "parallel"),
        name="compress_kv",
    )(xkv, pos, w1, b1, w2, b2)


def _operand_kernel(q_lo_ref, q_hi_ref, sel_ref, win_ref, qt_ref, ks_ref, vst_ref, kw_ref, vwt_ref):
    i = pl.program_id(0)
    n_tiles = pl.num_programs(0) - PAD_TILES
    rows = q_lo_ref.shape[0]
    lane = lax.broadcasted_iota(jnp.int32, (rows, 128), 1)
    pos = i * rows + lax.broadcasted_iota(jnp.int32, (rows, 128), 0)
    ones_rows = (lax.broadcasted_iota(jnp.int32, (V_AUG_ROWS - HEAD_DIM, rows), 0) == 0).astype(BF16)

    def key_pair(slab, idx):
        onehot = (lane - HEAD_DIM == idx).astype(F32)
        left = jnp.where(lane < HEAD_DIM, slab, onehot)
        right = jnp.where(lane < HEAD_DIM, pltpu.roll(slab, HEAD_DIM, 1), onehot)
        return left.astype(BF16), right.astype(BF16)

    def transposed_pair(slab):
        t = slab.T.astype(BF16)
        return t[:HEAD_DIM], t[HEAD_DIM:]

    def write_kv(src, keep, idx, k_ref, vt_ref):
        for pair in range(KV_GROUPS // 2):
            lanes = slice(pair * 128, (pair + 1) * 128)
            k_slab = jnp.where(keep, src[:, lanes].astype(F32), 0.0)
            v_slab = jnp.where(keep, src[:, KV_W + pair * 128:KV_W + (pair + 1) * 128].astype(F32), 0.0)
            for g, k, vt in zip((2 * pair, 2 * pair + 1), key_pair(k_slab, idx), transposed_pair(v_slab)):
                k_ref[g] = k
                vt_ref[g, 0:HEAD_DIM, :] = vt
                vt_ref[g, HEAD_DIM:, :] = ones_rows

    @pl.when(i < n_tiles)
    def _():
        for half, ref in enumerate((q_lo_ref, q_hi_ref)):
            for gg in range(2):
                for pp in range(HEADS_PER_GROUP // 2):
                    lanes = slice(gg * 256 + pp * 128, gg * 256 + (pp + 1) * 128)
                    for h, t in zip((2 * pp, 2 * pp + 1), transposed_pair(ref[:, lanes].astype(F32))):
                        qt_ref[2 * half + gg, 0, :, h * Q_BLOCK:(h + 1) * Q_BLOCK] = t
        write_kv(sel_ref, True, (pos // SEL_BLOCK) % (SEL_TILE // SEL_BLOCK), ks_ref, vst_ref)

    write_kv(win_ref, i >= PAD_TILES, (pos // WIN_CHUNK) % 8, kw_ref, vwt_ref)


def _attention_operands(proj):
    s = proj.shape[0]
    n_tiles = s // Q_BLOCK
    blk = MIX_W
    real = lambda i: jnp.minimum(i, n_tiles - 1)
    src = lambda c: pl.BlockSpec((Q_BLOCK, blk), lambda i: (real(i), c))
    return pl.pallas_call(
        _operand_kernel,
        grid=(n_tiles + PAD_TILES,),
        in_specs=[src(OFF_Q // blk), src(OFF_Q // blk + 1), src(OFF_KV // blk + 1),
                  pl.BlockSpec((Q_BLOCK, blk), lambda i: (jnp.maximum(i - PAD_TILES, 0), OFF_KV // blk + 2))],
        out_specs=[
            pl.BlockSpec((KV_GROUPS, 1, HEAD_DIM, QL), lambda i: (0, real(i), 0, 0)),
            pl.BlockSpec((KV_GROUPS, Q_BLOCK, KS_AUG_W), lambda i: (0, real(i), 0)),
            pl.BlockSpec((KV_GROUPS, V_AUG_ROWS, Q_BLOCK), lambda i: (0, 0, real(i))),
            pl.BlockSpec((KV_GROUPS, Q_BLOCK, KS_AUG_W), lambda i: (0, i, 0)),
            pl.BlockSpec((KV_GROUPS, V_AUG_ROWS, Q_BLOCK), lambda i: (0, 0, i)),
        ],
        out_shape=[
            jax.ShapeDtypeStruct((KV_GROUPS, n_tiles, HEAD_DIM, QL), BF16),
            jax.ShapeDtypeStruct((KV_GROUPS, s, KS_AUG_W), BF16),
            jax.ShapeDtypeStruct((KV_GROUPS, V_AUG_ROWS, s), BF16),
            jax.ShapeDtypeStruct((KV_GROUPS, s + WINDOW, KS_AUG_W), BF16),
            jax.ShapeDtypeStruct((KV_GROUPS, V_AUG_ROWS, s + WINDOW), BF16),
        ],
        compiler_params=_cparams("arbitrary"),
        name="attn_operands",
    )(proj, proj, proj, proj)


def _nsa_kernel(qt_ref, kc_ref, vct_ref, ks_ref, vst_ref, kw_ref, vwt_ref, gl_ref, o_ref,
                ps_scr, bias_scr, diag_bias_scr, qc_scr, qw_scr, qs_scr, sc_scr, s0_scr, s1_scr, p0_scr, p1_scr):
    qb = pl.program_id(1)
    t0 = qb * Q_BLOCK
    t0a = pl.multiple_of(t0, Q_BLOCK)
    qt = qt_ref[0, 0]
    nc = kc_ref.shape[1]
    n_sel = bias_scr.shape[0]
    q_rel = lax.broadcasted_iota(jnp.int32, (1, QL), 1) % Q_BLOCK
    tq = t0 + q_rel

    def bias_rows(cond, rows):
        return jnp.where(cond, NEG, 0.0).astype(BF16) + jnp.zeros((rows, QL), BF16)

    def flash_update(s, carry, vt):
        m_prev, acc = carry
        m_new = jnp.maximum(m_prev, jnp.max(s, axis=0, keepdims=True))
        p = jnp.exp2((s - m_new).astype(BF16))
        return m_new, jnp.exp2(m_prev - m_new) * acc + _dot(vt, p)

    init = (jnp.full((1, QL), NEG, F32), jnp.zeros((V_AUG_ROWS, QL), F32))

    grp = lax.broadcasted_iota(jnp.int32, (qc_scr.shape[0] - HEAD_DIM, 1), 0)
    visible = CMP_PER_QB * (qb + 1)
    qc_scr[0:HEAD_DIM, :] = qt
    qc_scr[HEAD_DIM:, :] = bias_rows((8 * grp >= visible) & (grp < nc // 8), qc_scr.shape[0] - HEAD_DIM)
    band = CMP_PER_QB + 8
    r0 = pl.multiple_of(jnp.maximum(visible - band, 0), 8)
    band_end = CMP_STRIDE * (r0 + lax.broadcasted_iota(jnp.int32, (band, 1), 0)) + (CMP_BLOCK - 1)

    def compressed(rows):
        def run():
            sc_scr[0:rows, :] = _dot(kc_ref[0, 0:rows, :], qc_scr[...])
            sc_scr[pl.ds(r0, band), :] = jnp.where(band_end <= tq, sc_scr[pl.ds(r0, band), :], NEG)
            s_c = sc_scr[0:rows, :]
            m_c = jnp.maximum(jnp.max(s_c, axis=0, keepdims=True), 0.1 * NEG)
            e_c = jnp.exp2(s_c - m_c)
            o_aug = _dot(vct_ref[0, :, 0:rows], e_c.astype(BF16))
            inv_c = 1.0 / jnp.maximum(o_aug[HEAD_DIM:HEAD_DIM + 1], 1e-30)
            p_grp = None
            for h in range(HEADS_PER_GROUP):
                lanes = slice(h * Q_BLOCK, (h + 1) * Q_BLOCK)
                p_h = e_c[:, lanes] * inv_c[:, lanes]
                p_grp = p_h if p_grp is None else p_grp + p_h
            for c in range(Q_BLOCK // 128):
                ps_scr[c, 8:8 + rows, :] = p_grp[:, c * 128:(c + 1) * 128]
                if rows < nc:
                    ps_scr[c, 8 + rows:, :] = jnp.zeros((nc - rows, 128), F32)
            return o_aug[:HEAD_DIM] * inv_c
        return run

    ps_scr[:, 0:8, :] = jnp.zeros((Q_BLOCK // 128, 8, 128), F32)
    quarter = nc // 4
    o_c = lax.cond(visible <= 2 * quarter,
                   lambda: lax.cond(visible <= quarter, compressed(quarter), compressed(2 * quarter)),
                   lambda: lax.cond(visible <= 3 * quarter, compressed(3 * quarter), compressed(nc)))
    imp = jnp.concatenate([sum(ps_scr[c, pl.ds(k, n_sel, stride=4), :] for k in range(7, 12))
                           for c in range(Q_BLOCK // 128)], axis=1)

    n_tri = Q_BLOCK // WIN_CHUNK
    n_chunk = WINDOW // WIN_CHUNK + n_tri
    span = n_chunk * WIN_CHUNK
    rho = lax.broadcasted_iota(jnp.int32, (16, 1), 0)
    first_real = WINDOW // WIN_CHUNK - n_tri * qb
    qw_scr[0:HEAD_DIM, :] = qt
    qw_scr[HEAD_DIM:HEAD_DIM + 16, :] = bias_rows((rho < 8) & (((rho - n_tri * qb) & 7) < first_real), 16)
    qw_scr[HEAD_DIM + 16:, :] = jnp.zeros((qw_scr.shape[0] - HEAD_DIM - 16, QL), BF16)
    s_w = _dot(kw_ref[0, pl.ds(t0a, span), :], qw_scr[...])
    i_rel = lax.broadcasted_iota(jnp.int32, (Q_BLOCK, 1), 0)
    s_w = jnp.concatenate([jnp.where(i_rel > q_rel, s_w[:Q_BLOCK], NEG), s_w[Q_BLOCK:WINDOW],
                           jnp.where(i_rel <= q_rel, s_w[WINDOW:], NEG)], axis=0)
    _, acc_w = flash_update(s_w, init, vwt_ref[0, :, pl.ds(t0a, span)])

    s0_scr[...] = _dot(ks_ref[0, 0:SEL_TILE, 0:HEAD_DIM], qt)

    j_idx = lax.broadcasted_iota(jnp.int32, (n_sel, Q_BLOCK), 0)
    cur = (t0 + lax.broadcasted_iota(jnp.int32, (n_sel, Q_BLOCK), 1)) // SEL_BLOCK
    forced = (j_idx == 0) | (j_idx == cur) | (j_idx == cur - 1)
    valid = j_idx <= cur
    score = jnp.where(forced, -2.0, jnp.where(valid, imp, -1.0))
    n_pick = SEL_TOPN - 3

    fast = score
    for _ in range(n_pick):
        fast = jnp.where(fast == jnp.max(fast, axis=0, keepdims=True), -2.0, fast)
    retired = (fast < -1.5) & valid & jnp.logical_not(forced)
    most_retired = jnp.max(jnp.sum(retired.astype(F32), axis=0, keepdims=True))

    def ranked_with_ties():
        def pick_one(_, sc):
            best = jnp.max(sc, axis=0, keepdims=True)
            first = jnp.min(jnp.where(sc == best, j_idx, n_sel), axis=0, keepdims=True)
            return jnp.where(j_idx == first, -2.0, sc)
        return lax.fori_loop(0, n_pick, pick_one, score)

    ranked = lax.cond(most_retired > n_pick, ranked_with_ties, lambda: fast)
    chosen = valid & (ranked < -1.5)
    diag_bias_scr[...] = jnp.where(chosen, 0.0, NEG)
    bias_scr[...] = jnp.where(chosen & (j_idx * SEL_BLOCK < t0), 0.0, NEG)

    blocks_per_tile = SEL_TILE // SEL_BLOCK
    last_tile = ks_ref.shape[1] // SEL_TILE - 1
    qs_scr[0:HEAD_DIM, :] = qt
    qs_scr[HEAD_DIM + 16:, :] = jnp.zeros((qs_scr.shape[0] - HEAD_DIM - 16, QL), BF16)

    def tile_start(kt):
        return pl.multiple_of(jnp.clip(kt, 0, last_tile) * SEL_TILE, SEL_TILE)

    def set_bias_rows(table, kt):
        b0 = pl.multiple_of(jnp.minimum(kt, last_tile) * blocks_per_tile, blocks_per_tile)
        b16 = jnp.concatenate([table[pl.ds(b0, blocks_per_tile), :],
                               jnp.zeros((16 - blocks_per_tile, Q_BLOCK), F32)], axis=0)
        qs_scr[HEAD_DIM:HEAD_DIM + 16, :] = jnp.concatenate([b16] * HEADS_PER_GROUP, axis=1).astype(BF16)

    set_bias_rows(diag_bias_scr, t0 // SEL_TILE)
    s_d = _dot(ks_ref[0, pl.ds(t0a, Q_BLOCK), :], qs_scr[...])
    m_d, acc_d = flash_update(jnp.where(i_rel <= q_rel, s_d, NEG), init, vst_ref[0, :, pl.ds(t0a, Q_BLOCK)])

    def half_step(kt, s_cur, s_nxt, p_cur, p_prev, carry):
        m_prev, alpha_prev, acc = carry
        acc = alpha_prev * acc + _dot(vst_ref[0, :, pl.ds(tile_start(kt - 1), SEL_TILE)], p_prev[...])
        m8 = jnp.max(s_cur[...].reshape(SEL_TILE // 8, 8, QL), axis=0)
        m_new = jnp.maximum(m_prev, jnp.max(m8, axis=0, keepdims=True))
        set_bias_rows(bias_scr, kt + 1)
        k_nxt = tile_start(kt + 1)
        for h in range(SEL_TILE // QK_PART):
            for c in range(QK_PART // P_CHUNK):
                rows = slice(h * QK_PART + c * P_CHUNK, h * QK_PART + (c + 1) * P_CHUNK)
                p_cur[rows, :] = jnp.exp2((s_cur[rows, :] - m_new).astype(BF16))
            part = slice(h * QK_PART, (h + 1) * QK_PART)
            s_nxt[part, :] = _dot(ks_ref[0, pl.ds(k_nxt + h * QK_PART, QK_PART), :], qs_scr[...])
        return m_new, jnp.exp2(m_prev - m_new), acc

    def pair_step(i, carry):
        carry = half_step(2 * i, s0_scr, s1_scr, p0_scr, p1_scr, carry)
        return half_step(2 * i + 1, s1_scr, s0_scr, p1_scr, p0_scr, carry)

    n_pairs = ((t0 + SEL_TILE - 1) // SEL_TILE + 1) // 2
    for b in range(blocks_per_tile):
        rows = slice(b * SEL_BLOCK, (b + 1) * SEL_BLOCK)
        s0_scr[rows, :] = s0_scr[rows, :] + jnp.concatenate([bias_scr[b:b + 1, :]] * HEADS_PER_GROUP, axis=1)
    p1_scr[...] = jnp.zeros(p1_scr.shape, BF16)
    carry = (m_d, jnp.ones((1, QL), F32), acc_d)
    _, alpha_last, acc_s = lax.fori_loop(0, n_pairs, pair_step, carry)
    acc_s = alpha_last * acc_s + _dot(vst_ref[0, :, pl.ds(tile_start(2 * n_pairs - 1), SEL_TILE)], p1_scr[...])

    gate = jax.nn.sigmoid(gl_ref[0, 0])
    o = (gate[0:1] * o_c + gate[1:2] * (acc_s[:HEAD_DIM] / acc_s[HEAD_DIM:HEAD_DIM + 1])
         + gate[2:3] * (acc_w[:HEAD_DIM] / acc_w[HEAD_DIM:HEAD_DIM + 1]))
    stacked = jnp.concatenate([o[:, p * Q_BLOCK:(p + 1) * Q_BLOCK] for p in range(HEADS_PER_GROUP)], axis=0)
    o_ref[...] = stacked.T.astype(o_ref.dtype)


def _sparse_attention(qt, kc_aug, vc_aug_t, ks_aug, vs_aug_t, kw_aug, vw_aug_t, gate_logits, s):
    n_group, n_qb = qt.shape[:2]
    nc = kc_aug.shape[1]
    n_sel = s // SEL_BLOCK
    assert HEAD_DIM + nc // 8 <= CMP_AUG_W and nc // 4 >= CMP_PER_QB + 8 and WINDOW // WIN_CHUNK + Q_BLOCK // WIN_CHUNK <= 8
    per_step = lambda g, i: (g, i, 0, 0)
    per_group = lambda shape: pl.BlockSpec((1,) + shape, lambda g, i: (g, 0, 0), pipeline_mode=pl.Buffered(1))
    tile_f32 = pltpu.VMEM((SEL_TILE, QL), F32)
    tile_bf16 = pltpu.VMEM((SEL_TILE, QL), BF16)
    sel_table = pltpu.VMEM((n_sel, Q_BLOCK), F32)
    return pl.pallas_call(
        _nsa_kernel,
        grid=(n_group, n_qb),
        in_specs=[
            pl.BlockSpec((1, 1, HEAD_DIM, QL), per_step),
            per_group((nc, CMP_AUG_W)),
            per_group((V_AUG_ROWS, nc)),
            per_group((s, KS_AUG_W)),
            per_group((V_AUG_ROWS, s)),
            per_group((s + WINDOW, KS_AUG_W)),
            per_group((V_AUG_ROWS, s + WINDOW)),
            pl.BlockSpec((1, 1, 3, QL), per_step),
        ],
        out_specs=pl.BlockSpec((Q_BLOCK, HEADS_PER_GROUP * HEAD_DIM), lambda g, i: (i, g)),
        out_shape=jax.ShapeDtypeStruct((s, Q_W), BF16),
        scratch_shapes=[pltpu.VMEM((Q_BLOCK // 128, 8 + nc, 128), F32), sel_table, sel_table,
                        pltpu.VMEM((CMP_AUG_W, QL), BF16), pltpu.VMEM((KS_AUG_W, QL), BF16),
                        pltpu.VMEM((KS_AUG_W, QL), BF16), pltpu.VMEM((nc, QL), F32),
                        tile_f32, tile_f32, tile_bf16, tile_bf16],
        compiler_params=_cparams("parallel", "arbitrary"),
        name="sparse_attn",
    )(qt, kc_aug, vc_aug_t, ks_aug, vs_aug_t, kw_aug, vw_aug_t, gate_logits)


def _merge_kernel(x_ref, ya_ref, yb_ref, yc_ref, g0_ref, g1_ref, g2_ref, wp_ref, wg_ref, wn_ref,
                  wo_ref, ng_ref, gate_ref, o_ref):
    merged = jax.nn.sigmoid(g0_ref[...].astype(F32)) * _dot(ya_ref[...], wp_ref[...])
    merged += jax.nn.sigmoid(g1_ref[...].astype(F32)) * _dot(yb_ref[...], wg_ref[...])
    merged += jax.nn.sigmoid(g2_ref[...].astype(F32)) * _dot(yc_ref[...], wn_ref[...])
    y = _dot(merged.astype(BF16), wo_ref[...])
    o_ref[...] = x_ref[...] + gate_ref[...] * _rms(y, ng_ref[...])


def _merge(x, ya, yb, yc, proj, wp, wg, wn, wo, norm_g, gate):
    s, d = x.shape
    tm = min(256, s)
    const = lambda i: (0, 0)
    rows = lambda i: (i, 0)
    whole = lambda a: pl.BlockSpec(a.shape, const, pipeline_mode=pl.Buffered(1))
    bg = OFF_BG // d
    return pl.pallas_call(
        _merge_kernel,
        grid=(s // tm,),
        in_specs=[
            pl.BlockSpec((tm, d), rows),
            pl.BlockSpec((tm, MIX_W), rows),
            pl.BlockSpec((tm, MIX_W), rows),
            pl.BlockSpec((tm, Q_W), rows),
            pl.BlockSpec((tm, d), lambda i: (i, bg)),
            pl.BlockSpec((tm, d), lambda i: (i, bg + 1)),
            pl.BlockSpec((tm, d), lambda i: (i, bg + 2)),
            whole(wp), whole(wg), whole(wn), whole(wo),
            pl.BlockSpec((1, d), const),
            pl.BlockSpec((1, d), const),
        ],
        out_specs=pl.BlockSpec((tm, d), rows),
        out_shape=jax.ShapeDtypeStruct((s, d), F32),
        compiler_params=_cparams("parallel"),
        name="merge_out",
    )(x, ya, yb, yc, proj, proj, proj, wp, wg, wn, wo, norm_g, gate)


def _ffn_kernel(x_ref, gi_ref, sc_ref, sh_ref, w1_ref, w2_ref, go_ref, gate_ref, o_ref, h_scr):
    j = pl.program_id(1)

    @pl.when(j == 0)
    def _():
        h = _rms(x_ref[...], gi_ref[...]) * (1.0 + sc_ref[...]) + sh_ref[...]
        h_scr[...] = h.astype(BF16)
        o_ref[...] = jnp.zeros_like(o_ref)

    a = jnp.square(jnp.maximum(_dot(h_scr[...], w1_ref[...]), 0.0))
    o_ref[...] += _dot(a.astype(BF16), w2_ref[...])

    @pl.when(j == pl.num_programs(1) - 1)
    def _():
        o_ref[...] = x_ref[...] + gate_ref[...] * _rms(o_ref[...], go_ref[...])


def _ffn(x, g_in, scale, shift, w1, w2, g_out, gate):
    s, d = x.shape
    tm, tf = min(1024, s), 512
    const = lambda i, j: (0, 0)
    rows = lambda i, j: (i, 0)
    return pl.pallas_call(
        _ffn_kernel,
        grid=(s // tm, D_FF // tf),
        in_specs=[
            pl.BlockSpec((tm, d), rows, pipeline_mode=pl.Buffered(1)),
            pl.BlockSpec((1, d), const),
            pl.BlockSpec((1, d), const),
            pl.BlockSpec((1, d), const),
            pl.BlockSpec((d, tf), lambda i, j: (0, j)),
            pl.BlockSpec((tf, d), lambda i, j: (j, 0)),
            pl.BlockSpec((1, d), const),
            pl.BlockSpec((1, d), const),
        ],
        out_specs=pl.BlockSpec((tm, d), rows),
        out_shape=jax.ShapeDtypeStruct((s, d), F32),
        scratch_shapes=[pltpu.VMEM((tm, d), BF16)],
        compiler_params=_cparams("parallel", "arbitrary"),
        name="ffn",
    )(x, g_in, scale, shift, w1, w2, g_out, gate)


def _token_mixing(x, mod, norm_g, w_in, pool_w, pool_scale, ln_g, ln_b, ws, bs, cmp_pos, cmp_w1,
                  cmp_b1, cmp_w2, cmp_b2, w_br_pool, w_br_gmlp, w_br_nsa, w_out):
    s, d = x.shape
    n_qb = s // Q_BLOCK
    ng0 = OFF_BG
    w_main = jnp.concatenate([w_in[:, :ng0], w_in[:, ng0 + N_GATE:]], axis=1).astype(BF16)
    w_ng = jnp.pad(w_in[:, ng0:ng0 + N_GATE], ((0, 0), (0, 128 - N_GATE))).astype(BF16)
    row = lambda v: v.reshape(1, -1)

    col_scale = jnp.ones((1, PROJ_W), F32).at[:, OFF_Q:OFF_KV].set(HEAD_DIM ** -0.5 * LOG2E)
    proj, ngate = _in_projection(x, row(norm_g[0]), row(mod[1]), row(mod[0]), w_main, w_ng, col_scale)

    ya, yb = _mixers(proj, pool_w.astype(BF16), row(pool_scale), row(ln_g), row(ln_b), ws, bs.T)

    qt, ks_aug, vs_aug_t, kw_aug, vw_aug_t = _attention_operands(proj)
    kv_cmp = proj[:, OFF_KV:OFF_KV + 2 * KV_W].reshape(s, 2, KV_GROUPS, HEAD_DIM).transpose(1, 2, 0, 3)
    xkv = kv_cmp.reshape(2, KV_GROUPS, s // CMP_STRIDE, CMP_STRIDE * HEAD_DIM)
    gl = ngate[:, :N_GATE].reshape(n_qb, Q_BLOCK, KV_GROUPS, HEADS_PER_GROUP, 3).transpose(2, 0, 4, 3, 1)
    gl = gl.reshape(KV_GROUPS, n_qb, 3, QL)

    cmp = _compress(xkv, cmp_pos.reshape(2, 1, CMP_BLOCK * HEAD_DIM), cmp_w1.astype(BF16),
                    cmp_b1.reshape(2, 1, -1), cmp_w2.astype(BF16), cmp_b2.reshape(2, 1, -1))
    def with_onehot(k, idx, width):
        onehot = (idx[:, None] == jnp.arange(width - HEAD_DIM)[None, :]).astype(BF16)
        return jnp.concatenate([k, jnp.broadcast_to(onehot, k.shape[:-1] + onehot.shape[-1:])], axis=-1)

    def with_ones_row(vt):
        extra = jnp.zeros(vt.shape[:1] + (V_AUG_ROWS - HEAD_DIM,) + vt.shape[2:], BF16).at[:, 0].set(1.0)
        return jnp.concatenate([vt, extra], axis=1)

    nc = s // CMP_STRIDE
    kc_aug = with_onehot(cmp[0].astype(BF16), jnp.arange(nc) // 8, CMP_AUG_W)
    vc_aug_t = with_ones_row(cmp[1].transpose(0, 2, 1).astype(BF16))
    yc = _sparse_attention(qt, kc_aug, vc_aug_t, ks_aug, vs_aug_t, kw_aug, vw_aug_t, gl, s)

    return _merge(x, ya, yb, yc, proj, w_br_pool.astype(BF16), w_br_gmlp.astype(BF16),
                  w_br_nsa.astype(BF16), w_out.astype(BF16), row(norm_g[1]), row(mod[2]))


def kernel(x, c, norm_g, w_ada, b_ada, w_in, pool_w, pool_scale, gmlp_ln_g, gmlp_ln_b, gmlp_ws, gmlp_bs,
           cmp_pos, cmp_w1, cmp_b1, cmp_w2, cmp_b2, w_br_pool, w_br_gmlp, w_br_nsa, w_out, w_ff1, w_ff2):
    b, s, d = x.shape
    assert b == 1 and d == D_MODEL and s % 1024 == 0
    n_layer = w_ada.shape[0]
    mod_all = _modulation(c, w_ada, b_ada).reshape(n_layer, 6, d)
    xs = x[0]
    row = lambda v: v.reshape(1, -1)
    for l in range(n_layer):
        mod = mod_all[l]
        xs = _token_mixing(xs, mod, norm_g[l], w_in[l], pool_w[l], pool_scale[l], gmlp_ln_g[l],
                           gmlp_ln_b[l], gmlp_ws[l], gmlp_bs[l], cmp_pos[l], cmp_w1[l], cmp_b1[l],
                           cmp_w2[l], cmp_b2[l], w_br_pool[l], w_br_gmlp[l], w_br_nsa[l], w_out[l])
        xs = _ffn(xs, row(norm_g[l, 2]), row(mod[4]), row(mod[3]), w_ff1[l].astype(BF16),
                  w_ff2[l].astype(BF16), row(norm_g[l, 3]), row(mod[5]))
    return xs[None]
```

```python
import jax
import jax.numpy as jnp
from jax import lax
from jax.experimental import pallas as pl
from jax.experimental.pallas import tpu as pltpu

F32 = jnp.float32
BF16 = jnp.bfloat16

D_MODEL = 2048
POOL_WINDOWS = (2, 4, 8, 16)
POOL_HALO = 16
GROUP_CH = 128
MIX_W = 4 * GROUP_CH
GMLP_CHUNK = 128
HEAD_DIM = 64
KV_GROUPS = 4
HEADS_PER_GROUP = 4
Q_W = 16 * HEAD_DIM
KV_W = KV_GROUPS * HEAD_DIM
CMP_BLOCK = 32
CMP_STRIDE = 16
SEL_BLOCK = 64
SEL_TOPN = 16
WINDOW = 512
FORCE_BONUS = 1000.0
LOG2E = 1.4426950408889634
NEG = -1e30
N_GATE = 3 * 16
D_FF = 4 * D_MODEL

Q_BLOCK = 256
QL = HEADS_PER_GROUP * Q_BLOCK
CMP_PER_QB = Q_BLOCK // CMP_STRIDE
WIN_CHUNK = 128
PAD_TILES = WINDOW // Q_BLOCK
SEL_TILE = 512
KS_AUG_W = 128
CMP_AUG_W = 256
V_AUG_ROWS = HEAD_DIM + 16
P_CHUNK = 64
QK_PART = 256

OFF_POOL, OFF_U, OFF_V, OFF_Q, OFF_KV, OFF_BG = 0, 512, 1024, 1536, 2560, 4096
PROJ_W = OFF_BG + 3 * D_MODEL

VMEM_LIMIT = 56 * 1024 * 1024


def _cparams(*sem):
    return pltpu.CompilerParams(dimension_semantics=sem, vmem_limit_bytes=VMEM_LIMIT)


def _dot(a, b):
    return jnp.dot(a, b, preferred_element_type=F32)


def _rms(x, g):
    return x * lax.rsqrt(jnp.mean(x * x, axis=-1, keepdims=True) + 1e-6) * g


def _mod_kernel(c_ref, w_ref, b_ref, o_ref):
    c = c_ref[...]
    act = c * jax.nn.sigmoid(c)
    o_ref[0] = jnp.sum(act * w_ref[0], axis=0, keepdims=True) + b_ref[0]


def _modulation(c, w_ada, b_ada):
    n_layer, d, n_out = w_ada.shape
    tn = 1024
    return pl.pallas_call(
        _mod_kernel,
        grid=(n_layer, n_out // tn),
        in_specs=[
            pl.BlockSpec((d, 1), lambda l, j: (0, 0)),
            pl.BlockSpec((1, d, tn), lambda l, j: (l, 0, j)),
            pl.BlockSpec((1, 1, tn), lambda l, j: (l, 0, j)),
        ],
        out_specs=pl.BlockSpec((1, 1, tn), lambda l, j: (l, 0, j)),
        out_shape=jax.ShapeDtypeStruct((n_layer, 1, n_out), F32),
        compiler_params=_cparams("parallel", "parallel"),
        name="adaln_mod",
    )(c.reshape(d, 1), w_ada, b_ada.reshape(n_layer, 1, n_out))


def _inproj_kernel(x_ref, g_ref, sc_ref, sh_ref, w_ref, wng_ref, cs_ref, o_ref, ng_ref, h_scr):
    @pl.when(pl.program_id(1) == 0)
    def _():
        h = _rms(x_ref[...], g_ref[...]) * (1.0 + sc_ref[...]) + sh_ref[...]
        hb = h.astype(BF16)
        h_scr[...] = hb
        ng_ref[...] = _dot(hb, wng_ref[...])

    o_ref[...] = (_dot(h_scr[...], w_ref[...]) * cs_ref[...]).astype(o_ref.dtype)


def _in_projection(x, g, scale, shift, w_main, w_ng, col_scale):
    s, d = x.shape
    tm, tn = min(1024, s), 1024
    row = lambda i, j: (0, 0)
    return pl.pallas_call(
        _inproj_kernel,
        grid=(s // tm, PROJ_W // tn),
        in_specs=[
            pl.BlockSpec((tm, d), lambda i, j: (i, 0)),
            pl.BlockSpec((1, d), row),
            pl.BlockSpec((1, d), row),
            pl.BlockSpec((1, d), row),
            pl.BlockSpec((d, tn), lambda i, j: (0, j)),
            pl.BlockSpec((d, 128), row),
            pl.BlockSpec((1, tn), lambda i, j: (0, j)),
        ],
        out_specs=[
            pl.BlockSpec((tm, tn), lambda i, j: (i, j)),
            pl.BlockSpec((tm, 128), lambda i, j: (i, 0)),
        ],
        out_shape=[
            jax.ShapeDtypeStruct((s, PROJ_W), BF16),
            jax.ShapeDtypeStruct((s, 128), F32),
        ],
        scratch_shapes=[pltpu.VMEM((tm, d), BF16)],
        compiler_params=_cparams("parallel", "arbitrary"),
        name="in_proj",
    )(x, g, scale, shift, w_main, w_ng, col_scale)


def _mixer_kernel(a_ref, halo_ref, u_ref, v_ref, pw_ref, ps_ref, lg_ref, lb_ref, ws_ref, bs_ref,
                  ya_ref, yb_ref):
    i = pl.program_id(0)
    tm = a_ref.shape[0]
    a = a_ref[...].astype(F32)
    halo = jnp.where(i > 0, halo_ref[...].astype(F32), 0.0)
    ext = jnp.concatenate([halo, a], axis=0)
    p2 = ext[1:] + ext[:-1]
    p4 = p2[2:] + p2[:-2]
    p8 = p4[4:] + p4[:-4]
    p16 = p8[8:] + p8[:-8]
    sums = (p2[15:15 + tm], p4[13:13 + tm], p8[9:9 + tm], p16[1:1 + tm])
    t = i * tm + lax.broadcasted_iota(jnp.int32, (tm, 1), 0)
    for gi, w in enumerate(POOL_WINDOWS):
        cols = slice(gi * GROUP_CH, (gi + 1) * GROUP_CH)
        cnt = jnp.minimum(t + 1, w).astype(F32)
        pooled = sums[gi][:, cols] / cnt - a[:, cols]
        y = _dot(pooled.astype(BF16), pw_ref[gi])
        ya_ref[:, cols] = (y * ps_ref[:, cols]).astype(ya_ref.dtype)

    u = jax.nn.gelu(u_ref[...].astype(F32))
    v = jax.nn.gelu(v_ref[...].astype(F32))
    mu = jnp.mean(v, axis=-1, keepdims=True)
    var = jnp.mean(jnp.square(v - mu), axis=-1, keepdims=True)
    vn = ((v - mu) * lax.rsqrt(var + 1e-5) * lg_ref[...] + lb_ref[...]).astype(BF16)
    r = lax.broadcasted_iota(jnp.int32, (GMLP_CHUNK, GMLP_CHUNK), 0)
    c = lax.broadcasted_iota(jnp.int32, (GMLP_CHUNK, GMLP_CHUNK), 1)
    for gi in range(4):
        cols = slice(gi * GROUP_CH, (gi + 1) * GROUP_CH)
        wsm = jnp.where(r >= c, ws_ref[gi], 0.0).astype(BF16)
        bias = bs_ref[:, gi:gi + 1]
        for ck in range(tm // GMLP_CHUNK):
            rows = slice(ck * GMLP_CHUNK, (ck + 1) * GMLP_CHUNK)
            mixed = _dot(wsm, vn[rows, cols]) + bias
            yb_ref[rows, cols] = (u[rows, cols] * mixed).astype(yb_ref.dtype)


def _mixers(proj, pool_w, pool_scale, ln_g, ln_b, ws, bs_t):
    s = proj.shape[0]
    tm = min(512, s)
    hb = tm // POOL_HALO
    const2 = lambda i: (0, 0)
    const3 = lambda i: (0, 0, 0)
    return pl.pallas_call(
        _mixer_kernel,
        grid=(s // tm,),
        in_specs=[
            pl.BlockSpec((tm, MIX_W), lambda i: (i, OFF_POOL // MIX_W)),
            pl.BlockSpec((POOL_HALO, MIX_W), lambda i: (jnp.maximum(i * hb - 1, 0), OFF_POOL // MIX_W)),
            pl.BlockSpec((tm, MIX_W), lambda i: (i, OFF_U // MIX_W)),
            pl.BlockSpec((tm, MIX_W), lambda i: (i, OFF_V // MIX_W)),
            pl.BlockSpec((4, GROUP_CH, GROUP_CH), const3),
            pl.BlockSpec((1, MIX_W), const2),
            pl.BlockSpec((1, MIX_W), const2),
            pl.BlockSpec((1, MIX_W), const2),
            pl.BlockSpec((4, GMLP_CHUNK, GMLP_CHUNK), const3),
            pl.BlockSpec((GMLP_CHUNK, 4), const2),
        ],
        out_specs=[
            pl.BlockSpec((tm, MIX_W), lambda i: (i, 0)),
            pl.BlockSpec((tm, MIX_W), lambda i: (i, 0)),
        ],
        out_shape=[jax.ShapeDtypeStruct((s, MIX_W), BF16)] * 2,
        compiler_params=_cparams("parallel"),
        name="mixers",
    )(proj, proj, proj, proj, pool_w, pool_scale, ln_g, ln_b, ws, bs_t)


def _compress_kernel(x_ref, pos_ref, w1_ref, b1_ref, w2_ref, b2_ref, kc_ref, vct_ref):
    half = CMP_STRIDE * HEAD_DIM
    x = x_ref[0, 0].astype(F32)
    nc = x.shape[0]
    pos = pos_ref[0]
    first = _dot((x + pos[:, :half]).astype(BF16), w1_ref[0, :half, :])
    second = _dot((x + pos[:, half:]).astype(BF16), w1_ref[0, half:, :])
    hid = jax.nn.gelu(first + pltpu.roll(second, nc - 1, 0) + b1_ref[0])
    out = _dot(hid.astype(BF16), w2_ref[0]) + b2_ref[0]

    @pl.when(pl.program_id(1) == 0)
    def _():
        lane = lax.broadcasted_iota(jnp.int32, (nc, 128), 1)
        grp = lax.broadcasted_iota(jnp.int32, (nc, 128), 0) // 8
        kc_ref[0, :, 0:128] = (out + (lane - HEAD_DIM == grp).astype(F32)).astype(BF16)
        kc_ref[0, :, 128:] = (lane + (128 - HEAD_DIM) == grp).astype(BF16)

    @pl.when(pl.program_id(1) == 1)
    def _():
        ones_row = lax.broadcasted_iota(jnp.int32, (V_AUG_ROWS, nc), 0) == HEAD_DIM
        vct_ref[0] = (out.T[0:V_AUG_ROWS] + ones_row.astype(F32)).astype(BF16)


def _compress(xkv, pos, w1, b1, w2, b2):
    _, n_group, nc, width = xkv.shape
    assert CMP_AUG_W == 256 and HEAD_DIM + nc // 8 <= CMP_AUG_W
    per_kv = lambda g, a: (a, 0, 0)
    per_group = lambda g, a: (g, 0, 0)
    return pl.pallas_call(
        _compress_kernel,
        grid=(n_group, 2),
        in_specs=[
            pl.BlockSpec((1, 1, nc, width), lambda g, a: (a, g, 0, 0)),
            pl.BlockSpec((1, 1, 2 * width), per_kv),
            pl.BlockSpec((1, 2 * width, 128), per_kv),
            pl.BlockSpec((1, 1, 128), per_kv),
            pl.BlockSpec((1, 128, 128), per_kv),
            pl.BlockSpec((1, 1, 128), per_kv),
        ],
        out_specs=[pl.BlockSpec((1, nc, CMP_AUG_W), per_group), pl.BlockSpec((1, V_AUG_ROWS, nc), per_group)],
        out_shape=[jax.ShapeDtypeStruct((n_group, nc, CMP_AUG_W), BF16),
                   jax.ShapeDtypeStruct((n_group, V_AUG_ROWS, nc), BF16)],
        compiler_params=_cparams("parallel", "arbitrary"),
        name="compress_kv",
    )(xkv, pos, w1, b1, w2, b2)


def _operand_kernel(q_lo_ref, q_hi_ref, sel_ref, win_ref, ng_ref, qt_ref, ks_ref, vst_ref, kw_ref, vwt_ref, gl_ref):
    i = pl.program_id(0)
    n_tiles = pl.num_programs(0) - PAD_TILES
    rows = q_lo_ref.shape[0]
    lane = lax.broadcasted_iota(jnp.int32, (rows, 128), 1)
    pos = i * rows + lax.broadcasted_iota(jnp.int32, (rows, 128), 0)
    ones_rows = (lax.broadcasted_iota(jnp.int32, (V_AUG_ROWS - HEAD_DIM, rows), 0) == 0).astype(BF16)

    def key_pair(slab, idx):
        onehot = (lane - HEAD_DIM == idx).astype(F32)
        left = jnp.where(lane < HEAD_DIM, slab, onehot)
        right = jnp.where(lane < HEAD_DIM, pltpu.roll(slab, HEAD_DIM, 1), onehot)
        return left.astype(BF16), right.astype(BF16)

    def transposed_pair(slab):
        t = slab.T.astype(BF16)
        return t[:HEAD_DIM], t[HEAD_DIM:]

    def write_kv(src, keep, idx, k_ref, vt_ref):
        for pair in range(KV_GROUPS // 2):
            lanes = slice(pair * 128, (pair + 1) * 128)
            k_slab = jnp.where(keep, src[:, lanes].astype(F32), 0.0)
            v_slab = jnp.where(keep, src[:, KV_W + pair * 128:KV_W + (pair + 1) * 128].astype(F32), 0.0)
            for g, k, vt in zip((2 * pair, 2 * pair + 1), key_pair(k_slab, idx), transposed_pair(v_slab)):
                k_ref[g] = k
                vt_ref[g, 0:HEAD_DIM, :] = vt
                vt_ref[g, HEAD_DIM:, :] = ones_rows

    @pl.when(i < n_tiles)
    def _():
        for half, ref in enumerate((q_lo_ref, q_hi_ref)):
            for gg in range(2):
                for pp in range(HEADS_PER_GROUP // 2):
                    lanes = slice(gg * 256 + pp * 128, gg * 256 + (pp + 1) * 128)
                    for h, t in zip((2 * pp, 2 * pp + 1), transposed_pair(ref[:, lanes].astype(F32))):
                        qt_ref[2 * half + gg, 0, :, h * Q_BLOCK:(h + 1) * Q_BLOCK] = t
        write_kv(sel_ref, True, (pos // SEL_BLOCK) % (SEL_TILE // SEL_BLOCK), ks_ref, vst_ref)
        logits_t = ng_ref[...].T
        for g in range(KV_GROUPS):
            for h in range(HEADS_PER_GROUP):
                for br in range(3):
                    r = (g * HEADS_PER_GROUP + h) * 3 + br
                    gl_ref[g, 0, br:br + 1, h * Q_BLOCK:(h + 1) * Q_BLOCK] = logits_t[r:r + 1, :]

    write_kv(win_ref, i >= PAD_TILES, (pos // WIN_CHUNK) % 8, kw_ref, vwt_ref)


def _attention_operands(proj, ngate):
    s = proj.shape[0]
    n_tiles = s // Q_BLOCK
    blk = MIX_W
    real = lambda i: jnp.minimum(i, n_tiles - 1)
    src = lambda c: pl.BlockSpec((Q_BLOCK, blk), lambda i: (real(i), c))
    return pl.pallas_call(
        _operand_kernel,
        grid=(n_tiles + PAD_TILES,),
        in_specs=[src(OFF_Q // blk), src(OFF_Q // blk + 1), src(OFF_KV // blk + 1),
                  pl.BlockSpec((Q_BLOCK, blk), lambda i: (jnp.maximum(i - PAD_TILES, 0), OFF_KV // blk + 2)),
                  pl.BlockSpec((Q_BLOCK, 128), lambda i: (real(i), 0))],
        out_specs=[
            pl.BlockSpec((KV_GROUPS, 1, HEAD_DIM, QL), lambda i: (0, real(i), 0, 0)),
            pl.BlockSpec((KV_GROUPS, Q_BLOCK, KS_AUG_W), lambda i: (0, real(i), 0)),
            pl.BlockSpec((KV_GROUPS, V_AUG_ROWS, Q_BLOCK), lambda i: (0, 0, real(i))),
            pl.BlockSpec((KV_GROUPS, Q_BLOCK, KS_AUG_W), lambda i: (0, i, 0)),
            pl.BlockSpec((KV_GROUPS, V_AUG_ROWS, Q_BLOCK), lambda i: (0, 0, i)),
            pl.BlockSpec((KV_GROUPS, 1, 3, QL), lambda i: (0, real(i), 0, 0)),
        ],
        out_shape=[
            jax.ShapeDtypeStruct((KV_GROUPS, n_tiles, HEAD_DIM, QL), BF16),
            jax.ShapeDtypeStruct((KV_GROUPS, s, KS_AUG_W), BF16),
            jax.ShapeDtypeStruct((KV_GROUPS, V_AUG_ROWS, s), BF16),
            jax.ShapeDtypeStruct((KV_GROUPS, s + WINDOW, KS_AUG_W), BF16),
            jax.ShapeDtypeStruct((KV_GROUPS, V_AUG_ROWS, s + WINDOW), BF16),
            jax.ShapeDtypeStruct((KV_GROUPS, n_tiles, 3, QL), F32),
        ],
        compiler_params=_cparams("arbitrary"),
        name="attn_operands",
    )(proj, proj, proj, proj, ngate)


def _nsa_kernel(qt_ref, kc_ref, vct_ref, ks_ref, vst_ref, kw_ref, vwt_ref, gl_ref, o_ref,
                ps_scr, bias_scr, diag_bias_scr, qc_scr, qw_scr, qs_scr, sc_scr, s0_scr, s1_scr, p0_scr, p1_scr):
    qb = pl.program_id(1)
    t0 = qb * Q_BLOCK
    t0a = pl.multiple_of(t0, Q_BLOCK)
    qt = qt_ref[0, 0]
    nc = kc_ref.shape[1]
    n_sel = bias_scr.shape[0]
    q_rel = lax.broadcasted_iota(jnp.int32, (1, QL), 1) % Q_BLOCK
    tq = t0 + q_rel

    def bias_rows(cond, rows):
        return jnp.where(cond, NEG, 0.0).astype(BF16) + jnp.zeros((rows, QL), BF16)

    def flash_update(s, carry, vt):
        m_prev, acc = carry
        m_new = jnp.maximum(m_prev, jnp.max(s, axis=0, keepdims=True))
        p = jnp.exp2((s - m_new).astype(BF16))
        return m_new, jnp.exp2(m_prev - m_new) * acc + _dot(vt, p)

    init = (jnp.full((1, QL), NEG, F32), jnp.zeros((V_AUG_ROWS, QL), F32))

    grp = lax.broadcasted_iota(jnp.int32, (qc_scr.shape[0] - HEAD_DIM, 1), 0)
    visible = CMP_PER_QB * (qb + 1)
    qc_scr[0:HEAD_DIM, :] = qt
    qc_scr[HEAD_DIM:, :] = bias_rows((8 * grp >= visible) & (grp < nc // 8), qc_scr.shape[0] - HEAD_DIM)
    band = CMP_PER_QB + 8
    r0 = pl.multiple_of(jnp.maximum(visible - band, 0), 8)
    band_end = CMP_STRIDE * (r0 + lax.broadcasted_iota(jnp.int32, (band, 1), 0)) + (CMP_BLOCK - 1)

    def compressed(rows):
        def run():
            sc_scr[0:rows, :] = _dot(kc_ref[0, 0:rows, :], qc_scr[...])
            sc_scr[pl.ds(r0, band), :] = jnp.where(band_end <= tq, sc_scr[pl.ds(r0, band), :], NEG)
            s_c = sc_scr[0:rows, :]
            m_c = jnp.maximum(jnp.max(s_c, axis=0, keepdims=True), 0.1 * NEG)
            e_c = jnp.exp2(s_c - m_c)
            o_aug = _dot(vct_ref[0, :, 0:rows], e_c.astype(BF16))
            inv_c = 1.0 / jnp.maximum(o_aug[HEAD_DIM:HEAD_DIM + 1], 1e-30)
            p_grp = None
            for h in range(HEADS_PER_GROUP):
                lanes = slice(h * Q_BLOCK, (h + 1) * Q_BLOCK)
                p_h = e_c[:, lanes] * inv_c[:, lanes]
                p_grp = p_h if p_grp is None else p_grp + p_h
            for c in range(Q_BLOCK // 128):
                ps_scr[c, 8:8 + rows, :] = p_grp[:, c * 128:(c + 1) * 128]
                if rows < nc:
                    ps_scr[c, 8 + rows:, :] = jnp.zeros((nc - rows, 128), F32)
            return o_aug[:HEAD_DIM] * inv_c
        return run

    ps_scr[:, 0:8, :] = jnp.zeros((Q_BLOCK // 128, 8, 128), F32)
    quarter = nc // 4
    o_c = lax.cond(visible <= 2 * quarter,
                   lambda: lax.cond(visible <= quarter, compressed(quarter), compressed(2 * quarter)),
                   lambda: lax.cond(visible <= 3 * quarter, compressed(3 * quarter), compressed(nc)))
    imp = jnp.concatenate([sum(ps_scr[c, pl.ds(k, n_sel, stride=4), :] for k in range(7, 12))
                           for c in range(Q_BLOCK // 128)], axis=1)

    n_tri = Q_BLOCK // WIN_CHUNK
    n_chunk = WINDOW // WIN_CHUNK + n_tri
    span = n_chunk * WIN_CHUNK
    rho = lax.broadcasted_iota(jnp.int32, (16, 1), 0)
    first_real = WINDOW // WIN_CHUNK - n_tri * qb
    qw_scr[0:HEAD_DIM, :] = qt
    qw_scr[HEAD_DIM:HEAD_DIM + 16, :] = bias_rows((rho < 8) & (((rho - n_tri * qb) & 7) < first_real), 16)
    qw_scr[HEAD_DIM + 16:, :] = jnp.zeros((qw_scr.shape[0] - HEAD_DIM - 16, QL), BF16)
    s_w = _dot(kw_ref[0, pl.ds(t0a, span), :], qw_scr[...])
    i_rel = lax.broadcasted_iota(jnp.int32, (Q_BLOCK, 1), 0)
    s_w = jnp.concatenate([jnp.where(i_rel > q_rel, s_w[:Q_BLOCK], NEG), s_w[Q_BLOCK:WINDOW],
                           jnp.where(i_rel <= q_rel, s_w[WINDOW:], NEG)], axis=0)
    _, acc_w = flash_update(s_w, init, vwt_ref[0, :, pl.ds(t0a, span)])

    s0_scr[...] = _dot(ks_ref[0, 0:SEL_TILE, 0:HEAD_DIM], qt)

    j_idx = lax.broadcasted_iota(jnp.int32, (n_sel, Q_BLOCK), 0)
    cur = (t0 + lax.broadcasted_iota(jnp.int32, (n_sel, Q_BLOCK), 1)) // SEL_BLOCK
    forced = (j_idx == 0) | (j_idx == cur) | (j_idx == cur - 1)
    valid = j_idx <= cur
    score = jnp.where(forced, -2.0, jnp.where(valid, imp, -1.0))
    n_pick = SEL_TOPN - 3

    fast = score
    for _ in range(n_pick):
        fast = jnp.where(fast == jnp.max(fast, axis=0, keepdims=True), -2.0, fast)
    retired = (fast < -1.5) & valid & jnp.logical_not(forced)
    most_retired = jnp.max(jnp.sum(retired.astype(F32), axis=0, keepdims=True))

    def ranked_with_ties():
        def pick_one(_, sc):
            best = jnp.max(sc, axis=0, keepdims=True)
            first = jnp.min(jnp.where(sc == best, j_idx, n_sel), axis=0, keepdims=True)
            return jnp.where(j_idx == first, -2.0, sc)
        return lax.fori_loop(0, n_pick, pick_one, score)

    ranked = lax.cond(most_retired > n_pick, ranked_with_ties, lambda: fast)
    chosen = valid & (ranked < -1.5)
    diag_bias_scr[...] = jnp.where(chosen, 0.0, NEG)
    bias_scr[...] = jnp.where(chosen & (j_idx * SEL_BLOCK < t0), 0.0, NEG)

    blocks_per_tile = SEL_TILE // SEL_BLOCK
    last_tile = ks_ref.shape[1] // SEL_TILE - 1
    qs_scr[0:HEAD_DIM, :] = qt
    qs_scr[HEAD_DIM + 16:, :] = jnp.zeros((qs_scr.shape[0] - HEAD_DIM - 16, QL), BF16)

    def tile_start(kt):
        return pl.multiple_of(jnp.clip(kt, 0, last_tile) * SEL_TILE, SEL_TILE)

    def set_bias_rows(table, kt):
        b0 = pl.multiple_of(jnp.minimum(kt, last_tile) * blocks_per_tile, blocks_per_tile)
        b16 = jnp.concatenate([table[pl.ds(b0, blocks_per_tile), :],
                               jnp.zeros((16 - blocks_per_tile, Q_BLOCK), F32)], axis=0)
        qs_scr[HEAD_DIM:HEAD_DIM + 16, :] = jnp.concatenate([b16] * HEADS_PER_GROUP, axis=1).astype(BF16)

    set_bias_rows(diag_bias_scr, t0 // SEL_TILE)
    s_d = _dot(ks_ref[0, pl.ds(t0a, Q_BLOCK), :], qs_scr[...])
    m_d, acc_d = flash_update(jnp.where(i_rel <= q_rel, s_d, NEG), init, vst_ref[0, :, pl.ds(t0a, Q_BLOCK)])

    def half_step(kt, s_cur, s_nxt, p_cur, p_prev, carry):
        m_prev, alpha_prev, acc = carry
        acc = alpha_prev * acc + _dot(vst_ref[0, :, pl.ds(tile_start(kt - 1), SEL_TILE)], p_prev[...])
        m8 = jnp.max(s_cur[...].reshape(SEL_TILE // 8, 8, QL), axis=0)
        m_new = jnp.maximum(m_prev, jnp.max(m8, axis=0, keepdims=True))
        set_bias_rows(bias_scr, kt + 1)
        k_nxt = tile_start(kt + 1)
        for h in range(SEL_TILE // QK_PART):
            for c in range(QK_PART // P_CHUNK):
                rows = slice(h * QK_PART + c * P_CHUNK, h * QK_PART + (c + 1) * P_CHUNK)
                p_cur[rows, :] = jnp.exp2((s_cur[rows, :] - m_new).astype(BF16))
            part = slice(h * QK_PART, (h + 1) * QK_PART)
            s_nxt[part, :] = _dot(ks_ref[0, pl.ds(k_nxt + h * QK_PART, QK_PART), :], qs_scr[...])
        return m_new, jnp.exp2(m_prev - m_new), acc

    def pair_step(i, carry):
        carry = half_step(2 * i, s0_scr, s1_scr, p0_scr, p1_scr, carry)
        return half_step(2 * i + 1, s1_scr, s0_scr, p1_scr, p0_scr, carry)

    n_pairs = ((t0 + SEL_TILE - 1) // SEL_TILE + 1) // 2
    for b in range(blocks_per_tile):
        rows = slice(b * SEL_BLOCK, (b + 1) * SEL_BLOCK)
        s0_scr[rows, :] = s0_scr[rows, :] + jnp.concatenate([bias_scr[b:b + 1, :]] * HEADS_PER_GROUP, axis=1)
    p1_scr[...] = jnp.zeros(p1_scr.shape, BF16)
    carry = (m_d, jnp.ones((1, QL), F32), acc_d)
    _, alpha_last, acc_s = lax.fori_loop(0, n_pairs, pair_step, carry)
    acc_s = alpha_last * acc_s + _dot(vst_ref[0, :, pl.ds(tile_start(2 * n_pairs - 1), SEL_TILE)], p1_scr[...])

    gate = jax.nn.sigmoid(gl_ref[0, 0])
    o = (gate[0:1] * o_c + gate[1:2] * (acc_s[:HEAD_DIM] / acc_s[HEAD_DIM:HEAD_DIM + 1])
         + gate[2:3] * (acc_w[:HEAD_DIM] / acc_w[HEAD_DIM:HEAD_DIM + 1]))
    stacked = jnp.concatenate([o[:, p * Q_BLOCK:(p + 1) * Q_BLOCK] for p in range(HEADS_PER_GROUP)], axis=0)
    o_ref[...] = stacked.T.astype(o_ref.dtype)


def _sparse_attention(qt, kc_aug, vc_aug_t, ks_aug, vs_aug_t, kw_aug, vw_aug_t, gate_logits, s):
    n_group, n_qb = qt.shape[:2]
    nc = kc_aug.shape[1]
    n_sel = s // SEL_BLOCK
    assert HEAD_DIM + nc // 8 <= CMP_AUG_W and nc // 4 >= CMP_PER_QB + 8 and WINDOW // WIN_CHUNK + Q_BLOCK // WIN_CHUNK <= 8
    per_step = lambda g, i: (g, i, 0, 0)
    per_group = lambda shape: pl.BlockSpec((1,) + shape, lambda g, i: (g, 0, 0), pipeline_mode=pl.Buffered(1))
    tile_f32 = pltpu.VMEM((SEL_TILE, QL), F32)
    tile_bf16 = pltpu.VMEM((SEL_TILE, QL), BF16)
    sel_table = pltpu.VMEM((n_sel, Q_BLOCK), F32)
    return pl.pallas_call(
        _nsa_kernel,
        grid=(n_group, n_qb),
        in_specs=[
            pl.BlockSpec((1, 1, HEAD_DIM, QL), per_step),
            per_group((nc, CMP_AUG_W)),
            per_group((V_AUG_ROWS, nc)),
            per_group((s, KS_AUG_W)),
            per_group((V_AUG_ROWS, s)),
            per_group((s + WINDOW, KS_AUG_W)),
            per_group((V_AUG_ROWS, s + WINDOW)),
            pl.BlockSpec((1, 1, 3, QL), per_step),
        ],
        out_specs=pl.BlockSpec((Q_BLOCK, HEADS_PER_GROUP * HEAD_DIM), lambda g, i: (i, g)),
        out_shape=jax.ShapeDtypeStruct((s, Q_W), BF16),
        scratch_shapes=[pltpu.VMEM((Q_BLOCK // 128, 8 + nc, 128), F32), sel_table, sel_table,
                        pltpu.VMEM((CMP_AUG_W, QL), BF16), pltpu.VMEM((KS_AUG_W, QL), BF16),
                        pltpu.VMEM((KS_AUG_W, QL), BF16), pltpu.VMEM((nc, QL), F32),
                        tile_f32, tile_f32, tile_bf16, tile_bf16],
        compiler_params=_cparams("parallel", "arbitrary"),
        name="sparse_attn",
    )(qt, kc_aug, vc_aug_t, ks_aug, vs_aug_t, kw_aug, vw_aug_t, gate_logits)


def _merge_kernel(x_ref, ya_ref, yb_ref, yc_ref, g0_ref, g1_ref, g2_ref, wp_ref, wg_ref, wn_ref,
                  wo_ref, ng_ref, gate_ref, o_ref):
    merged = jax.nn.sigmoid(g0_ref[...].astype(F32)) * _dot(ya_ref[...], wp_ref[...])
    merged += jax.nn.sigmoid(g1_ref[...].astype(F32)) * _dot(yb_ref[...], wg_ref[...])
    merged += jax.nn.sigmoid(g2_ref[...].astype(F32)) * _dot(yc_ref[...], wn_ref[...])
    y = _dot(merged.astype(BF16), wo_ref[...])
    o_ref[...] = x_ref[...] + gate_ref[...] * _rms(y, ng_ref[...])


def _merge(x, ya, yb, yc, proj, wp, wg, wn, wo, norm_g, gate):
    s, d = x.shape
    tm = min(256, s)
    const = lambda i: (0, 0)
    rows = lambda i: (i, 0)
    whole = lambda a: pl.BlockSpec(a.shape, const, pipeline_mode=pl.Buffered(1))
    bg = OFF_BG // d
    return pl.pallas_call(
        _merge_kernel,
        grid=(s // tm,),
        in_specs=[
            pl.BlockSpec((tm, d), rows),
            pl.BlockSpec((tm, MIX_W), rows),
            pl.BlockSpec((tm, MIX_W), rows),
            pl.BlockSpec((tm, Q_W), rows),
            pl.BlockSpec((tm, d), lambda i: (i, bg)),
            pl.BlockSpec((tm, d), lambda i: (i, bg + 1)),
            pl.BlockSpec((tm, d), lambda i: (i, bg + 2)),
            whole(wp), whole(wg), whole(wn), whole(wo),
            pl.BlockSpec((1, d), const),
            pl.BlockSpec((1, d), const),
        ],
        out_specs=pl.BlockSpec((tm, d), rows),
        out_shape=jax.ShapeDtypeStruct((s, d), F32),
        compiler_params=_cparams("parallel"),
        name="merge_out",
    )(x, ya, yb, yc, proj, proj, proj, wp, wg, wn, wo, norm_g, gate)


def _ffn_kernel(x_ref, gi_ref, sc_ref, sh_ref, w1_ref, w2_ref, go_ref, gate_ref, o_ref, h_scr):
    j = pl.program_id(1)

    @pl.when(j == 0)
    def _():
        h = _rms(x_ref[...], gi_ref[...]) * (1.0 + sc_ref[...]) + sh_ref[...]
        h_scr[...] = h.astype(BF16)
        o_ref[...] = jnp.zeros_like(o_ref)

    a = jnp.square(jnp.maximum(_dot(h_scr[...], w1_ref[...]), 0.0))
    o_ref[...] += _dot(a.astype(BF16), w2_ref[...])

    @pl.when(j == pl.num_programs(1) - 1)
    def _():
        o_ref[...] = x_ref[...] + gate_ref[...] * _rms(o_ref[...], go_ref[...])


def _ffn(x, g_in, scale, shift, w1, w2, g_out, gate):
    s, d = x.shape
    tm, tf = min(1024, s), 512
    const = lambda i, j: (0, 0)
    rows = lambda i, j: (i, 0)
    return pl.pallas_call(
        _ffn_kernel,
        grid=(s // tm, D_FF // tf),
        in_specs=[
            pl.BlockSpec((tm, d), rows, pipeline_mode=pl.Buffered(1)),
            pl.BlockSpec((1, d), const),
            pl.BlockSpec((1, d), const),
            pl.BlockSpec((1, d), const),
            pl.BlockSpec((d, tf), lambda i, j: (0, j)),
            pl.BlockSpec((tf, d), lambda i, j: (j, 0)),
            pl.BlockSpec((1, d), const),
            pl.BlockSpec((1, d), const),
        ],
        out_specs=pl.BlockSpec((tm, d), rows),
        out_shape=jax.ShapeDtypeStruct((s, d), F32),
        scratch_shapes=[pltpu.VMEM((tm, d), BF16)],
        compiler_params=_cparams("parallel", "arbitrary"),
        name="ffn",
    )(x, g_in, scale, shift, w1, w2, g_out, gate)


def _token_mixing(x, mod, norm_g, w_in, pool_w, pool_scale, ln_g, ln_b, ws, bs, cmp_pos, cmp_w1,
                  cmp_b1, cmp_w2, cmp_b2, w_br_pool, w_br_gmlp, w_br_nsa, w_out):
    s, d = x.shape
    ng0 = OFF_BG
    w_main = jnp.concatenate([w_in[:, :ng0], w_in[:, ng0 + N_GATE:]], axis=1).astype(BF16)
    w_ng = jnp.pad(w_in[:, ng0:ng0 + N_GATE], ((0, 0), (0, 128 - N_GATE))).astype(BF16)
    row = lambda v: v.reshape(1, -1)

    col_scale = jnp.ones((1, PROJ_W), F32).at[:, OFF_Q:OFF_KV].set(HEAD_DIM ** -0.5 * LOG2E)
    proj, ngate = _in_projection(x, row(norm_g[0]), row(mod[1]), row(mod[0]), w_main, w_ng, col_scale)

    ya, yb = _mixers(proj, pool_w.astype(BF16), row(pool_scale), row(ln_g), row(ln_b), ws, bs.T)

    qt, ks_aug, vs_aug_t, kw_aug, vw_aug_t, gl = _attention_operands(proj, ngate)
    kv_cmp = proj[:, OFF_KV:OFF_KV + 2 * KV_W].reshape(s, 2, KV_GROUPS, HEAD_DIM).transpose(1, 2, 0, 3)
    xkv = kv_cmp.reshape(2, KV_GROUPS, s // CMP_STRIDE, CMP_STRIDE * HEAD_DIM)
    lane_pad = ((0, 0), (0, 0), (0, 128 - HEAD_DIM))
    kc_aug, vc_aug_t = _compress(xkv, cmp_pos.reshape(2, 1, CMP_BLOCK * HEAD_DIM), cmp_w1.astype(BF16),
                                 cmp_b1.reshape(2, 1, -1), jnp.pad(cmp_w2, lane_pad).astype(BF16),
                                 jnp.pad(cmp_b2.reshape(2, 1, -1), lane_pad))
    yc = _sparse_attention(qt, kc_aug, vc_aug_t, ks_aug, vs_aug_t, kw_aug, vw_aug_t, gl, s)

    return _merge(x, ya, yb, yc, proj, w_br_pool.astype(BF16), w_br_gmlp.astype(BF16),
                  w_br_nsa.astype(BF16), w_out.astype(BF16), row(norm_g[1]), row(mod[2]))


def kernel(x, c, norm_g, w_ada, b_ada, w_in, pool_w, pool_scale, gmlp_ln_g, gmlp_ln_b, gmlp_ws, gmlp_bs,
           cmp_pos, cmp_w1, cmp_b1, cmp_w2, cmp_b2, w_br_pool, w_br_gmlp, w_br_nsa, w_out, w_ff1, w_ff2):
    b, s, d = x.shape
    assert b == 1 and d == D_MODEL and s % 1024 == 0
    n_layer = w_ada.shape[0]
    mod_all = _modulation(c, w_ada, b_ada).reshape(n_layer, 6, d)
    xs = x[0]
    row = lambda v: v.reshape(1, -1)
    for l in range(n_layer):
        mod = mod_all[l]
        xs = _token_mixing(xs, mod, norm_g[l], w_in[l], pool_w[l], pool_scale[l], gmlp_ln_g[l],
                           gmlp_ln_b[l], gmlp_ws[l], gmlp_bs[l], cmp_pos[l], cmp_w1[l], cmp_b1[l],
                           cmp_w2[l], cmp_b2[l], w_br_pool[l], w_br_gmlp[l], w_br_nsa[l], w_out[l])
        xs = _ffn(xs, row(norm_g[l, 2]), row(mod[4]), row(mod[3]), w_ff1[l].astype(BF16),
                  w_ff2[l].astype(BF16), row(norm_g[l, 3]), row(mod[5]))
    return xs[None]
```

```python
import jax
import jax.numpy as jnp
from jax import lax
from jax.experimental import pallas as pl
from jax.experimental.pallas import tpu as pltpu

F32 = jnp.float32
BF16 = jnp.bfloat16

D_MODEL = 2048
POOL_WINDOWS = (2, 4, 8, 16)
POOL_HALO = 16
GROUP_CH = 128
MIX_W = 4 * GROUP_CH
GMLP_CHUNK = 128
HEAD_DIM = 64
KV_GROUPS = 4
HEADS_PER_GROUP = 4
Q_W = 16 * HEAD_DIM
KV_W = KV_GROUPS * HEAD_DIM
CMP_BLOCK = 32
CMP_STRIDE = 16
SEL_BLOCK = 64
SEL_TOPN = 16
WINDOW = 512
FORCE_BONUS = 1000.0
LOG2E = 1.4426950408889634
NEG = -1e30
N_GATE = 3 * 16
D_FF = 4 * D_MODEL

Q_BLOCK = 256
QL = HEADS_PER_GROUP * Q_BLOCK
CMP_PER_QB = Q_BLOCK // CMP_STRIDE
WIN_CHUNK = 128
PAD_TILES = WINDOW // Q_BLOCK
SEL_TILE = 512
KS_AUG_W = 128
CMP_AUG_W = 256
V_AUG_ROWS = HEAD_DIM + 16
P_CHUNK = 64
QK_PART = 256

OFF_POOL, OFF_U, OFF_V, OFF_Q, OFF_KV, OFF_BG = 0, 512, 1024, 1536, 2560, 4096
PROJ_W = OFF_BG + 3 * D_MODEL

VMEM_LIMIT = 56 * 1024 * 1024


def _cparams(*sem):
    return pltpu.CompilerParams(dimension_semantics=sem, vmem_limit_bytes=VMEM_LIMIT)


def _dot(a, b):
    return jnp.dot(a, b, preferred_element_type=F32)


def _rms(x, g):
    return x * lax.rsqrt(jnp.mean(x * x, axis=-1, keepdims=True) + 1e-6) * g


def _mod_kernel(c_ref, w_ref, b_ref, o_ref):
    c = c_ref[...]
    act = c * jax.nn.sigmoid(c)
    o_ref[0] = jnp.sum(act * w_ref[0], axis=0, keepdims=True) + b_ref[0]


def _modulation(c, w_ada, b_ada):
    n_layer, d, n_out = w_ada.shape
    tn = 1024
    return pl.pallas_call(
        _mod_kernel,
        grid=(n_layer, n_out // tn),
        in_specs=[
            pl.BlockSpec((d, 1), lambda l, j: (0, 0)),
            pl.BlockSpec((1, d, tn), lambda l, j: (l, 0, j)),
            pl.BlockSpec((1, 1, tn), lambda l, j: (l, 0, j)),
        ],
        out_specs=pl.BlockSpec((1, 1, tn), lambda l, j: (l, 0, j)),
        out_shape=jax.ShapeDtypeStruct((n_layer, 1, n_out), F32),
        compiler_params=_cparams("parallel", "parallel"),
        name="adaln_mod",
    )(c.reshape(d, 1), w_ada, b_ada.reshape(n_layer, 1, n_out))


def _inproj_kernel(x_ref, g_ref, sc_ref, sh_ref, w_ref, wng_ref, cs_ref, o_ref, ng_ref, h_scr):
    @pl.when(pl.program_id(1) == 0)
    def _():
        h = _rms(x_ref[...], g_ref[...]) * (1.0 + sc_ref[...]) + sh_ref[...]
        hb = h.astype(BF16)
        h_scr[...] = hb
        ng_ref[...] = _dot(hb, wng_ref[...])

    o_ref[...] = (_dot(h_scr[...], w_ref[...]) * cs_ref[...]).astype(o_ref.dtype)


def _in_projection(x, g, scale, shift, w_main, w_ng, col_scale):
    s, d = x.shape
    tm, tn = min(1024, s), 1024
    row = lambda i, j: (0, 0)
    return pl.pallas_call(
        _inproj_kernel,
        grid=(s // tm, PROJ_W // tn),
        in_specs=[
            pl.BlockSpec((tm, d), lambda i, j: (i, 0)),
            pl.BlockSpec((1, d), row),
            pl.BlockSpec((1, d), row),
            pl.BlockSpec((1, d), row),
            pl.BlockSpec((d, tn), lambda i, j: (0, j)),
            pl.BlockSpec((d, 128), row),
            pl.BlockSpec((1, tn), lambda i, j: (0, j)),
        ],
        out_specs=[
            pl.BlockSpec((tm, tn), lambda i, j: (i, j)),
            pl.BlockSpec((tm, 128), lambda i, j: (i, 0)),
        ],
        out_shape=[
            jax.ShapeDtypeStruct((s, PROJ_W), BF16),
            jax.ShapeDtypeStruct((s, 128), F32),
        ],
        scratch_shapes=[pltpu.VMEM((tm, d), BF16)],
        compiler_params=_cparams("parallel", "arbitrary"),
        name="in_proj",
    )(x, g, scale, shift, w_main, w_ng, col_scale)


def _mixer_kernel(a_ref, halo_ref, u_ref, v_ref, pw_ref, ps_ref, lg_ref, lb_ref, ws_ref, bs_ref,
                  ya_ref, yb_ref):
    i = pl.program_id(0)
    tm = a_ref.shape[0]
    a = a_ref[...].astype(F32)
    halo = jnp.where(i > 0, halo_ref[...].astype(F32), 0.0)
    ext = jnp.concatenate([halo, a], axis=0)
    p2 = ext[1:] + ext[:-1]
    p4 = p2[2:] + p2[:-2]
    p8 = p4[4:] + p4[:-4]
    p16 = p8[8:] + p8[:-8]
    sums = (p2[15:15 + tm], p4[13:13 + tm], p8[9:9 + tm], p16[1:1 + tm])
    t = i * tm + lax.broadcasted_iota(jnp.int32, (tm, 1), 0)
    for gi, w in enumerate(POOL_WINDOWS):
        cols = slice(gi * GROUP_CH, (gi + 1) * GROUP_CH)
        cnt = jnp.minimum(t + 1, w).astype(F32)
        pooled = sums[gi][:, cols] / cnt - a[:, cols]
        y = _dot(pooled.astype(BF16), pw_ref[gi])
        ya_ref[:, cols] = (y * ps_ref[:, cols]).astype(ya_ref.dtype)

    u = jax.nn.gelu(u_ref[...].astype(F32))
    v = jax.nn.gelu(v_ref[...].astype(F32))
    mu = jnp.mean(v, axis=-1, keepdims=True)
    var = jnp.mean(jnp.square(v - mu), axis=-1, keepdims=True)
    vn = ((v - mu) * lax.rsqrt(var + 1e-5) * lg_ref[...] + lb_ref[...]).astype(BF16)
    r = lax.broadcasted_iota(jnp.int32, (GMLP_CHUNK, GMLP_CHUNK), 0)
    c = lax.broadcasted_iota(jnp.int32, (GMLP_CHUNK, GMLP_CHUNK), 1)
    for gi in range(4):
        cols = slice(gi * GROUP_CH, (gi + 1) * GROUP_CH)
        wsm = jnp.where(r >= c, ws_ref[gi], 0.0).astype(BF16)
        bias = bs_ref[:, gi:gi + 1]
        for ck in range(tm // GMLP_CHUNK):
            rows = slice(ck * GMLP_CHUNK, (ck + 1) * GMLP_CHUNK)
            mixed = _dot(wsm, vn[rows, cols]) + bias
            yb_ref[rows, cols] = (u[rows, cols] * mixed).astype(yb_ref.dtype)


def _mixers(proj, pool_w, pool_scale, ln_g, ln_b, ws, bs_t):
    s = proj.shape[0]
    tm = min(512, s)
    hb = tm // POOL_HALO
    const2 = lambda i: (0, 0)
    const3 = lambda i: (0, 0, 0)
    return pl.pallas_call(
        _mixer_kernel,
        grid=(s // tm,),
        in_specs=[
            pl.BlockSpec((tm, MIX_W), lambda i: (i, OFF_POOL // MIX_W)),
            pl.BlockSpec((POOL_HALO, MIX_W), lambda i: (jnp.maximum(i * hb - 1, 0), OFF_POOL // MIX_W)),
            pl.BlockSpec((tm, MIX_W), lambda i: (i, OFF_U // MIX_W)),
            pl.BlockSpec((tm, MIX_W), lambda i: (i, OFF_V // MIX_W)),
            pl.BlockSpec((4, GROUP_CH, GROUP_CH), const3),
            pl.BlockSpec((1, MIX_W), const2),
            pl.BlockSpec((1, MIX_W), const2),
            pl.BlockSpec((1, MIX_W), const2),
            pl.BlockSpec((4, GMLP_CHUNK, GMLP_CHUNK), const3),
            pl.BlockSpec((GMLP_CHUNK, 4), const2),
        ],
        out_specs=[
            pl.BlockSpec((tm, MIX_W), lambda i: (i, 0)),
            pl.BlockSpec((tm, MIX_W), lambda i: (i, 0)),
        ],
        out_shape=[jax.ShapeDtypeStruct((s, MIX_W), BF16)] * 2,
        compiler_params=_cparams("parallel"),
        name="mixers",
    )(proj, proj, proj, proj, pool_w, pool_scale, ln_g, ln_b, ws, bs_t)


def _compress_kernel(x_ref, pos_ref, w1_ref, b1_ref, w2_ref, b2_ref, kc_ref, vct_ref):
    half = CMP_STRIDE * HEAD_DIM
    x = x_ref[0, 0].astype(F32)
    nc = x.shape[0]
    pos = pos_ref[0]
    first = _dot((x + pos[:, :half]).astype(BF16), w1_ref[0, :half, :])
    second = _dot((x + pos[:, half:]).astype(BF16), w1_ref[0, half:, :])
    hid = jax.nn.gelu(first + pltpu.roll(second, nc - 1, 0) + b1_ref[0])
    out = _dot(hid.astype(BF16), w2_ref[0]) + b2_ref[0]

    @pl.when(pl.program_id(1) == 0)
    def _():
        lane = lax.broadcasted_iota(jnp.int32, (nc, 128), 1)
        grp = lax.broadcasted_iota(jnp.int32, (nc, 128), 0) // 8
        kc_ref[0, :, 0:128] = (out + (lane - HEAD_DIM == grp).astype(F32)).astype(BF16)
        kc_ref[0, :, 128:] = (lane + (128 - HEAD_DIM) == grp).astype(BF16)

    @pl.when(pl.program_id(1) == 1)
    def _():
        ones_row = lax.broadcasted_iota(jnp.int32, (V_AUG_ROWS, nc), 0) == HEAD_DIM
        vct_ref[0] = (out.T[0:V_AUG_ROWS] + ones_row.astype(F32)).astype(BF16)


def _compress(xkv, pos, w1, b1, w2, b2):
    _, n_group, nc, width = xkv.shape
    assert CMP_AUG_W == 256 and HEAD_DIM + nc // 8 <= CMP_AUG_W
    per_kv = lambda g, a: (a, 0, 0)
    per_group = lambda g, a: (g, 0, 0)
    return pl.pallas_call(
        _compress_kernel,
        grid=(n_group, 2),
        in_specs=[
            pl.BlockSpec((1, 1, nc, width), lambda g, a: (a, g, 0, 0)),
            pl.BlockSpec((1, 1, 2 * width), per_kv),
            pl.BlockSpec((1, 2 * width, 128), per_kv),
            pl.BlockSpec((1, 1, 128), per_kv),
            pl.BlockSpec((1, 128, 128), per_kv),
            pl.BlockSpec((1, 1, 128), per_kv),
        ],
        out_specs=[pl.BlockSpec((1, nc, CMP_AUG_W), per_group), pl.BlockSpec((1, V_AUG_ROWS, nc), per_group)],
        out_shape=[jax.ShapeDtypeStruct((n_group, nc, CMP_AUG_W), BF16),
                   jax.ShapeDtypeStruct((n_group, V_AUG_ROWS, nc), BF16)],
        compiler_params=_cparams("parallel", "arbitrary"),
        name="compress_kv",
    )(xkv, pos, w1, b1, w2, b2)


def _operand_kernel(q_lo_ref, q_hi_ref, sel_ref, win_ref, ng_ref, qt_ref, ks_ref, vst_ref, kw_ref, vwt_ref, gl_ref):
    i = pl.program_id(0)
    n_tiles = pl.num_programs(0) - PAD_TILES
    rows = q_lo_ref.shape[0]
    lane = lax.broadcasted_iota(jnp.int32, (rows, 128), 1)
    pos = i * rows + lax.broadcasted_iota(jnp.int32, (rows, 128), 0)
    ones_rows = (lax.broadcasted_iota(jnp.int32, (V_AUG_ROWS - HEAD_DIM, rows), 0) == 0).astype(BF16)

    def key_pair(slab, idx):
        onehot = (lane - HEAD_DIM == idx).astype(F32)
        left = jnp.where(lane < HEAD_DIM, slab, onehot)
        right = jnp.where(lane < HEAD_DIM, pltpu.roll(slab, HEAD_DIM, 1), onehot)
        return left.astype(BF16), right.astype(BF16)

    def transposed_pair(slab):
        t = slab.T.astype(BF16)
        return t[:HEAD_DIM], t[HEAD_DIM:]

    def write_kv(src, keep, idx, k_ref, vt_ref):
        for pair in range(KV_GROUPS // 2):
            lanes = slice(pair * 128, (pair + 1) * 128)
            k_slab = jnp.where(keep, src[:, lanes].astype(F32), 0.0)
            v_slab = jnp.where(keep, src[:, KV_W + pair * 128:KV_W + (pair + 1) * 128].astype(F32), 0.0)
            for g, k, vt in zip((2 * pair, 2 * pair + 1), key_pair(k_slab, idx), transposed_pair(v_slab)):
                k_ref[g] = k
                vt_ref[g, 0:HEAD_DIM, :] = vt
                vt_ref[g, HEAD_DIM:, :] = ones_rows

    @pl.when(i < n_tiles)
    def _():
        for half, ref in enumerate((q_lo_ref, q_hi_ref)):
            for gg in range(2):
                for pp in range(HEADS_PER_GROUP // 2):
                    lanes = slice(gg * 256 + pp * 128, gg * 256 + (pp + 1) * 128)
                    for h, t in zip((2 * pp, 2 * pp + 1), transposed_pair(ref[:, lanes].astype(F32))):
                        qt_ref[2 * half + gg, 0, :, h * Q_BLOCK:(h + 1) * Q_BLOCK] = t
        write_kv(sel_ref, True, (pos // SEL_BLOCK) % (SEL_TILE // SEL_BLOCK), ks_ref, vst_ref)
        logits_t = ng_ref[...].T
        for g in range(KV_GROUPS):
            for h in range(HEADS_PER_GROUP):
                for br in range(3):
                    r = (g * HEADS_PER_GROUP + h) * 3 + br
                    gl_ref[g, 0, br:br + 1, h * Q_BLOCK:(h + 1) * Q_BLOCK] = logits_t[r:r + 1, :]

    write_kv(win_ref, i >= PAD_TILES, (pos // WIN_CHUNK) % 8, kw_ref, vwt_ref)


def _attention_operands(proj, ngate):
    s = proj.shape[0]
    n_tiles = s // Q_BLOCK
    blk = MIX_W
    real = lambda i: jnp.minimum(i, n_tiles - 1)
    src = lambda c: pl.BlockSpec((Q_BLOCK, blk), lambda i: (real(i), c))
    return pl.pallas_call(
        _operand_kernel,
        grid=(n_tiles + PAD_TILES,),
        in_specs=[src(OFF_Q // blk), src(OFF_Q // blk + 1), src(OFF_KV // blk + 1),
                  pl.BlockSpec((Q_BLOCK, blk), lambda i: (jnp.maximum(i - PAD_TILES, 0), OFF_KV // blk + 2)),
                  pl.BlockSpec((Q_BLOCK, 128), lambda i: (real(i), 0))],
        out_specs=[
            pl.BlockSpec((KV_GROUPS, 1, HEAD_DIM, QL), lambda i: (0, real(i), 0, 0)),
            pl.BlockSpec((KV_GROUPS, Q_BLOCK, KS_AUG_W), lambda i: (0, real(i), 0)),
            pl.BlockSpec((KV_GROUPS, V_AUG_ROWS, Q_BLOCK), lambda i: (0, 0, real(i))),
            pl.BlockSpec((KV_GROUPS, Q_BLOCK, KS_AUG_W), lambda i: (0, i, 0)),
            pl.BlockSpec((KV_GROUPS, V_AUG_ROWS, Q_BLOCK), lambda i: (0, 0, i)),
            pl.BlockSpec((KV_GROUPS, 1, 3, QL), lambda i: (0, real(i), 0, 0)),
        ],
        out_shape=[
            jax.ShapeDtypeStruct((KV_GROUPS, n_tiles, HEAD_DIM, QL), BF16),
            jax.ShapeDtypeStruct((KV_GROUPS, s, KS_AUG_W), BF16),
            jax.ShapeDtypeStruct((KV_GROUPS, V_AUG_ROWS, s), BF16),
            jax.ShapeDtypeStruct((KV_GROUPS, s + WINDOW, KS_AUG_W), BF16),
            jax.ShapeDtypeStruct((KV_GROUPS, V_AUG_ROWS, s + WINDOW), BF16),
            jax.ShapeDtypeStruct((KV_GROUPS, n_tiles, 3, QL), F32),
        ],
        compiler_params=_cparams("arbitrary"),
        name="attn_operands",
    )(proj, proj, proj, proj, ngate)


def _nsa_kernel(qt_ref, kc_ref, vct_ref, ks_ref, vst_ref, kw_ref, vwt_ref, gl_ref, o_ref,
                ps_scr, bias_scr, diag_bias_scr, qc_scr, qw_scr, qs_scr, sc_scr, s0_scr, s1_scr, p0_scr, p1_scr):
    qb = pl.program_id(1)
    t0 = qb * Q_BLOCK
    t0a = pl.multiple_of(t0, Q_BLOCK)
    qt = qt_ref[0, 0]
    nc = kc_ref.shape[1]
    n_sel = bias_scr.shape[0]
    q_rel = lax.broadcasted_iota(jnp.int32, (1, QL), 1) % Q_BLOCK
    tq = t0 + q_rel

    def bias_rows(cond, rows):
        return jnp.where(cond, NEG, 0.0).astype(BF16) + jnp.zeros((rows, QL), BF16)

    def flash_update(s, carry, vt):
        m_prev, acc = carry
        m_new = jnp.maximum(m_prev, jnp.max(s, axis=0, keepdims=True))
        p = jnp.exp2((s - m_new).astype(BF16))
        return m_new, jnp.exp2(m_prev - m_new) * acc + _dot(vt, p)

    init = (jnp.full((1, QL), NEG, F32), jnp.zeros((V_AUG_ROWS, QL), F32))

    grp = lax.broadcasted_iota(jnp.int32, (qc_scr.shape[0] - HEAD_DIM, 1), 0)
    visible = CMP_PER_QB * (qb + 1)
    qc_scr[0:HEAD_DIM, :] = qt
    qc_scr[HEAD_DIM:, :] = bias_rows((8 * grp >= visible) & (grp < nc // 8), qc_scr.shape[0] - HEAD_DIM)
    band = CMP_PER_QB + 8
    r0 = pl.multiple_of(jnp.maximum(visible - band, 0), 8)
    band_end = CMP_STRIDE * (r0 + lax.broadcasted_iota(jnp.int32, (band, 1), 0)) + (CMP_BLOCK - 1)

    def compressed(rows):
        def run():
            sc_scr[0:rows, :] = _dot(kc_ref[0, 0:rows, :], qc_scr[...])
            sc_scr[pl.ds(r0, band), :] = jnp.where(band_end <= tq, sc_scr[pl.ds(r0, band), :], NEG)
            s_c = sc_scr[0:rows, :]
            m_c = jnp.maximum(jnp.max(s_c, axis=0, keepdims=True), 0.1 * NEG)
            e_c = jnp.exp2(s_c - m_c)
            o_aug = _dot(vct_ref[0, :, 0:rows], e_c.astype(BF16))
            inv_c = 1.0 / jnp.maximum(o_aug[HEAD_DIM:HEAD_DIM + 1], 1e-30)
            p_grp = None
            for h in range(HEADS_PER_GROUP):
                lanes = slice(h * Q_BLOCK, (h + 1) * Q_BLOCK)
                p_h = e_c[:, lanes] * inv_c[:, lanes]
                p_grp = p_h if p_grp is None else p_grp + p_h
            for c in range(Q_BLOCK // 128):
                ps_scr[c, 8:8 + rows, :] = p_grp[:, c * 128:(c + 1) * 128]
                if rows < nc:
                    ps_scr[c, 8 + rows:, :] = jnp.zeros((nc - rows, 128), F32)
            return o_aug[:HEAD_DIM] * inv_c
        return run

    ps_scr[:, 0:8, :] = jnp.zeros((Q_BLOCK // 128, 8, 128), F32)
    quarter = nc // 4
    o_c = lax.cond(visible <= 2 * quarter,
                   lambda: lax.cond(visible <= quarter, compressed(quarter), compressed(2 * quarter)),
                   lambda: lax.cond(visible <= 3 * quarter, compressed(3 * quarter), compressed(nc)))
    imp = jnp.concatenate([sum(ps_scr[c, pl.ds(k, n_sel, stride=4), :] for k in range(7, 12))
                           for c in range(Q_BLOCK // 128)], axis=1)

    n_tri = Q_BLOCK // WIN_CHUNK
    n_chunk = WINDOW // WIN_CHUNK + n_tri
    span = n_chunk * WIN_CHUNK
    rho = lax.broadcasted_iota(jnp.int32, (16, 1), 0)
    first_real = WINDOW // WIN_CHUNK - n_tri * qb
    qw_scr[0:HEAD_DIM, :] = qt
    qw_scr[HEAD_DIM:HEAD_DIM + 16, :] = bias_rows((rho < 8) & (((rho - n_tri * qb) & 7) < first_real), 16)
    qw_scr[HEAD_DIM + 16:, :] = jnp.zeros((qw_scr.shape[0] - HEAD_DIM - 16, QL), BF16)
    s_w = _dot(kw_ref[0, pl.ds(t0a, span), :], qw_scr[...])
    i_rel = lax.broadcasted_iota(jnp.int32, (Q_BLOCK, 1), 0)
    s_w = jnp.concatenate([jnp.where(i_rel > q_rel, s_w[:Q_BLOCK], NEG), s_w[Q_BLOCK:WINDOW],
                           jnp.where(i_rel <= q_rel, s_w[WINDOW:], NEG)], axis=0)
    _, acc_w = flash_update(s_w, init, vwt_ref[0, :, pl.ds(t0a, span)])

    s0_scr[...] = _dot(ks_ref[0, 0:SEL_TILE, 0:HEAD_DIM], qt)

    j_idx = lax.broadcasted_iota(jnp.int32, (n_sel, Q_BLOCK), 0)
    cur = (t0 + lax.broadcasted_iota(jnp.int32, (n_sel, Q_BLOCK), 1)) // SEL_BLOCK
    forced = (j_idx == 0) | (j_idx == cur) | (j_idx == cur - 1)
    valid = j_idx <= cur
    score = jnp.where(forced, -2.0, jnp.where(valid, imp, -1.0))
    n_pick = SEL_TOPN - 3

    fast = score
    for _ in range(n_pick):
        fast = jnp.where(fast == jnp.max(fast, axis=0, keepdims=True), -2.0, fast)
    retired = (fast < -1.5) & valid & jnp.logical_not(forced)
    most_retired = jnp.max(jnp.sum(retired.astype(F32), axis=0, keepdims=True))

    def ranked_with_ties():
        def pick_one(_, sc):
            best = jnp.max(sc, axis=0, keepdims=True)
            first = jnp.min(jnp.where(sc == best, j_idx, n_sel), axis=0, keepdims=True)
            return jnp.where(j_idx == first, -2.0, sc)
        return lax.fori_loop(0, n_pick, pick_one, score)

    ranked = lax.cond(most_retired > n_pick, ranked_with_ties, lambda: fast)
    chosen = valid & (ranked < -1.5)
    diag_bias_scr[...] = jnp.where(chosen, 0.0, NEG)
    bias_scr[...] = jnp.where(chosen & (j_idx * SEL_BLOCK < t0), 0.0, NEG)

    blocks_per_tile = SEL_TILE // SEL_BLOCK
    last_tile = ks_ref.shape[1] // SEL_TILE - 1
    qs_scr[0:HEAD_DIM, :] = qt
    qs_scr[HEAD_DIM + 16:, :] = jnp.zeros((qs_scr.shape[0] - HEAD_DIM - 16, QL), BF16)

    def tile_start(kt):
        return pl.multiple_of(jnp.clip(kt, 0, last_tile) * SEL_TILE, SEL_TILE)

    def set_bias_rows(table, kt):
        b0 = pl.multiple_of(jnp.minimum(kt, last_tile) * blocks_per_tile, blocks_per_tile)
        b16 = jnp.concatenate([table[pl.ds(b0, blocks_per_tile), :],
                               jnp.zeros((16 - blocks_per_tile, Q_BLOCK), F32)], axis=0)
        qs_scr[HEAD_DIM:HEAD_DIM + 16, :] = jnp.concatenate([b16] * HEADS_PER_GROUP, axis=1).astype(BF16)

    set_bias_rows(diag_bias_scr, t0 // SEL_TILE)
    s_d = _dot(ks_ref[0, pl.ds(t0a, Q_BLOCK), :], qs_scr[...])
    m_d, acc_d = flash_update(jnp.where(i_rel <= q_rel, s_d, NEG), init, vst_ref[0, :, pl.ds(t0a, Q_BLOCK)])

    def half_step(kt, s_cur, s_nxt, p_cur, p_prev, carry):
        m_prev, alpha_prev, acc = carry
        acc = alpha_prev * acc + _dot(vst_ref[0, :, pl.ds(tile_start(kt - 1), SEL_TILE)], p_prev[...])
        m8 = jnp.max(s_cur[...].reshape(SEL_TILE // 8, 8, QL), axis=0)
        m_new = jnp.maximum(m_prev, jnp.max(m8, axis=0, keepdims=True))
        set_bias_rows(bias_scr, kt + 1)
        k_nxt = tile_start(kt + 1)
        for h in range(SEL_TILE // QK_PART):
            for c in range(QK_PART // P_CHUNK):
                rows = slice(h * QK_PART + c * P_CHUNK, h * QK_PART + (c + 1) * P_CHUNK)
                p_cur[rows, :] = jnp.exp2((s_cur[rows, :] - m_new).astype(BF16))
            part = slice(h * QK_PART, (h + 1) * QK_PART)
            s_nxt[part, :] = _dot(ks_ref[0, pl.ds(k_nxt + h * QK_PART, QK_PART), :], qs_scr[...])
        return m_new, jnp.exp2(m_prev - m_new), acc

    def pair_step(i, carry):
        carry = half_step(2 * i, s0_scr, s1_scr, p0_scr, p1_scr, carry)
        return half_step(2 * i + 1, s1_scr, s0_scr, p1_scr, p0_scr, carry)

    n_pairs = ((t0 + SEL_TILE - 1) // SEL_TILE + 1) // 2
    for b in range(blocks_per_tile):
        rows = slice(b * SEL_BLOCK, (b + 1) * SEL_BLOCK)
        s0_scr[rows, :] = s0_scr[rows, :] + jnp.concatenate([bias_scr[b:b + 1, :]] * HEADS_PER_GROUP, axis=1)
    p1_scr[...] = jnp.zeros(p1_scr.shape, BF16)
    carry = (m_d, jnp.ones((1, QL), F32), acc_d)
    _, alpha_last, acc_s = lax.fori_loop(0, n_pairs, pair_step, carry)
    acc_s = alpha_last * acc_s + _dot(vst_ref[0, :, pl.ds(tile_start(2 * n_pairs - 1), SEL_TILE)], p1_scr[...])

    gate = jax.nn.sigmoid(gl_ref[0, 0])
    o = (gate[0:1] * o_c + gate[1:2] * (acc_s[:HEAD_DIM] / acc_s[HEAD_DIM:HEAD_DIM + 1])
         + gate[2:3] * (acc_w[:HEAD_DIM] / acc_w[HEAD_DIM:HEAD_DIM + 1]))
    stacked = jnp.concatenate([o[:, p * Q_BLOCK:(p + 1) * Q_BLOCK] for p in range(HEADS_PER_GROUP)], axis=0)
    o_ref[...] = stacked.T.astype(o_ref.dtype)


def _sparse_attention(qt, kc_aug, vc_aug_t, ks_aug, vs_aug_t, kw_aug, vw_aug_t, gate_logits, s):
    n_group, n_qb = qt.shape[:2]
    nc = kc_aug.shape[1]
    n_sel = s // SEL_BLOCK
    assert HEAD_DIM + nc // 8 <= CMP_AUG_W and nc // 4 >= CMP_PER_QB + 8 and WINDOW // WIN_CHUNK + Q_BLOCK // WIN_CHUNK <= 8
    per_step = lambda g, i: (g, i, 0, 0)
    per_group = lambda shape: pl.BlockSpec((1,) + shape, lambda g, i: (g, 0, 0), pipeline_mode=pl.Buffered(1))
    tile_f32 = pltpu.VMEM((SEL_TILE, QL), F32)
    tile_bf16 = pltpu.VMEM((SEL_TILE, QL), BF16)
    sel_table = pltpu.VMEM((n_sel, Q_BLOCK), F32)
    return pl.pallas_call(
        _nsa_kernel,
        grid=(n_group, n_qb),
        in_specs=[
            pl.BlockSpec((1, 1, HEAD_DIM, QL), per_step),
            per_group((nc, CMP_AUG_W)),
            per_group((V_AUG_ROWS, nc)),
            per_group((s, KS_AUG_W)),
            per_group((V_AUG_ROWS, s)),
            per_group((s + WINDOW, KS_AUG_W)),
            per_group((V_AUG_ROWS, s + WINDOW)),
            pl.BlockSpec((1, 1, 3, QL), per_step),
        ],
        out_specs=pl.BlockSpec((Q_BLOCK, HEADS_PER_GROUP * HEAD_DIM), lambda g, i: (i, g)),
        out_shape=jax.ShapeDtypeStruct((s, Q_W), BF16),
        scratch_shapes=[pltpu.VMEM((Q_BLOCK // 128, 8 + nc, 128), F32), sel_table, sel_table,
                        pltpu.VMEM((CMP_AUG_W, QL), BF16), pltpu.VMEM((KS_AUG_W, QL), BF16),
                        pltpu.VMEM((KS_AUG_W, QL), BF16), pltpu.VMEM((nc, QL), F32),
                        tile_f32, tile_f32, tile_bf16, tile_bf16],
        compiler_params=_cparams("parallel", "arbitrary"),
        name="sparse_attn",
    )(qt, kc_aug, vc_aug_t, ks_aug, vs_aug_t, kw_aug, vw_aug_t, gate_logits)


def _merge_kernel(x_ref, ya_ref, yb_ref, yc_ref, g0_ref, g1_ref, g2_ref, wp_ref, wg_ref, wn_ref,
                  wo_ref, ng_ref, gate_ref, o_ref):
    merged = jax.nn.sigmoid(g0_ref[...].astype(F32)) * _dot(ya_ref[...], wp_ref[0])
    merged += jax.nn.sigmoid(g1_ref[...].astype(F32)) * _dot(yb_ref[...], wg_ref[0])
    merged += jax.nn.sigmoid(g2_ref[...].astype(F32)) * _dot(yc_ref[...], wn_ref[0])
    y = _dot(merged.astype(BF16), wo_ref[0])
    o_ref[...] = x_ref[...] + gate_ref[...] * _rms(y, ng_ref[...])


def _merge(x, ya, yb, yc, proj, wp, wg, wn, wo, layer, norm_g, gate):
    s, d = x.shape
    tm = min(256, s)
    const = lambda i: (0, 0)
    rows = lambda i: (i, 0)
    whole = lambda a: pl.BlockSpec((1,) + a.shape[1:], lambda i: (layer, 0, 0), pipeline_mode=pl.Buffered(1))
    bg = OFF_BG // d
    return pl.pallas_call(
        _merge_kernel,
        grid=(s // tm,),
        in_specs=[
            pl.BlockSpec((tm, d), rows),
            pl.BlockSpec((tm, MIX_W), rows),
            pl.BlockSpec((tm, MIX_W), rows),
            pl.BlockSpec((tm, Q_W), rows),
            pl.BlockSpec((tm, d), lambda i: (i, bg)),
            pl.BlockSpec((tm, d), lambda i: (i, bg + 1)),
            pl.BlockSpec((tm, d), lambda i: (i, bg + 2)),
            whole(wp), whole(wg), whole(wn), whole(wo),
            pl.BlockSpec((1, d), const),
            pl.BlockSpec((1, d), const),
        ],
        out_specs=pl.BlockSpec((tm, d), rows),
        out_shape=jax.ShapeDtypeStruct((s, d), F32),
        compiler_params=_cparams("parallel"),
        name="merge_out",
    )(x, ya, yb, yc, proj, proj, proj, wp, wg, wn, wo, norm_g, gate)


def _ffn_kernel(x_ref, gi_ref, sc_ref, sh_ref, w1_ref, w2_ref, go_ref, gate_ref, o_ref, h_scr):
    j = pl.program_id(1)

    @pl.when(j == 0)
    def _():
        h = _rms(x_ref[...], gi_ref[...]) * (1.0 + sc_ref[...]) + sh_ref[...]
        h_scr[...] = h.astype(BF16)
        o_ref[...] = jnp.zeros_like(o_ref)

    a = jnp.square(jnp.maximum(_dot(h_scr[...], w1_ref[0]), 0.0))
    o_ref[...] += _dot(a.astype(BF16), w2_ref[0])

    @pl.when(j == pl.num_programs(1) - 1)
    def _():
        o_ref[...] = x_ref[...] + gate_ref[...] * _rms(o_ref[...], go_ref[...])


def _ffn(x, g_in, scale, shift, w1, w2, layer, g_out, gate):
    s, d = x.shape
    tm, tf = min(1024, s), 512
    const = lambda i, j: (0, 0)
    rows = lambda i, j: (i, 0)
    return pl.pallas_call(
        _ffn_kernel,
        grid=(s // tm, D_FF // tf),
        in_specs=[
            pl.BlockSpec((tm, d), rows, pipeline_mode=pl.Buffered(1)),
            pl.BlockSpec((1, d), const),
            pl.BlockSpec((1, d), const),
            pl.BlockSpec((1, d), const),
            pl.BlockSpec((1, d, tf), lambda i, j: (layer, 0, j)),
            pl.BlockSpec((1, tf, d), lambda i, j: (layer, j, 0)),
            pl.BlockSpec((1, d), const),
            pl.BlockSpec((1, d), const),
        ],
        out_specs=pl.BlockSpec((tm, d), rows),
        out_shape=jax.ShapeDtypeStruct((s, d), F32),
        scratch_shapes=[pltpu.VMEM((tm, d), BF16)],
        compiler_params=_cparams("parallel", "arbitrary"),
        name="ffn",
    )(x, g_in, scale, shift, w1, w2, g_out, gate)


def _token_mixing(x, mod, norm_g, w_in, pool_w, pool_scale, ln_g, ln_b, ws, bs, cmp_pos, cmp_w1,
                  cmp_b1, cmp_w2, cmp_b2, branch_weights, layer):
    s, d = x.shape
    ng0 = OFF_BG
    w_main = jnp.concatenate([w_in[:, :ng0], w_in[:, ng0 + N_GATE:]], axis=1).astype(BF16)
    w_ng = jnp.pad(w_in[:, ng0:ng0 + N_GATE], ((0, 0), (0, 128 - N_GATE))).astype(BF16)
    row = lambda v: v.reshape(1, -1)

    col_scale = jnp.ones((1, PROJ_W), F32).at[:, OFF_Q:OFF_KV].set(HEAD_DIM ** -0.5 * LOG2E)
    proj, ngate = _in_projection(x, row(norm_g[0]), row(mod[1]), row(mod[0]), w_main, w_ng, col_scale)

    ya, yb = _mixers(proj, pool_w.astype(BF16), row(pool_scale), row(ln_g), row(ln_b), ws, bs.T)

    qt, ks_aug, vs_aug_t, kw_aug, vw_aug_t, gl = _attention_operands(proj, ngate)
    kv_cmp = proj[:, OFF_KV:OFF_KV + 2 * KV_W].reshape(s, 2, KV_GROUPS, HEAD_DIM).transpose(1, 2, 0, 3)
    xkv = kv_cmp.reshape(2, KV_GROUPS, s // CMP_STRIDE, CMP_STRIDE * HEAD_DIM)
    lane_pad = ((0, 0), (0, 0), (0, 128 - HEAD_DIM))
    kc_aug, vc_aug_t = _compress(xkv, cmp_pos.reshape(2, 1, CMP_BLOCK * HEAD_DIM), cmp_w1.astype(BF16),
                                 cmp_b1.reshape(2, 1, -1), jnp.pad(cmp_w2, lane_pad).astype(BF16),
                                 jnp.pad(cmp_b2.reshape(2, 1, -1), lane_pad))
    yc = _sparse_attention(qt, kc_aug, vc_aug_t, ks_aug, vs_aug_t, kw_aug, vw_aug_t, gl, s)

    return _merge(x, ya, yb, yc, proj, *branch_weights, layer, row(norm_g[1]), row(mod[2]))


def kernel(x, c, norm_g, w_ada, b_ada, w_in, pool_w, pool_scale, gmlp_ln_g, gmlp_ln_b, gmlp_ws, gmlp_bs,
           cmp_pos, cmp_w1, cmp_b1, cmp_w2, cmp_b2, w_br_pool, w_br_gmlp, w_br_nsa, w_out, w_ff1, w_ff2):
    b, s, d = x.shape
    assert b == 1 and d == D_MODEL and s % 1024 == 0
    n_layer = w_ada.shape[0]
    mod_all = _modulation(c, w_ada, b_ada).reshape(n_layer, 6, d)
    xs = x[0]
    row = lambda v: v.reshape(1, -1)
    branch_weights = tuple(w.astype(BF16) for w in (w_br_pool, w_br_gmlp, w_br_nsa, w_out))
    w_ff1_b, w_ff2_b = w_ff1.astype(BF16), w_ff2.astype(BF16)
    for l in range(n_layer):
        mod = mod_all[l]
        xs = _token_mixing(xs, mod, norm_g[l], w_in[l], pool_w[l], pool_scale[l], gmlp_ln_g[l],
                           gmlp_ln_b[l], gmlp_ws[l], gmlp_bs[l], cmp_pos[l], cmp_w1[l], cmp_b1[l],
                           cmp_w2[l], cmp_b2[l], branch_weights, l)
        xs = _ffn(xs, row(norm_g[l, 2]), row(mod[4]), row(mod[3]), w_ff1_b, w_ff2_b, l,
                  row(norm_g[l, 3]), row(mod[5]))
    return xs[None]
```

```python
import jax
import jax.numpy as jnp
from jax import lax
from jax.experimental import pallas as pl
from jax.experimental.pallas import tpu as pltpu

F32 = jnp.float32
BF16 = jnp.bfloat16

D_MODEL = 2048
POOL_WINDOWS = (2, 4, 8, 16)
POOL_HALO = 16
GROUP_CH = 128
MIX_W = 4 * GROUP_CH
GMLP_CHUNK = 128
HEAD_DIM = 64
KV_GROUPS = 4
HEADS_PER_GROUP = 4
Q_W = 16 * HEAD_DIM
KV_W = KV_GROUPS * HEAD_DIM
CMP_BLOCK = 32
CMP_STRIDE = 16
SEL_BLOCK = 64
SEL_TOPN = 16
WINDOW = 512
FORCE_BONUS = 1000.0
LOG2E = 1.4426950408889634
NEG = -1e30
N_GATE = 3 * 16
D_FF = 4 * D_MODEL

Q_BLOCK = 256
QL = HEADS_PER_GROUP * Q_BLOCK
CMP_PER_QB = Q_BLOCK // CMP_STRIDE
WIN_CHUNK = 128
PAD_TILES = WINDOW // Q_BLOCK
SEL_TILE = 512
KS_AUG_W = 128
CMP_AUG_W = 256
V_AUG_ROWS = HEAD_DIM + 16
P_CHUNK = 64
QK_PART = 256

OFF_POOL, OFF_U, OFF_V, OFF_Q, OFF_KV, OFF_BG = 0, 512, 1024, 1536, 2560, 4096
PROJ_W = OFF_BG + 3 * D_MODEL

VMEM_LIMIT = 60 * 1024 * 1024


def _cparams(*sem):
    return pltpu.CompilerParams(dimension_semantics=sem, vmem_limit_bytes=VMEM_LIMIT)


def _dot(a, b):
    return jnp.dot(a, b, preferred_element_type=F32)


def _rms(x, g):
    return x * lax.rsqrt(jnp.mean(x * x, axis=-1, keepdims=True) + 1e-6) * g


def _mod_kernel(c_ref, w_ref, b_ref, o_ref):
    c = c_ref[...]
    act = c * jax.nn.sigmoid(c)
    o_ref[0] = jnp.sum(act * w_ref[0], axis=0, keepdims=True) + b_ref[0]


def _modulation(c, w_ada, b_ada):
    n_layer, d, n_out = w_ada.shape
    tn = 1024
    return pl.pallas_call(
        _mod_kernel,
        grid=(n_layer, n_out // tn),
        in_specs=[
            pl.BlockSpec((d, 1), lambda l, j: (0, 0)),
            pl.BlockSpec((1, d, tn), lambda l, j: (l, 0, j)),
            pl.BlockSpec((1, 1, tn), lambda l, j: (l, 0, j)),
        ],
        out_specs=pl.BlockSpec((1, 1, tn), lambda l, j: (l, 0, j)),
        out_shape=jax.ShapeDtypeStruct((n_layer, 1, n_out), F32),
        compiler_params=_cparams("parallel", "parallel"),
        name="adaln_mod",
    )(c.reshape(d, 1), w_ada, b_ada.reshape(n_layer, 1, n_out))


def _inproj_kernel(x_ref, g_ref, sc_ref, sh_ref, w_ref, wng_ref, cs_ref, o_ref, ng_ref, h_scr):
    @pl.when(pl.program_id(1) == 0)
    def _():
        h = _rms(x_ref[...], g_ref[...]) * (1.0 + sc_ref[...]) + sh_ref[...]
        hb = h.astype(BF16)
        h_scr[...] = hb
        ng_ref[...] = _dot(hb, wng_ref[...])

    o_ref[...] = (_dot(h_scr[...], w_ref[...]) * cs_ref[...]).astype(o_ref.dtype)


def _in_projection(x, g, scale, shift, w_main, w_ng, col_scale):
    s, d = x.shape
    tm, tn = min(1024, s), 2048
    row = lambda i, j: (0, 0)
    return pl.pallas_call(
        _inproj_kernel,
        grid=(s // tm, PROJ_W // tn),
        in_specs=[
            pl.BlockSpec((tm, d), lambda i, j: (i, 0)),
            pl.BlockSpec((1, d), row),
            pl.BlockSpec((1, d), row),
            pl.BlockSpec((1, d), row),
            pl.BlockSpec((d, tn), lambda i, j: (0, j)),
            pl.BlockSpec((d, 128), row),
            pl.BlockSpec((1, tn), lambda i, j: (0, j)),
        ],
        out_specs=[
            pl.BlockSpec((tm, tn), lambda i, j: (i, j)),
            pl.BlockSpec((tm, 128), lambda i, j: (i, 0)),
        ],
        out_shape=[
            jax.ShapeDtypeStruct((s, PROJ_W), BF16),
            jax.ShapeDtypeStruct((s, 128), F32),
        ],
        scratch_shapes=[pltpu.VMEM((tm, d), BF16)],
        compiler_params=_cparams("parallel", "arbitrary"),
        name="in_proj",
    )(x, g, scale, shift, w_main, w_ng, col_scale)


def _mixer_kernel(a_ref, halo_ref, u_ref, v_ref, pw_ref, ps_ref, lg_ref, lb_ref, ws_ref, bs_ref,
                  ya_ref, yb_ref):
    i = pl.program_id(0)
    tm = a_ref.shape[0]
    a = a_ref[...].astype(F32)
    halo = jnp.where(i > 0, halo_ref[...].astype(F32), 0.0)
    ext = jnp.concatenate([halo, a], axis=0)
    p2 = ext[1:] + ext[:-1]
    p4 = p2[2:] + p2[:-2]
    p8 = p4[4:] + p4[:-4]
    p16 = p8[8:] + p8[:-8]
    sums = (p2[15:15 + tm], p4[13:13 + tm], p8[9:9 + tm], p16[1:1 + tm])
    t = i * tm + lax.broadcasted_iota(jnp.int32, (tm, 1), 0)
    for gi, w in enumerate(POOL_WINDOWS):
        cols = slice(gi * GROUP_CH, (gi + 1) * GROUP_CH)
        cnt = jnp.minimum(t + 1, w).astype(F32)
        pooled = sums[gi][:, cols] / cnt - a[:, cols]
        y = _dot(pooled.astype(BF16), pw_ref[gi])
        ya_ref[:, cols] = (y * ps_ref[:, cols]).astype(ya_ref.dtype)

    u = jax.nn.gelu(u_ref[...].astype(F32))
    v = jax.nn.gelu(v_ref[...].astype(F32))
    mu = jnp.mean(v, axis=-1, keepdims=True)
    var = jnp.mean(jnp.square(v - mu), axis=-1, keepdims=True)
    vn = ((v - mu) * lax.rsqrt(var + 1e-5) * lg_ref[...] + lb_ref[...]).astype(BF16)
    r = lax.broadcasted_iota(jnp.int32, (GMLP_CHUNK, GMLP_CHUNK), 0)
    c = lax.broadcasted_iota(jnp.int32, (GMLP_CHUNK, GMLP_CHUNK), 1)
    for gi in range(4):
        cols = slice(gi * GROUP_CH, (gi + 1) * GROUP_CH)
        wsm = jnp.where(r >= c, ws_ref[gi], 0.0).astype(BF16)
        bias = bs_ref[:, gi:gi + 1]
        for ck in range(tm // GMLP_CHUNK):
            rows = slice(ck * GMLP_CHUNK, (ck + 1) * GMLP_CHUNK)
            mixed = _dot(wsm, vn[rows, cols]) + bias
            yb_ref[rows, cols] = (u[rows, cols] * mixed).astype(yb_ref.dtype)


def _mixers(proj, pool_w, pool_scale, ln_g, ln_b, ws, bs_t):
    s = proj.shape[0]
    tm = min(512, s)
    hb = tm // POOL_HALO
    const2 = lambda i: (0, 0)
    const3 = lambda i: (0, 0, 0)
    return pl.pallas_call(
        _mixer_kernel,
        grid=(s // tm,),
        in_specs=[
            pl.BlockSpec((tm, MIX_W), lambda i: (i, OFF_POOL // MIX_W)),
            pl.BlockSpec((POOL_HALO, MIX_W), lambda i: (jnp.maximum(i * hb - 1, 0), OFF_POOL // MIX_W)),
            pl.BlockSpec((tm, MIX_W), lambda i: (i, OFF_U // MIX_W)),
            pl.BlockSpec((tm, MIX_W), lambda i: (i, OFF_V // MIX_W)),
            pl.BlockSpec((4, GROUP_CH, GROUP_CH), const3),
            pl.BlockSpec((1, MIX_W), const2),
            pl.BlockSpec((1, MIX_W), const2),
            pl.BlockSpec((1, MIX_W), const2),
            pl.BlockSpec((4, GMLP_CHUNK, GMLP_CHUNK), const3),
            pl.BlockSpec((GMLP_CHUNK, 4), const2),
        ],
        out_specs=[
            pl.BlockSpec((tm, MIX_W), lambda i: (i, 0)),
            pl.BlockSpec((tm, MIX_W), lambda i: (i, 0)),
        ],
        out_shape=[jax.ShapeDtypeStruct((s, MIX_W), BF16)] * 2,
        compiler_params=_cparams("parallel"),
        name="mixers",
    )(proj, proj, proj, proj, pool_w, pool_scale, ln_g, ln_b, ws, bs_t)


def _compress_kernel(x_ref, pos_ref, w1_ref, b1_ref, w2_ref, b2_ref, kc_ref, vct_ref):
    half = CMP_STRIDE * HEAD_DIM
    x = x_ref[0, 0].astype(F32)
    nc = x.shape[0]
    pos = pos_ref[0]
    first = _dot((x + pos[:, :half]).astype(BF16), w1_ref[0, :half, :])
    second = _dot((x + pos[:, half:]).astype(BF16), w1_ref[0, half:, :])
    hid = jax.nn.gelu(first + pltpu.roll(second, nc - 1, 0) + b1_ref[0])
    out = _dot(hid.astype(BF16), w2_ref[0]) + b2_ref[0]

    @pl.when(pl.program_id(1) == 0)
    def _():
        lane = lax.broadcasted_iota(jnp.int32, (nc, 128), 1)
        grp = lax.broadcasted_iota(jnp.int32, (nc, 128), 0) // 8
        kc_ref[0, :, 0:128] = (out + (lane - HEAD_DIM == grp).astype(F32)).astype(BF16)
        kc_ref[0, :, 128:] = (lane + (128 - HEAD_DIM) == grp).astype(BF16)

    @pl.when(pl.program_id(1) == 1)
    def _():
        ones_row = lax.broadcasted_iota(jnp.int32, (V_AUG_ROWS, nc), 0) == HEAD_DIM
        vct_ref[0] = (out.T[0:V_AUG_ROWS] + ones_row.astype(F32)).astype(BF16)


def _compress(xkv, pos, w1, b1, w2, b2):
    _, n_group, nc, width = xkv.shape
    assert CMP_AUG_W == 256 and HEAD_DIM + nc // 8 <= CMP_AUG_W
    per_kv = lambda g, a: (a, 0, 0)
    per_group = lambda g, a: (g, 0, 0)
    return pl.pallas_call(
        _compress_kernel,
        grid=(n_group, 2),
        in_specs=[
            pl.BlockSpec((1, 1, nc, width), lambda g, a: (a, g, 0, 0)),
            pl.BlockSpec((1, 1, 2 * width), per_kv),
            pl.BlockSpec((1, 2 * width, 128), per_kv),
            pl.BlockSpec((1, 1, 128), per_kv),
            pl.BlockSpec((1, 128, 128), per_kv),
            pl.BlockSpec((1, 1, 128), per_kv),
        ],
        out_specs=[pl.BlockSpec((1, nc, CMP_AUG_W), per_group), pl.BlockSpec((1, V_AUG_ROWS, nc), per_group)],
        out_shape=[jax.ShapeDtypeStruct((n_group, nc, CMP_AUG_W), BF16),
                   jax.ShapeDtypeStruct((n_group, V_AUG_ROWS, nc), BF16)],
        compiler_params=_cparams("parallel", "arbitrary"),
        name="compress_kv",
    )(xkv, pos, w1, b1, w2, b2)


def _operand_kernel(q_lo_ref, q_hi_ref, sel_ref, win_ref, ng_ref, qt_ref, ks_ref, vst_ref, kw_ref, vwt_ref, gl_ref):
    i = pl.program_id(0)
    n_tiles = pl.num_programs(0) - PAD_TILES
    rows = q_lo_ref.shape[0]
    lane = lax.broadcasted_iota(jnp.int32, (rows, 128), 1)
    pos = i * rows + lax.broadcasted_iota(jnp.int32, (rows, 128), 0)
    ones_rows = (lax.broadcasted_iota(jnp.int32, (V_AUG_ROWS - HEAD_DIM, rows), 0) == 0).astype(BF16)

    def key_pair(slab, idx):
        onehot = (lane - HEAD_DIM == idx).astype(F32)
        left = jnp.where(lane < HEAD_DIM, slab, onehot)
        right = jnp.where(lane < HEAD_DIM, pltpu.roll(slab, HEAD_DIM, 1), onehot)
        return left.astype(BF16), right.astype(BF16)

    def transposed_pair(slab):
        t = slab.T.astype(BF16)
        return t[:HEAD_DIM], t[HEAD_DIM:]

    def write_kv(src, keep, idx, k_ref, vt_ref):
        for pair in range(KV_GROUPS // 2):
            lanes = slice(pair * 128, (pair + 1) * 128)
            k_slab = jnp.where(keep, src[:, lanes].astype(F32), 0.0)
            v_slab = jnp.where(keep, src[:, KV_W + pair * 128:KV_W + (pair + 1) * 128].astype(F32), 0.0)
            for g, k, vt in zip((2 * pair, 2 * pair + 1), key_pair(k_slab, idx), transposed_pair(v_slab)):
                k_ref[g] = k
                vt_ref[g, 0:HEAD_DIM, :] = vt
                vt_ref[g, HEAD_DIM:, :] = ones_rows

    @pl.when(i < n_tiles)
    def _():
        for half, ref in enumerate((q_lo_ref, q_hi_ref)):
            for gg in range(2):
                for pp in range(HEADS_PER_GROUP // 2):
                    lanes = slice(gg * 256 + pp * 128, gg * 256 + (pp + 1) * 128)
                    for h, t in zip((2 * pp, 2 * pp + 1), transposed_pair(ref[:, lanes].astype(F32))):
                        qt_ref[2 * half + gg, 0, :, h * Q_BLOCK:(h + 1) * Q_BLOCK] = t
        write_kv(sel_ref, True, (pos // SEL_BLOCK) % (SEL_TILE // SEL_BLOCK), ks_ref, vst_ref)
        logits_t = ng_ref[...].T
        for g in range(KV_GROUPS):
            for h in range(HEADS_PER_GROUP):
                for br in range(3):
                    r = (g * HEADS_PER_GROUP + h) * 3 + br
                    gl_ref[g, 0, br:br + 1, h * Q_BLOCK:(h + 1) * Q_BLOCK] = logits_t[r:r + 1, :]

    write_kv(win_ref, i >= PAD_TILES, (pos // WIN_CHUNK) % 8, kw_ref, vwt_ref)


def _attention_operands(proj, ngate):
    s = proj.shape[0]
    n_tiles = s // Q_BLOCK
    blk = MIX_W
    real = lambda i: jnp.minimum(i, n_tiles - 1)
    src = lambda c: pl.BlockSpec((Q_BLOCK, blk), lambda i: (real(i), c))
    return pl.pallas_call(
        _operand_kernel,
        grid=(n_tiles + PAD_TILES,),
        in_specs=[src(OFF_Q // blk), src(OFF_Q // blk + 1), src(OFF_KV // blk + 1),
                  pl.BlockSpec((Q_BLOCK, blk), lambda i: (jnp.maximum(i - PAD_TILES, 0), OFF_KV // blk + 2)),
                  pl.BlockSpec((Q_BLOCK, 128), lambda i: (real(i), 0))],
        out_specs=[
            pl.BlockSpec((KV_GROUPS, 1, HEAD_DIM, QL), lambda i: (0, real(i), 0, 0)),
            pl.BlockSpec((KV_GROUPS, Q_BLOCK, KS_AUG_W), lambda i: (0, real(i), 0)),
            pl.BlockSpec((KV_GROUPS, V_AUG_ROWS, Q_BLOCK), lambda i: (0, 0, real(i))),
            pl.BlockSpec((KV_GROUPS, Q_BLOCK, KS_AUG_W), lambda i: (0, i, 0)),
            pl.BlockSpec((KV_GROUPS, V_AUG_ROWS, Q_BLOCK), lambda i: (0, 0, i)),
            pl.BlockSpec((KV_GROUPS, 1, 3, QL), lambda i: (0, real(i), 0, 0)),
        ],
        out_shape=[
            jax.ShapeDtypeStruct((KV_GROUPS, n_tiles, HEAD_DIM, QL), BF16),
            jax.ShapeDtypeStruct((KV_GROUPS, s, KS_AUG_W), BF16),
            jax.ShapeDtypeStruct((KV_GROUPS, V_AUG_ROWS, s), BF16),
            jax.ShapeDtypeStruct((KV_GROUPS, s + WINDOW, KS_AUG_W), BF16),
            jax.ShapeDtypeStruct((KV_GROUPS, V_AUG_ROWS, s + WINDOW), BF16),
            jax.ShapeDtypeStruct((KV_GROUPS, n_tiles, 3, QL), F32),
        ],
        compiler_params=_cparams("arbitrary"),
        name="attn_operands",
    )(proj, proj, proj, proj, ngate)


def _nsa_kernel(qt_ref, kc_ref, vct_ref, ks_ref, vst_ref, kw_ref, vwt_ref, gl_ref, o_ref,
                ps_scr, bias_scr, diag_bias_scr, qc_scr, qw_scr, qs_scr, sc_scr, s0_scr, s1_scr, p0_scr, p1_scr):
    qb = pl.program_id(1)
    t0 = qb * Q_BLOCK
    t0a = pl.multiple_of(t0, Q_BLOCK)
    qt = qt_ref[0, 0]
    nc = kc_ref.shape[1]
    n_sel = bias_scr.shape[0]
    q_rel = lax.broadcasted_iota(jnp.int32, (1, QL), 1) % Q_BLOCK
    tq = t0 + q_rel

    def bias_rows(cond, rows):
        return jnp.where(cond, NEG, 0.0).astype(BF16) + jnp.zeros((rows, QL), BF16)

    def flash_update(s, carry, vt):
        m_prev, acc = carry
        m_new = jnp.maximum(m_prev, jnp.max(s, axis=0, keepdims=True))
        p = jnp.exp2((s - m_new).astype(BF16))
        return m_new, jnp.exp2(m_prev - m_new) * acc + _dot(vt, p)

    init = (jnp.full((1, QL), NEG, F32), jnp.zeros((V_AUG_ROWS, QL), F32))

    grp = lax.broadcasted_iota(jnp.int32, (qc_scr.shape[0] - HEAD_DIM, 1), 0)
    visible = CMP_PER_QB * (qb + 1)
    qc_scr[0:HEAD_DIM, :] = qt
    qc_scr[HEAD_DIM:, :] = bias_rows((8 * grp >= visible) & (grp < nc // 8), qc_scr.shape[0] - HEAD_DIM)
    band = CMP_PER_QB + 8
    r0 = pl.multiple_of(jnp.maximum(visible - band, 0), 8)
    band_end = CMP_STRIDE * (r0 + lax.broadcasted_iota(jnp.int32, (band, 1), 0)) + (CMP_BLOCK - 1)

    def compressed(rows):
        def run():
            sc_scr[0:rows, :] = _dot(kc_ref[0, 0:rows, :], qc_scr[...])
            sc_scr[pl.ds(r0, band), :] = jnp.where(band_end <= tq, sc_scr[pl.ds(r0, band), :], NEG)
            s_c = sc_scr[0:rows, :]
            m_c = jnp.maximum(jnp.max(s_c, axis=0, keepdims=True), 0.1 * NEG)
            e_c = jnp.exp2(s_c - m_c)
            o_aug = _dot(vct_ref[0, :, 0:rows], e_c.astype(BF16))
            inv_c = 1.0 / jnp.maximum(o_aug[HEAD_DIM:HEAD_DIM + 1], 1e-30)
            p_grp = None
            for h in range(HEADS_PER_GROUP):
                lanes = slice(h * Q_BLOCK, (h + 1) * Q_BLOCK)
                p_h = e_c[:, lanes] * inv_c[:, lanes]
                p_grp = p_h if p_grp is None else p_grp + p_h
            for c in range(Q_BLOCK // 128):
                ps_scr[c, 8:8 + rows, :] = p_grp[:, c * 128:(c + 1) * 128]
                if rows < nc:
                    ps_scr[c, 8 + rows:, :] = jnp.zeros((nc - rows, 128), F32)
            return o_aug[:HEAD_DIM] * inv_c
        return run

    ps_scr[:, 0:8, :] = jnp.zeros((Q_BLOCK // 128, 8, 128), F32)
    n_part = 8 if nc // 8 >= band else 4
    part = nc // n_part

    def choose(lo, hi):
        if lo == hi:
            return compressed(lo * part)
        mid = (lo + hi) // 2
        return lambda: lax.cond(visible <= mid * part, choose(lo, mid), choose(mid + 1, hi))

    o_c = choose(1, n_part)()
    imp = jnp.concatenate([sum(ps_scr[c, pl.ds(k, n_sel, stride=4), :] for k in range(7, 12))
                           for c in range(Q_BLOCK // 128)], axis=1)

    n_tri = Q_BLOCK // WIN_CHUNK
    n_chunk = WINDOW // WIN_CHUNK + n_tri
    span = n_chunk * WIN_CHUNK
    rho = lax.broadcasted_iota(jnp.int32, (16, 1), 0)
    first_real = WINDOW // WIN_CHUNK - n_tri * qb
    qw_scr[0:HEAD_DIM, :] = qt
    qw_scr[HEAD_DIM:HEAD_DIM + 16, :] = bias_rows((rho < 8) & (((rho - n_tri * qb) & 7) < first_real), 16)
    qw_scr[HEAD_DIM + 16:, :] = jnp.zeros((qw_scr.shape[0] - HEAD_DIM - 16, QL), BF16)
    s_w = _dot(kw_ref[0, pl.ds(t0a, span), :], qw_scr[...])
    i_rel = lax.broadcasted_iota(jnp.int32, (Q_BLOCK, 1), 0)
    s_w = jnp.concatenate([jnp.where(i_rel > q_rel, s_w[:Q_BLOCK], NEG), s_w[Q_BLOCK:WINDOW],
                           jnp.where(i_rel <= q_rel, s_w[WINDOW:], NEG)], axis=0)
    _, acc_w = flash_update(s_w, init, vwt_ref[0, :, pl.ds(t0a, span)])

    s0_scr[...] = _dot(ks_ref[0, 0:SEL_TILE, 0:HEAD_DIM], qt)

    j_idx = lax.broadcasted_iota(jnp.int32, (n_sel, Q_BLOCK), 0)
    cur = (t0 + lax.broadcasted_iota(jnp.int32, (n_sel, Q_BLOCK), 1)) // SEL_BLOCK
    forced = (j_idx == 0) | (j_idx == cur) | (j_idx == cur - 1)
    valid = j_idx <= cur
    score = jnp.where(forced, -2.0, jnp.where(valid, imp, -1.0))
    n_pick = SEL_TOPN - 3

    fast = score
    for _ in range(n_pick):
        fast = jnp.where(fast == jnp.max(fast, axis=0, keepdims=True), -2.0, fast)
    retired = (fast < -1.5) & valid & jnp.logical_not(forced)
    most_retired = jnp.max(jnp.sum(retired.astype(F32), axis=0, keepdims=True))

    def ranked_with_ties():
        def pick_one(_, sc):
            best = jnp.max(sc, axis=0, keepdims=True)
            first = jnp.min(jnp.where(sc == best, j_idx, n_sel), axis=0, keepdims=True)
            return jnp.where(j_idx == first, -2.0, sc)
        return lax.fori_loop(0, n_pick, pick_one, score)

    ranked = lax.cond(most_retired > n_pick, ranked_with_ties, lambda: fast)
    chosen = valid & (ranked < -1.5)
    diag_bias_scr[...] = jnp.where(chosen, 0.0, NEG)
    bias_scr[...] = jnp.where(chosen & (j_idx * SEL_BLOCK < t0), 0.0, NEG)

    blocks_per_tile = SEL_TILE // SEL_BLOCK
    last_tile = ks_ref.shape[1] // SEL_TILE - 1
    qs_scr[0:HEAD_DIM, :] = qt
    qs_scr[HEAD_DIM + 16:, :] = jnp.zeros((qs_scr.shape[0] - HEAD_DIM - 16, QL), BF16)

    def tile_start(kt):
        return pl.multiple_of(jnp.clip(kt, 0, last_tile) * SEL_TILE, SEL_TILE)

    def set_bias_rows(table, kt):
        b0 = pl.multiple_of(jnp.minimum(kt, last_tile) * blocks_per_tile, blocks_per_tile)
        b16 = jnp.concatenate([table[pl.ds(b0, blocks_per_tile), :],
                               jnp.zeros((16 - blocks_per_tile, Q_BLOCK), F32)], axis=0)
        qs_scr[HEAD_DIM:HEAD_DIM + 16, :] = jnp.concatenate([b16] * HEADS_PER_GROUP, axis=1).astype(BF16)

    set_bias_rows(diag_bias_scr, t0 // SEL_TILE)
    s_d = _dot(ks_ref[0, pl.ds(t0a, Q_BLOCK), :], qs_scr[...])
    m_d, acc_d = flash_update(jnp.where(i_rel <= q_rel, s_d, NEG), init, vst_ref[0, :, pl.ds(t0a, Q_BLOCK)])

    def half_step(kt, s_cur, s_nxt, p_cur, p_prev, carry):
        m_prev, alpha_prev, acc = carry
        acc = alpha_prev * acc + _dot(vst_ref[0, :, pl.ds(tile_start(kt - 1), SEL_TILE)], p_prev[...])
        m8 = jnp.max(s_cur[...].reshape(SEL_TILE // 8, 8, QL), axis=0)
        m_new = jnp.maximum(m_prev, jnp.max(m8, axis=0, keepdims=True))
        set_bias_rows(bias_scr, kt + 1)
        k_nxt = tile_start(kt + 1)
        for h in range(SEL_TILE // QK_PART):
            for c in range(QK_PART // P_CHUNK):
                rows = slice(h * QK_PART + c * P_CHUNK, h * QK_PART + (c + 1) * P_CHUNK)
                p_cur[rows, :] = jnp.exp2((s_cur[rows, :] - m_new).astype(BF16))
            part = slice(h * QK_PART, (h + 1) * QK_PART)
            s_nxt[part, :] = _dot(ks_ref[0, pl.ds(k_nxt + h * QK_PART, QK_PART), :], qs_scr[...])
        return m_new, jnp.exp2(m_prev - m_new), acc

    def pair_step(i, carry):
        carry = half_step(2 * i, s0_scr, s1_scr, p0_scr, p1_scr, carry)
        return half_step(2 * i + 1, s1_scr, s0_scr, p1_scr, p0_scr, carry)

    n_pairs = ((t0 + SEL_TILE - 1) // SEL_TILE + 1) // 2
    for b in range(blocks_per_tile):
        rows = slice(b * SEL_BLOCK, (b + 1) * SEL_BLOCK)
        s0_scr[rows, :] = s0_scr[rows, :] + jnp.concatenate([bias_scr[b:b + 1, :]] * HEADS_PER_GROUP, axis=1)
    p1_scr[...] = jnp.zeros(p1_scr.shape, BF16)
    carry = (m_d, jnp.ones((1, QL), F32), acc_d)
    _, alpha_last, acc_s = lax.fori_loop(0, n_pairs, pair_step, carry)
    acc_s = alpha_last * acc_s + _dot(vst_ref[0, :, pl.ds(tile_start(2 * n_pairs - 1), SEL_TILE)], p1_scr[...])

    gate = jax.nn.sigmoid(gl_ref[0, 0])
    o = (gate[0:1] * o_c + gate[1:2] * (acc_s[:HEAD_DIM] / acc_s[HEAD_DIM:HEAD_DIM + 1])
         + gate[2:3] * (acc_w[:HEAD_DIM] / acc_w[HEAD_DIM:HEAD_DIM + 1]))
    stacked = jnp.concatenate([o[:, p * Q_BLOCK:(p + 1) * Q_BLOCK] for p in range(HEADS_PER_GROUP)], axis=0)
    o_ref[...] = stacked.T.astype(o_ref.dtype)


def _sparse_attention(qt, kc_aug, vc_aug_t, ks_aug, vs_aug_t, kw_aug, vw_aug_t, gate_logits, s):
    n_group, n_qb = qt.shape[:2]
    nc = kc_aug.shape[1]
    n_sel = s // SEL_BLOCK
    assert HEAD_DIM + nc // 8 <= CMP_AUG_W and nc // 4 >= CMP_PER_QB + 8 and WINDOW // WIN_CHUNK + Q_BLOCK // WIN_CHUNK <= 8
    per_step = lambda g, i: (g, i, 0, 0)
    per_group = lambda shape: pl.BlockSpec((1,) + shape, lambda g, i: (g, 0, 0), pipeline_mode=pl.Buffered(1))
    tile_f32 = pltpu.VMEM((SEL_TILE, QL), F32)
    tile_bf16 = pltpu.VMEM((SEL_TILE, QL), BF16)
    sel_table = pltpu.VMEM((n_sel, Q_BLOCK), F32)
    return pl.pallas_call(
        _nsa_kernel,
        grid=(n_group, n_qb),
        in_specs=[
            pl.BlockSpec((1, 1, HEAD_DIM, QL), per_step),
            per_group((nc, CMP_AUG_W)),
            per_group((V_AUG_ROWS, nc)),
            per_group((s, KS_AUG_W)),
            per_group((V_AUG_ROWS, s)),
            per_group((s + WINDOW, KS_AUG_W)),
            per_group((V_AUG_ROWS, s + WINDOW)),
            pl.BlockSpec((1, 1, 3, QL), per_step),
        ],
        out_specs=pl.BlockSpec((Q_BLOCK, HEADS_PER_GROUP * HEAD_DIM), lambda g, i: (i, g)),
        out_shape=jax.ShapeDtypeStruct((s, Q_W), BF16),
        scratch_shapes=[pltpu.VMEM((Q_BLOCK // 128, 8 + nc, 128), F32), sel_table, sel_table,
                        pltpu.VMEM((CMP_AUG_W, QL), BF16), pltpu.VMEM((KS_AUG_W, QL), BF16),
                        pltpu.VMEM((KS_AUG_W, QL), BF16), pltpu.VMEM((nc, QL), F32),
                        tile_f32, tile_f32, tile_bf16, tile_bf16],
        compiler_params=_cparams("parallel", "arbitrary"),
        name="sparse_attn",
    )(qt, kc_aug, vc_aug_t, ks_aug, vs_aug_t, kw_aug, vw_aug_t, gate_logits)


def _merge_kernel(x_ref, ya_ref, yb_ref, yc_ref, g0_ref, g1_ref, g2_ref, wp_ref, wg_ref, wn_ref,
                  wo_ref, ng_ref, gate_ref, o_ref):
    merged = jax.nn.sigmoid(g0_ref[...].astype(F32)) * _dot(ya_ref[...], wp_ref[0])
    merged += jax.nn.sigmoid(g1_ref[...].astype(F32)) * _dot(yb_ref[...], wg_ref[0])
    merged += jax.nn.sigmoid(g2_ref[...].astype(F32)) * _dot(yc_ref[...], wn_ref[0])
    y = _dot(merged.astype(BF16), wo_ref[0])
    o_ref[...] = x_ref[...] + gate_ref[...] * _rms(y, ng_ref[...])


def _merge(x, ya, yb, yc, proj, wp, wg, wn, wo, layer, norm_g, gate):
    s, d = x.shape
    tm = min(256, s)
    const = lambda i: (0, 0)
    rows = lambda i: (i, 0)
    whole = lambda a: pl.BlockSpec((1,) + a.shape[1:], lambda i: (layer, 0, 0), pipeline_mode=pl.Buffered(1))
    bg = OFF_BG // d
    return pl.pallas_call(
        _merge_kernel,
        grid=(s // tm,),
        in_specs=[
            pl.BlockSpec((tm, d), rows),
            pl.BlockSpec((tm, MIX_W), rows),
            pl.BlockSpec((tm, MIX_W), rows),
            pl.BlockSpec((tm, Q_W), rows),
            pl.BlockSpec((tm, d), lambda i: (i, bg)),
            pl.BlockSpec((tm, d), lambda i: (i, bg + 1)),
            pl.BlockSpec((tm, d), lambda i: (i, bg + 2)),
            whole(wp), whole(wg), whole(wn), whole(wo),
            pl.BlockSpec((1, d), const),
            pl.BlockSpec((1, d), const),
        ],
        out_specs=pl.BlockSpec((tm, d), rows),
        out_shape=jax.ShapeDtypeStruct((s, d), F32),
        compiler_params=_cparams("parallel"),
        name="merge_out",
    )(x, ya, yb, yc, proj, proj, proj, wp, wg, wn, wo, norm_g, gate)


def _ffn_kernel(x_ref, gi_ref, sc_ref, sh_ref, w1_ref, w2_ref, go_ref, gate_ref, o_ref, h_scr):
    j = pl.program_id(1)

    @pl.when(j == 0)
    def _():
        h = _rms(x_ref[...], gi_ref[...]) * (1.0 + sc_ref[...]) + sh_ref[...]
        h_scr[...] = h.astype(BF16)
        o_ref[...] = jnp.zeros_like(o_ref)

    a = jnp.square(jnp.maximum(_dot(h_scr[...], w1_ref[0]), 0.0))
    o_ref[...] += _dot(a.astype(BF16), w2_ref[0])

    @pl.when(j == pl.num_programs(1) - 1)
    def _():
        o_ref[...] = x_ref[...] + gate_ref[...] * _rms(o_ref[...], go_ref[...])


def _ffn(x, g_in, scale, shift, w1, w2, layer, g_out, gate):
    s, d = x.shape
    tm, tf = min(1024, s), 1024
    const = lambda i, j: (0, 0)
    rows = lambda i, j: (i, 0)
    return pl.pallas_call(
        _ffn_kernel,
        grid=(s // tm, D_FF // tf),
        in_specs=[
            pl.BlockSpec((tm, d), rows, pipeline_mode=pl.Buffered(1)),
            pl.BlockSpec((1, d), const),
            pl.BlockSpec((1, d), const),
            pl.BlockSpec((1, d), const),
            pl.BlockSpec((1, d, tf), lambda i, j: (layer, 0, j)),
            pl.BlockSpec((1, tf, d), lambda i, j: (layer, j, 0)),
            pl.BlockSpec((1, d), const),
            pl.BlockSpec((1, d), const),
        ],
        out_specs=pl.BlockSpec((tm, d), rows),
        out_shape=jax.ShapeDtypeStruct((s, d), F32),
        scratch_shapes=[pltpu.VMEM((tm, d), BF16)],
        compiler_params=_cparams("parallel", "arbitrary"),
        name="ffn",
    )(x, g_in, scale, shift, w1, w2, g_out, gate)


def _token_mixing(x, mod, norm_g, w_in, pool_w, pool_scale, ln_g, ln_b, ws, bs, cmp_pos, cmp_w1,
                  cmp_b1, cmp_w2, cmp_b2, branch_weights, layer):
    s, d = x.shape
    ng0 = OFF_BG
    w_main = jnp.concatenate([w_in[:, :ng0], w_in[:, ng0 + N_GATE:]], axis=1).astype(BF16)
    w_ng = jnp.pad(w_in[:, ng0:ng0 + N_GATE], ((0, 0), (0, 128 - N_GATE))).astype(BF16)
    row = lambda v: v.reshape(1, -1)

    col_scale = jnp.ones((1, PROJ_W), F32).at[:, OFF_Q:OFF_KV].set(HEAD_DIM ** -0.5 * LOG2E)
    proj, ngate = _in_projection(x, row(norm_g[0]), row(mod[1]), row(mod[0]), w_main, w_ng, col_scale)

    ya, yb = _mixers(proj, pool_w.astype(BF16), row(pool_scale), row(ln_g), row(ln_b), ws, bs.T)

    qt, ks_aug, vs_aug_t, kw_aug, vw_aug_t, gl = _attention_operands(proj, ngate)
    kv_cmp = proj[:, OFF_KV:OFF_KV + 2 * KV_W].reshape(s, 2, KV_GROUPS, HEAD_DIM).transpose(1, 2, 0, 3)
    xkv = kv_cmp.reshape(2, KV_GROUPS, s // CMP_STRIDE, CMP_STRIDE * HEAD_DIM)
    lane_pad = ((0, 0), (0, 0), (0, 128 - HEAD_DIM))
    kc_aug, vc_aug_t = _compress(xkv, cmp_pos.reshape(2, 1, CMP_BLOCK * HEAD_DIM), cmp_w1.astype(BF16),
                                 cmp_b1.reshape(2, 1, -1), jnp.pad(cmp_w2, lane_pad).astype(BF16),
                                 jnp.pad(cmp_b2.reshape(2, 1, -1), lane_pad))
    yc = _sparse_attention(qt, kc_aug, vc_aug_t, ks_aug, vs_aug_t, kw_aug, vw_aug_t, gl, s)

    return _merge(x, ya, yb, yc, proj, *branch_weights, layer, row(norm_g[1]), row(mod[2]))


def kernel(x, c, norm_g, w_ada, b_ada, w_in, pool_w, pool_scale, gmlp_ln_g, gmlp_ln_b, gmlp_ws, gmlp_bs,
           cmp_pos, cmp_w1, cmp_b1, cmp_w2, cmp_b2, w_br_pool, w_br_gmlp, w_br_nsa, w_out, w_ff1, w_ff2):
    b, s, d = x.shape
    assert b == 1 and d == D_MODEL and s % 1024 == 0
    n_layer = w_ada.shape[0]
    mod_all = _modulation(c, w_ada, b_ada).reshape(n_layer, 6, d)
    xs = x[0]
    row = lambda v: v.reshape(1, -1)
    branch_weights = tuple(w.astype(BF16) for w in (w_br_pool, w_br_gmlp, w_br_nsa, w_out))
    w_ff1_b, w_ff2_b = w_ff1.astype(BF16), w_ff2.astype(BF16)
    for l in range(n_layer):
        mod = mod_all[l]
        xs = _token_mixing(xs, mod, norm_g[l], w_in[l], pool_w[l], pool_scale[l], gmlp_ln_g[l],
                           gmlp_ln_b[l], gmlp_ws[l], gmlp_bs[l], cmp_pos[l], cmp_w1[l], cmp_b1[l],
                           cmp_w2[l], cmp_b2[l], branch_weights, l)
        xs = _ffn(xs, row(norm_g[l, 2]), row(mod[4]), row(mod[3]), w_ff1_b, w_ff2_b, l,
                  row(norm_g[l, 3]), row(mod[5]))
    return xs[None]
```

```python
import jax
import jax.numpy as jnp
from jax import lax
from jax.experimental import pallas as pl
from jax.experimental.pallas import tpu as pltpu

F32 = jnp.float32
BF16 = jnp.bfloat16

D_MODEL = 2048
POOL_WINDOWS = (2, 4, 8, 16)
POOL_HALO = 16
GROUP_CH = 128
MIX_W = 4 * GROUP_CH
GMLP_CHUNK = 128
HEAD_DIM = 64
KV_GROUPS = 4
HEADS_PER_GROUP = 4
Q_W = 16 * HEAD_DIM
KV_W = KV_GROUPS * HEAD_DIM
CMP_BLOCK = 32
CMP_STRIDE = 16
SEL_BLOCK = 64
SEL_TOPN = 16
WINDOW = 512
FORCE_BONUS = 1000.0
LOG2E = 1.4426950408889634
NEG = -1e30
N_GATE = 3 * 16
D_FF = 4 * D_MODEL

Q_BLOCK = 256
QL = HEADS_PER_GROUP * Q_BLOCK
CMP_PER_QB = Q_BLOCK // CMP_STRIDE
WIN_CHUNK = 128
PAD_TILES = WINDOW // Q_BLOCK
SEL_TILE = 512
KS_AUG_W = 128
CMP_AUG_W = 256
V_AUG_ROWS = HEAD_DIM + 16
P_CHUNK = 64
QK_PART = 512

OFF_POOL, OFF_U, OFF_V, OFF_Q, OFF_KV, OFF_BG = 0, 512, 1024, 1536, 2560, 4096
PROJ_W = OFF_BG + 3 * D_MODEL

VMEM_LIMIT = 60 * 1024 * 1024


def _cparams(*sem):
    return pltpu.CompilerParams(dimension_semantics=sem, vmem_limit_bytes=VMEM_LIMIT)


def _dot(a, b):
    return jnp.dot(a, b, preferred_element_type=F32)


def _rms(x, g):
    return x * lax.rsqrt(jnp.mean(x * x, axis=-1, keepdims=True) + 1e-6) * g


def _mod_kernel(c_ref, w_ref, b_ref, o_ref):
    c = c_ref[...]
    act = c * jax.nn.sigmoid(c)
    o_ref[0] = jnp.sum(act * w_ref[0], axis=0, keepdims=True) + b_ref[0]


def _modulation(c, w_ada, b_ada):
    n_layer, d, n_out = w_ada.shape
    tn = 1024
    return pl.pallas_call(
        _mod_kernel,
        grid=(n_layer, n_out // tn),
        in_specs=[
            pl.BlockSpec((d, 1), lambda l, j: (0, 0)),
            pl.BlockSpec((1, d, tn), lambda l, j: (l, 0, j)),
            pl.BlockSpec((1, 1, tn), lambda l, j: (l, 0, j)),
        ],
        out_specs=pl.BlockSpec((1, 1, tn), lambda l, j: (l, 0, j)),
        out_shape=jax.ShapeDtypeStruct((n_layer, 1, n_out), F32),
        compiler_params=_cparams("parallel", "parallel"),
        name="adaln_mod",
    )(c.reshape(d, 1), w_ada, b_ada.reshape(n_layer, 1, n_out))


def _inproj_kernel(x_ref, g_ref, sc_ref, sh_ref, w_ref, wng_ref, cs_ref, o_ref, ng_ref, h_scr):
    @pl.when(pl.program_id(1) == 0)
    def _():
        h = _rms(x_ref[...], g_ref[...]) * (1.0 + sc_ref[...]) + sh_ref[...]
        hb = h.astype(BF16)
        h_scr[...] = hb
        ng_ref[...] = _dot(hb, wng_ref[...])

    o_ref[...] = (_dot(h_scr[...], w_ref[...]) * cs_ref[...]).astype(o_ref.dtype)


def _in_projection(x, g, scale, shift, w_main, w_ng, col_scale):
    s, d = x.shape
    tm, tn = min(1024, s), 2048
    row = lambda i, j: (0, 0)
    return pl.pallas_call(
        _inproj_kernel,
        grid=(s // tm, PROJ_W // tn),
        in_specs=[
            pl.BlockSpec((tm, d), lambda i, j: (i, 0)),
            pl.BlockSpec((1, d), row),
            pl.BlockSpec((1, d), row),
            pl.BlockSpec((1, d), row),
            pl.BlockSpec((d, tn), lambda i, j: (0, j)),
            pl.BlockSpec((d, 128), row),
            pl.BlockSpec((1, tn), lambda i, j: (0, j)),
        ],
        out_specs=[
            pl.BlockSpec((tm, tn), lambda i, j: (i, j)),
            pl.BlockSpec((tm, 128), lambda i, j: (i, 0)),
        ],
        out_shape=[
            jax.ShapeDtypeStruct((s, PROJ_W), BF16),
            jax.ShapeDtypeStruct((s, 128), F32),
        ],
        scratch_shapes=[pltpu.VMEM((tm, d), BF16)],
        compiler_params=_cparams("parallel", "arbitrary"),
        name="in_proj",
    )(x, g, scale, shift, w_main, w_ng, col_scale)


def _mixer_kernel(a_ref, halo_ref, u_ref, v_ref, pw_ref, ps_ref, lg_ref, lb_ref, ws_ref, bs_ref,
                  ya_ref, yb_ref):
    i = pl.program_id(0)
    tm = a_ref.shape[0]
    a = a_ref[...].astype(F32)
    halo = jnp.where(i > 0, halo_ref[...].astype(F32), 0.0)
    ext = jnp.concatenate([halo, a], axis=0)
    p2 = ext[1:] + ext[:-1]
    p4 = p2[2:] + p2[:-2]
    p8 = p4[4:] + p4[:-4]
    p16 = p8[8:] + p8[:-8]
    sums = (p2[15:15 + tm], p4[13:13 + tm], p8[9:9 + tm], p16[1:1 + tm])
    t = i * tm + lax.broadcasted_iota(jnp.int32, (tm, 1), 0)
    for gi, w in enumerate(POOL_WINDOWS):
        cols = slice(gi * GROUP_CH, (gi + 1) * GROUP_CH)
        cnt = jnp.minimum(t + 1, w).astype(F32)
        pooled = sums[gi][:, cols] / cnt - a[:, cols]
        y = _dot(pooled.astype(BF16), pw_ref[gi])
        ya_ref[:, cols] = (y * ps_ref[:, cols]).astype(ya_ref.dtype)

    u = jax.nn.gelu(u_ref[...].astype(F32))
    v = jax.nn.gelu(v_ref[...].astype(F32))
    mu = jnp.mean(v, axis=-1, keepdims=True)
    var = jnp.mean(jnp.square(v - mu), axis=-1, keepdims=True)
    vn = ((v - mu) * lax.rsqrt(var + 1e-5) * lg_ref[...] + lb_ref[...]).astype(BF16)
    r = lax.broadcasted_iota(jnp.int32, (GMLP_CHUNK, GMLP_CHUNK), 0)
    c = lax.broadcasted_iota(jnp.int32, (GMLP_CHUNK, GMLP_CHUNK), 1)
    for gi in range(4):
        cols = slice(gi * GROUP_CH, (gi + 1) * GROUP_CH)
        wsm = jnp.where(r >= c, ws_ref[gi], 0.0).astype(BF16)
        bias = bs_ref[:, gi:gi + 1]
        for ck in range(tm // GMLP_CHUNK):
            rows = slice(ck * GMLP_CHUNK, (ck + 1) * GMLP_CHUNK)
            mixed = _dot(wsm, vn[rows, cols]) + bias
            yb_ref[rows, cols] = (u[rows, cols] * mixed).astype(yb_ref.dtype)


def _mixers(proj, pool_w, pool_scale, ln_g, ln_b, ws, bs_t):
    s = proj.shape[0]
    tm = min(512, s)
    hb = tm // POOL_HALO
    const2 = lambda i: (0, 0)
    const3 = lambda i: (0, 0, 0)
    return pl.pallas_call(
        _mixer_kernel,
        grid=(s // tm,),
        in_specs=[
            pl.BlockSpec((tm, MIX_W), lambda i: (i, OFF_POOL // MIX_W)),
            pl.BlockSpec((POOL_HALO, MIX_W), lambda i: (jnp.maximum(i * hb - 1, 0), OFF_POOL // MIX_W)),
            pl.BlockSpec((tm, MIX_W), lambda i: (i, OFF_U // MIX_W)),
            pl.BlockSpec((tm, MIX_W), lambda i: (i, OFF_V // MIX_W)),
            pl.BlockSpec((4, GROUP_CH, GROUP_CH), const3),
            pl.BlockSpec((1, MIX_W), const2),
            pl.BlockSpec((1, MIX_W), const2),
            pl.BlockSpec((1, MIX_W), const2),
            pl.BlockSpec((4, GMLP_CHUNK, GMLP_CHUNK), const3),
            pl.BlockSpec((GMLP_CHUNK, 4), const2),
        ],
        out_specs=[
            pl.BlockSpec((tm, MIX_W), lambda i: (i, 0)),
            pl.BlockSpec((tm, MIX_W), lambda i: (i, 0)),
        ],
        out_shape=[jax.ShapeDtypeStruct((s, MIX_W), BF16)] * 2,
        compiler_params=_cparams("parallel"),
        name="mixers",
    )(proj, proj, proj, proj, pool_w, pool_scale, ln_g, ln_b, ws, bs_t)


def _compress_kernel(x_ref, pos_ref, w1_ref, b1_ref, w2_ref, b2_ref, kc_ref, vct_ref):
    half = CMP_STRIDE * HEAD_DIM
    x = x_ref[0, 0].astype(F32)
    nc = x.shape[0]
    pos = pos_ref[0]
    first = _dot((x + pos[:, :half]).astype(BF16), w1_ref[0, :half, :])
    second = _dot((x + pos[:, half:]).astype(BF16), w1_ref[0, half:, :])
    hid = jax.nn.gelu(first + pltpu.roll(second, nc - 1, 0) + b1_ref[0])
    out = _dot(hid.astype(BF16), w2_ref[0]) + b2_ref[0]

    @pl.when(pl.program_id(1) == 0)
    def _():
        lane = lax.broadcasted_iota(jnp.int32, (nc, 128), 1)
        grp = lax.broadcasted_iota(jnp.int32, (nc, 128), 0) // 8
        kc_ref[0, :, 0:128] = (out + (lane - HEAD_DIM == grp).astype(F32)).astype(BF16)
        kc_ref[0, :, 128:] = (lane + (128 - HEAD_DIM) == grp).astype(BF16)

    @pl.when(pl.program_id(1) == 1)
    def _():
        ones_row = lax.broadcasted_iota(jnp.int32, (V_AUG_ROWS, nc), 0) == HEAD_DIM
        vct_ref[0] = (out.T[0:V_AUG_ROWS] + ones_row.astype(F32)).astype(BF16)


def _compress(xkv, pos, w1, b1, w2, b2):
    _, n_group, nc, width = xkv.shape
    assert CMP_AUG_W == 256 and HEAD_DIM + nc // 8 <= CMP_AUG_W
    per_kv = lambda g, a: (a, 0, 0)
    per_group = lambda g, a: (g, 0, 0)
    return pl.pallas_call(
        _compress_kernel,
        grid=(n_group, 2),
        in_specs=[
            pl.BlockSpec((1, 1, nc, width), lambda g, a: (a, g, 0, 0)),
            pl.BlockSpec((1, 1, 2 * width), per_kv),
            pl.BlockSpec((1, 2 * width, 128), per_kv),
            pl.BlockSpec((1, 1, 128), per_kv),
            pl.BlockSpec((1, 128, 128), per_kv),
            pl.BlockSpec((1, 1, 128), per_kv),
        ],
        out_specs=[pl.BlockSpec((1, nc, CMP_AUG_W), per_group), pl.BlockSpec((1, V_AUG_ROWS, nc), per_group)],
        out_shape=[jax.ShapeDtypeStruct((n_group, nc, CMP_AUG_W), BF16),
                   jax.ShapeDtypeStruct((n_group, V_AUG_ROWS, nc), BF16)],
        compiler_params=_cparams("parallel", "arbitrary"),
        name="compress_kv",
    )(xkv, pos, w1, b1, w2, b2)


def _operand_kernel(q_lo_ref, q_hi_ref, sel_ref, win_ref, ng_ref, qt_ref, ks_ref, vst_ref, kw_ref, vwt_ref, gl_ref):
    i = pl.program_id(0)
    n_tiles = pl.num_programs(0) - PAD_TILES
    rows = q_lo_ref.shape[0]
    lane = lax.broadcasted_iota(jnp.int32, (rows, 128), 1)
    pos = i * rows + lax.broadcasted_iota(jnp.int32, (rows, 128), 0)
    ones_rows = (lax.broadcasted_iota(jnp.int32, (V_AUG_ROWS - HEAD_DIM, rows), 0) == 0).astype(BF16)

    def key_pair(slab, idx):
        onehot = (lane - HEAD_DIM == idx).astype(F32)
        left = jnp.where(lane < HEAD_DIM, slab, onehot)
        right = jnp.where(lane < HEAD_DIM, pltpu.roll(slab, HEAD_DIM, 1), onehot)
        return left.astype(BF16), right.astype(BF16)

    def transposed_pair(slab):
        t = slab.T.astype(BF16)
        return t[:HEAD_DIM], t[HEAD_DIM:]

    def write_kv(src, keep, idx, k_ref, vt_ref):
        for pair in range(KV_GROUPS // 2):
            lanes = slice(pair * 128, (pair + 1) * 128)
            k_slab = jnp.where(keep, src[:, lanes].astype(F32), 0.0)
            v_slab = jnp.where(keep, src[:, KV_W + pair * 128:KV_W + (pair + 1) * 128].astype(F32), 0.0)
            for g, k, vt in zip((2 * pair, 2 * pair + 1), key_pair(k_slab, idx), transposed_pair(v_slab)):
                k_ref[g] = k
                vt_ref[g, 0:HEAD_DIM, :] = vt
                vt_ref[g, HEAD_DIM:, :] = ones_rows

    @pl.when(i < n_tiles)
    def _():
        for half, ref in enumerate((q_lo_ref, q_hi_ref)):
            for gg in range(2):
                for pp in range(HEADS_PER_GROUP // 2):
                    lanes = slice(gg * 256 + pp * 128, gg * 256 + (pp + 1) * 128)
                    for h, t in zip((2 * pp, 2 * pp + 1), transposed_pair(ref[:, lanes].astype(F32))):
                        qt_ref[2 * half + gg, 0, :, h * Q_BLOCK:(h + 1) * Q_BLOCK] = t
        write_kv(sel_ref, True, (pos // SEL_BLOCK) % (SEL_TILE // SEL_BLOCK), ks_ref, vst_ref)
        logits_t = ng_ref[...].T
        for g in range(KV_GROUPS):
            for h in range(HEADS_PER_GROUP):
                for br in range(3):
                    r = (g * HEADS_PER_GROUP + h) * 3 + br
                    gl_ref[g, 0, br:br + 1, h * Q_BLOCK:(h + 1) * Q_BLOCK] = logits_t[r:r + 1, :]

    write_kv(win_ref, i >= PAD_TILES, (pos // WIN_CHUNK) % 8, kw_ref, vwt_ref)


def _attention_operands(proj, ngate):
    s = proj.shape[0]
    n_tiles = s // Q_BLOCK
    blk = MIX_W
    real = lambda i: jnp.minimum(i, n_tiles - 1)
    src = lambda c: pl.BlockSpec((Q_BLOCK, blk), lambda i: (real(i), c))
    return pl.pallas_call(
        _operand_kernel,
        grid=(n_tiles + PAD_TILES,),
        in_specs=[src(OFF_Q // blk), src(OFF_Q // blk + 1), src(OFF_KV // blk + 1),
                  pl.BlockSpec((Q_BLOCK, blk), lambda i: (jnp.maximum(i - PAD_TILES, 0), OFF_KV // blk + 2)),
                  pl.BlockSpec((Q_BLOCK, 128), lambda i: (real(i), 0))],
        out_specs=[
            pl.BlockSpec((KV_GROUPS, 1, HEAD_DIM, QL), lambda i: (0, real(i), 0, 0)),
            pl.BlockSpec((KV_GROUPS, Q_BLOCK, KS_AUG_W), lambda i: (0, real(i), 0)),
            pl.BlockSpec((KV_GROUPS, V_AUG_ROWS, Q_BLOCK), lambda i: (0, 0, real(i))),
            pl.BlockSpec((KV_GROUPS, Q_BLOCK, KS_AUG_W), lambda i: (0, i, 0)),
            pl.BlockSpec((KV_GROUPS, V_AUG_ROWS, Q_BLOCK), lambda i: (0, 0, i)),
            pl.BlockSpec((KV_GROUPS, 1, 3, QL), lambda i: (0, real(i), 0, 0)),
        ],
        out_shape=[
            jax.ShapeDtypeStruct((KV_GROUPS, n_tiles, HEAD_DIM, QL), BF16),
            jax.ShapeDtypeStruct((KV_GROUPS, s, KS_AUG_W), BF16),
            jax.ShapeDtypeStruct((KV_GROUPS, V_AUG_ROWS, s), BF16),
            jax.ShapeDtypeStruct((KV_GROUPS, s + WINDOW, KS_AUG_W), BF16),
            jax.ShapeDtypeStruct((KV_GROUPS, V_AUG_ROWS, s + WINDOW), BF16),
            jax.ShapeDtypeStruct((KV_GROUPS, n_tiles, 3, QL), F32),
        ],
        compiler_params=_cparams("arbitrary"),
        name="attn_operands",
    )(proj, proj, proj, proj, ngate)


def _nsa_kernel(qt_ref, kc_ref, vct_ref, ks_ref, vst_ref, kw_ref, vwt_ref, gl_ref, o_ref,
                ps_scr, bias_scr, diag_bias_scr, qc_scr, qw_scr, qs_scr, sc_scr, s0_scr, s1_scr, p0_scr, p1_scr):
    qb = pl.program_id(1)
    t0 = qb * Q_BLOCK
    t0a = pl.multiple_of(t0, Q_BLOCK)
    qt = qt_ref[0, 0]
    nc = kc_ref.shape[1]
    n_sel = bias_scr.shape[0]
    q_rel = lax.broadcasted_iota(jnp.int32, (1, QL), 1) % Q_BLOCK
    tq = t0 + q_rel

    def bias_rows(cond, rows):
        return jnp.where(cond, NEG, 0.0).astype(BF16) + jnp.zeros((rows, QL), BF16)

    def flash_update(s, carry, vt):
        m_prev, acc = carry
        m_new = jnp.maximum(m_prev, jnp.max(s, axis=0, keepdims=True))
        p = jnp.exp2((s - m_new).astype(BF16))
        return m_new, jnp.exp2(m_prev - m_new) * acc + _dot(vt, p)

    init = (jnp.full((1, QL), NEG, F32), jnp.zeros((V_AUG_ROWS, QL), F32))

    grp = lax.broadcasted_iota(jnp.int32, (qc_scr.shape[0] - HEAD_DIM, 1), 0)
    visible = CMP_PER_QB * (qb + 1)
    qc_scr[0:HEAD_DIM, :] = qt
    qc_scr[HEAD_DIM:, :] = bias_rows((8 * grp >= visible) & (grp < nc // 8), qc_scr.shape[0] - HEAD_DIM)
    band = CMP_PER_QB + 8
    r0 = pl.multiple_of(jnp.maximum(visible - band, 0), 8)
    band_end = CMP_STRIDE * (r0 + lax.broadcasted_iota(jnp.int32, (band, 1), 0)) + (CMP_BLOCK - 1)

    def compressed(rows):
        def run():
            sc_scr[0:rows, :] = _dot(kc_ref[0, 0:rows, :], qc_scr[...])
            sc_scr[pl.ds(r0, band), :] = jnp.where(band_end <= tq, sc_scr[pl.ds(r0, band), :], NEG)
            s_c = sc_scr[0:rows, :]
            m_c = jnp.maximum(jnp.max(s_c, axis=0, keepdims=True), 0.1 * NEG)
            e_c = jnp.exp2(s_c - m_c)
            o_aug = _dot(vct_ref[0, :, 0:rows], e_c.astype(BF16))
            inv_c = 1.0 / jnp.maximum(o_aug[HEAD_DIM:HEAD_DIM + 1], 1e-30)
            p_grp = None
            for h in range(HEADS_PER_GROUP):
                lanes = slice(h * Q_BLOCK, (h + 1) * Q_BLOCK)
                p_h = e_c[:, lanes] * inv_c[:, lanes]
                p_grp = p_h if p_grp is None else p_grp + p_h
            for c in range(Q_BLOCK // 128):
                ps_scr[c, 8:8 + rows, :] = p_grp[:, c * 128:(c + 1) * 128]
                if rows < nc:
                    ps_scr[c, 8 + rows:, :] = jnp.zeros((nc - rows, 128), F32)
            return o_aug[:HEAD_DIM] * inv_c
        return run

    ps_scr[:, 0:8, :] = jnp.zeros((Q_BLOCK // 128, 8, 128), F32)
    n_part = 8 if nc // 8 >= band else 4
    part = nc // n_part

    def choose(lo, hi):
        if lo == hi:
            return compressed(lo * part)
        mid = (lo + hi) // 2
        return lambda: lax.cond(visible <= mid * part, choose(lo, mid), choose(mid + 1, hi))

    o_c = choose(1, n_part)()
    imp = jnp.concatenate([sum(ps_scr[c, pl.ds(k, n_sel, stride=4), :] for k in range(7, 12))
                           for c in range(Q_BLOCK // 128)], axis=1)

    n_tri = Q_BLOCK // WIN_CHUNK
    n_chunk = WINDOW // WIN_CHUNK + n_tri
    span = n_chunk * WIN_CHUNK
    rho = lax.broadcasted_iota(jnp.int32, (16, 1), 0)
    first_real = WINDOW // WIN_CHUNK - n_tri * qb
    qw_scr[0:HEAD_DIM, :] = qt
    qw_scr[HEAD_DIM:HEAD_DIM + 16, :] = bias_rows((rho < 8) & (((rho - n_tri * qb) & 7) < first_real), 16)
    qw_scr[HEAD_DIM + 16:, :] = jnp.zeros((qw_scr.shape[0] - HEAD_DIM - 16, QL), BF16)
    s_w = _dot(kw_ref[0, pl.ds(t0a, span), :], qw_scr[...])
    i_rel = lax.broadcasted_iota(jnp.int32, (Q_BLOCK, 1), 0)
    s_w = jnp.concatenate([jnp.where(i_rel > q_rel, s_w[:Q_BLOCK], NEG), s_w[Q_BLOCK:WINDOW],
                           jnp.where(i_rel <= q_rel, s_w[WINDOW:], NEG)], axis=0)
    _, acc_w = flash_update(s_w, init, vwt_ref[0, :, pl.ds(t0a, span)])

    s0_scr[...] = _dot(ks_ref[0, 0:SEL_TILE, 0:HEAD_DIM], qt)

    j_idx = lax.broadcasted_iota(jnp.int32, (n_sel, Q_BLOCK), 0)
    cur = (t0 + lax.broadcasted_iota(jnp.int32, (n_sel, Q_BLOCK), 1)) // SEL_BLOCK
    forced = (j_idx == 0) | (j_idx == cur) | (j_idx == cur - 1)
    valid = j_idx <= cur
    score = jnp.where(forced, -2.0, jnp.where(valid, imp, -1.0))
    n_pick = SEL_TOPN - 3

    fast = score
    for _ in range(n_pick):
        fast = jnp.where(fast == jnp.max(fast, axis=0, keepdims=True), -2.0, fast)
    retired = (fast < -1.5) & valid & jnp.logical_not(forced)
    most_retired = jnp.max(jnp.sum(retired.astype(F32), axis=0, keepdims=True))

    def ranked_with_ties():
        def pick_one(_, sc):
            best = jnp.max(sc, axis=0, keepdims=True)
            first = jnp.min(jnp.where(sc == best, j_idx, n_sel), axis=0, keepdims=True)
            return jnp.where(j_idx == first, -2.0, sc)
        return lax.fori_loop(0, n_pick, pick_one, score)

    ranked = lax.cond(most_retired > n_pick, ranked_with_ties, lambda: fast)
    chosen = valid & (ranked < -1.5)
    diag_bias_scr[...] = jnp.where(chosen, 0.0, NEG)
    bias_scr[...] = jnp.where(chosen & (j_idx * SEL_BLOCK < t0), 0.0, NEG)

    blocks_per_tile = SEL_TILE // SEL_BLOCK
    last_tile = ks_ref.shape[1] // SEL_TILE - 1
    qs_scr[0:HEAD_DIM, :] = qt
    qs_scr[HEAD_DIM + 16:, :] = jnp.zeros((qs_scr.shape[0] - HEAD_DIM - 16, QL), BF16)

    def tile_start(kt):
        return pl.multiple_of(jnp.clip(kt, 0, last_tile) * SEL_TILE, SEL_TILE)

    def set_bias_rows(table, kt):
        b0 = pl.multiple_of(jnp.minimum(kt, last_tile) * blocks_per_tile, blocks_per_tile)
        b16 = jnp.concatenate([table[pl.ds(b0, blocks_per_tile), :],
                               jnp.zeros((16 - blocks_per_tile, Q_BLOCK), F32)], axis=0)
        qs_scr[HEAD_DIM:HEAD_DIM + 16, :] = jnp.concatenate([b16] * HEADS_PER_GROUP, axis=1).astype(BF16)

    set_bias_rows(diag_bias_scr, t0 // SEL_TILE)
    s_d = _dot(ks_ref[0, pl.ds(t0a, Q_BLOCK), :], qs_scr[...])
    m_d, acc_d = flash_update(jnp.where(i_rel <= q_rel, s_d, NEG), init, vst_ref[0, :, pl.ds(t0a, Q_BLOCK)])

    def half_step(kt, s_cur, s_nxt, p_cur, p_prev, carry):
        m_prev, alpha_prev, acc = carry
        acc = alpha_prev * acc + _dot(vst_ref[0, :, pl.ds(tile_start(kt - 1), SEL_TILE)], p_prev[...])
        m8 = jnp.max(s_cur[...].reshape(SEL_TILE // 8, 8, QL), axis=0)
        m_new = jnp.maximum(m_prev, jnp.max(m8, axis=0, keepdims=True))
        set_bias_rows(bias_scr, kt + 1)
        k_nxt = tile_start(kt + 1)
        for h in range(SEL_TILE // QK_PART):
            for c in range(QK_PART // P_CHUNK):
                rows = slice(h * QK_PART + c * P_CHUNK, h * QK_PART + (c + 1) * P_CHUNK)
                p_cur[rows, :] = jnp.exp2((s_cur[rows, :] - m_new).astype(BF16))
            part = slice(h * QK_PART, (h + 1) * QK_PART)
            s_nxt[part, :] = _dot(ks_ref[0, pl.ds(k_nxt + h * QK_PART, QK_PART), :], qs_scr[...])
        return m_new, jnp.exp2(m_prev - m_new), acc

    def pair_step(i, carry):
        carry = half_step(2 * i, s0_scr, s1_scr, p0_scr, p1_scr, carry)
        return half_step(2 * i + 1, s1_scr, s0_scr, p1_scr, p0_scr, carry)

    n_pairs = ((t0 + SEL_TILE - 1) // SEL_TILE + 1) // 2
    for b in range(blocks_per_tile):
        rows = slice(b * SEL_BLOCK, (b + 1) * SEL_BLOCK)
        s0_scr[rows, :] = s0_scr[rows, :] + jnp.concatenate([bias_scr[b:b + 1, :]] * HEADS_PER_GROUP, axis=1)
    p1_scr[...] = jnp.zeros(p1_scr.shape, BF16)
    carry = (m_d, jnp.ones((1, QL), F32), acc_d)
    _, alpha_last, acc_s = lax.fori_loop(0, n_pairs, pair_step, carry)
    acc_s = alpha_last * acc_s + _dot(vst_ref[0, :, pl.ds(tile_start(2 * n_pairs - 1), SEL_TILE)], p1_scr[...])

    gate = jax.nn.sigmoid(gl_ref[0, 0])
    o = (gate[0:1] * o_c + gate[1:2] * (acc_s[:HEAD_DIM] / acc_s[HEAD_DIM:HEAD_DIM + 1])
         + gate[2:3] * (acc_w[:HEAD_DIM] / acc_w[HEAD_DIM:HEAD_DIM + 1]))
    stacked = jnp.concatenate([o[:, p * Q_BLOCK:(p + 1) * Q_BLOCK] for p in range(HEADS_PER_GROUP)], axis=0)
    o_ref[...] = stacked.T.astype(o_ref.dtype)


def _sparse_attention(qt, kc_aug, vc_aug_t, ks_aug, vs_aug_t, kw_aug, vw_aug_t, gate_logits, s):
    n_group, n_qb = qt.shape[:2]
    nc = kc_aug.shape[1]
    n_sel = s // SEL_BLOCK
    assert HEAD_DIM + nc // 8 <= CMP_AUG_W and nc // 4 >= CMP_PER_QB + 8 and WINDOW // WIN_CHUNK + Q_BLOCK // WIN_CHUNK <= 8
    per_step = lambda g, i: (g, i, 0, 0)
    per_group = lambda shape: pl.BlockSpec((1,) + shape, lambda g, i: (g, 0, 0), pipeline_mode=pl.Buffered(1))
    tile_f32 = pltpu.VMEM((SEL_TILE, QL), F32)
    tile_bf16 = pltpu.VMEM((SEL_TILE, QL), BF16)
    sel_table = pltpu.VMEM((n_sel, Q_BLOCK), F32)
    return pl.pallas_call(
        _nsa_kernel,
        grid=(n_group, n_qb),
        in_specs=[
            pl.BlockSpec((1, 1, HEAD_DIM, QL), per_step),
            per_group((nc, CMP_AUG_W)),
            per_group((V_AUG_ROWS, nc)),
            per_group((s, KS_AUG_W)),
            per_group((V_AUG_ROWS, s)),
            per_group((s + WINDOW, KS_AUG_W)),
            per_group((V_AUG_ROWS, s + WINDOW)),
            pl.BlockSpec((1, 1, 3, QL), per_step),
        ],
        out_specs=pl.BlockSpec((Q_BLOCK, HEADS_PER_GROUP * HEAD_DIM), lambda g, i: (i, g)),
        out_shape=jax.ShapeDtypeStruct((s, Q_W), BF16),
        scratch_shapes=[pltpu.VMEM((Q_BLOCK // 128, 8 + nc, 128), F32), sel_table, sel_table,
                        pltpu.VMEM((CMP_AUG_W, QL), BF16), pltpu.VMEM((KS_AUG_W, QL), BF16),
                        pltpu.VMEM((KS_AUG_W, QL), BF16), pltpu.VMEM((nc, QL), F32),
                        tile_f32, tile_f32, tile_bf16, tile_bf16],
        compiler_params=_cparams("parallel", "arbitrary"),
        name="sparse_attn",
    )(qt, kc_aug, vc_aug_t, ks_aug, vs_aug_t, kw_aug, vw_aug_t, gate_logits)


def _merge_kernel(x_ref, ya_ref, yb_ref, yc_ref, g0_ref, g1_ref, g2_ref, wp_ref, wg_ref, wn_ref,
                  wo_ref, ng_ref, gate_ref, o_ref):
    merged = jax.nn.sigmoid(g0_ref[...].astype(F32)) * _dot(ya_ref[...], wp_ref[0])
    merged += jax.nn.sigmoid(g1_ref[...].astype(F32)) * _dot(yb_ref[...], wg_ref[0])
    merged += jax.nn.sigmoid(g2_ref[...].astype(F32)) * _dot(yc_ref[...], wn_ref[0])
    y = _dot(merged.astype(BF16), wo_ref[0])
    o_ref[...] = x_ref[...] + gate_ref[...] * _rms(y, ng_ref[...])


def _merge(x, ya, yb, yc, proj, wp, wg, wn, wo, layer, norm_g, gate):
    s, d = x.shape
    tm = min(256, s)
    const = lambda i: (0, 0)
    rows = lambda i: (i, 0)
    whole = lambda a: pl.BlockSpec((1,) + a.shape[1:], lambda i: (layer, 0, 0), pipeline_mode=pl.Buffered(1))
    bg = OFF_BG // d
    return pl.pallas_call(
        _merge_kernel,
        grid=(s // tm,),
        in_specs=[
            pl.BlockSpec((tm, d), rows),
            pl.BlockSpec((tm, MIX_W), rows),
            pl.BlockSpec((tm, MIX_W), rows),
            pl.BlockSpec((tm, Q_W), rows),
            pl.BlockSpec((tm, d), lambda i: (i, bg)),
            pl.BlockSpec((tm, d), lambda i: (i, bg + 1)),
            pl.BlockSpec((tm, d), lambda i: (i, bg + 2)),
            whole(wp), whole(wg), whole(wn), whole(wo),
            pl.BlockSpec((1, d), const),
            pl.BlockSpec((1, d), const),
        ],
        out_specs=pl.BlockSpec((tm, d), rows),
        out_shape=jax.ShapeDtypeStruct((s, d), F32),
        compiler_params=_cparams("parallel"),
        name="merge_out",
    )(x, ya, yb, yc, proj, proj, proj, wp, wg, wn, wo, norm_g, gate)


def _ffn_kernel(x_ref, gi_ref, sc_ref, sh_ref, w1_ref, w2_ref, go_ref, gate_ref, o_ref, h_scr):
    j = pl.program_id(1)

    @pl.when(j == 0)
    def _():
        h = _rms(x_ref[...], gi_ref[...]) * (1.0 + sc_ref[...]) + sh_ref[...]
        h_scr[...] = h.astype(BF16)
        o_ref[...] = jnp.zeros_like(o_ref)

    a = jnp.square(jnp.maximum(_dot(h_scr[...], w1_ref[0]), 0.0))
    o_ref[...] += _dot(a.astype(BF16), w2_ref[0])

    @pl.when(j == pl.num_programs(1) - 1)
    def _():
        o_ref[...] = x_ref[...] + gate_ref[...] * _rms(o_ref[...], go_ref[...])


def _ffn(x, g_in, scale, shift, w1, w2, layer, g_out, gate):
    s, d = x.shape
    tm, tf = min(1024, s), 1024
    const = lambda i, j: (0, 0)
    rows = lambda i, j: (i, 0)
    return pl.pallas_call(
        _ffn_kernel,
        grid=(s // tm, D_FF // tf),
        in_specs=[
            pl.BlockSpec((tm, d), rows, pipeline_mode=pl.Buffered(1)),
            pl.BlockSpec((1, d), const),
            pl.BlockSpec((1, d), const),
            pl.BlockSpec((1, d), const),
            pl.BlockSpec((1, d, tf), lambda i, j: (layer, 0, j)),
            pl.BlockSpec((1, tf, d), lambda i, j: (layer, j, 0)),
            pl.BlockSpec((1, d), const),
            pl.BlockSpec((1, d), const),
        ],
        out_specs=pl.BlockSpec((tm, d), rows),
        out_shape=jax.ShapeDtypeStruct((s, d), F32),
        scratch_shapes=[pltpu.VMEM((tm, d), BF16)],
        compiler_params=_cparams("parallel", "arbitrary"),
        name="ffn",
    )(x, g_in, scale, shift, w1, w2, g_out, gate)


def _token_mixing(x, mod, norm_g, w_in, pool_w, pool_scale, ln_g, ln_b, ws, bs, cmp_pos, cmp_w1,
                  cmp_b1, cmp_w2, cmp_b2, branch_weights, layer):
    s, d = x.shape
    ng0 = OFF_BG
    w_main = jnp.concatenate([w_in[:, :ng0], w_in[:, ng0 + N_GATE:]], axis=1).astype(BF16)
    w_ng = jnp.pad(w_in[:, ng0:ng0 + N_GATE], ((0, 0), (0, 128 - N_GATE))).astype(BF16)
    row = lambda v: v.reshape(1, -1)

    col_scale = jnp.ones((1, PROJ_W), F32).at[:, OFF_Q:OFF_KV].set(HEAD_DIM ** -0.5 * LOG2E)
    proj, ngate = _in_projection(x, row(norm_g[0]), row(mod[1]), row(mod[0]), w_main, w_ng, col_scale)

    ya, yb = _mixers(proj, pool_w.astype(BF16), row(pool_scale), row(ln_g), row(ln_b), ws, bs.T)

    qt, ks_aug, vs_aug_t, kw_aug, vw_aug_t, gl = _attention_operands(proj, ngate)
    kv_cmp = proj[:, OFF_KV:OFF_KV + 2 * KV_W].reshape(s, 2, KV_GROUPS, HEAD_DIM).transpose(1, 2, 0, 3)
    xkv = kv_cmp.reshape(2, KV_GROUPS, s // CMP_STRIDE, CMP_STRIDE * HEAD_DIM)
    lane_pad = ((0, 0), (0, 0), (0, 128 - HEAD_DIM))
    kc_aug, vc_aug_t = _compress(xkv, cmp_pos.reshape(2, 1, CMP_BLOCK * HEAD_DIM), cmp_w1.astype(BF16),
                                 cmp_b1.reshape(2, 1, -1), jnp.pad(cmp_w2, lane_pad).astype(BF16),
                                 jnp.pad(cmp_b2.reshape(2, 1, -1), lane_pad))
    yc = _sparse_attention(qt, kc_aug, vc_aug_t, ks_aug, vs_aug_t, kw_aug, vw_aug_t, gl, s)

    return _merge(x, ya, yb, yc, proj, *branch_weights, layer, row(norm_g[1]), row(mod[2]))


def kernel(x, c, norm_g, w_ada, b_ada, w_in, pool_w, pool_scale, gmlp_ln_g, gmlp_ln_b, gmlp_ws, gmlp_bs,
           cmp_pos, cmp_w1, cmp_b1, cmp_w2, cmp_b2, w_br_pool, w_br_gmlp, w_br_nsa, w_out, w_ff1, w_ff2):
    b, s, d = x.shape
    assert b == 1 and d == D_MODEL and s % 1024 == 0
    n_layer = w_ada.shape[0]
    mod_all = _modulation(c, w_ada, b_ada).reshape(n_layer, 6, d)
    xs = x[0]
    row = lambda v: v.reshape(1, -1)
    branch_weights = tuple(w.astype(BF16) for w in (w_br_pool, w_br_gmlp, w_br_nsa, w_out))
    w_ff1_b, w_ff2_b = w_ff1.astype(BF16), w_ff2.astype(BF16)
    for l in range(n_layer):
        mod = mod_all[l]
        xs = _token_mixing(xs, mod, norm_g[l], w_in[l], pool_w[l], pool_scale[l], gmlp_ln_g[l],
                           gmlp_ln_b[l], gmlp_ws[l], gmlp_bs[l], cmp_pos[l], cmp_w1[l], cmp_b1[l],
                           cmp_w2[l], cmp_b2[l], branch_weights, l)
        xs = _ffn(xs, row(norm_g[l, 2]), row(mod[4]), row(mod[3]), w_ff1_b, w_ff2_b, l,
                  row(norm_g[l, 3]), row(mod[5]))
    return xs[None]
```

```python
import jax
import jax.numpy as jnp
from jax import lax
from jax.experimental import pallas as pl
from jax.experimental.pallas import tpu as pltpu

F32 = jnp.float32
BF16 = jnp.bfloat16

D_MODEL = 2048
POOL_WINDOWS = (2, 4, 8, 16)
POOL_HALO = 16
GROUP_CH = 128
MIX_W = 4 * GROUP_CH
GMLP_CHUNK = 128
HEAD_DIM = 64
KV_GROUPS = 4
HEADS_PER_GROUP = 4
Q_W = 16 * HEAD_DIM
KV_W = KV_GROUPS * HEAD_DIM
CMP_BLOCK = 32
CMP_STRIDE = 16
SEL_BLOCK = 64
SEL_TOPN = 16
WINDOW = 512
FORCE_BONUS = 1000.0
LOG2E = 1.4426950408889634
NEG = -1e30
N_GATE = 3 * 16
D_FF = 4 * D_MODEL

Q_BLOCK = 256
QL = HEADS_PER_GROUP * Q_BLOCK
CMP_PER_QB = Q_BLOCK // CMP_STRIDE
WIN_CHUNK = 128
PAD_TILES = WINDOW // Q_BLOCK
SEL_TILE = 512
KS_AUG_W = 128
CMP_AUG_W = 256
V_AUG_ROWS = HEAD_DIM + 16
P_CHUNK = 64
QK_PART = 128

OFF_POOL, OFF_U, OFF_V, OFF_Q, OFF_KV, OFF_BG = 0, 512, 1024, 1536, 2560, 4096
PROJ_W = OFF_BG + 3 * D_MODEL

VMEM_LIMIT = 60 * 1024 * 1024


def _cparams(*sem):
    return pltpu.CompilerParams(dimension_semantics=sem, vmem_limit_bytes=VMEM_LIMIT)


def _dot(a, b):
    return jnp.dot(a, b, preferred_element_type=F32)


def _rms(x, g):
    return x * lax.rsqrt(jnp.mean(x * x, axis=-1, keepdims=True) + 1e-6) * g


def _mod_kernel(c_ref, w_ref, b_ref, o_ref):
    c = c_ref[...]
    act = c * jax.nn.sigmoid(c)
    o_ref[0] = jnp.sum(act * w_ref[0], axis=0, keepdims=True) + b_ref[0]


def _modulation(c, w_ada, b_ada):
    n_layer, d, n_out = w_ada.shape
    tn = 1024
    return pl.pallas_call(
        _mod_kernel,
        grid=(n_layer, n_out // tn),
        in_specs=[
            pl.BlockSpec((d, 1), lambda l, j: (0, 0)),
            pl.BlockSpec((1, d, tn), lambda l, j: (l, 0, j)),
            pl.BlockSpec((1, 1, tn), lambda l, j: (l, 0, j)),
        ],
        out_specs=pl.BlockSpec((1, 1, tn), lambda l, j: (l, 0, j)),
        out_shape=jax.ShapeDtypeStruct((n_layer, 1, n_out), F32),
        compiler_params=_cparams("parallel", "parallel"),
        name="adaln_mod",
    )(c.reshape(d, 1), w_ada, b_ada.reshape(n_layer, 1, n_out))


def _inproj_kernel(x_ref, g_ref, sc_ref, sh_ref, w_ref, wng_ref, cs_ref, o_ref, ng_ref, h_scr):
    @pl.when(pl.program_id(1) == 0)
    def _():
        h = _rms(x_ref[...], g_ref[...]) * (1.0 + sc_ref[...]) + sh_ref[...]
        hb = h.astype(BF16)
        h_scr[...] = hb
        ng_ref[...] = _dot(hb, wng_ref[...])

    o_ref[...] = (_dot(h_scr[...], w_ref[...]) * cs_ref[...]).astype(o_ref.dtype)


def _in_projection(x, g, scale, shift, w_main, w_ng, col_scale):
    s, d = x.shape
    tm, tn = min(1024, s), 2048
    row = lambda i, j: (0, 0)
    return pl.pallas_call(
        _inproj_kernel,
        grid=(s // tm, PROJ_W // tn),
        in_specs=[
            pl.BlockSpec((tm, d), lambda i, j: (i, 0)),
            pl.BlockSpec((1, d), row),
            pl.BlockSpec((1, d), row),
            pl.BlockSpec((1, d), row),
            pl.BlockSpec((d, tn), lambda i, j: (0, j)),
            pl.BlockSpec((d, 128), row),
            pl.BlockSpec((1, tn), lambda i, j: (0, j)),
        ],
        out_specs=[
            pl.BlockSpec((tm, tn), lambda i, j: (i, j)),
            pl.BlockSpec((tm, 128), lambda i, j: (i, 0)),
        ],
        out_shape=[
            jax.ShapeDtypeStruct((s, PROJ_W), BF16),
            jax.ShapeDtypeStruct((s, 128), F32),
        ],
        scratch_shapes=[pltpu.VMEM((tm, d), BF16)],
        compiler_params=_cparams("parallel", "arbitrary"),
        name="in_proj",
    )(x, g, scale, shift, w_main, w_ng, col_scale)


def _mixer_kernel(a_ref, halo_ref, u_ref, v_ref, pw_ref, ps_ref, lg_ref, lb_ref, ws_ref, bs_ref,
                  ya_ref, yb_ref):
    i = pl.program_id(0)
    tm = a_ref.shape[0]
    a = a_ref[...].astype(F32)
    halo = jnp.where(i > 0, halo_ref[...].astype(F32), 0.0)
    ext = jnp.concatenate([halo, a], axis=0)
    p2 = ext[1:] + ext[:-1]
    p4 = p2[2:] + p2[:-2]
    p8 = p4[4:] + p4[:-4]
    p16 = p8[8:] + p8[:-8]
    sums = (p2[15:15 + tm], p4[13:13 + tm], p8[9:9 + tm], p16[1:1 + tm])
    t = i * tm + lax.broadcasted_iota(jnp.int32, (tm, 1), 0)
    for gi, w in enumerate(POOL_WINDOWS):
        cols = slice(gi * GROUP_CH, (gi + 1) * GROUP_CH)
        cnt = jnp.minimum(t + 1, w).astype(F32)
        pooled = sums[gi][:, cols] / cnt - a[:, cols]
        y = _dot(pooled.astype(BF16), pw_ref[gi])
        ya_ref[:, cols] = (y * ps_ref[:, cols]).astype(ya_ref.dtype)

    u = jax.nn.gelu(u_ref[...].astype(F32))
    v = jax.nn.gelu(v_ref[...].astype(F32))
    mu = jnp.mean(v, axis=-1, keepdims=True)
    var = jnp.mean(jnp.square(v - mu), axis=-1, keepdims=True)
    vn = ((v - mu) * lax.rsqrt(var + 1e-5) * lg_ref[...] + lb_ref[...]).astype(BF16)
    r = lax.broadcasted_iota(jnp.int32, (GMLP_CHUNK, GMLP_CHUNK), 0)
    c = lax.broadcasted_iota(jnp.int32, (GMLP_CHUNK, GMLP_CHUNK), 1)
    for gi in range(4):
        cols = slice(gi * GROUP_CH, (gi + 1) * GROUP_CH)
        wsm = jnp.where(r >= c, ws_ref[gi], 0.0).astype(BF16)
        bias = bs_ref[:, gi:gi + 1]
        for ck in range(tm // GMLP_CHUNK):
            rows = slice(ck * GMLP_CHUNK, (ck + 1) * GMLP_CHUNK)
            mixed = _dot(wsm, vn[rows, cols]) + bias
            yb_ref[rows, cols] = (u[rows, cols] * mixed).astype(yb_ref.dtype)


def _mixers(proj, pool_w, pool_scale, ln_g, ln_b, ws, bs_t):
    s = proj.shape[0]
    tm = min(512, s)
    hb = tm // POOL_HALO
    const2 = lambda i: (0, 0)
    const3 = lambda i: (0, 0, 0)
    return pl.pallas_call(
        _mixer_kernel,
        grid=(s // tm,),
        in_specs=[
            pl.BlockSpec((tm, MIX_W), lambda i: (i, OFF_POOL // MIX_W)),
            pl.BlockSpec((POOL_HALO, MIX_W), lambda i: (jnp.maximum(i * hb - 1, 0), OFF_POOL // MIX_W)),
            pl.BlockSpec((tm, MIX_W), lambda i: (i, OFF_U // MIX_W)),
            pl.BlockSpec((tm, MIX_W), lambda i: (i, OFF_V // MIX_W)),
            pl.BlockSpec((4, GROUP_CH, GROUP_CH), const3),
            pl.BlockSpec((1, MIX_W), const2),
            pl.BlockSpec((1, MIX_W), const2),
            pl.BlockSpec((1, MIX_W), const2),
            pl.BlockSpec((4, GMLP_CHUNK, GMLP_CHUNK), const3),
            pl.BlockSpec((GMLP_CHUNK, 4), const2),
        ],
        out_specs=[
            pl.BlockSpec((tm, MIX_W), lambda i: (i, 0)),
            pl.BlockSpec((tm, MIX_W), lambda i: (i, 0)),
        ],
        out_shape=[jax.ShapeDtypeStruct((s, MIX_W), BF16)] * 2,
        compiler_params=_cparams("parallel"),
        name="mixers",
    )(proj, proj, proj, proj, pool_w, pool_scale, ln_g, ln_b, ws, bs_t)


def _compress_kernel(x_ref, pos_ref, w1_ref, b1_ref, w2_ref, b2_ref, kc_ref, vct_ref):
    half = CMP_STRIDE * HEAD_DIM
    x = x_ref[0, 0].astype(F32)
    nc = x.shape[0]
    pos = pos_ref[0]
    first = _dot((x + pos[:, :half]).astype(BF16), w1_ref[0, :half, :])
    second = _dot((x + pos[:, half:]).astype(BF16), w1_ref[0, half:, :])
    hid = jax.nn.gelu(first + pltpu.roll(second, nc - 1, 0) + b1_ref[0])
    out = _dot(hid.astype(BF16), w2_ref[0]) + b2_ref[0]

    @pl.when(pl.program_id(1) == 0)
    def _():
        lane = lax.broadcasted_iota(jnp.int32, (nc, 128), 1)
        grp = lax.broadcasted_iota(jnp.int32, (nc, 128), 0) // 8
        kc_ref[0, :, 0:128] = (out + (lane - HEAD_DIM == grp).astype(F32)).astype(BF16)
        kc_ref[0, :, 128:] = (lane + (128 - HEAD_DIM) == grp).astype(BF16)

    @pl.when(pl.program_id(1) == 1)
    def _():
        ones_row = lax.broadcasted_iota(jnp.int32, (V_AUG_ROWS, nc), 0) == HEAD_DIM
        vct_ref[0] = (out.T[0:V_AUG_ROWS] + ones_row.astype(F32)).astype(BF16)


def _compress(xkv, pos, w1, b1, w2, b2):
    _, n_group, nc, width = xkv.shape
    assert CMP_AUG_W == 256 and HEAD_DIM + nc // 8 <= CMP_AUG_W
    per_kv = lambda g, a: (a, 0, 0)
    per_group = lambda g, a: (g, 0, 0)
    return pl.pallas_call(
        _compress_kernel,
        grid=(n_group, 2),
        in_specs=[
            pl.BlockSpec((1, 1, nc, width), lambda g, a: (a, g, 0, 0)),
            pl.BlockSpec((1, 1, 2 * width), per_kv),
            pl.BlockSpec((1, 2 * width, 128), per_kv),
            pl.BlockSpec((1, 1, 128), per_kv),
            pl.BlockSpec((1, 128, 128), per_kv),
            pl.BlockSpec((1, 1, 128), per_kv),
        ],
        out_specs=[pl.BlockSpec((1, nc, CMP_AUG_W), per_group), pl.BlockSpec((1, V_AUG_ROWS, nc), per_group)],
        out_shape=[jax.ShapeDtypeStruct((n_group, nc, CMP_AUG_W), BF16),
                   jax.ShapeDtypeStruct((n_group, V_AUG_ROWS, nc), BF16)],
        compiler_params=_cparams("parallel", "arbitrary"),
        name="compress_kv",
    )(xkv, pos, w1, b1, w2, b2)


def _operand_kernel(q_lo_ref, q_hi_ref, sel_ref, win_ref, ng_ref, qt_ref, ks_ref, vst_ref, kw_ref, vwt_ref, gl_ref):
    i = pl.program_id(0)
    n_tiles = pl.num_programs(0) - PAD_TILES
    rows = q_lo_ref.shape[0]
    lane = lax.broadcasted_iota(jnp.int32, (rows, 128), 1)
    pos = i * rows + lax.broadcasted_iota(jnp.int32, (rows, 128), 0)
    ones_rows = (lax.broadcasted_iota(jnp.int32, (V_AUG_ROWS - HEAD_DIM, rows), 0) == 0).astype(BF16)

    def key_pair(slab, idx):
        onehot = (lane - HEAD_DIM == idx).astype(F32)
        left = jnp.where(lane < HEAD_DIM, slab, onehot)
        right = jnp.where(lane < HEAD_DIM, pltpu.roll(slab, HEAD_DIM, 1), onehot)
        return left.astype(BF16), right.astype(BF16)

    def transposed_pair(slab):
        t = slab.T.astype(BF16)
        return t[:HEAD_DIM], t[HEAD_DIM:]

    def write_kv(src, keep, idx, k_ref, vt_ref):
        for pair in range(KV_GROUPS // 2):
            lanes = slice(pair * 128, (pair + 1) * 128)
            k_slab = jnp.where(keep, src[:, lanes].astype(F32), 0.0)
            v_slab = jnp.where(keep, src[:, KV_W + pair * 128:KV_W + (pair + 1) * 128].astype(F32), 0.0)
            for g, k, vt in zip((2 * pair, 2 * pair + 1), key_pair(k_slab, idx), transposed_pair(v_slab)):
                k_ref[g] = k
                vt_ref[g, 0:HEAD_DIM, :] = vt
                vt_ref[g, HEAD_DIM:, :] = ones_rows

    @pl.when(i < n_tiles)
    def _():
        for half, ref in enumerate((q_lo_ref, q_hi_ref)):
            for gg in range(2):
                for pp in range(HEADS_PER_GROUP // 2):
                    lanes = slice(gg * 256 + pp * 128, gg * 256 + (pp + 1) * 128)
                    for h, t in zip((2 * pp, 2 * pp + 1), transposed_pair(ref[:, lanes].astype(F32))):
                        qt_ref[2 * half + gg, 0, :, h * Q_BLOCK:(h + 1) * Q_BLOCK] = t
        write_kv(sel_ref, True, (pos // SEL_BLOCK) % (SEL_TILE // SEL_BLOCK), ks_ref, vst_ref)
        logits_t = ng_ref[...].T
        for g in range(KV_GROUPS):
            for h in range(HEADS_PER_GROUP):
                for br in range(3):
                    r = (g * HEADS_PER_GROUP + h) * 3 + br
                    gl_ref[g, 0, br:br + 1, h * Q_BLOCK:(h + 1) * Q_BLOCK] = logits_t[r:r + 1, :]

    write_kv(win_ref, i >= PAD_TILES, (pos // WIN_CHUNK) % 8, kw_ref, vwt_ref)


def _attention_operands(proj, ngate):
    s = proj.shape[0]
    n_tiles = s // Q_BLOCK
    blk = MIX_W
    real = lambda i: jnp.minimum(i, n_tiles - 1)
    src = lambda c: pl.BlockSpec((Q_BLOCK, blk), lambda i: (real(i), c))
    return pl.pallas_call(
        _operand_kernel,
        grid=(n_tiles + PAD_TILES,),
        in_specs=[src(OFF_Q // blk), src(OFF_Q // blk + 1), src(OFF_KV // blk + 1),
                  pl.BlockSpec((Q_BLOCK, blk), lambda i: (jnp.maximum(i - PAD_TILES, 0), OFF_KV // blk + 2)),
                  pl.BlockSpec((Q_BLOCK, 128), lambda i: (real(i), 0))],
        out_specs=[
            pl.BlockSpec((KV_GROUPS, 1, HEAD_DIM, QL), lambda i: (0, real(i), 0, 0)),
            pl.BlockSpec((KV_GROUPS, Q_BLOCK, KS_AUG_W), lambda i: (0, real(i), 0)),
            pl.BlockSpec((KV_GROUPS, V_AUG_ROWS, Q_BLOCK), lambda i: (0, 0, real(i))),
            pl.BlockSpec((KV_GROUPS, Q_BLOCK, KS_AUG_W), lambda i: (0, i, 0)),
            pl.BlockSpec((KV_GROUPS, V_AUG_ROWS, Q_BLOCK), lambda i: (0, 0, i)),
            pl.BlockSpec((KV_GROUPS, 1, 3, QL), lambda i: (0, real(i), 0, 0)),
        ],
        out_shape=[
            jax.ShapeDtypeStruct((KV_GROUPS, n_tiles, HEAD_DIM, QL), BF16),
            jax.ShapeDtypeStruct((KV_GROUPS, s, KS_AUG_W), BF16),
            jax.ShapeDtypeStruct((KV_GROUPS, V_AUG_ROWS, s), BF16),
            jax.ShapeDtypeStruct((KV_GROUPS, s + WINDOW, KS_AUG_W), BF16),
            jax.ShapeDtypeStruct((KV_GROUPS, V_AUG_ROWS, s + WINDOW), BF16),
            jax.ShapeDtypeStruct((KV_GROUPS, n_tiles, 3, QL), F32),
        ],
        compiler_params=_cparams("arbitrary"),
        name="attn_operands",
    )(proj, proj, proj, proj, ngate)


def _nsa_kernel(qt_ref, kc_ref, vct_ref, ks_ref, vst_ref, kw_ref, vwt_ref, gl_ref, o_ref,
                ps_scr, bias_scr, diag_bias_scr, qc_scr, qw_scr, qs_scr, sc_scr, s0_scr, s1_scr, p0_scr, p1_scr):
    qb = pl.program_id(1)
    t0 = qb * Q_BLOCK
    t0a = pl.multiple_of(t0, Q_BLOCK)
    qt = qt_ref[0, 0]
    nc = kc_ref.shape[1]
    n_sel = bias_scr.shape[0]
    q_rel = lax.broadcasted_iota(jnp.int32, (1, QL), 1) % Q_BLOCK
    tq = t0 + q_rel

    def bias_rows(cond, rows):
        return jnp.where(cond, NEG, 0.0).astype(BF16) + jnp.zeros((rows, QL), BF16)

    def flash_update(s, carry, vt):
        m_prev, acc = carry
        m_new = jnp.maximum(m_prev, jnp.max(s, axis=0, keepdims=True))
        p = jnp.exp2((s - m_new).astype(BF16))
        return m_new, jnp.exp2(m_prev - m_new) * acc + _dot(vt, p)

    init = (jnp.full((1, QL), NEG, F32), jnp.zeros((V_AUG_ROWS, QL), F32))

    grp = lax.broadcasted_iota(jnp.int32, (qc_scr.shape[0] - HEAD_DIM, 1), 0)
    visible = CMP_PER_QB * (qb + 1)
    qc_scr[0:HEAD_DIM, :] = qt
    qc_scr[HEAD_DIM:, :] = bias_rows((8 * grp >= visible) & (grp < nc // 8), qc_scr.shape[0] - HEAD_DIM)
    band = CMP_PER_QB + 8
    r0 = pl.multiple_of(jnp.maximum(visible - band, 0), 8)
    band_end = CMP_STRIDE * (r0 + lax.broadcasted_iota(jnp.int32, (band, 1), 0)) + (CMP_BLOCK - 1)

    def compressed(rows):
        def run():
            sc_scr[0:rows, :] = _dot(kc_ref[0, 0:rows, :], qc_scr[...])
            sc_scr[pl.ds(r0, band), :] = jnp.where(band_end <= tq, sc_scr[pl.ds(r0, band), :], NEG)
            s_c = sc_scr[0:rows, :]
            m_c = jnp.maximum(jnp.max(s_c, axis=0, keepdims=True), 0.1 * NEG)
            e_c = jnp.exp2(s_c - m_c)
            o_aug = _dot(vct_ref[0, :, 0:rows], e_c.astype(BF16))
            inv_c = 1.0 / jnp.maximum(o_aug[HEAD_DIM:HEAD_DIM + 1], 1e-30)
            p_grp = None
            for h in range(HEADS_PER_GROUP):
                lanes = slice(h * Q_BLOCK, (h + 1) * Q_BLOCK)
                p_h = e_c[:, lanes] * inv_c[:, lanes]
                p_grp = p_h if p_grp is None else p_grp + p_h
            for c in range(Q_BLOCK // 128):
                ps_scr[c, 8:8 + rows, :] = p_grp[:, c * 128:(c + 1) * 128]
                if rows < nc:
                    ps_scr[c, 8 + rows:, :] = jnp.zeros((nc - rows, 128), F32)
            return o_aug[:HEAD_DIM] * inv_c
        return run

    ps_scr[:, 0:8, :] = jnp.zeros((Q_BLOCK // 128, 8, 128), F32)
    n_part = 8 if nc // 8 >= band else 4
    part = nc // n_part

    def choose(lo, hi):
        if lo == hi:
            return compressed(lo * part)
        mid = (lo + hi) // 2
        return lambda: lax.cond(visible <= mid * part, choose(lo, mid), choose(mid + 1, hi))

    o_c = choose(1, n_part)()
    imp = jnp.concatenate([sum(ps_scr[c, pl.ds(k, n_sel, stride=4), :] for k in range(7, 12))
                           for c in range(Q_BLOCK // 128)], axis=1)

    n_tri = Q_BLOCK // WIN_CHUNK
    n_chunk = WINDOW // WIN_CHUNK + n_tri
    span = n_chunk * WIN_CHUNK
    rho = lax.broadcasted_iota(jnp.int32, (16, 1), 0)
    first_real = WINDOW // WIN_CHUNK - n_tri * qb
    qw_scr[0:HEAD_DIM, :] = qt
    qw_scr[HEAD_DIM:HEAD_DIM + 16, :] = bias_rows((rho < 8) & (((rho - n_tri * qb) & 7) < first_real), 16)
    qw_scr[HEAD_DIM + 16:, :] = jnp.zeros((qw_scr.shape[0] - HEAD_DIM - 16, QL), BF16)
    s_w = _dot(kw_ref[0, pl.ds(t0a, span), :], qw_scr[...])
    i_rel = lax.broadcasted_iota(jnp.int32, (Q_BLOCK, 1), 0)
    s_w = jnp.concatenate([jnp.where(i_rel > q_rel, s_w[:Q_BLOCK], NEG), s_w[Q_BLOCK:WINDOW],
                           jnp.where(i_rel <= q_rel, s_w[WINDOW:], NEG)], axis=0)
    _, acc_w = flash_update(s_w, init, vwt_ref[0, :, pl.ds(t0a, span)])

    s0_scr[...] = _dot(ks_ref[0, 0:SEL_TILE, 0:HEAD_DIM], qt)

    j_idx = lax.broadcasted_iota(jnp.int32, (n_sel, Q_BLOCK), 0)
    cur = (t0 + lax.broadcasted_iota(jnp.int32, (n_sel, Q_BLOCK), 1)) // SEL_BLOCK
    forced = (j_idx == 0) | (j_idx == cur) | (j_idx == cur - 1)
    valid = j_idx <= cur
    score = jnp.where(forced, -2.0, jnp.where(valid, imp, -1.0))
    n_pick = SEL_TOPN - 3

    fast = score
    for _ in range(n_pick):
        fast = jnp.where(fast == jnp.max(fast, axis=0, keepdims=True), -2.0, fast)
    retired = (fast < -1.5) & valid & jnp.logical_not(forced)
    most_retired = jnp.max(jnp.sum(retired.astype(F32), axis=0, keepdims=True))

    def ranked_with_ties():
        def pick_one(_, sc):
            best = jnp.max(sc, axis=0, keepdims=True)
            first = jnp.min(jnp.where(sc == best, j_idx, n_sel), axis=0, keepdims=True)
            return jnp.where(j_idx == first, -2.0, sc)
        return lax.fori_loop(0, n_pick, pick_one, score)

    ranked = lax.cond(most_retired > n_pick, ranked_with_ties, lambda: fast)
    chosen = valid & (ranked < -1.5)
    diag_bias_scr[...] = jnp.where(chosen, 0.0, NEG)
    bias_scr[...] = jnp.where(chosen & (j_idx * SEL_BLOCK < t0), 0.0, NEG)

    blocks_per_tile = SEL_TILE // SEL_BLOCK
    last_tile = ks_ref.shape[1] // SEL_TILE - 1
    qs_scr[0:HEAD_DIM, :] = qt
    qs_scr[HEAD_DIM + 16:, :] = jnp.zeros((qs_scr.shape[0] - HEAD_DIM - 16, QL), BF16)

    def tile_start(kt):
        return pl.multiple_of(jnp.clip(kt, 0, last_tile) * SEL_TILE, SEL_TILE)

    def set_bias_rows(table, kt):
        b0 = pl.multiple_of(jnp.minimum(kt, last_tile) * blocks_per_tile, blocks_per_tile)
        b16 = jnp.concatenate([table[pl.ds(b0, blocks_per_tile), :],
                               jnp.zeros((16 - blocks_per_tile, Q_BLOCK), F32)], axis=0)
        qs_scr[HEAD_DIM:HEAD_DIM + 16, :] = jnp.concatenate([b16] * HEADS_PER_GROUP, axis=1).astype(BF16)

    set_bias_rows(diag_bias_scr, t0 // SEL_TILE)
    s_d = _dot(ks_ref[0, pl.ds(t0a, Q_BLOCK), :], qs_scr[...])
    m_d, acc_d = flash_update(jnp.where(i_rel <= q_rel, s_d, NEG), init, vst_ref[0, :, pl.ds(t0a, Q_BLOCK)])

    def half_step(kt, s_cur, s_nxt, p_cur, p_prev, carry):
        m_prev, alpha_prev, acc = carry
        acc = alpha_prev * acc + _dot(vst_ref[0, :, pl.ds(tile_start(kt - 1), SEL_TILE)], p_prev[...])
        m8 = jnp.max(s_cur[...].reshape(SEL_TILE // 8, 8, QL), axis=0)
        m_new = jnp.maximum(m_prev, jnp.max(m8, axis=0, keepdims=True))
        set_bias_rows(bias_scr, kt + 1)
        k_nxt = tile_start(kt + 1)
        for h in range(SEL_TILE // QK_PART):
            for c in range(QK_PART // P_CHUNK):
                rows = slice(h * QK_PART + c * P_CHUNK, h * QK_PART + (c + 1) * P_CHUNK)
                p_cur[rows, :] = jnp.exp2((s_cur[rows, :] - m_new).astype(BF16))
            part = slice(h * QK_PART, (h + 1) * QK_PART)
            s_nxt[part, :] = _dot(ks_ref[0, pl.ds(k_nxt + h * QK_PART, QK_PART), :], qs_scr[...])
        return m_new, jnp.exp2(m_prev - m_new), acc

    def pair_step(i, carry):
        carry = half_step(2 * i, s0_scr, s1_scr, p0_scr, p1_scr, carry)
        return half_step(2 * i + 1, s1_scr, s0_scr, p1_scr, p0_scr, carry)

    n_pairs = ((t0 + SEL_TILE - 1) // SEL_TILE + 1) // 2
    for b in range(blocks_per_tile):
        rows = slice(b * SEL_BLOCK, (b + 1) * SEL_BLOCK)
        s0_scr[rows, :] = s0_scr[rows, :] + jnp.concatenate([bias_scr[b:b + 1, :]] * HEADS_PER_GROUP, axis=1)
    p1_scr[...] = jnp.zeros(p1_scr.shape, BF16)
    carry = (m_d, jnp.ones((1, QL), F32), acc_d)
    _, alpha_last, acc_s = lax.fori_loop(0, n_pairs, pair_step, carry)
    acc_s = alpha_last * acc_s + _dot(vst_ref[0, :, pl.ds(tile_start(2 * n_pairs - 1), SEL_TILE)], p1_scr[...])

    gate = jax.nn.sigmoid(gl_ref[0, 0])
    o = (gate[0:1] * o_c + gate[1:2] * (acc_s[:HEAD_DIM] / acc_s[HEAD_DIM:HEAD_DIM + 1])
         + gate[2:3] * (acc_w[:HEAD_DIM] / acc_w[HEAD_DIM:HEAD_DIM + 1]))
    stacked = jnp.concatenate([o[:, p * Q_BLOCK:(p + 1) * Q_BLOCK] for p in range(HEADS_PER_GROUP)], axis=0)
    o_ref[...] = stacked.T.astype(o_ref.dtype)


def _sparse_attention(qt, kc_aug, vc_aug_t, ks_aug, vs_aug_t, kw_aug, vw_aug_t, gate_logits, s):
    n_group, n_qb = qt.shape[:2]
    nc = kc_aug.shape[1]
    n_sel = s // SEL_BLOCK
    assert HEAD_DIM + nc // 8 <= CMP_AUG_W and nc // 4 >= CMP_PER_QB + 8 and WINDOW // WIN_CHUNK + Q_BLOCK // WIN_CHUNK <= 8
    per_step = lambda g, i: (g, i, 0, 0)
    per_group = lambda shape: pl.BlockSpec((1,) + shape, lambda g, i: (g, 0, 0), pipeline_mode=pl.Buffered(1))
    tile_f32 = pltpu.VMEM((SEL_TILE, QL), F32)
    tile_bf16 = pltpu.VMEM((SEL_TILE, QL), BF16)
    sel_table = pltpu.VMEM((n_sel, Q_BLOCK), F32)
    return pl.pallas_call(
        _nsa_kernel,
        grid=(n_group, n_qb),
        in_specs=[
            pl.BlockSpec((1, 1, HEAD_DIM, QL), per_step),
            per_group((nc, CMP_AUG_W)),
            per_group((V_AUG_ROWS, nc)),
            per_group((s, KS_AUG_W)),
            per_group((V_AUG_ROWS, s)),
            per_group((s + WINDOW, KS_AUG_W)),
            per_group((V_AUG_ROWS, s + WINDOW)),
            pl.BlockSpec((1, 1, 3, QL), per_step),
        ],
        out_specs=pl.BlockSpec((Q_BLOCK, HEADS_PER_GROUP * HEAD_DIM), lambda g, i: (i, g)),
        out_shape=jax.ShapeDtypeStruct((s, Q_W), BF16),
        scratch_shapes=[pltpu.VMEM((Q_BLOCK // 128, 8 + nc, 128), F32), sel_table, sel_table,
                        pltpu.VMEM((CMP_AUG_W, QL), BF16), pltpu.VMEM((KS_AUG_W, QL), BF16),
                        pltpu.VMEM((KS_AUG_W, QL), BF16), pltpu.VMEM((nc, QL), F32),
                        tile_f32, tile_f32, tile_bf16, tile_bf16],
        compiler_params=_cparams("parallel", "arbitrary"),
        name="sparse_attn",
    )(qt, kc_aug, vc_aug_t, ks_aug, vs_aug_t, kw_aug, vw_aug_t, gate_logits)


def _merge_kernel(x_ref, ya_ref, yb_ref, yc_ref, g0_ref, g1_ref, g2_ref, wp_ref, wg_ref, wn_ref,
                  wo_ref, ng_ref, gate_ref, o_ref):
    merged = jax.nn.sigmoid(g0_ref[...].astype(F32)) * _dot(ya_ref[...], wp_ref[0])
    merged += jax.nn.sigmoid(g1_ref[...].astype(F32)) * _dot(yb_ref[...], wg_ref[0])
    merged += jax.nn.sigmoid(g2_ref[...].astype(F32)) * _dot(yc_ref[...], wn_ref[0])
    y = _dot(merged.astype(BF16), wo_ref[0])
    o_ref[...] = x_ref[...] + gate_ref[...] * _rms(y, ng_ref[...])


def _merge(x, ya, yb, yc, proj, wp, wg, wn, wo, layer, norm_g, gate):
    s, d = x.shape
    tm = min(256, s)
    const = lambda i: (0, 0)
    rows = lambda i: (i, 0)
    whole = lambda a: pl.BlockSpec((1,) + a.shape[1:], lambda i: (layer, 0, 0), pipeline_mode=pl.Buffered(1))
    bg = OFF_BG // d
    return pl.pallas_call(
        _merge_kernel,
        grid=(s // tm,),
        in_specs=[
            pl.BlockSpec((tm, d), rows),
            pl.BlockSpec((tm, MIX_W), rows),
            pl.BlockSpec((tm, MIX_W), rows),
            pl.BlockSpec((tm, Q_W), rows),
            pl.BlockSpec((tm, d), lambda i: (i, bg)),
            pl.BlockSpec((tm, d), lambda i: (i, bg + 1)),
            pl.BlockSpec((tm, d), lambda i: (i, bg + 2)),
            whole(wp), whole(wg), whole(wn), whole(wo),
            pl.BlockSpec((1, d), const),
            pl.BlockSpec((1, d), const),
        ],
        out_specs=pl.BlockSpec((tm, d), rows),
        out_shape=jax.ShapeDtypeStruct((s, d), F32),
        compiler_params=_cparams("parallel"),
        name="merge_out",
    )(x, ya, yb, yc, proj, proj, proj, wp, wg, wn, wo, norm_g, gate)


def _ffn_kernel(x_ref, gi_ref, sc_ref, sh_ref, w1_ref, w2_ref, go_ref, gate_ref, o_ref, h_scr):
    j = pl.program_id(1)

    @pl.when(j == 0)
    def _():
        h = _rms(x_ref[...], gi_ref[...]) * (1.0 + sc_ref[...]) + sh_ref[...]
        h_scr[...] = h.astype(BF16)
        o_ref[...] = jnp.zeros_like(o_ref)

    a = jnp.square(jnp.maximum(_dot(h_scr[...], w1_ref[0]), 0.0))
    o_ref[...] += _dot(a.astype(BF16), w2_ref[0])

    @pl.when(j == pl.num_programs(1) - 1)
    def _():
        o_ref[...] = x_ref[...] + gate_ref[...] * _rms(o_ref[...], go_ref[...])


def _ffn(x, g_in, scale, shift, w1, w2, layer, g_out, gate):
    s, d = x.shape
    tm, tf = min(1024, s), 1024
    const = lambda i, j: (0, 0)
    rows = lambda i, j: (i, 0)
    return pl.pallas_call(
        _ffn_kernel,
        grid=(s // tm, D_FF // tf),
        in_specs=[
            pl.BlockSpec((tm, d), rows, pipeline_mode=pl.Buffered(1)),
            pl.BlockSpec((1, d), const),
            pl.BlockSpec((1, d), const),
            pl.BlockSpec((1, d), const),
            pl.BlockSpec((1, d, tf), lambda i, j: (layer, 0, j)),
            pl.BlockSpec((1, tf, d), lambda i, j: (layer, j, 0)),
            pl.BlockSpec((1, d), const),
            pl.BlockSpec((1, d), const),
        ],
        out_specs=pl.BlockSpec((tm, d), rows),
        out_shape=jax.ShapeDtypeStruct((s, d), F32),
        scratch_shapes=[pltpu.VMEM((tm, d), BF16)],
        compiler_params=_cparams("parallel", "arbitrary"),
        name="ffn",
    )(x, g_in, scale, shift, w1, w2, g_out, gate)


def _token_mixing(x, mod, norm_g, w_in, pool_w, pool_scale, ln_g, ln_b, ws, bs, cmp_pos, cmp_w1,
                  cmp_b1, cmp_w2, cmp_b2, branch_weights, layer):
    s, d = x.shape
    ng0 = OFF_BG
    w_main = jnp.concatenate([w_in[:, :ng0], w_in[:, ng0 + N_GATE:]], axis=1).astype(BF16)
    w_ng = jnp.pad(w_in[:, ng0:ng0 + N_GATE], ((0, 0), (0, 128 - N_GATE))).astype(BF16)
    row = lambda v: v.reshape(1, -1)

    col_scale = jnp.ones((1, PROJ_W), F32).at[:, OFF_Q:OFF_KV].set(HEAD_DIM ** -0.5 * LOG2E)
    proj, ngate = _in_projection(x, row(norm_g[0]), row(mod[1]), row(mod[0]), w_main, w_ng, col_scale)

    ya, yb = _mixers(proj, pool_w.astype(BF16), row(pool_scale), row(ln_g), row(ln_b), ws, bs.T)

    qt, ks_aug, vs_aug_t, kw_aug, vw_aug_t, gl = _attention_operands(proj, ngate)
    kv_cmp = proj[:, OFF_KV:OFF_KV + 2 * KV_W].reshape(s, 2, KV_GROUPS, HEAD_DIM).transpose(1, 2, 0, 3)
    xkv = kv_cmp.reshape(2, KV_GROUPS, s // CMP_STRIDE, CMP_STRIDE * HEAD_DIM)
    lane_pad = ((0, 0), (0, 0), (0, 128 - HEAD_DIM))
    kc_aug, vc_aug_t = _compress(xkv, cmp_pos.reshape(2, 1, CMP_BLOCK * HEAD_DIM), cmp_w1.astype(BF16),
                                 cmp_b1.reshape(2, 1, -1), jnp.pad(cmp_w2, lane_pad).astype(BF16),
                                 jnp.pad(cmp_b2.reshape(2, 1, -1), lane_pad))
    yc = _sparse_attention(qt, kc_aug, vc_aug_t, ks_aug, vs_aug_t, kw_aug, vw_aug_t, gl, s)

    return _merge(x, ya, yb, yc, proj, *branch_weights, layer, row(norm_g[1]), row(mod[2]))


def kernel(x, c, norm_g, w_ada, b_ada, w_in, pool_w, pool_scale, gmlp_ln_g, gmlp_ln_b, gmlp_ws, gmlp_bs,
           cmp_pos, cmp_w1, cmp_b1, cmp_w2, cmp_b2, w_br_pool, w_br_gmlp, w_br_nsa, w_out, w_ff1, w_ff2):
    b, s, d = x.shape
    assert b == 1 and d == D_MODEL and s % 1024 == 0
    n_layer = w_ada.shape[0]
    mod_all = _modulation(c, w_ada, b_ada).reshape(n_layer, 6, d)
    xs = x[0]
    row = lambda v: v.reshape(1, -1)
    branch_weights = tuple(w.astype(BF16) for w in (w_br_pool, w_br_gmlp, w_br_nsa, w_out))
    w_ff1_b, w_ff2_b = w_ff1.astype(BF16), w_ff2.astype(BF16)
    for l in range(n_layer):
        mod = mod_all[l]
        xs = _token_mixing(xs, mod, norm_g[l], w_in[l], pool_w[l], pool_scale[l], gmlp_ln_g[l],
                           gmlp_ln_b[l], gmlp_ws[l], gmlp_bs[l], cmp_pos[l], cmp_w1[l], cmp_b1[l],
                           cmp_w2[l], cmp_b2[l], branch_weights, l)
        xs = _ffn(xs, row(norm_g[l, 2]), row(mod[4]), row(mod[3]), w_ff1_b, w_ff2_b, l,
                  row(norm_g[l, 3]), row(mod[5]))
    return xs[None]
```

```python
import jax
import jax.numpy as jnp
from jax import lax
from jax.experimental import pallas as pl
from jax.experimental.pallas import tpu as pltpu

F32 = jnp.float32
BF16 = jnp.bfloat16

D_MODEL = 2048
POOL_WINDOWS = (2, 4, 8, 16)
POOL_HALO = 16
GROUP_CH = 128
MIX_W = 4 * GROUP_CH
GMLP_CHUNK = 128
HEAD_DIM = 64
KV_GROUPS = 4
HEADS_PER_GROUP = 4
Q_W = 16 * HEAD_DIM
KV_W = KV_GROUPS * HEAD_DIM
CMP_BLOCK = 32
CMP_STRIDE = 16
SEL_BLOCK = 64
SEL_TOPN = 16
WINDOW = 512
FORCE_BONUS = 1000.0
LOG2E = 1.4426950408889634
NEG = -1e30
N_GATE = 3 * 16
D_FF = 4 * D_MODEL

Q_BLOCK = 256
QL = HEADS_PER_GROUP * Q_BLOCK
CMP_PER_QB = Q_BLOCK // CMP_STRIDE
WIN_CHUNK = 128
PAD_TILES = WINDOW // Q_BLOCK
SEL_TILE = 512
KS_AUG_W = 128
CMP_AUG_W = 256
V_AUG_ROWS = HEAD_DIM + 16
P_CHUNK = 64
QK_PART = 64

OFF_POOL, OFF_U, OFF_V, OFF_Q, OFF_KV, OFF_BG = 0, 512, 1024, 1536, 2560, 4096
PROJ_W = OFF_BG + 3 * D_MODEL

VMEM_LIMIT = 60 * 1024 * 1024


def _cparams(*sem):
    return pltpu.CompilerParams(dimension_semantics=sem, vmem_limit_bytes=VMEM_LIMIT)


def _dot(a, b):
    return jnp.dot(a, b, preferred_element_type=F32)


def _rms(x, g):
    return x * lax.rsqrt(jnp.mean(x * x, axis=-1, keepdims=True) + 1e-6) * g


def _mod_kernel(c_ref, w_ref, b_ref, o_ref):
    c = c_ref[...]
    act = c * jax.nn.sigmoid(c)
    o_ref[0] = jnp.sum(act * w_ref[0], axis=0, keepdims=True) + b_ref[0]


def _modulation(c, w_ada, b_ada):
    n_layer, d, n_out = w_ada.shape
    tn = 1024
    return pl.pallas_call(
        _mod_kernel,
        grid=(n_layer, n_out // tn),
        in_specs=[
            pl.BlockSpec((d, 1), lambda l, j: (0, 0)),
            pl.BlockSpec((1, d, tn), lambda l, j: (l, 0, j)),
            pl.BlockSpec((1, 1, tn), lambda l, j: (l, 0, j)),
        ],
        out_specs=pl.BlockSpec((1, 1, tn), lambda l, j: (l, 0, j)),
        out_shape=jax.ShapeDtypeStruct((n_layer, 1, n_out), F32),
        compiler_params=_cparams("parallel", "parallel"),
        name="adaln_mod",
    )(c.reshape(d, 1), w_ada, b_ada.reshape(n_layer, 1, n_out))


def _inproj_kernel(x_ref, g_ref, sc_ref, sh_ref, w_ref, wng_ref, cs_ref, o_ref, ng_ref, h_scr):
    @pl.when(pl.program_id(1) == 0)
    def _():
        h = _rms(x_ref[...], g_ref[...]) * (1.0 + sc_ref[...]) + sh_ref[...]
        hb = h.astype(BF16)
        h_scr[...] = hb
        ng_ref[...] = _dot(hb, wng_ref[...])

    o_ref[...] = (_dot(h_scr[...], w_ref[...]) * cs_ref[...]).astype(o_ref.dtype)


def _in_projection(x, g, scale, shift, w_main, w_ng, col_scale):
    s, d = x.shape
    tm, tn = min(1024, s), 2048
    row = lambda i, j: (0, 0)
    return pl.pallas_call(
        _inproj_kernel,
        grid=(s // tm, PROJ_W // tn),
        in_specs=[
            pl.BlockSpec((tm, d), lambda i, j: (i, 0)),
            pl.BlockSpec((1, d), row),
            pl.BlockSpec((1, d), row),
            pl.BlockSpec((1, d), row),
            pl.BlockSpec((d, tn), lambda i, j: (0, j)),
            pl.BlockSpec((d, 128), row),
            pl.BlockSpec((1, tn), lambda i, j: (0, j)),
        ],
        out_specs=[
            pl.BlockSpec((tm, tn), lambda i, j: (i, j)),
            pl.BlockSpec((tm, 128), lambda i, j: (i, 0)),
        ],
        out_shape=[
            jax.ShapeDtypeStruct((s, PROJ_W), BF16),
            jax.ShapeDtypeStruct((s, 128), F32),
        ],
        scratch_shapes=[pltpu.VMEM((tm, d), BF16)],
        compiler_params=_cparams("parallel", "arbitrary"),
        name="in_proj",
    )(x, g, scale, shift, w_main, w_ng, col_scale)


def _mixer_kernel(a_ref, halo_ref, u_ref, v_ref, pw_ref, ps_ref, lg_ref, lb_ref, ws_ref, bs_ref,
                  ya_ref, yb_ref):
    i = pl.program_id(0)
    tm = a_ref.shape[0]
    a = a_ref[...].astype(F32)
    halo = jnp.where(i > 0, halo_ref[...].astype(F32), 0.0)
    ext = jnp.concatenate([halo, a], axis=0)
    p2 = ext[1:] + ext[:-1]
    p4 = p2[2:] + p2[:-2]
    p8 = p4[4:] + p4[:-4]
    p16 = p8[8:] + p8[:-8]
    sums = (p2[15:15 + tm], p4[13:13 + tm], p8[9:9 + tm], p16[1:1 + tm])
    t = i * tm + lax.broadcasted_iota(jnp.int32, (tm, 1), 0)
    for gi, w in enumerate(POOL_WINDOWS):
        cols = slice(gi * GROUP_CH, (gi + 1) * GROUP_CH)
        cnt = jnp.minimum(t + 1, w).astype(F32)
        pooled = sums[gi][:, cols] / cnt - a[:, cols]
        y = _dot(pooled.astype(BF16), pw_ref[gi])
        ya_ref[:, cols] = (y * ps_ref[:, cols]).astype(ya_ref.dtype)

    u = jax.nn.gelu(u_ref[...].astype(F32))
    v = jax.nn.gelu(v_ref[...].astype(F32))
    mu = jnp.mean(v, axis=-1, keepdims=True)
    var = jnp.mean(jnp.square(v - mu), axis=-1, keepdims=True)
    vn = ((v - mu) * lax.rsqrt(var + 1e-5) * lg_ref[...] + lb_ref[...]).astype(BF16)
    r = lax.broadcasted_iota(jnp.int32, (GMLP_CHUNK, GMLP_CHUNK), 0)
    c = lax.broadcasted_iota(jnp.int32, (GMLP_CHUNK, GMLP_CHUNK), 1)
    for gi in range(4):
        cols = slice(gi * GROUP_CH, (gi + 1) * GROUP_CH)
        wsm = jnp.where(r >= c, ws_ref[gi], 0.0).astype(BF16)
        bias = bs_ref[:, gi:gi + 1]
        for ck in range(tm // GMLP_CHUNK):
            rows = slice(ck * GMLP_CHUNK, (ck + 1) * GMLP_CHUNK)
            mixed = _dot(wsm, vn[rows, cols]) + bias
            yb_ref[rows, cols] = (u[rows, cols] * mixed).astype(yb_ref.dtype)


def _mixers(proj, pool_w, pool_scale, ln_g, ln_b, ws, bs_t):
    s = proj.shape[0]
    tm = min(512, s)
    hb = tm // POOL_HALO
    const2 = lambda i: (0, 0)
    const3 = lambda i: (0, 0, 0)
    return pl.pallas_call(
        _mixer_kernel,
        grid=(s // tm,),
        in_specs=[
            pl.BlockSpec((tm, MIX_W), lambda i: (i, OFF_POOL // MIX_W)),
            pl.BlockSpec((POOL_HALO, MIX_W), lambda i: (jnp.maximum(i * hb - 1, 0), OFF_POOL // MIX_W)),
            pl.BlockSpec((tm, MIX_W), lambda i: (i, OFF_U // MIX_W)),
            pl.BlockSpec((tm, MIX_W), lambda i: (i, OFF_V // MIX_W)),
            pl.BlockSpec((4, GROUP_CH, GROUP_CH), const3),
            pl.BlockSpec((1, MIX_W), const2),
            pl.BlockSpec((1, MIX_W), const2),
            pl.BlockSpec((1, MIX_W), const2),
            pl.BlockSpec((4, GMLP_CHUNK, GMLP_CHUNK), const3),
            pl.BlockSpec((GMLP_CHUNK, 4), const2),
        ],
        out_specs=[
            pl.BlockSpec((tm, MIX_W), lambda i: (i, 0)),
            pl.BlockSpec((tm, MIX_W), lambda i: (i, 0)),
        ],
        out_shape=[jax.ShapeDtypeStruct((s, MIX_W), BF16)] * 2,
        compiler_params=_cparams("parallel"),
        name="mixers",
    )(proj, proj, proj, proj, pool_w, pool_scale, ln_g, ln_b, ws, bs_t)


def _compress_kernel(x_ref, pos_ref, w1_ref, b1_ref, w2_ref, b2_ref, kc_ref, vct_ref):
    half = CMP_STRIDE * HEAD_DIM
    x = x_ref[0, 0].astype(F32)
    nc = x.shape[0]
    pos = pos_ref[0]
    first = _dot((x + pos[:, :half]).astype(BF16), w1_ref[0, :half, :])
    second = _dot((x + pos[:, half:]).astype(BF16), w1_ref[0, half:, :])
    hid = jax.nn.gelu(first + pltpu.roll(second, nc - 1, 0) + b1_ref[0])
    out = _dot(hid.astype(BF16), w2_ref[0]) + b2_ref[0]

    @pl.when(pl.program_id(1) == 0)
    def _():
        lane = lax.broadcasted_iota(jnp.int32, (nc, 128), 1)
        grp = lax.broadcasted_iota(jnp.int32, (nc, 128), 0) // 8
        kc_ref[0, :, 0:128] = (out + (lane - HEAD_DIM == grp).astype(F32)).astype(BF16)
        kc_ref[0, :, 128:] = (lane + (128 - HEAD_DIM) == grp).astype(BF16)

    @pl.when(pl.program_id(1) == 1)
    def _():
        ones_row = lax.broadcasted_iota(jnp.int32, (V_AUG_ROWS, nc), 0) == HEAD_DIM
        vct_ref[0] = (out.T[0:V_AUG_ROWS] + ones_row.astype(F32)).astype(BF16)


def _compress(xkv, pos, w1, b1, w2, b2):
    _, n_group, nc, width = xkv.shape
    assert CMP_AUG_W == 256 and HEAD_DIM + nc // 8 <= CMP_AUG_W
    per_kv = lambda g, a: (a, 0, 0)
    per_group = lambda g, a: (g, 0, 0)
    return pl.pallas_call(
        _compress_kernel,
        grid=(n_group, 2),
        in_specs=[
            pl.BlockSpec((1, 1, nc, width), lambda g, a: (a, g, 0, 0)),
            pl.BlockSpec((1, 1, 2 * width), per_kv),
            pl.BlockSpec((1, 2 * width, 128), per_kv),
            pl.BlockSpec((1, 1, 128), per_kv),
            pl.BlockSpec((1, 128, 128), per_kv),
            pl.BlockSpec((1, 1, 128), per_kv),
        ],
        out_specs=[pl.BlockSpec((1, nc, CMP_AUG_W), per_group), pl.BlockSpec((1, V_AUG_ROWS, nc), per_group)],
        out_shape=[jax.ShapeDtypeStruct((n_group, nc, CMP_AUG_W), BF16),
                   jax.ShapeDtypeStruct((n_group, V_AUG_ROWS, nc), BF16)],
        compiler_params=_cparams("parallel", "arbitrary"),
        name="compress_kv",
    )(xkv, pos, w1, b1, w2, b2)


def _operand_kernel(q_lo_ref, q_hi_ref, sel_ref, win_ref, ng_ref, qt_ref, ks_ref, vst_ref, kw_ref, vwt_ref, gl_ref):
    i = pl.program_id(0)
    n_tiles = pl.num_programs(0) - PAD_TILES
    rows = q_lo_ref.shape[0]
    lane = lax.broadcasted_iota(jnp.int32, (rows, 128), 1)
    pos = i * rows + lax.broadcasted_iota(jnp.int32, (rows, 128), 0)
    ones_rows = (lax.broadcasted_iota(jnp.int32, (V_AUG_ROWS - HEAD_DIM, rows), 0) == 0).astype(BF16)

    def key_pair(slab, idx):
        onehot = (lane - HEAD_DIM == idx).astype(F32)
        left = jnp.where(lane < HEAD_DIM, slab, onehot)
        right = jnp.where(lane < HEAD_DIM, pltpu.roll(slab, HEAD_DIM, 1), onehot)
        return left.astype(BF16), right.astype(BF16)

    def transposed_pair(slab):
        t = slab.T.astype(BF16)
        return t[:HEAD_DIM], t[HEAD_DIM:]

    def write_kv(src, keep, idx, k_ref, vt_ref):
        for pair in range(KV_GROUPS // 2):
            lanes = slice(pair * 128, (pair + 1) * 128)
            k_slab = jnp.where(keep, src[:, lanes].astype(F32), 0.0)
            v_slab = jnp.where(keep, src[:, KV_W + pair * 128:KV_W + (pair + 1) * 128].astype(F32), 0.0)
            for g, k, vt in zip((2 * pair, 2 * pair + 1), key_pair(k_slab, idx), transposed_pair(v_slab)):
                k_ref[g] = k
                vt_ref[g, 0:HEAD_DIM, :] = vt
                vt_ref[g, HEAD_DIM:, :] = ones_rows

    @pl.when(i < n_tiles)
    def _():
        for half, ref in enumerate((q_lo_ref, q_hi_ref)):
            for gg in range(2):
                for pp in range(HEADS_PER_GROUP // 2):
                    lanes = slice(gg * 256 + pp * 128, gg * 256 + (pp + 1) * 128)
                    for h, t in zip((2 * pp, 2 * pp + 1), transposed_pair(ref[:, lanes].astype(F32))):
                        qt_ref[2 * half + gg, 0, :, h * Q_BLOCK:(h + 1) * Q_BLOCK] = t
        write_kv(sel_ref, True, (pos // SEL_BLOCK) % (SEL_TILE // SEL_BLOCK), ks_ref, vst_ref)
        logits_t = ng_ref[...].T
        for g in range(KV_GROUPS):
            for h in range(HEADS_PER_GROUP):
                for br in range(3):
                    r = (g * HEADS_PER_GROUP + h) * 3 + br
                    gl_ref[g, 0, br:br + 1, h * Q_BLOCK:(h + 1) * Q_BLOCK] = logits_t[r:r + 1, :]

    write_kv(win_ref, i >= PAD_TILES, (pos // WIN_CHUNK) % 8, kw_ref, vwt_ref)


def _attention_operands(proj, ngate):
    s = proj.shape[0]
    n_tiles = s // Q_BLOCK
    blk = MIX_W
    real = lambda i: jnp.minimum(i, n_tiles - 1)
    src = lambda c: pl.BlockSpec((Q_BLOCK, blk), lambda i: (real(i), c))
    return pl.pallas_call(
        _operand_kernel,
        grid=(n_tiles + PAD_TILES,),
        in_specs=[src(OFF_Q // blk), src(OFF_Q // blk + 1), src(OFF_KV // blk + 1),
                  pl.BlockSpec((Q_BLOCK, blk), lambda i: (jnp.maximum(i - PAD_TILES, 0), OFF_KV // blk + 2)),
                  pl.BlockSpec((Q_BLOCK, 128), lambda i: (real(i), 0))],
        out_specs=[
            pl.BlockSpec((KV_GROUPS, 1, HEAD_DIM, QL), lambda i: (0, real(i), 0, 0)),
            pl.BlockSpec((KV_GROUPS, Q_BLOCK, KS_AUG_W), lambda i: (0, real(i), 0)),
            pl.BlockSpec((KV_GROUPS, V_AUG_ROWS, Q_BLOCK), lambda i: (0, 0, real(i))),
            pl.BlockSpec((KV_GROUPS, Q_BLOCK, KS_AUG_W), lambda i: (0, i, 0)),
            pl.BlockSpec((KV_GROUPS, V_AUG_ROWS, Q_BLOCK), lambda i: (0, 0, i)),
            pl.BlockSpec((KV_GROUPS, 1, 3, QL), lambda i: (0, real(i), 0, 0)),
        ],
        out_shape=[
            jax.ShapeDtypeStruct((KV_GROUPS, n_tiles, HEAD_DIM, QL), BF16),
            jax.ShapeDtypeStruct((KV_GROUPS, s, KS_AUG_W), BF16),
            jax.ShapeDtypeStruct((KV_GROUPS, V_AUG_ROWS, s), BF16),
            jax.ShapeDtypeStruct((KV_GROUPS, s + WINDOW, KS_AUG_W), BF16),
            jax.ShapeDtypeStruct((KV_GROUPS, V_AUG_ROWS, s + WINDOW), BF16),
            jax.ShapeDtypeStruct((KV_GROUPS, n_tiles, 3, QL), F32),
        ],
        compiler_params=_cparams("arbitrary"),
        name="attn_operands",
    )(proj, proj, proj, proj, ngate)


def _nsa_kernel(qt_ref, kc_ref, vct_ref, ks_ref, vst_ref, kw_ref, vwt_ref, gl_ref, o_ref,
                ps_scr, bias_scr, diag_bias_scr, qc_scr, qw_scr, qs_scr, sc_scr, s0_scr, s1_scr, p0_scr, p1_scr):
    qb = pl.program_id(1)
    t0 = qb * Q_BLOCK
    t0a = pl.multiple_of(t0, Q_BLOCK)
    qt = qt_ref[0, 0]
    nc = kc_ref.shape[1]
    n_sel = bias_scr.shape[0]
    q_rel = lax.broadcasted_iota(jnp.int32, (1, QL), 1) % Q_BLOCK
    tq = t0 + q_rel

    def bias_rows(cond, rows):
        return jnp.where(cond, NEG, 0.0).astype(BF16) + jnp.zeros((rows, QL), BF16)

    def flash_update(s, carry, vt):
        m_prev, acc = carry
        m_new = jnp.maximum(m_prev, jnp.max(s, axis=0, keepdims=True))
        p = jnp.exp2((s - m_new).astype(BF16))
        return m_new, jnp.exp2(m_prev - m_new) * acc + _dot(vt, p)

    init = (jnp.full((1, QL), NEG, F32), jnp.zeros((V_AUG_ROWS, QL), F32))

    grp = lax.broadcasted_iota(jnp.int32, (qc_scr.shape[0] - HEAD_DIM, 1), 0)
    visible = CMP_PER_QB * (qb + 1)
    qc_scr[0:HEAD_DIM, :] = qt
    qc_scr[HEAD_DIM:, :] = bias_rows((8 * grp >= visible) & (grp < nc // 8), qc_scr.shape[0] - HEAD_DIM)
    band = CMP_PER_QB + 8
    r0 = pl.multiple_of(jnp.maximum(visible - band, 0), 8)
    band_end = CMP_STRIDE * (r0 + lax.broadcasted_iota(jnp.int32, (band, 1), 0)) + (CMP_BLOCK - 1)

    def compressed(rows):
        def run():
            sc_scr[0:rows, :] = _dot(kc_ref[0, 0:rows, :], qc_scr[...])
            sc_scr[pl.ds(r0, band), :] = jnp.where(band_end <= tq, sc_scr[pl.ds(r0, band), :], NEG)
            s_c = sc_scr[0:rows, :]
            m_c = jnp.maximum(jnp.max(s_c, axis=0, keepdims=True), 0.1 * NEG)
            e_c = jnp.exp2(s_c - m_c)
            o_aug = _dot(vct_ref[0, :, 0:rows], e_c.astype(BF16))
            inv_c = 1.0 / jnp.maximum(o_aug[HEAD_DIM:HEAD_DIM + 1], 1e-30)
            p_grp = None
            for h in range(HEADS_PER_GROUP):
                lanes = slice(h * Q_BLOCK, (h + 1) * Q_BLOCK)
                p_h = e_c[:, lanes] * inv_c[:, lanes]
                p_grp = p_h if p_grp is None else p_grp + p_h
            for c in range(Q_BLOCK // 128):
                ps_scr[c, 8:8 + rows, :] = p_grp[:, c * 128:(c + 1) * 128]
                if rows < nc:
                    ps_scr[c, 8 + rows:, :] = jnp.zeros((nc - rows, 128), F32)
            return o_aug[:HEAD_DIM] * inv_c
        return run

    ps_scr[:, 0:8, :] = jnp.zeros((Q_BLOCK // 128, 8, 128), F32)
    n_part = 8 if nc // 8 >= band else 4
    part = nc // n_part

    def choose(lo, hi):
        if lo == hi:
            return compressed(lo * part)
        mid = (lo + hi) // 2
        return lambda: lax.cond(visible <= mid * part, choose(lo, mid), choose(mid + 1, hi))

    o_c = choose(1, n_part)()
    imp = jnp.concatenate([sum(ps_scr[c, pl.ds(k, n_sel, stride=4), :] for k in range(7, 12))
                           for c in range(Q_BLOCK // 128)], axis=1)

    n_tri = Q_BLOCK // WIN_CHUNK
    n_chunk = WINDOW // WIN_CHUNK + n_tri
    span = n_chunk * WIN_CHUNK
    rho = lax.broadcasted_iota(jnp.int32, (16, 1), 0)
    first_real = WINDOW // WIN_CHUNK - n_tri * qb
    qw_scr[0:HEAD_DIM, :] = qt
    qw_scr[HEAD_DIM:HEAD_DIM + 16, :] = bias_rows((rho < 8) & (((rho - n_tri * qb) & 7) < first_real), 16)
    qw_scr[HEAD_DIM + 16:, :] = jnp.zeros((qw_scr.shape[0] - HEAD_DIM - 16, QL), BF16)
    s_w = _dot(kw_ref[0, pl.ds(t0a, span), :], qw_scr[...])
    i_rel = lax.broadcasted_iota(jnp.int32, (Q_BLOCK, 1), 0)
    s_w = jnp.concatenate([jnp.where(i_rel > q_rel, s_w[:Q_BLOCK], NEG), s_w[Q_BLOCK:WINDOW],
                           jnp.where(i_rel <= q_rel, s_w[WINDOW:], NEG)], axis=0)
    _, acc_w = flash_update(s_w, init, vwt_ref[0, :, pl.ds(t0a, span)])

    s0_scr[...] = _dot(ks_ref[0, 0:SEL_TILE, 0:HEAD_DIM], qt)

    j_idx = lax.broadcasted_iota(jnp.int32, (n_sel, Q_BLOCK), 0)
    cur = (t0 + lax.broadcasted_iota(jnp.int32, (n_sel, Q_BLOCK), 1)) // SEL_BLOCK
    forced = (j_idx == 0) | (j_idx == cur) | (j_idx == cur - 1)
    valid = j_idx <= cur
    score = jnp.where(forced, -2.0, jnp.where(valid, imp, -1.0))
    n_pick = SEL_TOPN - 3

    fast = score
    for _ in range(n_pick):
        fast = jnp.where(fast == jnp.max(fast, axis=0, keepdims=True), -2.0, fast)
    retired = (fast < -1.5) & valid & jnp.logical_not(forced)
    most_retired = jnp.max(jnp.sum(retired.astype(F32), axis=0, keepdims=True))

    def ranked_with_ties():
        def pick_one(_, sc):
            best = jnp.max(sc, axis=0, keepdims=True)
            first = jnp.min(jnp.where(sc == best, j_idx, n_sel), axis=0, keepdims=True)
            return jnp.where(j_idx == first, -2.0, sc)
        return lax.fori_loop(0, n_pick, pick_one, score)

    ranked = lax.cond(most_retired > n_pick, ranked_with_ties, lambda: fast)
    chosen = valid & (ranked < -1.5)
    diag_bias_scr[...] = jnp.where(chosen, 0.0, NEG)
    bias_scr[...] = jnp.where(chosen & (j_idx * SEL_BLOCK < t0), 0.0, NEG)

    blocks_per_tile = SEL_TILE // SEL_BLOCK
    last_tile = ks_ref.shape[1] // SEL_TILE - 1
    qs_scr[0:HEAD_DIM, :] = qt
    qs_scr[HEAD_DIM + 16:, :] = jnp.zeros((qs_scr.shape[0] - HEAD_DIM - 16, QL), BF16)

    def tile_start(kt):
        return pl.multiple_of(jnp.clip(kt, 0, last_tile) * SEL_TILE, SEL_TILE)

    def set_bias_rows(table, kt):
        b0 = pl.multiple_of(jnp.minimum(kt, last_tile) * blocks_per_tile, blocks_per_tile)
        b16 = jnp.concatenate([table[pl.ds(b0, blocks_per_tile), :],
                               jnp.zeros((16 - blocks_per_tile, Q_BLOCK), F32)], axis=0)
        qs_scr[HEAD_DIM:HEAD_DIM + 16, :] = jnp.concatenate([b16] * HEADS_PER_GROUP, axis=1).astype(BF16)

    set_bias_rows(diag_bias_scr, t0 // SEL_TILE)
    s_d = _dot(ks_ref[0, pl.ds(t0a, Q_BLOCK), :], qs_scr[...])
    m_d, acc_d = flash_update(jnp.where(i_rel <= q_rel, s_d, NEG), init, vst_ref[0, :, pl.ds(t0a, Q_BLOCK)])

    def half_step(kt, s_cur, s_nxt, p_cur, p_prev, carry):
        m_prev, alpha_prev, acc = carry
        acc = alpha_prev * acc + _dot(vst_ref[0, :, pl.ds(tile_start(kt - 1), SEL_TILE)], p_prev[...])
        m8 = jnp.max(s_cur[...].reshape(SEL_TILE // 8, 8, QL), axis=0)
        m_new = jnp.maximum(m_prev, jnp.max(m8, axis=0, keepdims=True))
        set_bias_rows(bias_scr, kt + 1)
        k_nxt = tile_start(kt + 1)
        for h in range(SEL_TILE // QK_PART):
            for c in range(QK_PART // P_CHUNK):
                rows = slice(h * QK_PART + c * P_CHUNK, h * QK_PART + (c + 1) * P_CHUNK)
                p_cur[rows, :] = jnp.exp2((s_cur[rows, :] - m_new).astype(BF16))
            part = slice(h * QK_PART, (h + 1) * QK_PART)
            s_nxt[part, :] = _dot(ks_ref[0, pl.ds(k_nxt + h * QK_PART, QK_PART), :], qs_scr[...])
        return m_new, jnp.exp2(m_prev - m_new), acc

    def pair_step(i, carry):
        carry = half_step(2 * i, s0_scr, s1_scr, p0_scr, p1_scr, carry)
        return half_step(2 * i + 1, s1_scr, s0_scr, p1_scr, p0_scr, carry)

    n_pairs = ((t0 + SEL_TILE - 1) // SEL_TILE + 1) // 2
    for b in range(blocks_per_tile):
        rows = slice(b * SEL_BLOCK, (b + 1) * SEL_BLOCK)
        s0_scr[rows, :] = s0_scr[rows, :] + jnp.concatenate([bias_scr[b:b + 1, :]] * HEADS_PER_GROUP, axis=1)
    p1_scr[...] = jnp.zeros(p1_scr.shape, BF16)
    carry = (m_d, jnp.ones((1, QL), F32), acc_d)
    _, alpha_last, acc_s = lax.fori_loop(0, n_pairs, pair_step, carry)
    acc_s = alpha_last * acc_s + _dot(vst_ref[0, :, pl.ds(tile_start(2 * n_pairs - 1), SEL_TILE)], p1_scr[...])

    gate = jax.nn.sigmoid(gl_ref[0, 0])
    o = (gate[0:1] * o_c + gate[1:2] * (acc_s[:HEAD_DIM] / acc_s[HEAD_DIM:HEAD_DIM + 1])
         + gate[2:3] * (acc_w[:HEAD_DIM] / acc_w[HEAD_DIM:HEAD_DIM + 1]))
    stacked = jnp.concatenate([o[:, p * Q_BLOCK:(p + 1) * Q_BLOCK] for p in range(HEADS_PER_GROUP)], axis=0)
    o_ref[...] = stacked.T.astype(o_ref.dtype)


def _sparse_attention(qt, kc_aug, vc_aug_t, ks_aug, vs_aug_t, kw_aug, vw_aug_t, gate_logits, s):
    n_group, n_qb = qt.shape[:2]
    nc = kc_aug.shape[1]
    n_sel = s // SEL_BLOCK
    assert HEAD_DIM + nc // 8 <= CMP_AUG_W and nc // 4 >= CMP_PER_QB + 8 and WINDOW // WIN_CHUNK + Q_BLOCK // WIN_CHUNK <= 8
    per_step = lambda g, i: (g, i, 0, 0)
    per_group = lambda shape: pl.BlockSpec((1,) + shape, lambda g, i: (g, 0, 0), pipeline_mode=pl.Buffered(1))
    tile_f32 = pltpu.VMEM((SEL_TILE, QL), F32)
    tile_bf16 = pltpu.VMEM((SEL_TILE, QL), BF16)
    sel_table = pltpu.VMEM((n_sel, Q_BLOCK), F32)
    return pl.pallas_call(
        _nsa_kernel,
        grid=(n_group, n_qb),
        in_specs=[
            pl.BlockSpec((1, 1, HEAD_DIM, QL), per_step),
            per_group((nc, CMP_AUG_W)),
            per_group((V_AUG_ROWS, nc)),
            per_group((s, KS_AUG_W)),
            per_group((V_AUG_ROWS, s)),
            per_group((s + WINDOW, KS_AUG_W)),
            per_group((V_AUG_ROWS, s + WINDOW)),
            pl.BlockSpec((1, 1, 3, QL), per_step),
        ],
        out_specs=pl.BlockSpec((Q_BLOCK, HEADS_PER_GROUP * HEAD_DIM), lambda g, i: (i, g)),
        out_shape=jax.ShapeDtypeStruct((s, Q_W), BF16),
        scratch_shapes=[pltpu.VMEM((Q_BLOCK // 128, 8 + nc, 128), F32), sel_table, sel_table,
                        pltpu.VMEM((CMP_AUG_W, QL), BF16), pltpu.VMEM((KS_AUG_W, QL), BF16),
                        pltpu.VMEM((KS_AUG_W, QL), BF16), pltpu.VMEM((nc, QL), F32),
                        tile_f32, tile_f32, tile_bf16, tile_bf16],
        compiler_params=_cparams("parallel", "arbitrary"),
        name="sparse_attn",
    )(qt, kc_aug, vc_aug_t, ks_aug, vs_aug_t, kw_aug, vw_aug_t, gate_logits)


def _merge_kernel(x_ref, ya_ref, yb_ref, yc_ref, g0_ref, g1_ref, g2_ref, wp_ref, wg_ref, wn_ref,
                  wo_ref, ng_ref, gate_ref, o_ref):
    merged = jax.nn.sigmoid(g0_ref[...].astype(F32)) * _dot(ya_ref[...], wp_ref[0])
    merged += jax.nn.sigmoid(g1_ref[...].astype(F32)) * _dot(yb_ref[...], wg_ref[0])
    merged += jax.nn.sigmoid(g2_ref[...].astype(F32)) * _dot(yc_ref[...], wn_ref[0])
    y = _dot(merged.astype(BF16), wo_ref[0])
    o_ref[...] = x_ref[...] + gate_ref[...] * _rms(y, ng_ref[...])


def _merge(x, ya, yb, yc, proj, wp, wg, wn, wo, layer, norm_g, gate):
    s, d = x.shape
    tm = min(256, s)
    const = lambda i: (0, 0)
    rows = lambda i: (i, 0)
    whole = lambda a: pl.BlockSpec((1,) + a.shape[1:], lambda i: (layer, 0, 0), pipeline_mode=pl.Buffered(1))
    bg = OFF_BG // d
    return pl.pallas_call(
        _merge_kernel,
        grid=(s // tm,),
        in_specs=[
            pl.BlockSpec((tm, d), rows),
            pl.BlockSpec((tm, MIX_W), rows),
            pl.BlockSpec((tm, MIX_W), rows),
            pl.BlockSpec((tm, Q_W), rows),
            pl.BlockSpec((tm, d), lambda i: (i, bg)),
            pl.BlockSpec((tm, d), lambda i: (i, bg + 1)),
            pl.BlockSpec((tm, d), lambda i: (i, bg + 2)),
            whole(wp), whole(wg), whole(wn), whole(wo),
            pl.BlockSpec((1, d), const),
            pl.BlockSpec((1, d), const),
        ],
        out_specs=pl.BlockSpec((tm, d), rows),
        out_shape=jax.ShapeDtypeStruct((s, d), F32),
        compiler_params=_cparams("parallel"),
        name="merge_out",
    )(x, ya, yb, yc, proj, proj, proj, wp, wg, wn, wo, norm_g, gate)


def _ffn_kernel(x_ref, gi_ref, sc_ref, sh_ref, w1_ref, w2_ref, go_ref, gate_ref, o_ref, h_scr):
    j = pl.program_id(1)

    @pl.when(j == 0)
    def _():
        h = _rms(x_ref[...], gi_ref[...]) * (1.0 + sc_ref[...]) + sh_ref[...]
        h_scr[...] = h.astype(BF16)
        o_ref[...] = jnp.zeros_like(o_ref)

    a = jnp.square(jnp.maximum(_dot(h_scr[...], w1_ref[0]), 0.0))
    o_ref[...] += _dot(a.astype(BF16), w2_ref[0])

    @pl.when(j == pl.num_programs(1) - 1)
    def _():
        o_ref[...] = x_ref[...] + gate_ref[...] * _rms(o_ref[...], go_ref[...])


def _ffn(x, g_in, scale, shift, w1, w2, layer, g_out, gate):
    s, d = x.shape
    tm, tf = min(1024, s), 1024
    const = lambda i, j: (0, 0)
    rows = lambda i, j: (i, 0)
    return pl.pallas_call(
        _ffn_kernel,
        grid=(s // tm, D_FF // tf),
        in_specs=[
            pl.BlockSpec((tm, d), rows, pipeline_mode=pl.Buffered(1)),
            pl.BlockSpec((1, d), const),
            pl.BlockSpec((1, d), const),
            pl.BlockSpec((1, d), const),
            pl.BlockSpec((1, d, tf), lambda i, j: (layer, 0, j)),
            pl.BlockSpec((1, tf, d), lambda i, j: (layer, j, 0)),
            pl.BlockSpec((1, d), const),
            pl.BlockSpec((1, d), const),
        ],
        out_specs=pl.BlockSpec((tm, d), rows),
        out_shape=jax.ShapeDtypeStruct((s, d), F32),
        scratch_shapes=[pltpu.VMEM((tm, d), BF16)],
        compiler_params=_cparams("parallel", "arbitrary"),
        name="ffn",
    )(x, g_in, scale, shift, w1, w2, g_out, gate)


def _token_mixing(x, mod, norm_g, w_in, pool_w, pool_scale, ln_g, ln_b, ws, bs, cmp_pos, cmp_w1,
                  cmp_b1, cmp_w2, cmp_b2, branch_weights, layer):
    s, d = x.shape
    ng0 = OFF_BG
    w_main = jnp.concatenate([w_in[:, :ng0], w_in[:, ng0 + N_GATE:]], axis=1).astype(BF16)
    w_ng = jnp.pad(w_in[:, ng0:ng0 + N_GATE], ((0, 0), (0, 128 - N_GATE))).astype(BF16)
    row = lambda v: v.reshape(1, -1)

    col_scale = jnp.ones((1, PROJ_W), F32).at[:, OFF_Q:OFF_KV].set(HEAD_DIM ** -0.5 * LOG2E)
    proj, ngate = _in_projection(x, row(norm_g[0]), row(mod[1]), row(mod[0]), w_main, w_ng, col_scale)

    ya, yb = _mixers(proj, pool_w.astype(BF16), row(pool_scale), row(ln_g), row(ln_b), ws, bs.T)

    qt, ks_aug, vs_aug_t, kw_aug, vw_aug_t, gl = _attention_operands(proj, ngate)
    kv_cmp = proj[:, OFF_KV:OFF_KV + 2 * KV_W].reshape(s, 2, KV_GROUPS, HEAD_DIM).transpose(1, 2, 0, 3)
    xkv = kv_cmp.reshape(2, KV_GROUPS, s // CMP_STRIDE, CMP_STRIDE * HEAD_DIM)
    lane_pad = ((0, 0), (0, 0), (0, 128 - HEAD_DIM))
    kc_aug, vc_aug_t = _compress(xkv, cmp_pos.reshape(2, 1, CMP_BLOCK * HEAD_DIM), cmp_w1.astype(BF16),
                                 cmp_b1.reshape(2, 1, -1), jnp.pad(cmp_w2, lane_pad).astype(BF16),
                                 jnp.pad(cmp_b2.reshape(2, 1, -1), lane_pad))
    yc = _sparse_attention(qt, kc_aug, vc_aug_t, ks_aug, vs_aug_t, kw_aug, vw_aug_t, gl, s)

    return _merge(x, ya, yb, yc, proj, *branch_weights, layer, row(norm_g[1]), row(mod[2]))


def kernel(x, c, norm_g, w_ada, b_ada, w_in, pool_w, pool_scale, gmlp_ln_g, gmlp_ln_b, gmlp_ws, gmlp_bs,
           cmp_pos, cmp_w1, cmp_b1, cmp_w2, cmp_b2, w_br_pool, w_br_gmlp, w_br_nsa, w_out, w_ff1, w_ff2):
    b, s, d = x.shape
    assert b == 1 and d == D_MODEL and s % 1024 == 0
    n_layer = w_ada.shape[0]
    mod_all = _modulation(c, w_ada, b_ada).reshape(n_layer, 6, d)
    xs = x[0]
    row = lambda v: v.reshape(1, -1)
    branch_weights = tuple(w.astype(BF16) for w in (w_br_pool, w_br_gmlp, w_br_nsa, w_out))
    w_ff1_b, w_ff2_b = w_ff1.astype(BF16), w_ff2.astype(BF16)
    for l in range(n_layer):
        mod = mod_all[l]
        xs = _token_mixing(xs, mod, norm_g[l], w_in[l], pool_w[l], pool_scale[l], gmlp_ln_g[l],
                           gmlp_ln_b[l], gmlp_ws[l], gmlp_bs[l], cmp_pos[l], cmp_w1[l], cmp_b1[l],
                           cmp_w2[l], cmp_b2[l], branch_weights, l)
        xs = _ffn(xs, row(norm_g[l, 2]), row(mod[4]), row(mod[3]), w_ff1_b, w_ff2_b, l,
                  row(norm_g[l, 3]), row(mod[5]))
    return xs[None]
```

```python
import jax
import jax.numpy as jnp
from jax import lax
from jax.experimental import pallas as pl
from jax.experimental.pallas import tpu as pltpu

F32 = jnp.float32
BF16 = jnp.bfloat16

D_MODEL = 2048
POOL_WINDOWS = (2, 4, 8, 16)
POOL_HALO = 16
GROUP_CH = 128
MIX_W = 4 * GROUP_CH
GMLP_CHUNK = 128
HEAD_DIM = 64
KV_GROUPS = 4
HEADS_PER_GROUP = 4
Q_W = 16 * HEAD_DIM
KV_W = KV_GROUPS * HEAD_DIM
CMP_BLOCK = 32
CMP_STRIDE = 16
SEL_BLOCK = 64
SEL_TOPN = 16
WINDOW = 512
FORCE_BONUS = 1000.0
LOG2E = 1.4426950408889634
NEG = -1e30
N_GATE = 3 * 16
D_FF = 4 * D_MODEL

Q_BLOCK = 256
QL = HEADS_PER_GROUP * Q_BLOCK
CMP_PER_QB = Q_BLOCK // CMP_STRIDE
WIN_CHUNK = 128
PAD_TILES = WINDOW // Q_BLOCK
SEL_TILE = 512
KS_AUG_W = 128
CMP_AUG_W = 256
V_AUG_ROWS = HEAD_DIM + 16
P_CHUNK = 128
QK_PART = 128

OFF_POOL, OFF_U, OFF_V, OFF_Q, OFF_KV, OFF_BG = 0, 512, 1024, 1536, 2560, 4096
PROJ_W = OFF_BG + 3 * D_MODEL

VMEM_LIMIT = 60 * 1024 * 1024


def _cparams(*sem):
    return pltpu.CompilerParams(dimension_semantics=sem, vmem_limit_bytes=VMEM_LIMIT)


def _dot(a, b):
    return jnp.dot(a, b, preferred_element_type=F32)


def _rms(x, g):
    return x * lax.rsqrt(jnp.mean(x * x, axis=-1, keepdims=True) + 1e-6) * g


def _mod_kernel(c_ref, w_ref, b_ref, o_ref):
    c = c_ref[...]
    act = c * jax.nn.sigmoid(c)
    o_ref[0] = jnp.sum(act * w_ref[0], axis=0, keepdims=True) + b_ref[0]


def _modulation(c, w_ada, b_ada):
    n_layer, d, n_out = w_ada.shape
    tn = 1024
    return pl.pallas_call(
        _mod_kernel,
        grid=(n_layer, n_out // tn),
        in_specs=[
            pl.BlockSpec((d, 1), lambda l, j: (0, 0)),
            pl.BlockSpec((1, d, tn), lambda l, j: (l, 0, j)),
            pl.BlockSpec((1, 1, tn), lambda l, j: (l, 0, j)),
        ],
        out_specs=pl.BlockSpec((1, 1, tn), lambda l, j: (l, 0, j)),
        out_shape=jax.ShapeDtypeStruct((n_layer, 1, n_out), F32),
        compiler_params=_cparams("parallel", "parallel"),
        name="adaln_mod",
    )(c.reshape(d, 1), w_ada, b_ada.reshape(n_layer, 1, n_out))


def _inproj_kernel(x_ref, g_ref, sc_ref, sh_ref, w_ref, wng_ref, cs_ref, o_ref, ng_ref, h_scr):
    @pl.when(pl.program_id(1) == 0)
    def _():
        h = _rms(x_ref[...], g_ref[...]) * (1.0 + sc_ref[...]) + sh_ref[...]
        hb = h.astype(BF16)
        h_scr[...] = hb
        ng_ref[...] = _dot(hb, wng_ref[...])

    o_ref[...] = (_dot(h_scr[...], w_ref[...]) * cs_ref[...]).astype(o_ref.dtype)


def _in_projection(x, g, scale, shift, w_main, w_ng, col_scale):
    s, d = x.shape
    tm, tn = min(1024, s), 2048
    row = lambda i, j: (0, 0)
    return pl.pallas_call(
        _inproj_kernel,
        grid=(s // tm, PROJ_W // tn),
        in_specs=[
            pl.BlockSpec((tm, d), lambda i, j: (i, 0)),
            pl.BlockSpec((1, d), row),
            pl.BlockSpec((1, d), row),
            pl.BlockSpec((1, d), row),
            pl.BlockSpec((d, tn), lambda i, j: (0, j)),
            pl.BlockSpec((d, 128), row),
            pl.BlockSpec((1, tn), lambda i, j: (0, j)),
        ],
        out_specs=[
            pl.BlockSpec((tm, tn), lambda i, j: (i, j)),
            pl.BlockSpec((tm, 128), lambda i, j: (i, 0)),
        ],
        out_shape=[
            jax.ShapeDtypeStruct((s, PROJ_W), BF16),
            jax.ShapeDtypeStruct((s, 128), F32),
        ],
        scratch_shapes=[pltpu.VMEM((tm, d), BF16)],
        compiler_params=_cparams("parallel", "arbitrary"),
        name="in_proj",
    )(x, g, scale, shift, w_main, w_ng, col_scale)


def _mixer_kernel(a_ref, halo_ref, u_ref, v_ref, pw_ref, ps_ref, lg_ref, lb_ref, ws_ref, bs_ref,
                  ya_ref, yb_ref):
    i = pl.program_id(0)
    tm = a_ref.shape[0]
    a = a_ref[...].astype(F32)
    halo = jnp.where(i > 0, halo_ref[...].astype(F32), 0.0)
    ext = jnp.concatenate([halo, a], axis=0)
    p2 = ext[1:] + ext[:-1]
    p4 = p2[2:] + p2[:-2]
    p8 = p4[4:] + p4[:-4]
    p16 = p8[8:] + p8[:-8]
    sums = (p2[15:15 + tm], p4[13:13 + tm], p8[9:9 + tm], p16[1:1 + tm])
    t = i * tm + lax.broadcasted_iota(jnp.int32, (tm, 1), 0)
    for gi, w in enumerate(POOL_WINDOWS):
        cols = slice(gi * GROUP_CH, (gi + 1) * GROUP_CH)
        cnt = jnp.minimum(t + 1, w).astype(F32)
        pooled = sums[gi][:, cols] / cnt - a[:, cols]
        y = _dot(pooled.astype(BF16), pw_ref[gi])
        ya_ref[:, cols] = (y * ps_ref[:, cols]).astype(ya_ref.dtype)

    u = jax.nn.gelu(u_ref[...].astype(F32))
    v = jax.nn.gelu(v_ref[...].astype(F32))
    mu = jnp.mean(v, axis=-1, keepdims=True)
    var = jnp.mean(jnp.square(v - mu), axis=-1, keepdims=True)
    vn = ((v - mu) * lax.rsqrt(var + 1e-5) * lg_ref[...] + lb_ref[...]).astype(BF16)
    r = lax.broadcasted_iota(jnp.int32, (GMLP_CHUNK, GMLP_CHUNK), 0)
    c = lax.broadcasted_iota(jnp.int32, (GMLP_CHUNK, GMLP_CHUNK), 1)
    for gi in range(4):
        cols = slice(gi * GROUP_CH, (gi + 1) * GROUP_CH)
        wsm = jnp.where(r >= c, ws_ref[gi], 0.0).astype(BF16)
        bias = bs_ref[:, gi:gi + 1]
        for ck in range(tm // GMLP_CHUNK):
            rows = slice(ck * GMLP_CHUNK, (ck + 1) * GMLP_CHUNK)
            mixed = _dot(wsm, vn[rows, cols]) + bias
            yb_ref[rows, cols] = (u[rows, cols] * mixed).astype(yb_ref.dtype)


def _mixers(proj, pool_w, pool_scale, ln_g, ln_b, ws, bs_t):
    s = proj.shape[0]
    tm = min(512, s)
    hb = tm // POOL_HALO
    const2 = lambda i: (0, 0)
    const3 = lambda i: (0, 0, 0)
    return pl.pallas_call(
        _mixer_kernel,
        grid=(s // tm,),
        in_specs=[
            pl.BlockSpec((tm, MIX_W), lambda i: (i, OFF_POOL // MIX_W)),
            pl.BlockSpec((POOL_HALO, MIX_W), lambda i: (jnp.maximum(i * hb - 1, 0), OFF_POOL // MIX_W)),
            pl.BlockSpec((tm, MIX_W), lambda i: (i, OFF_U // MIX_W)),
            pl.BlockSpec((tm, MIX_W), lambda i: (i, OFF_V // MIX_W)),
            pl.BlockSpec((4, GROUP_CH, GROUP_CH), const3),
            pl.BlockSpec((1, MIX_W), const2),
            pl.BlockSpec((1, MIX_W), const2),
            pl.BlockSpec((1, MIX_W), const2),
            pl.BlockSpec((4, GMLP_CHUNK, GMLP_CHUNK), const3),
            pl.BlockSpec((GMLP_CHUNK, 4), const2),
        ],
        out_specs=[
            pl.BlockSpec((tm, MIX_W), lambda i: (i, 0)),
            pl.BlockSpec((tm, MIX_W), lambda i: (i, 0)),
        ],
        out_shape=[jax.ShapeDtypeStruct((s, MIX_W), BF16)] * 2,
        compiler_params=_cparams("parallel"),
        name="mixers",
    )(proj, proj, proj, proj, pool_w, pool_scale, ln_g, ln_b, ws, bs_t)


def _compress_kernel(x_ref, pos_ref, w1_ref, b1_ref, w2_ref, b2_ref, kc_ref, vct_ref):
    half = CMP_STRIDE * HEAD_DIM
    x = x_ref[0, 0].astype(F32)
    nc = x.shape[0]
    pos = pos_ref[0]
    first = _dot((x + pos[:, :half]).astype(BF16), w1_ref[0, :half, :])
    second = _dot((x + pos[:, half:]).astype(BF16), w1_ref[0, half:, :])
    hid = jax.nn.gelu(first + pltpu.roll(second, nc - 1, 0) + b1_ref[0])
    out = _dot(hid.astype(BF16), w2_ref[0]) + b2_ref[0]

    @pl.when(pl.program_id(1) == 0)
    def _():
        lane = lax.broadcasted_iota(jnp.int32, (nc, 128), 1)
        grp = lax.broadcasted_iota(jnp.int32, (nc, 128), 0) // 8
        kc_ref[0, :, 0:128] = (out + (lane - HEAD_DIM == grp).astype(F32)).astype(BF16)
        kc_ref[0, :, 128:] = (lane + (128 - HEAD_DIM) == grp).astype(BF16)

    @pl.when(pl.program_id(1) == 1)
    def _():
        ones_row = lax.broadcasted_iota(jnp.int32, (V_AUG_ROWS, nc), 0) == HEAD_DIM
        vct_ref[0] = (out.T[0:V_AUG_ROWS] + ones_row.astype(F32)).astype(BF16)


def _compress(xkv, pos, w1, b1, w2, b2):
    _, n_group, nc, width = xkv.shape
    assert CMP_AUG_W == 256 and HEAD_DIM + nc // 8 <= CMP_AUG_W
    per_kv = lambda g, a: (a, 0, 0)
    per_group = lambda g, a: (g, 0, 0)
    return pl.pallas_call(
        _compress_kernel,
        grid=(n_group, 2),
        in_specs=[
            pl.BlockSpec((1, 1, nc, width), lambda g, a: (a, g, 0, 0)),
            pl.BlockSpec((1, 1, 2 * width), per_kv),
            pl.BlockSpec((1, 2 * width, 128), per_kv),
            pl.BlockSpec((1, 1, 128), per_kv),
            pl.BlockSpec((1, 128, 128), per_kv),
            pl.BlockSpec((1, 1, 128), per_kv),
        ],
        out_specs=[pl.BlockSpec((1, nc, CMP_AUG_W), per_group), pl.BlockSpec((1, V_AUG_ROWS, nc), per_group)],
        out_shape=[jax.ShapeDtypeStruct((n_group, nc, CMP_AUG_W), BF16),
                   jax.ShapeDtypeStruct((n_group, V_AUG_ROWS, nc), BF16)],
        compiler_params=_cparams("parallel", "arbitrary"),
        name="compress_kv",
    )(xkv, pos, w1, b1, w2, b2)


def _operand_kernel(q_lo_ref, q_hi_ref, sel_ref, win_ref, ng_ref, qt_ref, ks_ref, vst_ref, kw_ref, vwt_ref, gl_ref):
    i = pl.program_id(0)
    n_tiles = pl.num_programs(0) - PAD_TILES
    rows = q_lo_ref.shape[0]
    lane = lax.broadcasted_iota(jnp.int32, (rows, 128), 1)
    pos = i * rows + lax.broadcasted_iota(jnp.int32, (rows, 128), 0)
    ones_rows = (lax.broadcasted_iota(jnp.int32, (V_AUG_ROWS - HEAD_DIM, rows), 0) == 0).astype(BF16)

    def key_pair(slab, idx):
        onehot = (lane - HEAD_DIM == idx).astype(F32)
        left = jnp.where(lane < HEAD_DIM, slab, onehot)
        right = jnp.where(lane < HEAD_DIM, pltpu.roll(slab, HEAD_DIM, 1), onehot)
        return left.astype(BF16), right.astype(BF16)

    def transposed_pair(slab):
        t = slab.T.astype(BF16)
        return t[:HEAD_DIM], t[HEAD_DIM:]

    def write_kv(src, keep, idx, k_ref, vt_ref):
        for pair in range(KV_GROUPS // 2):
            lanes = slice(pair * 128, (pair + 1) * 128)
            k_slab = jnp.where(keep, src[:, lanes].astype(F32), 0.0)
            v_slab = jnp.where(keep, src[:, KV_W + pair * 128:KV_W + (pair + 1) * 128].astype(F32), 0.0)
            for g, k, vt in zip((2 * pair, 2 * pair + 1), key_pair(k_slab, idx), transposed_pair(v_slab)):
                k_ref[g] = k
                vt_ref[g, 0:HEAD_DIM, :] = vt
                vt_ref[g, HEAD_DIM:, :] = ones_rows

    @pl.when(i < n_tiles)
    def _():
        for half, ref in enumerate((q_lo_ref, q_hi_ref)):
            for gg in range(2):
                for pp in range(HEADS_PER_GROUP // 2):
                    lanes = slice(gg * 256 + pp * 128, gg * 256 + (pp + 1) * 128)
                    for h, t in zip((2 * pp, 2 * pp + 1), transposed_pair(ref[:, lanes].astype(F32))):
                        qt_ref[2 * half + gg, 0, :, h * Q_BLOCK:(h + 1) * Q_BLOCK] = t
        write_kv(sel_ref, True, (pos // SEL_BLOCK) % (SEL_TILE // SEL_BLOCK), ks_ref, vst_ref)
        logits_t = ng_ref[...].T
        for g in range(KV_GROUPS):
            for h in range(HEADS_PER_GROUP):
                for br in range(3):
                    r = (g * HEADS_PER_GROUP + h) * 3 + br
                    gl_ref[g, 0, br:br + 1, h * Q_BLOCK:(h + 1) * Q_BLOCK] = logits_t[r:r + 1, :]

    write_kv(win_ref, i >= PAD_TILES, (pos // WIN_CHUNK) % 8, kw_ref, vwt_ref)


def _attention_operands(proj, ngate):
    s = proj.shape[0]
    n_tiles = s // Q_BLOCK
    blk = MIX_W
    real = lambda i: jnp.minimum(i, n_tiles - 1)
    src = lambda c: pl.BlockSpec((Q_BLOCK, blk), lambda i: (real(i), c))
    return pl.pallas_call(
        _operand_kernel,
        grid=(n_tiles + PAD_TILES,),
        in_specs=[src(OFF_Q // blk), src(OFF_Q // blk + 1), src(OFF_KV // blk + 1),
                  pl.BlockSpec((Q_BLOCK, blk), lambda i: (jnp.maximum(i - PAD_TILES, 0), OFF_KV // blk + 2)),
                  pl.BlockSpec((Q_BLOCK, 128), lambda i: (real(i), 0))],
        out_specs=[
            pl.BlockSpec((KV_GROUPS, 1, HEAD_DIM, QL), lambda i: (0, real(i), 0, 0)),
            pl.BlockSpec((KV_GROUPS, Q_BLOCK, KS_AUG_W), lambda i: (0, real(i), 0)),
            pl.BlockSpec((KV_GROUPS, V_AUG_ROWS, Q_BLOCK), lambda i: (0, 0, real(i))),
            pl.BlockSpec((KV_GROUPS, Q_BLOCK, KS_AUG_W), lambda i: (0, i, 0)),
            pl.BlockSpec((KV_GROUPS, V_AUG_ROWS, Q_BLOCK), lambda i: (0, 0, i)),
            pl.BlockSpec((KV_GROUPS, 1, 3, QL), lambda i: (0, real(i), 0, 0)),
        ],
        out_shape=[
            jax.ShapeDtypeStruct((KV_GROUPS, n_tiles, HEAD_DIM, QL), BF16),
            jax.ShapeDtypeStruct((KV_GROUPS, s, KS_AUG_W), BF16),
            jax.ShapeDtypeStruct((KV_GROUPS, V_AUG_ROWS, s), BF16),
            jax.ShapeDtypeStruct((KV_GROUPS, s + WINDOW, KS_AUG_W), BF16),
            jax.ShapeDtypeStruct((KV_GROUPS, V_AUG_ROWS, s + WINDOW), BF16),
            jax.ShapeDtypeStruct((KV_GROUPS, n_tiles, 3, QL), F32),
        ],
        compiler_params=_cparams("arbitrary"),
        name="attn_operands",
    )(proj, proj, proj, proj, ngate)


def _nsa_kernel(qt_ref, kc_ref, vct_ref, ks_ref, vst_ref, kw_ref, vwt_ref, gl_ref, o_ref,
                ps_scr, bias_scr, diag_bias_scr, qc_scr, qw_scr, qs_scr, sc_scr, s0_scr, s1_scr, p0_scr, p1_scr):
    qb = pl.program_id(1)
    t0 = qb * Q_BLOCK
    t0a = pl.multiple_of(t0, Q_BLOCK)
    qt = qt_ref[0, 0]
    nc = kc_ref.shape[1]
    n_sel = bias_scr.shape[0]
    q_rel = lax.broadcasted_iota(jnp.int32, (1, QL), 1) % Q_BLOCK
    tq = t0 + q_rel

    def bias_rows(cond, rows):
        return jnp.where(cond, NEG, 0.0).astype(BF16) + jnp.zeros((rows, QL), BF16)

    def flash_update(s, carry, vt):
        m_prev, acc = carry
        m_new = jnp.maximum(m_prev, jnp.max(s, axis=0, keepdims=True))
        p = jnp.exp2((s - m_new).astype(BF16))
        return m_new, jnp.exp2(m_prev - m_new) * acc + _dot(vt, p)

    init = (jnp.full((1, QL), NEG, F32), jnp.zeros((V_AUG_ROWS, QL), F32))

    grp = lax.broadcasted_iota(jnp.int32, (qc_scr.shape[0] - HEAD_DIM, 1), 0)
    visible = CMP_PER_QB * (qb + 1)
    qc_scr[0:HEAD_DIM, :] = qt
    qc_scr[HEAD_DIM:, :] = bias_rows((8 * grp >= visible) & (grp < nc // 8), qc_scr.shape[0] - HEAD_DIM)
    band = CMP_PER_QB + 8
    r0 = pl.multiple_of(jnp.maximum(visible - band, 0), 8)
    band_end = CMP_STRIDE * (r0 + lax.broadcasted_iota(jnp.int32, (band, 1), 0)) + (CMP_BLOCK - 1)

    def compressed(rows):
        def run():
            sc_scr[0:rows, :] = _dot(kc_ref[0, 0:rows, :], qc_scr[...])
            sc_scr[pl.ds(r0, band), :] = jnp.where(band_end <= tq, sc_scr[pl.ds(r0, band), :], NEG)
            s_c = sc_scr[0:rows, :]
            m_c = jnp.maximum(jnp.max(s_c, axis=0, keepdims=True), 0.1 * NEG)
            e_c = jnp.exp2(s_c - m_c)
            o_aug = _dot(vct_ref[0, :, 0:rows], e_c.astype(BF16))
            inv_c = 1.0 / jnp.maximum(o_aug[HEAD_DIM:HEAD_DIM + 1], 1e-30)
            p_grp = None
            for h in range(HEADS_PER_GROUP):
                lanes = slice(h * Q_BLOCK, (h + 1) * Q_BLOCK)
                p_h = e_c[:, lanes] * inv_c[:, lanes]
                p_grp = p_h if p_grp is None else p_grp + p_h
            for c in range(Q_BLOCK // 128):
                ps_scr[c, 8:8 + rows, :] = p_grp[:, c * 128:(c + 1) * 128]
                if rows < nc:
                    ps_scr[c, 8 + rows:, :] = jnp.zeros((nc - rows, 128), F32)
            return o_aug[:HEAD_DIM] * inv_c
        return run

    ps_scr[:, 0:8, :] = jnp.zeros((Q_BLOCK // 128, 8, 128), F32)
    n_part = 8 if nc // 8 >= band else 4
    part = nc // n_part

    def choose(lo, hi):
        if lo == hi:
            return compressed(lo * part)
        mid = (lo + hi) // 2
        return lambda: lax.cond(visible <= mid * part, choose(lo, mid), choose(mid + 1, hi))

    o_c = choose(1, n_part)()
    imp = jnp.concatenate([sum(ps_scr[c, pl.ds(k, n_sel, stride=4), :] for k in range(7, 12))
                           for c in range(Q_BLOCK // 128)], axis=1)

    n_tri = Q_BLOCK // WIN_CHUNK
    n_chunk = WINDOW // WIN_CHUNK + n_tri
    span = n_chunk * WIN_CHUNK
    rho = lax.broadcasted_iota(jnp.int32, (16, 1), 0)
    first_real = WINDOW // WIN_CHUNK - n_tri * qb
    qw_scr[0:HEAD_DIM, :] = qt
    qw_scr[HEAD_DIM:HEAD_DIM + 16, :] = bias_rows((rho < 8) & (((rho - n_tri * qb) & 7) < first_real), 16)
    qw_scr[HEAD_DIM + 16:, :] = jnp.zeros((qw_scr.shape[0] - HEAD_DIM - 16, QL), BF16)
    s_w = _dot(kw_ref[0, pl.ds(t0a, span), :], qw_scr[...])
    i_rel = lax.broadcasted_iota(jnp.int32, (Q_BLOCK, 1), 0)
    s_w = jnp.concatenate([jnp.where(i_rel > q_rel, s_w[:Q_BLOCK], NEG), s_w[Q_BLOCK:WINDOW],
                           jnp.where(i_rel <= q_rel, s_w[WINDOW:], NEG)], axis=0)
    _, acc_w = flash_update(s_w, init, vwt_ref[0, :, pl.ds(t0a, span)])

    s0_scr[...] = _dot(ks_ref[0, 0:SEL_TILE, 0:HEAD_DIM], qt)

    j_idx = lax.broadcasted_iota(jnp.int32, (n_sel, Q_BLOCK), 0)
    cur = (t0 + lax.broadcasted_iota(jnp.int32, (n_sel, Q_BLOCK), 1)) // SEL_BLOCK
    forced = (j_idx == 0) | (j_idx == cur) | (j_idx == cur - 1)
    valid = j_idx <= cur
    score = jnp.where(forced, -2.0, jnp.where(valid, imp, -1.0))
    n_pick = SEL_TOPN - 3

    fast = score
    for _ in range(n_pick):
        fast = jnp.where(fast == jnp.max(fast, axis=0, keepdims=True), -2.0, fast)
    retired = (fast < -1.5) & valid & jnp.logical_not(forced)
    most_retired = jnp.max(jnp.sum(retired.astype(F32), axis=0, keepdims=True))

    def ranked_with_ties():
        def pick_one(_, sc):
            best = jnp.max(sc, axis=0, keepdims=True)
            first = jnp.min(jnp.where(sc == best, j_idx, n_sel), axis=0, keepdims=True)
            return jnp.where(j_idx == first, -2.0, sc)
        return lax.fori_loop(0, n_pick, pick_one, score)

    ranked = lax.cond(most_retired > n_pick, ranked_with_ties, lambda: fast)
    chosen = valid & (ranked < -1.5)
    diag_bias_scr[...] = jnp.where(chosen, 0.0, NEG)
    bias_scr[...] = jnp.where(chosen & (j_idx * SEL_BLOCK < t0), 0.0, NEG)

    blocks_per_tile = SEL_TILE // SEL_BLOCK
    last_tile = ks_ref.shape[1] // SEL_TILE - 1
    qs_scr[0:HEAD_DIM, :] = qt
    qs_scr[HEAD_DIM + 16:, :] = jnp.zeros((qs_scr.shape[0] - HEAD_DIM - 16, QL), BF16)

    def tile_start(kt):
        return pl.multiple_of(jnp.clip(kt, 0, last_tile) * SEL_TILE, SEL_TILE)

    def set_bias_rows(table, kt):
        b0 = pl.multiple_of(jnp.minimum(kt, last_tile) * blocks_per_tile, blocks_per_tile)
        b16 = jnp.concatenate([table[pl.ds(b0, blocks_per_tile), :],
                               jnp.zeros((16 - blocks_per_tile, Q_BLOCK), F32)], axis=0)
        qs_scr[HEAD_DIM:HEAD_DIM + 16, :] = jnp.concatenate([b16] * HEADS_PER_GROUP, axis=1).astype(BF16)

    set_bias_rows(diag_bias_scr, t0 // SEL_TILE)
    s_d = _dot(ks_ref[0, pl.ds(t0a, Q_BLOCK), :], qs_scr[...])
    m_d, acc_d = flash_update(jnp.where(i_rel <= q_rel, s_d, NEG), init, vst_ref[0, :, pl.ds(t0a, Q_BLOCK)])

    def half_step(kt, s_cur, s_nxt, p_cur, p_prev, carry):
        m_prev, alpha_prev, acc = carry
        acc = alpha_prev * acc + _dot(vst_ref[0, :, pl.ds(tile_start(kt - 1), SEL_TILE)], p_prev[...])
        m8 = jnp.max(s_cur[...].reshape(SEL_TILE // 8, 8, QL), axis=0)
        m_new = jnp.maximum(m_prev, jnp.max(m8, axis=0, keepdims=True))
        set_bias_rows(bias_scr, kt + 1)
        k_nxt = tile_start(kt + 1)
        for h in range(SEL_TILE // QK_PART):
            for c in range(QK_PART // P_CHUNK):
                rows = slice(h * QK_PART + c * P_CHUNK, h * QK_PART + (c + 1) * P_CHUNK)
                p_cur[rows, :] = jnp.exp2((s_cur[rows, :] - m_new).astype(BF16))
            part = slice(h * QK_PART, (h + 1) * QK_PART)
            s_nxt[part, :] = _dot(ks_ref[0, pl.ds(k_nxt + h * QK_PART, QK_PART), :], qs_scr[...])
        return m_new, jnp.exp2(m_prev - m_new), acc

    def pair_step(i, carry):
        carry = half_step(2 * i, s0_scr, s1_scr, p0_scr, p1_scr, carry)
        return half_step(2 * i + 1, s1_scr, s0_scr, p1_scr, p0_scr, carry)

    n_pairs = ((t0 + SEL_TILE - 1) // SEL_TILE + 1) // 2
    for b in range(blocks_per_tile):
        rows = slice(b * SEL_BLOCK, (b + 1) * SEL_BLOCK)
        s0_scr[rows, :] = s0_scr[rows, :] + jnp.concatenate([bias_scr[b:b + 1, :]] * HEADS_PER_GROUP, axis=1)
    p1_scr[...] = jnp.zeros(p1_scr.shape, BF16)
    carry = (m_d, jnp.ones((1, QL), F32), acc_d)
    _, alpha_last, acc_s = lax.fori_loop(0, n_pairs, pair_step, carry)
    acc_s = alpha_last * acc_s + _dot(vst_ref[0, :, pl.ds(tile_start(2 * n_pairs - 1), SEL_TILE)], p1_scr[...])

    gate = jax.nn.sigmoid(gl_ref[0, 0])
    o = (gate[0:1] * o_c + gate[1:2] * (acc_s[:HEAD_DIM] / acc_s[HEAD_DIM:HEAD_DIM + 1])
         + gate[2:3] * (acc_w[:HEAD_DIM] / acc_w[HEAD_DIM:HEAD_DIM + 1]))
    stacked = jnp.concatenate([o[:, p * Q_BLOCK:(p + 1) * Q_BLOCK] for p in range(HEADS_PER_GROUP)], axis=0)
    o_ref[...] = stacked.T.astype(o_ref.dtype)


def _sparse_attention(qt, kc_aug, vc_aug_t, ks_aug, vs_aug_t, kw_aug, vw_aug_t, gate_logits, s):
    n_group, n_qb = qt.shape[:2]
    nc = kc_aug.shape[1]
    n_sel = s // SEL_BLOCK
    assert HEAD_DIM + nc // 8 <= CMP_AUG_W and nc // 4 >= CMP_PER_QB + 8 and WINDOW // WIN_CHUNK + Q_BLOCK // WIN_CHUNK <= 8
    per_step = lambda g, i: (g, i, 0, 0)
    per_group = lambda shape: pl.BlockSpec((1,) + shape, lambda g, i: (g, 0, 0), pipeline_mode=pl.Buffered(1))
    tile_f32 = pltpu.VMEM((SEL_TILE, QL), F32)
    tile_bf16 = pltpu.VMEM((SEL_TILE, QL), BF16)
    sel_table = pltpu.VMEM((n_sel, Q_BLOCK), F32)
    return pl.pallas_call(
        _nsa_kernel,
        grid=(n_group, n_qb),
        in_specs=[
            pl.BlockSpec((1, 1, HEAD_DIM, QL), per_step),
            per_group((nc, CMP_AUG_W)),
            per_group((V_AUG_ROWS, nc)),
            per_group((s, KS_AUG_W)),
            per_group((V_AUG_ROWS, s)),
            per_group((s + WINDOW, KS_AUG_W)),
            per_group((V_AUG_ROWS, s + WINDOW)),
            pl.BlockSpec((1, 1, 3, QL), per_step),
        ],
        out_specs=pl.BlockSpec((Q_BLOCK, HEADS_PER_GROUP * HEAD_DIM), lambda g, i: (i, g)),
        out_shape=jax.ShapeDtypeStruct((s, Q_W), BF16),
        scratch_shapes=[pltpu.VMEM((Q_BLOCK // 128, 8 + nc, 128), F32), sel_table, sel_table,
                        pltpu.VMEM((CMP_AUG_W, QL), BF16), pltpu.VMEM((KS_AUG_W, QL), BF16),
                        pltpu.VMEM((KS_AUG_W, QL), BF16), pltpu.VMEM((nc, QL), F32),
                        tile_f32, tile_f32, tile_bf16, tile_bf16],
        compiler_params=_cparams("parallel", "arbitrary"),
        name="sparse_attn",
    )(qt, kc_aug, vc_aug_t, ks_aug, vs_aug_t, kw_aug, vw_aug_t, gate_logits)


def _merge_kernel(x_ref, ya_ref, yb_ref, yc_ref, g0_ref, g1_ref, g2_ref, wp_ref, wg_ref, wn_ref,
                  wo_ref, ng_ref, gate_ref, o_ref):
    merged = jax.nn.sigmoid(g0_ref[...].astype(F32)) * _dot(ya_ref[...], wp_ref[0])
    merged += jax.nn.sigmoid(g1_ref[...].astype(F32)) * _dot(yb_ref[...], wg_ref[0])
    merged += jax.nn.sigmoid(g2_ref[...].astype(F32)) * _dot(yc_ref[...], wn_ref[0])
    y = _dot(merged.astype(BF16), wo_ref[0])
    o_ref[...] = x_ref[...] + gate_ref[...] * _rms(y, ng_ref[...])


def _merge(x, ya, yb, yc, proj, wp, wg, wn, wo, layer, norm_g, gate):
    s, d = x.shape
    tm = min(256, s)
    const = lambda i: (0, 0)
    rows = lambda i: (i, 0)
    whole = lambda a: pl.BlockSpec((1,) + a.shape[1:], lambda i: (layer, 0, 0), pipeline_mode=pl.Buffered(1))
    bg = OFF_BG // d
    return pl.pallas_call(
        _merge_kernel,
        grid=(s // tm,),
        in_specs=[
            pl.BlockSpec((tm, d), rows),
            pl.BlockSpec((tm, MIX_W), rows),
            pl.BlockSpec((tm, MIX_W), rows),
            pl.BlockSpec((tm, Q_W), rows),
            pl.BlockSpec((tm, d), lambda i: (i, bg)),
            pl.BlockSpec((tm, d), lambda i: (i, bg + 1)),
            pl.BlockSpec((tm, d), lambda i: (i, bg + 2)),
            whole(wp), whole(wg), whole(wn), whole(wo),
            pl.BlockSpec((1, d), const),
            pl.BlockSpec((1, d), const),
        ],
        out_specs=pl.BlockSpec((tm, d), rows),
        out_shape=jax.ShapeDtypeStruct((s, d), F32),
        compiler_params=_cparams("parallel"),
        name="merge_out",
    )(x, ya, yb, yc, proj, proj, proj, wp, wg, wn, wo, norm_g, gate)


def _ffn_kernel(x_ref, gi_ref, sc_ref, sh_ref, w1_ref, w2_ref, go_ref, gate_ref, o_ref, h_scr):
    j = pl.program_id(1)

    @pl.when(j == 0)
    def _():
        h = _rms(x_ref[...], gi_ref[...]) * (1.0 + sc_ref[...]) + sh_ref[...]
        h_scr[...] = h.astype(BF16)
        o_ref[...] = jnp.zeros_like(o_ref)

    a = jnp.square(jnp.maximum(_dot(h_scr[...], w1_ref[0]), 0.0))
    o_ref[...] += _dot(a.astype(BF16), w2_ref[0])

    @pl.when(j == pl.num_programs(1) - 1)
    def _():
        o_ref[...] = x_ref[...] + gate_ref[...] * _rms(o_ref[...], go_ref[...])


def _ffn(x, g_in, scale, shift, w1, w2, layer, g_out, gate):
    s, d = x.shape
    tm, tf = min(1024, s), 1024
    const = lambda i, j: (0, 0)
    rows = lambda i, j: (i, 0)
    return pl.pallas_call(
        _ffn_kernel,
        grid=(s // tm, D_FF // tf),
        in_specs=[
            pl.BlockSpec((tm, d), rows, pipeline_mode=pl.Buffered(1)),
            pl.BlockSpec((1, d), const),
            pl.BlockSpec((1, d), const),
            pl.BlockSpec((1, d), const),
            pl.BlockSpec((1, d, tf), lambda i, j: (layer, 0, j)),
            pl.BlockSpec((1, tf, d), lambda i, j: (layer, j, 0)),
            pl.BlockSpec((1, d), const),
            pl.BlockSpec((1, d), const),
        ],
        out_specs=pl.BlockSpec((tm, d), rows),
        out_shape=jax.ShapeDtypeStruct((s, d), F32),
        scratch_shapes=[pltpu.VMEM((tm, d), BF16)],
        compiler_params=_cparams("parallel", "arbitrary"),
        name="ffn",
    )(x, g_in, scale, shift, w1, w2, g_out, gate)


def _token_mixing(x, mod, norm_g, w_in, pool_w, pool_scale, ln_g, ln_b, ws, bs, cmp_pos, cmp_w1,
                  cmp_b1, cmp_w2, cmp_b2, branch_weights, layer):
    s, d = x.shape
    ng0 = OFF_BG
    w_main = jnp.concatenate([w_in[:, :ng0], w_in[:, ng0 + N_GATE:]], axis=1).astype(BF16)
    w_ng = jnp.pad(w_in[:, ng0:ng0 + N_GATE], ((0, 0), (0, 128 - N_GATE))).astype(BF16)
    row = lambda v: v.reshape(1, -1)

    col_scale = jnp.ones((1, PROJ_W), F32).at[:, OFF_Q:OFF_KV].set(HEAD_DIM ** -0.5 * LOG2E)
    proj, ngate = _in_projection(x, row(norm_g[0]), row(mod[1]), row(mod[0]), w_main, w_ng, col_scale)

    ya, yb = _mixers(proj, pool_w.astype(BF16), row(pool_scale), row(ln_g), row(ln_b), ws, bs.T)

    qt, ks_aug, vs_aug_t, kw_aug, vw_aug_t, gl = _attention_operands(proj, ngate)
    kv_cmp = proj[:, OFF_KV:OFF_KV + 2 * KV_W].reshape(s, 2, KV_GROUPS, HEAD_DIM).transpose(1, 2, 0, 3)
    xkv = kv_cmp.reshape(2, KV_GROUPS, s // CMP_STRIDE, CMP_STRIDE * HEAD_DIM)
    lane_pad = ((0, 0), (0, 0), (0, 128 - HEAD_DIM))
    kc_aug, vc_aug_t = _compress(xkv, cmp_pos.reshape(2, 1, CMP_BLOCK * HEAD_DIM), cmp_w1.astype(BF16),
                                 cmp_b1.reshape(2, 1, -1), jnp.pad(cmp_w2, lane_pad).astype(BF16),
                                 jnp.pad(cmp_b2.reshape(2, 1, -1), lane_pad))
    yc = _sparse_attention(qt, kc_aug, vc_aug_t, ks_aug, vs_aug_t, kw_aug, vw_aug_t, gl, s)

    return _merge(x, ya, yb, yc, proj, *branch_weights, layer, row(norm_g[1]), row(mod[2]))


def kernel(x, c, norm_g, w_ada, b_ada, w_in, pool_w, pool_scale, gmlp_ln_g, gmlp_ln_b, gmlp_ws, gmlp_bs,
           cmp_pos, cmp_w1, cmp_b1, cmp_w2, cmp_b2, w_br_pool, w_br_gmlp, w_br_nsa, w_out, w_ff1, w_ff2):
    b, s, d = x.shape
    assert b == 1 and d == D_MODEL and s % 1024 == 0
    n_layer = w_ada.shape[0]
    mod_all = _modulation(c, w_ada, b_ada).reshape(n_layer, 6, d)
    xs = x[0]
    row = lambda v: v.reshape(1, -1)
    branch_weights = tuple(w.astype(BF16) for w in (w_br_pool, w_br_gmlp, w_br_nsa, w_out))
    w_ff1_b, w_ff2_b = w_ff1.astype(BF16), w_ff2.astype(BF16)
    for l in range(n_layer):
        mod = mod_all[l]
        xs = _token_mixing(xs, mod, norm_g[l], w_in[l], pool_w[l], pool_scale[l], gmlp_ln_g[l],
                           gmlp_ln_b[l], gmlp_ws[l], gmlp_bs[l], cmp_pos[l], cmp_w1[l], cmp_b1[l],
                           cmp_w2[l], cmp_b2[l], branch_weights, l)
        xs = _ffn(xs, row(norm_g[l, 2]), row(mod[4]), row(mod[3]), w_ff1_b, w_ff2_b, l,
                  row(norm_g[l, 3]), row(mod[5]))
    return xs[None]
```

```python
import jax
import jax.numpy as jnp
from jax import lax
from jax.experimental import pallas as pl
from jax.experimental.pallas import tpu as pltpu

F32 = jnp.float32
BF16 = jnp.bfloat16

D_MODEL = 2048
POOL_WINDOWS = (2, 4, 8, 16)
POOL_HALO = 16
GROUP_CH = 128
MIX_W = 4 * GROUP_CH
GMLP_CHUNK = 128
HEAD_DIM = 64
KV_GROUPS = 4
HEADS_PER_GROUP = 4
Q_W = 16 * HEAD_DIM
KV_W = KV_GROUPS * HEAD_DIM
CMP_BLOCK = 32
CMP_STRIDE = 16
SEL_BLOCK = 64
SEL_TOPN = 16
WINDOW = 512
FORCE_BONUS = 1000.0
LOG2E = 1.4426950408889634
NEG = -1e30
N_GATE = 3 * 16
D_FF = 4 * D_MODEL

Q_BLOCK = 256
QL = HEADS_PER_GROUP * Q_BLOCK
CMP_PER_QB = Q_BLOCK // CMP_STRIDE
WIN_CHUNK = 128
PAD_TILES = WINDOW // Q_BLOCK
SEL_TILE = 512
KS_AUG_W = 128
CMP_AUG_W = 256
V_AUG_ROWS = HEAD_DIM + 16
P_CHUNK = 64
QK_PART = 128

OFF_POOL, OFF_U, OFF_V, OFF_Q, OFF_KV, OFF_BG = 0, 512, 1024, 1536, 2560, 4096
PROJ_W = OFF_BG + 3 * D_MODEL

VMEM_LIMIT = 60 * 1024 * 1024


def _cparams(*sem):
    return pltpu.CompilerParams(dimension_semantics=sem, vmem_limit_bytes=VMEM_LIMIT)


def _dot(a, b):
    return jnp.dot(a, b, preferred_element_type=F32)


def _rms(x, g):
    return x * lax.rsqrt(jnp.mean(x * x, axis=-1, keepdims=True) + 1e-6) * g


def _mod_kernel(c_ref, w_ref, b_ref, o_ref):
    c = c_ref[...]
    act = c * jax.nn.sigmoid(c)
    o_ref[0] = jnp.sum(act * w_ref[0], axis=0, keepdims=True) + b_ref[0]


def _modulation(c, w_ada, b_ada):
    n_layer, d, n_out = w_ada.shape
    tn = 1024
    return pl.pallas_call(
        _mod_kernel,
        grid=(n_layer, n_out // tn),
        in_specs=[
            pl.BlockSpec((d, 1), lambda l, j: (0, 0)),
            pl.BlockSpec((1, d, tn), lambda l, j: (l, 0, j)),
            pl.BlockSpec((1, 1, tn), lambda l, j: (l, 0, j)),
        ],
        out_specs=pl.BlockSpec((1, 1, tn), lambda l, j: (l, 0, j)),
        out_shape=jax.ShapeDtypeStruct((n_layer, 1, n_out), F32),
        compiler_params=_cparams("parallel", "parallel"),
        name="adaln_mod",
    )(c.reshape(d, 1), w_ada, b_ada.reshape(n_layer, 1, n_out))


def _inproj_kernel(x_ref, g_ref, sc_ref, sh_ref, w_ref, wng_ref, cs_ref, o_ref, ng_ref, h_scr):
    @pl.when(pl.program_id(1) == 0)
    def _():
        h = _rms(x_ref[...], g_ref[...]) * (1.0 + sc_ref[...]) + sh_ref[...]
        hb = h.astype(BF16)
        h_scr[...] = hb
        ng_ref[...] = _dot(hb, wng_ref[...])

    o_ref[...] = (_dot(h_scr[...], w_ref[...]) * cs_ref[...]).astype(o_ref.dtype)


def _in_projection(x, g, scale, shift, w_main, w_ng, col_scale):
    s, d = x.shape
    tm, tn = min(1024, s), 2048
    row = lambda i, j: (0, 0)
    return pl.pallas_call(
        _inproj_kernel,
        grid=(s // tm, PROJ_W // tn),
        in_specs=[
            pl.BlockSpec((tm, d), lambda i, j: (i, 0)),
            pl.BlockSpec((1, d), row),
            pl.BlockSpec((1, d), row),
            pl.BlockSpec((1, d), row),
            pl.BlockSpec((d, tn), lambda i, j: (0, j)),
            pl.BlockSpec((d, 128), row),
            pl.BlockSpec((1, tn), lambda i, j: (0, j)),
        ],
        out_specs=[
            pl.BlockSpec((tm, tn), lambda i, j: (i, j)),
            pl.BlockSpec((tm, 128), lambda i, j: (i, 0)),
        ],
        out_shape=[
            jax.ShapeDtypeStruct((s, PROJ_W), BF16),
            jax.ShapeDtypeStruct((s, 128), F32),
        ],
        scratch_shapes=[pltpu.VMEM((tm, d), BF16)],
        compiler_params=_cparams("parallel", "arbitrary"),
        name="in_proj",
    )(x, g, scale, shift, w_main, w_ng, col_scale)


def _mixer_kernel(a_ref, halo_ref, u_ref, v_ref, pw_ref, ps_ref, lg_ref, lb_ref, ws_ref, bs_ref,
                  ya_ref, yb_ref):
    i = pl.program_id(0)
    tm = a_ref.shape[0]
    a = a_ref[...].astype(F32)
    halo = jnp.where(i > 0, halo_ref[...].astype(F32), 0.0)
    ext = jnp.concatenate([halo, a], axis=0)
    p2 = ext[1:] + ext[:-1]
    p4 = p2[2:] + p2[:-2]
    p8 = p4[4:] + p4[:-4]
    p16 = p8[8:] + p8[:-8]
    sums = (p2[15:15 + tm], p4[13:13 + tm], p8[9:9 + tm], p16[1:1 + tm])
    t = i * tm + lax.broadcasted_iota(jnp.int32, (tm, 1), 0)
    for gi, w in enumerate(POOL_WINDOWS):
        cols = slice(gi * GROUP_CH, (gi + 1) * GROUP_CH)
        cnt = jnp.minimum(t + 1, w).astype(F32)
        pooled = sums[gi][:, cols] / cnt - a[:, cols]
        y = _dot(pooled.astype(BF16), pw_ref[gi])
        ya_ref[:, cols] = (y * ps_ref[:, cols]).astype(ya_ref.dtype)

    u = jax.nn.gelu(u_ref[...].astype(F32))
    v = jax.nn.gelu(v_ref[...].astype(F32))
    mu = jnp.mean(v, axis=-1, keepdims=True)
    var = jnp.mean(jnp.square(v - mu), axis=-1, keepdims=True)
    vn = ((v - mu) * lax.rsqrt(var + 1e-5) * lg_ref[...] + lb_ref[...]).astype(BF16)
    r = lax.broadcasted_iota(jnp.int32, (GMLP_CHUNK, GMLP_CHUNK), 0)
    c = lax.broadcasted_iota(jnp.int32, (GMLP_CHUNK, GMLP_CHUNK), 1)
    for gi in range(4):
        cols = slice(gi * GROUP_CH, (gi + 1) * GROUP_CH)
        wsm = jnp.where(r >= c, ws_ref[gi], 0.0).astype(BF16)
        bias = bs_ref[:, gi:gi + 1]
        for ck in range(tm // GMLP_CHUNK):
            rows = slice(ck * GMLP_CHUNK, (ck + 1) * GMLP_CHUNK)
            mixed = _dot(wsm, vn[rows, cols]) + bias
            yb_ref[rows, cols] = (u[rows, cols] * mixed).astype(yb_ref.dtype)


def _mixers(proj, pool_w, pool_scale, ln_g, ln_b, ws, bs_t):
    s = proj.shape[0]
    tm = min(512, s)
    hb = tm // POOL_HALO
    const2 = lambda i: (0, 0)
    const3 = lambda i: (0, 0, 0)
    return pl.pallas_call(
        _mixer_kernel,
        grid=(s // tm,),
        in_specs=[
            pl.BlockSpec((tm, MIX_W), lambda i: (i, OFF_POOL // MIX_W)),
            pl.BlockSpec((POOL_HALO, MIX_W), lambda i: (jnp.maximum(i * hb - 1, 0), OFF_POOL // MIX_W)),
            pl.BlockSpec((tm, MIX_W), lambda i: (i, OFF_U // MIX_W)),
            pl.BlockSpec((tm, MIX_W), lambda i: (i, OFF_V // MIX_W)),
            pl.BlockSpec((4, GROUP_CH, GROUP_CH), const3),
            pl.BlockSpec((1, MIX_W), const2),
            pl.BlockSpec((1, MIX_W), const2),
            pl.BlockSpec((1, MIX_W), const2),
            pl.BlockSpec((4, GMLP_CHUNK, GMLP_CHUNK), const3),
            pl.BlockSpec((GMLP_CHUNK, 4), const2),
        ],
        out_specs=[
            pl.BlockSpec((tm, MIX_W), lambda i: (i, 0)),
            pl.BlockSpec((tm, MIX_W), lambda i: (i, 0)),
        ],
        out_shape=[jax.ShapeDtypeStruct((s, MIX_W), BF16)] * 2,
        compiler_params=_cparams("parallel"),
        name="mixers",
    )(proj, proj, proj, proj, pool_w, pool_scale, ln_g, ln_b, ws, bs_t)


def _compress_kernel(x_ref, pos_ref, w1_ref, b1_ref, w2_ref, b2_ref, kc_ref, vct_ref):
    half = CMP_STRIDE * HEAD_DIM
    x = x_ref[0, 0].astype(F32)
    nc = x.shape[0]
    pos = pos_ref[0]
    first = _dot((x + pos[:, :half]).astype(BF16), w1_ref[0, :half, :])
    second = _dot((x + pos[:, half:]).astype(BF16), w1_ref[0, half:, :])
    hid = jax.nn.gelu(first + pltpu.roll(second, nc - 1, 0) + b1_ref[0])
    out = _dot(hid.astype(BF16), w2_ref[0]) + b2_ref[0]

    @pl.when(pl.program_id(1) == 0)
    def _():
        lane = lax.broadcasted_iota(jnp.int32, (nc, 128), 1)
        grp = lax.broadcasted_iota(jnp.int32, (nc, 128), 0) // 8
        kc_ref[0, :, 0:128] = (out + (lane - HEAD_DIM == grp).astype(F32)).astype(BF16)
        kc_ref[0, :, 128:] = (lane + (128 - HEAD_DIM) == grp).astype(BF16)

    @pl.when(pl.program_id(1) == 1)
    def _():
        ones_row = lax.broadcasted_iota(jnp.int32, (V_AUG_ROWS, nc), 0) == HEAD_DIM
        vct_ref[0] = (out.T[0:V_AUG_ROWS] + ones_row.astype(F32)).astype(BF16)


def _compress(xkv, pos, w1, b1, w2, b2):
    _, n_group, nc, width = xkv.shape
    assert CMP_AUG_W == 256 and HEAD_DIM + nc // 8 <= CMP_AUG_W
    per_kv = lambda g, a: (a, 0, 0)
    per_group = lambda g, a: (g, 0, 0)
    return pl.pallas_call(
        _compress_kernel,
        grid=(n_group, 2),
        in_specs=[
            pl.BlockSpec((1, 1, nc, width), lambda g, a: (a, g, 0, 0)),
            pl.BlockSpec((1, 1, 2 * width), per_kv),
            pl.BlockSpec((1, 2 * width, 128), per_kv),
            pl.BlockSpec((1, 1, 128), per_kv),
            pl.BlockSpec((1, 128, 128), per_kv),
            pl.BlockSpec((1, 1, 128), per_kv),
        ],
        out_specs=[pl.BlockSpec((1, nc, CMP_AUG_W), per_group), pl.BlockSpec((1, V_AUG_ROWS, nc), per_group)],
        out_shape=[jax.ShapeDtypeStruct((n_group, nc, CMP_AUG_W), BF16),
                   jax.ShapeDtypeStruct((n_group, V_AUG_ROWS, nc), BF16)],
        compiler_params=_cparams("parallel", "arbitrary"),
        name="compress_kv",
    )(xkv, pos, w1, b1, w2, b2)


def _operand_kernel(q_lo_ref, q_hi_ref, sel_ref, win_ref, ng_ref, qt_ref, ks_ref, vst_ref, kw_ref, vwt_ref, gl_ref):
    i = pl.program_id(0)
    n_tiles = pl.num_programs(0) - PAD_TILES
    rows = q_lo_ref.shape[0]
    lane = lax.broadcasted_iota(jnp.int32, (rows, 128), 1)
    pos = i * rows + lax.broadcasted_iota(jnp.int32, (rows, 128), 0)
    ones_rows = (lax.broadcasted_iota(jnp.int32, (V_AUG_ROWS - HEAD_DIM, rows), 0) == 0).astype(BF16)

    def key_pair(slab, idx):
        onehot = (lane - HEAD_DIM == idx).astype(F32)
        left = jnp.where(lane < HEAD_DIM, slab, onehot)
        right = jnp.where(lane < HEAD_DIM, pltpu.roll(slab, HEAD_DIM, 1), onehot)
        return left.astype(BF16), right.astype(BF16)

    def transposed_pair(slab):
        t = slab.T.astype(BF16)
        return t[:HEAD_DIM], t[HEAD_DIM:]

    def write_kv(src, keep, idx, k_ref, vt_ref):
        for pair in range(KV_GROUPS // 2):
            lanes = slice(pair * 128, (pair + 1) * 128)
            k_slab = jnp.where(keep, src[:, lanes].astype(F32), 0.0)
            v_slab = jnp.where(keep, src[:, KV_W + pair * 128:KV_W + (pair + 1) * 128].astype(F32), 0.0)
            for g, k, vt in zip((2 * pair, 2 * pair + 1), key_pair(k_slab, idx), transposed_pair(v_slab)):
                k_ref[g] = k
                vt_ref[g, 0:HEAD_DIM, :] = vt
                vt_ref[g, HEAD_DIM:, :] = ones_rows

    @pl.when(i < n_tiles)
    def _():
        for half, ref in enumerate((q_lo_ref, q_hi_ref)):
            for gg in range(2):
                for pp in range(HEADS_PER_GROUP // 2):
                    lanes = slice(gg * 256 + pp * 128, gg * 256 + (pp + 1) * 128)
                    for h, t in zip((2 * pp, 2 * pp + 1), transposed_pair(ref[:, lanes].astype(F32))):
                        qt_ref[2 * half + gg, 0, :, h * Q_BLOCK:(h + 1) * Q_BLOCK] = t
        write_kv(sel_ref, True, (pos // SEL_BLOCK) % (SEL_TILE // SEL_BLOCK), ks_ref, vst_ref)
        logits_t = ng_ref[...].T
        for g in range(KV_GROUPS):
            for h in range(HEADS_PER_GROUP):
                for br in range(3):
                    r = (g * HEADS_PER_GROUP + h) * 3 + br
                    gl_ref[g, 0, br:br + 1, h * Q_BLOCK:(h + 1) * Q_BLOCK] = logits_t[r:r + 1, :]

    write_kv(win_ref, i >= PAD_TILES, (pos // WIN_CHUNK) % 8, kw_ref, vwt_ref)


def _attention_operands(proj, ngate):
    s = proj.shape[0]
    n_tiles = s // Q_BLOCK
    blk = MIX_W
    real = lambda i: jnp.minimum(i, n_tiles - 1)
    src = lambda c: pl.BlockSpec((Q_BLOCK, blk), lambda i: (real(i), c))
    return pl.pallas_call(
        _operand_kernel,
        grid=(n_tiles + PAD_TILES,),
        in_specs=[src(OFF_Q // blk), src(OFF_Q // blk + 1), src(OFF_KV // blk + 1),
                  pl.BlockSpec((Q_BLOCK, blk), lambda i: (jnp.maximum(i - PAD_TILES, 0), OFF_KV // blk + 2)),
                  pl.BlockSpec((Q_BLOCK, 128), lambda i: (real(i), 0))],
        out_specs=[
            pl.BlockSpec((KV_GROUPS, 1, HEAD_DIM, QL), lambda i: (0, real(i), 0, 0)),
            pl.BlockSpec((KV_GROUPS, Q_BLOCK, KS_AUG_W), lambda i: (0, real(i), 0)),
            pl.BlockSpec((KV_GROUPS, V_AUG_ROWS, Q_BLOCK), lambda i: (0, 0, real(i))),
            pl.BlockSpec((KV_GROUPS, Q_BLOCK, KS_AUG_W), lambda i: (0, i, 0)),
            pl.BlockSpec((KV_GROUPS, V_AUG_ROWS, Q_BLOCK), lambda i: (0, 0, i)),
            pl.BlockSpec((KV_GROUPS, 1, 3, QL), lambda i: (0, real(i), 0, 0)),
        ],
        out_shape=[
            jax.ShapeDtypeStruct((KV_GROUPS, n_tiles, HEAD_DIM, QL), BF16),
            jax.ShapeDtypeStruct((KV_GROUPS, s, KS_AUG_W), BF16),
            jax.ShapeDtypeStruct((KV_GROUPS, V_AUG_ROWS, s), BF16),
            jax.ShapeDtypeStruct((KV_GROUPS, s + WINDOW, KS_AUG_W), BF16),
            jax.ShapeDtypeStruct((KV_GROUPS, V_AUG_ROWS, s + WINDOW), BF16),
            jax.ShapeDtypeStruct((KV_GROUPS, n_tiles, 3, QL), F32),
        ],
        compiler_params=_cparams("arbitrary"),
        name="attn_operands",
    )(proj, proj, proj, proj, ngate)


def _nsa_kernel(qt_ref, kc_ref, vct_ref, ks_ref, vst_ref, kw_ref, vwt_ref, gl_ref, o_ref,
                ps_scr, bias_scr, diag_bias_scr, qc_scr, qw_scr, qs_scr, sc_scr, s0_scr, s1_scr, p0_scr, p1_scr):
    qb = pl.program_id(1)
    t0 = qb * Q_BLOCK
    t0a = pl.multiple_of(t0, Q_BLOCK)
    qt = qt_ref[0, 0]
    nc = kc_ref.shape[1]
    n_sel = bias_scr.shape[0]
    q_rel = lax.broadcasted_iota(jnp.int32, (1, QL), 1) % Q_BLOCK
    tq = t0 + q_rel

    def bias_rows(cond, rows):
        return jnp.where(cond, NEG, 0.0).astype(BF16) + jnp.zeros((rows, QL), BF16)

    def flash_update(s, carry, vt):
        m_prev, acc = carry
        m_new = jnp.maximum(m_prev, jnp.max(s, axis=0, keepdims=True))
        p = jnp.exp2((s - m_new).astype(BF16))
        return m_new, jnp.exp2(m_prev - m_new) * acc + _dot(vt, p)

    init = (jnp.full((1, QL), NEG, F32), jnp.zeros((V_AUG_ROWS, QL), F32))

    grp = lax.broadcasted_iota(jnp.int32, (qc_scr.shape[0] - HEAD_DIM, 1), 0)
    visible = CMP_PER_QB * (qb + 1)
    qc_scr[0:HEAD_DIM, :] = qt
    qc_scr[HEAD_DIM:, :] = bias_rows((8 * grp >= visible) & (grp < nc // 8), qc_scr.shape[0] - HEAD_DIM)
    band = CMP_PER_QB + 8
    r0 = pl.multiple_of(jnp.maximum(visible - band, 0), 8)
    band_end = CMP_STRIDE * (r0 + lax.broadcasted_iota(jnp.int32, (band, 1), 0)) + (CMP_BLOCK - 1)

    def compressed(rows):
        def run():
            sc_scr[0:rows, :] = _dot(kc_ref[0, 0:rows, :], qc_scr[...])
            sc_scr[pl.ds(r0, band), :] = jnp.where(band_end <= tq, sc_scr[pl.ds(r0, band), :], NEG)
            s_c = sc_scr[0:rows, :]
            m_c = jnp.maximum(jnp.max(s_c, axis=0, keepdims=True), 0.1 * NEG)
            e_c = jnp.exp2(s_c - m_c)
            o_aug = _dot(vct_ref[0, :, 0:rows], e_c.astype(BF16))
            inv_c = 1.0 / jnp.maximum(o_aug[HEAD_DIM:HEAD_DIM + 1], 1e-30)
            p_grp = None
            for h in range(HEADS_PER_GROUP):
                lanes = slice(h * Q_BLOCK, (h + 1) * Q_BLOCK)
                p_h = e_c[:, lanes] * inv_c[:, lanes]
                p_grp = p_h if p_grp is None else p_grp + p_h
            for c in range(Q_BLOCK // 128):
                ps_scr[c, 8:8 + rows, :] = p_grp[:, c * 128:(c + 1) * 128]
                if rows < nc:
                    ps_scr[c, 8 + rows:, :] = jnp.zeros((nc - rows, 128), F32)
            return o_aug[:HEAD_DIM] * inv_c
        return run

    ps_scr[:, 0:8, :] = jnp.zeros((Q_BLOCK // 128, 8, 128), F32)
    n_part = 8 if nc // 8 >= band else 4
    part = nc // n_part

    def choose(lo, hi):
        if lo == hi:
            return compressed(lo * part)
        mid = (lo + hi) // 2
        return lambda: lax.cond(visible <= mid * part, choose(lo, mid), choose(mid + 1, hi))

    o_c = choose(1, n_part)()
    imp = jnp.concatenate([sum(ps_scr[c, pl.ds(k, n_sel, stride=4), :] for k in range(7, 12))
                           for c in range(Q_BLOCK // 128)], axis=1)

    n_tri = Q_BLOCK // WIN_CHUNK
    n_chunk = WINDOW // WIN_CHUNK + n_tri
    span = n_chunk * WIN_CHUNK
    rho = lax.broadcasted_iota(jnp.int32, (16, 1), 0)
    first_real = WINDOW // WIN_CHUNK - n_tri * qb
    qw_scr[0:HEAD_DIM, :] = qt
    qw_scr[HEAD_DIM:HEAD_DIM + 16, :] = bias_rows((rho < 8) & (((rho - n_tri * qb) & 7) < first_real), 16)
    qw_scr[HEAD_DIM + 16:, :] = jnp.zeros((qw_scr.shape[0] - HEAD_DIM - 16, QL), BF16)
    s_w = _dot(kw_ref[0, pl.ds(t0a, span), :], qw_scr[...])
    i_rel = lax.broadcasted_iota(jnp.int32, (Q_BLOCK, 1), 0)
    s_w = jnp.concatenate([jnp.where(i_rel > q_rel, s_w[:Q_BLOCK], NEG), s_w[Q_BLOCK:WINDOW],
                           jnp.where(i_rel <= q_rel, s_w[WINDOW:], NEG)], axis=0)
    _, acc_w = flash_update(s_w, init, vwt_ref[0, :, pl.ds(t0a, span)])

    s0_scr[...] = _dot(ks_ref[0, 0:SEL_TILE, 0:HEAD_DIM], qt)

    j_idx = lax.broadcasted_iota(jnp.int32, (n_sel, Q_BLOCK), 0)
    cur = (t0 + lax.broadcasted_iota(jnp.int32, (n_sel, Q_BLOCK), 1)) // SEL_BLOCK
    forced = (j_idx == 0) | (j_idx == cur) | (j_idx == cur - 1)
    valid = j_idx <= cur
    score = jnp.where(forced, -2.0, jnp.where(valid, imp, -1.0))
    n_pick = SEL_TOPN - 3

    fast = score
    for _ in range(n_pick):
        fast = jnp.where(fast == jnp.max(fast, axis=0, keepdims=True), -2.0, fast)
    retired = (fast < -1.5) & valid & jnp.logical_not(forced)
    most_retired = jnp.max(jnp.sum(retired.astype(F32), axis=0, keepdims=True))

    def ranked_with_ties():
        def pick_one(_, sc):
            best = jnp.max(sc, axis=0, keepdims=True)
            first = jnp.min(jnp.where(sc == best, j_idx, n_sel), axis=0, keepdims=True)
            return jnp.where(j_idx == first, -2.0, sc)
        return lax.fori_loop(0, n_pick, pick_one, score)

    ranked = lax.cond(most_retired > n_pick, ranked_with_ties, lambda: fast)
    chosen = valid & (ranked < -1.5)
    diag_bias_scr[...] = jnp.where(chosen, 0.0, NEG)
    bias_scr[...] = jnp.where(chosen & (j_idx * SEL_BLOCK < t0), 0.0, NEG)

    blocks_per_tile = SEL_TILE // SEL_BLOCK
    last_tile = ks_ref.shape[1] // SEL_TILE - 1
    qs_scr[0:HEAD_DIM, :] = qt
    qs_scr[HEAD_DIM + 16:, :] = jnp.zeros((qs_scr.shape[0] - HEAD_DIM - 16, QL), BF16)

    def tile_start(kt):
        return pl.multiple_of(jnp.clip(kt, 0, last_tile) * SEL_TILE, SEL_TILE)

    def set_bias_rows(table, kt):
        b0 = pl.multiple_of(jnp.minimum(kt, last_tile) * blocks_per_tile, blocks_per_tile)
        b16 = jnp.concatenate([table[pl.ds(b0, blocks_per_tile), :],
                               jnp.zeros((16 - blocks_per_tile, Q_BLOCK), F32)], axis=0)
        qs_scr[HEAD_DIM:HEAD_DIM + 16, :] = jnp.concatenate([b16] * HEADS_PER_GROUP, axis=1).astype(BF16)

    set_bias_rows(diag_bias_scr, t0 // SEL_TILE)
    s_d = _dot(ks_ref[0, pl.ds(t0a, Q_BLOCK), :], qs_scr[...])
    m_d, acc_d = flash_update(jnp.where(i_rel <= q_rel, s_d, NEG), init, vst_ref[0, :, pl.ds(t0a, Q_BLOCK)])

    def half_step(kt, s_cur, s_nxt, p_cur, p_prev, carry):
        m_prev, alpha_prev, acc = carry
        k_prev = tile_start(kt - 1)

        def pv_half(h):
            keys = SEL_TILE // 2
            return _dot(vst_ref[0, :, pl.ds(k_prev + h * keys, keys)], p_prev[h * keys:(h + 1) * keys, :])

        acc = alpha_prev * acc + pv_half(0)
        m8 = jnp.max(s_cur[...].reshape(SEL_TILE // 8, 8, QL), axis=0)
        m_new = jnp.maximum(m_prev, jnp.max(m8, axis=0, keepdims=True))
        set_bias_rows(bias_scr, kt + 1)
        k_nxt = tile_start(kt + 1)
        n_part = SEL_TILE // QK_PART
        for h in range(n_part):
            for c in range(QK_PART // P_CHUNK):
                rows = slice(h * QK_PART + c * P_CHUNK, h * QK_PART + (c + 1) * P_CHUNK)
                p_cur[rows, :] = jnp.exp2((s_cur[rows, :] - m_new).astype(BF16))
            part = slice(h * QK_PART, (h + 1) * QK_PART)
            s_nxt[part, :] = _dot(ks_ref[0, pl.ds(k_nxt + h * QK_PART, QK_PART), :], qs_scr[...])
            if h == n_part // 2 - 1:
                acc = acc + pv_half(1)
        return m_new, jnp.exp2(m_prev - m_new), acc

    def pair_step(i, carry):
        carry = half_step(2 * i, s0_scr, s1_scr, p0_scr, p1_scr, carry)
        return half_step(2 * i + 1, s1_scr, s0_scr, p1_scr, p0_scr, carry)

    n_pairs = ((t0 + SEL_TILE - 1) // SEL_TILE + 1) // 2
    for b in range(blocks_per_tile):
        rows = slice(b * SEL_BLOCK, (b + 1) * SEL_BLOCK)
        s0_scr[rows, :] = s0_scr[rows, :] + jnp.concatenate([bias_scr[b:b + 1, :]] * HEADS_PER_GROUP, axis=1)
    p1_scr[...] = jnp.zeros(p1_scr.shape, BF16)
    carry = (m_d, jnp.ones((1, QL), F32), acc_d)
    _, alpha_last, acc_s = lax.fori_loop(0, n_pairs, pair_step, carry)
    acc_s = alpha_last * acc_s + _dot(vst_ref[0, :, pl.ds(tile_start(2 * n_pairs - 1), SEL_TILE)], p1_scr[...])

    gate = jax.nn.sigmoid(gl_ref[0, 0])
    o = (gate[0:1] * o_c + gate[1:2] * (acc_s[:HEAD_DIM] / acc_s[HEAD_DIM:HEAD_DIM + 1])
         + gate[2:3] * (acc_w[:HEAD_DIM] / acc_w[HEAD_DIM:HEAD_DIM + 1]))
    stacked = jnp.concatenate([o[:, p * Q_BLOCK:(p + 1) * Q_BLOCK] for p in range(HEADS_PER_GROUP)], axis=0)
    o_ref[...] = stacked.T.astype(o_ref.dtype)


def _sparse_attention(qt, kc_aug, vc_aug_t, ks_aug, vs_aug_t, kw_aug, vw_aug_t, gate_logits, s):
    n_group, n_qb = qt.shape[:2]
    nc = kc_aug.shape[1]
    n_sel = s // SEL_BLOCK
    assert HEAD_DIM + nc // 8 <= CMP_AUG_W and nc // 4 >= CMP_PER_QB + 8 and WINDOW // WIN_CHUNK + Q_BLOCK // WIN_CHUNK <= 8
    per_step = lambda g, i: (g, i, 0, 0)
    per_group = lambda shape: pl.BlockSpec((1,) + shape, lambda g, i: (g, 0, 0), pipeline_mode=pl.Buffered(1))
    tile_f32 = pltpu.VMEM((SEL_TILE, QL), F32)
    tile_bf16 = pltpu.VMEM((SEL_TILE, QL), BF16)
    sel_table = pltpu.VMEM((n_sel, Q_BLOCK), F32)
    return pl.pallas_call(
        _nsa_kernel,
        grid=(n_group, n_qb),
        in_specs=[
            pl.BlockSpec((1, 1, HEAD_DIM, QL), per_step),
            per_group((nc, CMP_AUG_W)),
            per_group((V_AUG_ROWS, nc)),
            per_group((s, KS_AUG_W)),
            per_group((V_AUG_ROWS, s)),
            per_group((s + WINDOW, KS_AUG_W)),
            per_group((V_AUG_ROWS, s + WINDOW)),
            pl.BlockSpec((1, 1, 3, QL), per_step),
        ],
        out_specs=pl.BlockSpec((Q_BLOCK, HEADS_PER_GROUP * HEAD_DIM), lambda g, i: (i, g)),
        out_shape=jax.ShapeDtypeStruct((s, Q_W), BF16),
        scratch_shapes=[pltpu.VMEM((Q_BLOCK // 128, 8 + nc, 128), F32), sel_table, sel_table,
                        pltpu.VMEM((CMP_AUG_W, QL), BF16), pltpu.VMEM((KS_AUG_W, QL), BF16),
                        pltpu.VMEM((KS_AUG_W, QL), BF16), pltpu.VMEM((nc, QL), F32),
                        tile_f32, tile_f32, tile_bf16, tile_bf16],
        compiler_params=_cparams("parallel", "arbitrary"),
        name="sparse_attn",
    )(qt, kc_aug, vc_aug_t, ks_aug, vs_aug_t, kw_aug, vw_aug_t, gate_logits)


def _merge_kernel(x_ref, ya_ref, yb_ref, yc_ref, g0_ref, g1_ref, g2_ref, wp_ref, wg_ref, wn_ref,
                  wo_ref, ng_ref, gate_ref, o_ref):
    merged = jax.nn.sigmoid(g0_ref[...].astype(F32)) * _dot(ya_ref[...], wp_ref[0])
    merged += jax.nn.sigmoid(g1_ref[...].astype(F32)) * _dot(yb_ref[...], wg_ref[0])
    merged += jax.nn.sigmoid(g2_ref[...].astype(F32)) * _dot(yc_ref[...], wn_ref[0])
    y = _dot(merged.astype(BF16), wo_ref[0])
    o_ref[...] = x_ref[...] + gate_ref[...] * _rms(y, ng_ref[...])


def _merge(x, ya, yb, yc, proj, wp, wg, wn, wo, layer, norm_g, gate):
    s, d = x.shape
    tm = min(256, s)
    const = lambda i: (0, 0)
    rows = lambda i: (i, 0)
    whole = lambda a: pl.BlockSpec((1,) + a.shape[1:], lambda i: (layer, 0, 0), pipeline_mode=pl.Buffered(1))
    bg = OFF_BG // d
    return pl.pallas_call(
        _merge_kernel,
        grid=(s // tm,),
        in_specs=[
            pl.BlockSpec((tm, d), rows),
            pl.BlockSpec((tm, MIX_W), rows),
            pl.BlockSpec((tm, MIX_W), rows),
            pl.BlockSpec((tm, Q_W), rows),
            pl.BlockSpec((tm, d), lambda i: (i, bg)),
            pl.BlockSpec((tm, d), lambda i: (i, bg + 1)),
            pl.BlockSpec((tm, d), lambda i: (i, bg + 2)),
            whole(wp), whole(wg), whole(wn), whole(wo),
            pl.BlockSpec((1, d), const),
            pl.BlockSpec((1, d), const),
        ],
        out_specs=pl.BlockSpec((tm, d), rows),
        out_shape=jax.ShapeDtypeStruct((s, d), F32),
        compiler_params=_cparams("parallel"),
        name="merge_out",
    )(x, ya, yb, yc, proj, proj, proj, wp, wg, wn, wo, norm_g, gate)


def _ffn_kernel(x_ref, gi_ref, sc_ref, sh_ref, w1_ref, w2_ref, go_ref, gate_ref, o_ref, h_scr):
    j = pl.program_id(1)

    @pl.when(j == 0)
    def _():
        h = _rms(x_ref[...], gi_ref[...]) * (1.0 + sc_ref[...]) + sh_ref[...]
        h_scr[...] = h.astype(BF16)
        o_ref[...] = jnp.zeros_like(o_ref)

    a = jnp.square(jnp.maximum(_dot(h_scr[...], w1_ref[0]), 0.0))
    o_ref[...] += _dot(a.astype(BF16), w2_ref[0])

    @pl.when(j == pl.num_programs(1) - 1)
    def _():
        o_ref[...] = x_ref[...] + gate_ref[...] * _rms(o_ref[...], go_ref[...])


def _ffn(x, g_in, scale, shift, w1, w2, layer, g_out, gate):
    s, d = x.shape
    tm, tf = min(1024, s), 1024
    const = lambda i, j: (0, 0)
    rows = lambda i, j: (i, 0)
    return pl.pallas_call(
        _ffn_kernel,
        grid=(s // tm, D_FF // tf),
        in_specs=[
            pl.BlockSpec((tm, d), rows, pipeline_mode=pl.Buffered(1)),
            pl.BlockSpec((1, d), const),
            pl.BlockSpec((1, d), const),
            pl.BlockSpec((1, d), const),
            pl.BlockSpec((1, d, tf), lambda i, j: (layer, 0, j)),
            pl.BlockSpec((1, tf, d), lambda i, j: (layer, j, 0)),
            pl.BlockSpec((1, d), const),
            pl.BlockSpec((1, d), const),
        ],
        out_specs=pl.BlockSpec((tm, d), rows),
        out_shape=jax.ShapeDtypeStruct((s, d), F32),
        scratch_shapes=[pltpu.VMEM((tm, d), BF16)],
        compiler_params=_cparams("parallel", "arbitrary"),
        name="ffn",
    )(x, g_in, scale, shift, w1, w2, g_out, gate)


def _token_mixing(x, mod, norm_g, w_in, pool_w, pool_scale, ln_g, ln_b, ws, bs, cmp_pos, cmp_w1,
                  cmp_b1, cmp_w2, cmp_b2, branch_weights, layer):
    s, d = x.shape
    ng0 = OFF_BG
    w_main = jnp.concatenate([w_in[:, :ng0], w_in[:, ng0 + N_GATE:]], axis=1).astype(BF16)
    w_ng = jnp.pad(w_in[:, ng0:ng0 + N_GATE], ((0, 0), (0, 128 - N_GATE))).astype(BF16)
    row = lambda v: v.reshape(1, -1)

    col_scale = jnp.ones((1, PROJ_W), F32).at[:, OFF_Q:OFF_KV].set(HEAD_DIM ** -0.5 * LOG2E)
    proj, ngate = _in_projection(x, row(norm_g[0]), row(mod[1]), row(mod[0]), w_main, w_ng, col_scale)

    ya, yb = _mixers(proj, pool_w.astype(BF16), row(pool_scale), row(ln_g), row(ln_b), ws, bs.T)

    qt, ks_aug, vs_aug_t, kw_aug, vw_aug_t, gl = _attention_operands(proj, ngate)
    kv_cmp = proj[:, OFF_KV:OFF_KV + 2 * KV_W].reshape(s, 2, KV_GROUPS, HEAD_DIM).transpose(1, 2, 0, 3)
    xkv = kv_cmp.reshape(2, KV_GROUPS, s // CMP_STRIDE, CMP_STRIDE * HEAD_DIM)
    lane_pad = ((0, 0), (0, 0), (0, 128 - HEAD_DIM))
    kc_aug, vc_aug_t = _compress(xkv, cmp_pos.reshape(2, 1, CMP_BLOCK * HEAD_DIM), cmp_w1.astype(BF16),
                                 cmp_b1.reshape(2, 1, -1), jnp.pad(cmp_w2, lane_pad).astype(BF16),
                                 jnp.pad(cmp_b2.reshape(2, 1, -1), lane_pad))
    yc = _sparse_attention(qt, kc_aug, vc_aug_t, ks_aug, vs_aug_t, kw_aug, vw_aug_t, gl, s)

    return _merge(x, ya, yb, yc, proj, *branch_weights, layer, row(norm_g[1]), row(mod[2]))


def kernel(x, c, norm_g, w_ada, b_ada, w_in, pool_w, pool_scale, gmlp_ln_g, gmlp_ln_b, gmlp_ws, gmlp_bs,
           cmp_pos, cmp_w1, cmp_b1, cmp_w2, cmp_b2, w_br_pool, w_br_gmlp, w_br_nsa, w_out, w_ff1, w_ff2):
    b, s, d = x.shape
    assert b == 1 and d == D_MODEL and s % 1024 == 0
    n_layer = w_ada.shape[0]
    mod_all = _modulation(c, w_ada, b_ada).reshape(n_layer, 6, d)
    xs = x[0]
    row = lambda v: v.reshape(1, -1)
    branch_weights = tuple(w.astype(BF16) for w in (w_br_pool, w_br_gmlp, w_br_nsa, w_out))
    w_ff1_b, w_ff2_b = w_ff1.astype(BF16), w_ff2.astype(BF16)
    for l in range(n_layer):
        mod = mod_all[l]
        xs = _token_mixing(xs, mod, norm_g[l], w_in[l], pool_w[l], pool_scale[l], gmlp_ln_g[l],
                           gmlp_ln_b[l], gmlp_ws[l], gmlp_bs[l], cmp_pos[l], cmp_w1[l], cmp_b1[l],
                           cmp_w2[l], cmp_b2[l], branch_weights, l)
        xs = _ffn(xs, row(norm_g[l, 2]), row(mod[4]), row(mod[3]), w_ff1_b, w_ff2_b, l,
                  row(norm_g[l, 3]), row(mod[5]))
    return xs[None]
```

```python
import jax
import jax.numpy as jnp
from jax import lax
from jax.experimental import pallas as pl
from jax.experimental.pallas import tpu as pltpu

F32 = jnp.float32
BF16 = jnp.bfloat16

D_MODEL = 2048
POOL_WINDOWS = (2, 4, 8, 16)
POOL_HALO = 16
GROUP_CH = 128
MIX_W = 4 * GROUP_CH
GMLP_CHUNK = 128
HEAD_DIM = 64
KV_GROUPS = 4
HEADS_PER_GROUP = 4
Q_W = 16 * HEAD_DIM
KV_W = KV_GROUPS * HEAD_DIM
CMP_BLOCK = 32
CMP_STRIDE = 16
SEL_BLOCK = 64
SEL_TOPN = 16
WINDOW = 512
FORCE_BONUS = 1000.0
LOG2E = 1.4426950408889634
NEG = -1e30
N_GATE = 3 * 16
D_FF = 4 * D_MODEL

Q_BLOCK = 256
QL = HEADS_PER_GROUP * Q_BLOCK
CMP_PER_QB = Q_BLOCK // CMP_STRIDE
WIN_CHUNK = 128
PAD_TILES = WINDOW // Q_BLOCK
SEL_TILE = 512
KS_AUG_W = 128
CMP_AUG_W = 256
V_AUG_ROWS = HEAD_DIM + 16
P_CHUNK = 64
QK_PART = 128

OFF_POOL, OFF_U, OFF_V, OFF_Q, OFF_KV, OFF_BG = 0, 512, 1024, 1536, 2560, 4096
PROJ_W = OFF_BG + 3 * D_MODEL

VMEM_LIMIT = 60 * 1024 * 1024


def _cparams(*sem):
    return pltpu.CompilerParams(dimension_semantics=sem, vmem_limit_bytes=VMEM_LIMIT)


def _dot(a, b):
    return jnp.dot(a, b, preferred_element_type=F32)


def _rms(x, g):
    return x * lax.rsqrt(jnp.mean(x * x, axis=-1, keepdims=True) + 1e-6) * g


def _mod_kernel(c_ref, w_ref, b_ref, o_ref):
    c = c_ref[...]
    act = c * jax.nn.sigmoid(c)
    o_ref[0] = jnp.sum(act * w_ref[0], axis=0, keepdims=True) + b_ref[0]


def _modulation(c, w_ada, b_ada):
    n_layer, d, n_out = w_ada.shape
    tn = 1024
    return pl.pallas_call(
        _mod_kernel,
        grid=(n_layer, n_out // tn),
        in_specs=[
            pl.BlockSpec((d, 1), lambda l, j: (0, 0)),
            pl.BlockSpec((1, d, tn), lambda l, j: (l, 0, j)),
            pl.BlockSpec((1, 1, tn), lambda l, j: (l, 0, j)),
        ],
        out_specs=pl.BlockSpec((1, 1, tn), lambda l, j: (l, 0, j)),
        out_shape=jax.ShapeDtypeStruct((n_layer, 1, n_out), F32),
        compiler_params=_cparams("parallel", "parallel"),
        name="adaln_mod",
    )(c.reshape(d, 1), w_ada, b_ada.reshape(n_layer, 1, n_out))


def _inproj_kernel(x_ref, g_ref, sc_ref, sh_ref, w_ref, wng_ref, cs_ref, o_ref, ng_ref, h_scr):
    @pl.when(pl.program_id(1) == 0)
    def _():
        h = _rms(x_ref[...], g_ref[...]) * (1.0 + sc_ref[...]) + sh_ref[...]
        hb = h.astype(BF16)
        h_scr[...] = hb
        ng_ref[...] = _dot(hb, wng_ref[...])

    o_ref[...] = (_dot(h_scr[...], w_ref[...]) * cs_ref[...]).astype(o_ref.dtype)


def _in_projection(x, g, scale, shift, w_main, w_ng, col_scale):
    s, d = x.shape
    tm, tn = min(1024, s), 2048
    row = lambda i, j: (0, 0)
    return pl.pallas_call(
        _inproj_kernel,
        grid=(s // tm, PROJ_W // tn),
        in_specs=[
            pl.BlockSpec((tm, d), lambda i, j: (i, 0)),
            pl.BlockSpec((1, d), row),
            pl.BlockSpec((1, d), row),
            pl.BlockSpec((1, d), row),
            pl.BlockSpec((d, tn), lambda i, j: (0, j)),
            pl.BlockSpec((d, 128), row),
            pl.BlockSpec((1, tn), lambda i, j: (0, j)),
        ],
        out_specs=[
            pl.BlockSpec((tm, tn), lambda i, j: (i, j)),
            pl.BlockSpec((tm, 128), lambda i, j: (i, 0)),
        ],
        out_shape=[
            jax.ShapeDtypeStruct((s, PROJ_W), BF16),
            jax.ShapeDtypeStruct((s, 128), F32),
        ],
        scratch_shapes=[pltpu.VMEM((tm, d), BF16)],
        compiler_params=_cparams("parallel", "arbitrary"),
        name="in_proj",
    )(x, g, scale, shift, w_main, w_ng, col_scale)


def _mixer_kernel(a_ref, halo_ref, u_ref, v_ref, pw_ref, ps_ref, lg_ref, lb_ref, ws_ref, bs_ref,
                  ya_ref, yb_ref):
    i = pl.program_id(0)
    tm = a_ref.shape[0]
    a = a_ref[...].astype(F32)
    halo = jnp.where(i > 0, halo_ref[...].astype(F32), 0.0)
    ext = jnp.concatenate([halo, a], axis=0)
    p2 = ext[1:] + ext[:-1]
    p4 = p2[2:] + p2[:-2]
    p8 = p4[4:] + p4[:-4]
    p16 = p8[8:] + p8[:-8]
    sums = (p2[15:15 + tm], p4[13:13 + tm], p8[9:9 + tm], p16[1:1 + tm])
    t = i * tm + lax.broadcasted_iota(jnp.int32, (tm, 1), 0)
    for gi, w in enumerate(POOL_WINDOWS):
        cols = slice(gi * GROUP_CH, (gi + 1) * GROUP_CH)
        cnt = jnp.minimum(t + 1, w).astype(F32)
        pooled = sums[gi][:, cols] / cnt - a[:, cols]
        y = _dot(pooled.astype(BF16), pw_ref[gi])
        ya_ref[:, cols] = (y * ps_ref[:, cols]).astype(ya_ref.dtype)

    u = jax.nn.gelu(u_ref[...].astype(F32))
    v = jax.nn.gelu(v_ref[...].astype(F32))
    mu = jnp.mean(v, axis=-1, keepdims=True)
    var = jnp.mean(jnp.square(v - mu), axis=-1, keepdims=True)
    vn = ((v - mu) * lax.rsqrt(var + 1e-5) * lg_ref[...] + lb_ref[...]).astype(BF16)
    r = lax.broadcasted_iota(jnp.int32, (GMLP_CHUNK, GMLP_CHUNK), 0)
    c = lax.broadcasted_iota(jnp.int32, (GMLP_CHUNK, GMLP_CHUNK), 1)
    for gi in range(4):
        cols = slice(gi * GROUP_CH, (gi + 1) * GROUP_CH)
        wsm = jnp.where(r >= c, ws_ref[gi], 0.0).astype(BF16)
        bias = bs_ref[:, gi:gi + 1]
        for ck in range(tm // GMLP_CHUNK):
            rows = slice(ck * GMLP_CHUNK, (ck + 1) * GMLP_CHUNK)
            mixed = _dot(wsm, vn[rows, cols]) + bias
            yb_ref[rows, cols] = (u[rows, cols] * mixed).astype(yb_ref.dtype)


def _mixers(proj, pool_w, pool_scale, ln_g, ln_b, ws, bs_t):
    s = proj.shape[0]
    tm = min(512, s)
    hb = tm // POOL_HALO
    const2 = lambda i: (0, 0)
    const3 = lambda i: (0, 0, 0)
    return pl.pallas_call(
        _mixer_kernel,
        grid=(s // tm,),
        in_specs=[
            pl.BlockSpec((tm, MIX_W), lambda i: (i, OFF_POOL // MIX_W)),
            pl.BlockSpec((POOL_HALO, MIX_W), lambda i: (jnp.maximum(i * hb - 1, 0), OFF_POOL // MIX_W)),
            pl.BlockSpec((tm, MIX_W), lambda i: (i, OFF_U // MIX_W)),
            pl.BlockSpec((tm, MIX_W), lambda i: (i, OFF_V // MIX_W)),
            pl.BlockSpec((4, GROUP_CH, GROUP_CH), const3),
            pl.BlockSpec((1, MIX_W), const2),
            pl.BlockSpec((1, MIX_W), const2),
            pl.BlockSpec((1, MIX_W), const2),
            pl.BlockSpec((4, GMLP_CHUNK, GMLP_CHUNK), const3),
            pl.BlockSpec((GMLP_CHUNK, 4), const2),
        ],
        out_specs=[
            pl.BlockSpec((tm, MIX_W), lambda i: (i, 0)),
            pl.BlockSpec((tm, MIX_W), lambda i: (i, 0)),
        ],
        out_shape=[jax.ShapeDtypeStruct((s, MIX_W), BF16)] * 2,
        compiler_params=_cparams("parallel"),
        name="mixers",
    )(proj, proj, proj, proj, pool_w, pool_scale, ln_g, ln_b, ws, bs_t)


def _compress_kernel(x_ref, pos_ref, w1_ref, b1_ref, w2_ref, b2_ref, kc_ref, vct_ref):
    half = CMP_STRIDE * HEAD_DIM
    x = x_ref[0, 0].astype(F32)
    nc = x.shape[0]
    pos = pos_ref[0]
    first = _dot((x + pos[:, :half]).astype(BF16), w1_ref[0, :half, :])
    second = _dot((x + pos[:, half:]).astype(BF16), w1_ref[0, half:, :])
    hid = jax.nn.gelu(first + pltpu.roll(second, nc - 1, 0) + b1_ref[0])
    out = _dot(hid.astype(BF16), w2_ref[0]) + b2_ref[0]

    @pl.when(pl.program_id(1) == 0)
    def _():
        lane = lax.broadcasted_iota(jnp.int32, (nc, 128), 1)
        grp = lax.broadcasted_iota(jnp.int32, (nc, 128), 0) // 8
        kc_ref[0, :, 0:128] = (out + (lane - HEAD_DIM == grp).astype(F32)).astype(BF16)
        kc_ref[0, :, 128:] = (lane + (128 - HEAD_DIM) == grp).astype(BF16)

    @pl.when(pl.program_id(1) == 1)
    def _():
        ones_row = lax.broadcasted_iota(jnp.int32, (V_AUG_ROWS, nc), 0) == HEAD_DIM
        vct_ref[0] = (out.T[0:V_AUG_ROWS] + ones_row.astype(F32)).astype(BF16)


def _compress(xkv, pos, w1, b1, w2, b2):
    _, n_group, nc, width = xkv.shape
    assert CMP_AUG_W == 256 and HEAD_DIM + nc // 8 <= CMP_AUG_W
    per_kv = lambda g, a: (a, 0, 0)
    per_group = lambda g, a: (g, 0, 0)
    return pl.pallas_call(
        _compress_kernel,
        grid=(n_group, 2),
        in_specs=[
            pl.BlockSpec((1, 1, nc, width), lambda g, a: (a, g, 0, 0)),
            pl.BlockSpec((1, 1, 2 * width), per_kv),
            pl.BlockSpec((1, 2 * width, 128), per_kv),
            pl.BlockSpec((1, 1, 128), per_kv),
            pl.BlockSpec((1, 128, 128), per_kv),
            pl.BlockSpec((1, 1, 128), per_kv),
        ],
        out_specs=[pl.BlockSpec((1, nc, CMP_AUG_W), per_group), pl.BlockSpec((1, V_AUG_ROWS, nc), per_group)],
        out_shape=[jax.ShapeDtypeStruct((n_group, nc, CMP_AUG_W), BF16),
                   jax.ShapeDtypeStruct((n_group, V_AUG_ROWS, nc), BF16)],
        compiler_params=_cparams("parallel", "arbitrary"),
        name="compress_kv",
    )(xkv, pos, w1, b1, w2, b2)


def _operand_kernel(q_lo_ref, q_hi_ref, sel_ref, win_ref, ng_ref, qt_ref, ks_ref, vst_ref, kw_ref, vwt_ref, gl_ref):
    i = pl.program_id(0)
    n_tiles = pl.num_programs(0) - PAD_TILES
    rows = q_lo_ref.shape[0]
    lane = lax.broadcasted_iota(jnp.int32, (rows, 128), 1)
    pos = i * rows + lax.broadcasted_iota(jnp.int32, (rows, 128), 0)
    ones_rows = (lax.broadcasted_iota(jnp.int32, (V_AUG_ROWS - HEAD_DIM, rows), 0) == 0).astype(BF16)

    def key_pair(slab, idx):
        onehot = (lane - HEAD_DIM == idx).astype(F32)
        left = jnp.where(lane < HEAD_DIM, slab, onehot)
        right = jnp.where(lane < HEAD_DIM, pltpu.roll(slab, HEAD_DIM, 1), onehot)
        return left.astype(BF16), right.astype(BF16)

    def transposed_pair(slab):
        t = slab.T.astype(BF16)
        return t[:HEAD_DIM], t[HEAD_DIM:]

    def write_kv(src, keep, idx, k_ref, vt_ref):
        for pair in range(KV_GROUPS // 2):
            lanes = slice(pair * 128, (pair + 1) * 128)
            k_slab = jnp.where(keep, src[:, lanes].astype(F32), 0.0)
            v_slab = jnp.where(keep, src[:, KV_W + pair * 128:KV_W + (pair + 1) * 128].astype(F32), 0.0)
            for g, k, vt in zip((2 * pair, 2 * pair + 1), key_pair(k_slab, idx), transposed_pair(v_slab)):
                k_ref[g] = k
                vt_ref[g, 0:HEAD_DIM, :] = vt
                vt_ref[g, HEAD_DIM:, :] = ones_rows

    @pl.when(i < n_tiles)
    def _():
        for half, ref in enumerate((q_lo_ref, q_hi_ref)):
            for gg in range(2):
                for pp in range(HEADS_PER_GROUP // 2):
                    lanes = slice(gg * 256 + pp * 128, gg * 256 + (pp + 1) * 128)
                    for h, t in zip((2 * pp, 2 * pp + 1), transposed_pair(ref[:, lanes].astype(F32))):
                        qt_ref[2 * half + gg, 0, :, h * Q_BLOCK:(h + 1) * Q_BLOCK] = t
        write_kv(sel_ref, True, (pos // SEL_BLOCK) % (SEL_TILE // SEL_BLOCK), ks_ref, vst_ref)
        logits_t = ng_ref[...].T
        for g in range(KV_GROUPS):
            for h in range(HEADS_PER_GROUP):
                for br in range(3):
                    r = (g * HEADS_PER_GROUP + h) * 3 + br
                    gl_ref[g, 0, br:br + 1, h * Q_BLOCK:(h + 1) * Q_BLOCK] = logits_t[r:r + 1, :]

    write_kv(win_ref, i >= PAD_TILES, (pos // WIN_CHUNK) % 8, kw_ref, vwt_ref)


def _attention_operands(proj, ngate):
    s = proj.shape[0]
    n_tiles = s // Q_BLOCK
    blk = MIX_W
    real = lambda i: jnp.minimum(i, n_tiles - 1)
    src = lambda c: pl.BlockSpec((Q_BLOCK, blk), lambda i: (real(i), c))
    return pl.pallas_call(
        _operand_kernel,
        grid=(n_tiles + PAD_TILES,),
        in_specs=[src(OFF_Q // blk), src(OFF_Q // blk + 1), src(OFF_KV // blk + 1),
                  pl.BlockSpec((Q_BLOCK, blk), lambda i: (jnp.maximum(i - PAD_TILES, 0), OFF_KV // blk + 2)),
                  pl.BlockSpec((Q_BLOCK, 128), lambda i: (real(i), 0))],
        out_specs=[
            pl.BlockSpec((KV_GROUPS, 1, HEAD_DIM, QL), lambda i: (0, real(i), 0, 0)),
            pl.BlockSpec((KV_GROUPS, Q_BLOCK, KS_AUG_W), lambda i: (0, real(i), 0)),
            pl.BlockSpec((KV_GROUPS, V_AUG_ROWS, Q_BLOCK), lambda i: (0, 0, real(i))),
            pl.BlockSpec((KV_GROUPS, Q_BLOCK, KS_AUG_W), lambda i: (0, i, 0)),
            pl.BlockSpec((KV_GROUPS, V_AUG_ROWS, Q_BLOCK), lambda i: (0, 0, i)),
            pl.BlockSpec((KV_GROUPS, 1, 3, QL), lambda i: (0, real(i), 0, 0)),
        ],
        out_shape=[
            jax.ShapeDtypeStruct((KV_GROUPS, n_tiles, HEAD_DIM, QL), BF16),
            jax.ShapeDtypeStruct((KV_GROUPS, s, KS_AUG_W), BF16),
            jax.ShapeDtypeStruct((KV_GROUPS, V_AUG_ROWS, s), BF16),
            jax.ShapeDtypeStruct((KV_GROUPS, s + WINDOW, KS_AUG_W), BF16),
            jax.ShapeDtypeStruct((KV_GROUPS, V_AUG_ROWS, s + WINDOW), BF16),
            jax.ShapeDtypeStruct((KV_GROUPS, n_tiles, 3, QL), F32),
        ],
        compiler_params=_cparams("arbitrary"),
        name="attn_operands",
    )(proj, proj, proj, proj, ngate)


def _nsa_kernel(qt_ref, kc_ref, vct_ref, ks_ref, vst_ref, kw_ref, vwt_ref, gl_ref, o_ref,
                ps_scr, bias_scr, diag_bias_scr, qc_scr, qw_scr, qs_scr, sc_scr, s0_scr, s1_scr, p0_scr, p1_scr):
    qb = pl.program_id(1)
    t0 = qb * Q_BLOCK
    t0a = pl.multiple_of(t0, Q_BLOCK)
    qt = qt_ref[0, 0]
    nc = kc_ref.shape[1]
    n_sel = bias_scr.shape[0]
    q_rel = lax.broadcasted_iota(jnp.int32, (1, QL), 1) % Q_BLOCK
    tq = t0 + q_rel

    def bias_rows(cond, rows):
        return jnp.where(cond, NEG, 0.0).astype(BF16) + jnp.zeros((rows, QL), BF16)

    def flash_update(s, carry, vt):
        m_prev, acc = carry
        m_new = jnp.maximum(m_prev, jnp.max(s, axis=0, keepdims=True))
        p = jnp.exp2((s - m_new).astype(BF16))
        return m_new, jnp.exp2(m_prev - m_new) * acc + _dot(vt, p)

    init = (jnp.full((1, QL), NEG, F32), jnp.zeros((V_AUG_ROWS, QL), F32))

    grp = lax.broadcasted_iota(jnp.int32, (qc_scr.shape[0] - HEAD_DIM, 1), 0)
    visible = CMP_PER_QB * (qb + 1)
    qc_scr[0:HEAD_DIM, :] = qt
    qc_scr[HEAD_DIM:, :] = bias_rows((8 * grp >= visible) & (grp < nc // 8), qc_scr.shape[0] - HEAD_DIM)
    band = CMP_PER_QB + 8
    r0 = pl.multiple_of(jnp.maximum(visible - band, 0), 8)
    band_end = CMP_STRIDE * (r0 + lax.broadcasted_iota(jnp.int32, (band, 1), 0)) + (CMP_BLOCK - 1)

    def compressed(rows):
        def run():
            sc_scr[0:rows, :] = _dot(kc_ref[0, 0:rows, :], qc_scr[...])
            sc_scr[pl.ds(r0, band), :] = jnp.where(band_end <= tq, sc_scr[pl.ds(r0, band), :], NEG)
            s_c = sc_scr[0:rows, :]
            m_c = jnp.maximum(jnp.max(s_c, axis=0, keepdims=True), 0.1 * NEG)
            e_c = jnp.exp2(s_c - m_c)
            o_aug = _dot(vct_ref[0, :, 0:rows], e_c.astype(BF16))
            inv_c = 1.0 / jnp.maximum(o_aug[HEAD_DIM:HEAD_DIM + 1], 1e-30)
            p_grp = None
            for h in range(HEADS_PER_GROUP):
                lanes = slice(h * Q_BLOCK, (h + 1) * Q_BLOCK)
                p_h = e_c[:, lanes] * inv_c[:, lanes]
                p_grp = p_h if p_grp is None else p_grp + p_h
            for c in range(Q_BLOCK // 128):
                ps_scr[c, 8:8 + rows, :] = p_grp[:, c * 128:(c + 1) * 128]
                if rows < nc:
                    ps_scr[c, 8 + rows:, :] = jnp.zeros((nc - rows, 128), F32)
            return o_aug[:HEAD_DIM] * inv_c
        return run

    ps_scr[:, 0:8, :] = jnp.zeros((Q_BLOCK // 128, 8, 128), F32)
    n_part = 8 if nc // 8 >= band else 4
    part = nc // n_part

    def choose(lo, hi):
        if lo == hi:
            return compressed(lo * part)
        mid = (lo + hi) // 2
        return lambda: lax.cond(visible <= mid * part, choose(lo, mid), choose(mid + 1, hi))

    o_c = choose(1, n_part)()
    imp = jnp.concatenate([sum(ps_scr[c, pl.ds(k, n_sel, stride=4), :] for k in range(7, 12))
                           for c in range(Q_BLOCK // 128)], axis=1)

    n_tri = Q_BLOCK // WIN_CHUNK
    n_chunk = WINDOW // WIN_CHUNK + n_tri
    span = n_chunk * WIN_CHUNK
    rho = lax.broadcasted_iota(jnp.int32, (16, 1), 0)
    first_real = WINDOW // WIN_CHUNK - n_tri * qb
    qw_scr[0:HEAD_DIM, :] = qt
    qw_scr[HEAD_DIM:HEAD_DIM + 16, :] = bias_rows((rho < 8) & (((rho - n_tri * qb) & 7) < first_real), 16)
    qw_scr[HEAD_DIM + 16:, :] = jnp.zeros((qw_scr.shape[0] - HEAD_DIM - 16, QL), BF16)
    s_w = _dot(kw_ref[0, pl.ds(t0a, span), :], qw_scr[...])
    i_rel = lax.broadcasted_iota(jnp.int32, (Q_BLOCK, 1), 0)
    s_w = jnp.concatenate([jnp.where(i_rel > q_rel, s_w[:Q_BLOCK], NEG), s_w[Q_BLOCK:WINDOW],
                           jnp.where(i_rel <= q_rel, s_w[WINDOW:], NEG)], axis=0)
    _, acc_w = flash_update(s_w, init, vwt_ref[0, :, pl.ds(t0a, span)])

    s0_scr[...] = _dot(ks_ref[0, 0:SEL_TILE, 0:HEAD_DIM], qt)

    j_idx = lax.broadcasted_iota(jnp.int32, (n_sel, Q_BLOCK), 0)
    cur = (t0 + lax.broadcasted_iota(jnp.int32, (n_sel, Q_BLOCK), 1)) // SEL_BLOCK
    forced = (j_idx == 0) | (j_idx == cur) | (j_idx == cur - 1)
    valid = j_idx <= cur
    score = jnp.where(forced, -2.0, jnp.where(valid, imp, -1.0))
    n_pick = SEL_TOPN - 3

    fast = score
    for _ in range(n_pick):
        fast = jnp.where(fast == jnp.max(fast, axis=0, keepdims=True), -2.0, fast)
    retired = (fast < -1.5) & valid & jnp.logical_not(forced)
    most_retired = jnp.max(jnp.sum(retired.astype(F32), axis=0, keepdims=True))

    def ranked_with_ties():
        def pick_one(_, sc):
            best = jnp.max(sc, axis=0, keepdims=True)
            first = jnp.min(jnp.where(sc == best, j_idx, n_sel), axis=0, keepdims=True)
            return jnp.where(j_idx == first, -2.0, sc)
        return lax.fori_loop(0, n_pick, pick_one, score)

    ranked = lax.cond(most_retired > n_pick, ranked_with_ties, lambda: fast)
    chosen = valid & (ranked < -1.5)
    diag_bias_scr[...] = jnp.where(chosen, 0.0, NEG)
    bias_scr[...] = jnp.where(chosen & (j_idx * SEL_BLOCK < t0), 0.0, NEG)

    blocks_per_tile = SEL_TILE // SEL_BLOCK
    last_tile = ks_ref.shape[1] // SEL_TILE - 1
    qs_scr[0:HEAD_DIM, :] = qt
    qs_scr[HEAD_DIM + 16:, :] = jnp.zeros((qs_scr.shape[0] - HEAD_DIM - 16, QL), BF16)

    def tile_start(kt):
        return pl.multiple_of(jnp.clip(kt, 0, last_tile) * SEL_TILE, SEL_TILE)

    def set_bias_rows(table, kt):
        b0 = pl.multiple_of(jnp.minimum(kt, last_tile) * blocks_per_tile, blocks_per_tile)
        b16 = jnp.concatenate([table[pl.ds(b0, blocks_per_tile), :],
                               jnp.zeros((16 - blocks_per_tile, Q_BLOCK), F32)], axis=0)
        qs_scr[HEAD_DIM:HEAD_DIM + 16, :] = jnp.concatenate([b16] * HEADS_PER_GROUP, axis=1).astype(BF16)

    set_bias_rows(diag_bias_scr, t0 // SEL_TILE)
    s_d = _dot(ks_ref[0, pl.ds(t0a, Q_BLOCK), :], qs_scr[...])
    m_d, acc_d = flash_update(jnp.where(i_rel <= q_rel, s_d, NEG), init, vst_ref[0, :, pl.ds(t0a, Q_BLOCK)])

    def half_step(kt, s_cur, s_nxt, p_cur, p_prev, carry):
        m_prev, alpha_prev, acc = carry
        set_bias_rows(bias_scr, kt + 1)
        m8 = jnp.max(s_cur[...].reshape(SEL_TILE // 8, 8, QL), axis=0)
        m_new = jnp.maximum(m_prev, jnp.max(m8, axis=0, keepdims=True))
        acc = alpha_prev * acc + _dot(vst_ref[0, :, pl.ds(tile_start(kt - 1), SEL_TILE)], p_prev[...])
        k_nxt = tile_start(kt + 1)
        for h in range(SEL_TILE // QK_PART):
            for c in range(QK_PART // P_CHUNK):
                rows = slice(h * QK_PART + c * P_CHUNK, h * QK_PART + (c + 1) * P_CHUNK)
                p_cur[rows, :] = jnp.exp2((s_cur[rows, :] - m_new).astype(BF16))
            part = slice(h * QK_PART, (h + 1) * QK_PART)
            s_nxt[part, :] = _dot(ks_ref[0, pl.ds(k_nxt + h * QK_PART, QK_PART), :], qs_scr[...])
        return m_new, jnp.exp2(m_prev - m_new), acc

    def pair_step(i, carry):
        carry = half_step(2 * i, s0_scr, s1_scr, p0_scr, p1_scr, carry)
        return half_step(2 * i + 1, s1_scr, s0_scr, p1_scr, p0_scr, carry)

    n_pairs = ((t0 + SEL_TILE - 1) // SEL_TILE + 1) // 2
    for b in range(blocks_per_tile):
        rows = slice(b * SEL_BLOCK, (b + 1) * SEL_BLOCK)
        s0_scr[rows, :] = s0_scr[rows, :] + jnp.concatenate([bias_scr[b:b + 1, :]] * HEADS_PER_GROUP, axis=1)
    p1_scr[...] = jnp.zeros(p1_scr.shape, BF16)
    carry = (m_d, jnp.ones((1, QL), F32), acc_d)
    _, alpha_last, acc_s = lax.fori_loop(0, n_pairs, pair_step, carry)
    acc_s = alpha_last * acc_s + _dot(vst_ref[0, :, pl.ds(tile_start(2 * n_pairs - 1), SEL_TILE)], p1_scr[...])

    gate = jax.nn.sigmoid(gl_ref[0, 0])
    o = (gate[0:1] * o_c + gate[1:2] * (acc_s[:HEAD_DIM] / acc_s[HEAD_DIM:HEAD_DIM + 1])
         + gate[2:3] * (acc_w[:HEAD_DIM] / acc_w[HEAD_DIM:HEAD_DIM + 1]))
    stacked = jnp.concatenate([o[:, p * Q_BLOCK:(p + 1) * Q_BLOCK] for p in range(HEADS_PER_GROUP)], axis=0)
    o_ref[...] = stacked.T.astype(o_ref.dtype)


def _sparse_attention(qt, kc_aug, vc_aug_t, ks_aug, vs_aug_t, kw_aug, vw_aug_t, gate_logits, s):
    n_group, n_qb = qt.shape[:2]
    nc = kc_aug.shape[1]
    n_sel = s // SEL_BLOCK
    assert HEAD_DIM + nc // 8 <= CMP_AUG_W and nc // 4 >= CMP_PER_QB + 8 and WINDOW // WIN_CHUNK + Q_BLOCK // WIN_CHUNK <= 8
    per_step = lambda g, i: (g, i, 0, 0)
    per_group = lambda shape: pl.BlockSpec((1,) + shape, lambda g, i: (g, 0, 0), pipeline_mode=pl.Buffered(1))
    tile_f32 = pltpu.VMEM((SEL_TILE, QL), F32)
    tile_bf16 = pltpu.VMEM((SEL_TILE, QL), BF16)
    sel_table = pltpu.VMEM((n_sel, Q_BLOCK), F32)
    return pl.pallas_call(
        _nsa_kernel,
        grid=(n_group, n_qb),
        in_specs=[
            pl.BlockSpec((1, 1, HEAD_DIM, QL), per_step),
            per_group((nc, CMP_AUG_W)),
            per_group((V_AUG_ROWS, nc)),
            per_group((s, KS_AUG_W)),
            per_group((V_AUG_ROWS, s)),
            per_group((s + WINDOW, KS_AUG_W)),
            per_group((V_AUG_ROWS, s + WINDOW)),
            pl.BlockSpec((1, 1, 3, QL), per_step),
        ],
        out_specs=pl.BlockSpec((Q_BLOCK, HEADS_PER_GROUP * HEAD_DIM), lambda g, i: (i, g)),
        out_shape=jax.ShapeDtypeStruct((s, Q_W), BF16),
        scratch_shapes=[pltpu.VMEM((Q_BLOCK // 128, 8 + nc, 128), F32), sel_table, sel_table,
                        pltpu.VMEM((CMP_AUG_W, QL), BF16), pltpu.VMEM((KS_AUG_W, QL), BF16),
                        pltpu.VMEM((KS_AUG_W, QL), BF16), pltpu.VMEM((nc, QL), F32),
                        tile_f32, tile_f32, tile_bf16, tile_bf16],
        compiler_params=_cparams("parallel", "arbitrary"),
        name="sparse_attn",
    )(qt, kc_aug, vc_aug_t, ks_aug, vs_aug_t, kw_aug, vw_aug_t, gate_logits)


def _merge_kernel(x_ref, ya_ref, yb_ref, yc_ref, g0_ref, g1_ref, g2_ref, wp_ref, wg_ref, wn_ref,
                  wo_ref, ng_ref, gate_ref, o_ref):
    merged = jax.nn.sigmoid(g0_ref[...].astype(F32)) * _dot(ya_ref[...], wp_ref[0])
    merged += jax.nn.sigmoid(g1_ref[...].astype(F32)) * _dot(yb_ref[...], wg_ref[0])
    merged += jax.nn.sigmoid(g2_ref[...].astype(F32)) * _dot(yc_ref[...], wn_ref[0])
    y = _dot(merged.astype(BF16), wo_ref[0])
    o_ref[...] = x_ref[...] + gate_ref[...] * _rms(y, ng_ref[...])


def _merge(x, ya, yb, yc, proj, wp, wg, wn, wo, layer, norm_g, gate):
    s, d = x.shape
    tm = min(256, s)
    const = lambda i: (0, 0)
    rows = lambda i: (i, 0)
    whole = lambda a: pl.BlockSpec((1,) + a.shape[1:], lambda i: (layer, 0, 0), pipeline_mode=pl.Buffered(1))
    bg = OFF_BG // d
    return pl.pallas_call(
        _merge_kernel,
        grid=(s // tm,),
        in_specs=[
            pl.BlockSpec((tm, d), rows),
            pl.BlockSpec((tm, MIX_W), rows),
            pl.BlockSpec((tm, MIX_W), rows),
            pl.BlockSpec((tm, Q_W), rows),
            pl.BlockSpec((tm, d), lambda i: (i, bg)),
            pl.BlockSpec((tm, d), lambda i: (i, bg + 1)),
            pl.BlockSpec((tm, d), lambda i: (i, bg + 2)),
            whole(wp), whole(wg), whole(wn), whole(wo),
            pl.BlockSpec((1, d), const),
            pl.BlockSpec((1, d), const),
        ],
        out_specs=pl.BlockSpec((tm, d), rows),
        out_shape=jax.ShapeDtypeStruct((s, d), F32),
        compiler_params=_cparams("parallel"),
        name="merge_out",
    )(x, ya, yb, yc, proj, proj, proj, wp, wg, wn, wo, norm_g, gate)


def _ffn_kernel(x_ref, gi_ref, sc_ref, sh_ref, w1_ref, w2_ref, go_ref, gate_ref, o_ref, h_scr):
    j = pl.program_id(1)

    @pl.when(j == 0)
    def _():
        h = _rms(x_ref[...], gi_ref[...]) * (1.0 + sc_ref[...]) + sh_ref[...]
        h_scr[...] = h.astype(BF16)
        o_ref[...] = jnp.zeros_like(o_ref)

    a = jnp.square(jnp.maximum(_dot(h_scr[...], w1_ref[0]), 0.0))
    o_ref[...] += _dot(a.astype(BF16), w2_ref[0])

    @pl.when(j == pl.num_programs(1) - 1)
    def _():
        o_ref[...] = x_ref[...] + gate_ref[...] * _rms(o_ref[...], go_ref[...])


def _ffn(x, g_in, scale, shift, w1, w2, layer, g_out, gate):
    s, d = x.shape
    tm, tf = min(1024, s), 1024
    const = lambda i, j: (0, 0)
    rows = lambda i, j: (i, 0)
    return pl.pallas_call(
        _ffn_kernel,
        grid=(s // tm, D_FF // tf),
        in_specs=[
            pl.BlockSpec((tm, d), rows, pipeline_mode=pl.Buffered(1)),
            pl.BlockSpec((1, d), const),
            pl.BlockSpec((1, d), const),
            pl.BlockSpec((1, d), const),
            pl.BlockSpec((1, d, tf), lambda i, j: (layer, 0, j)),
            pl.BlockSpec((1, tf, d), lambda i, j: (layer, j, 0)),
            pl.BlockSpec((1, d), const),
            pl.BlockSpec((1, d), const),
        ],
        out_specs=pl.BlockSpec((tm, d), rows),
        out_shape=jax.ShapeDtypeStruct((s, d), F32),
        scratch_shapes=[pltpu.VMEM((tm, d), BF16)],
        compiler_params=_cparams("parallel", "arbitrary"),
        name="ffn",
    )(x, g_in, scale, shift, w1, w2, g_out, gate)


def _token_mixing(x, mod, norm_g, w_in, pool_w, pool_scale, ln_g, ln_b, ws, bs, cmp_pos, cmp_w1,
                  cmp_b1, cmp_w2, cmp_b2, branch_weights, layer):
    s, d = x.shape
    ng0 = OFF_BG
    w_main = jnp.concatenate([w_in[:, :ng0], w_in[:, ng0 + N_GATE:]], axis=1).astype(BF16)
    w_ng = jnp.pad(w_in[:, ng0:ng0 + N_GATE], ((0, 0), (0, 128 - N_GATE))).astype(BF16)
    row = lambda v: v.reshape(1, -1)

    col_scale = jnp.ones((1, PROJ_W), F32).at[:, OFF_Q:OFF_KV].set(HEAD_DIM ** -0.5 * LOG2E)
    proj, ngate = _in_projection(x, row(norm_g[0]), row(mod[1]), row(mod[0]), w_main, w_ng, col_scale)

    ya, yb = _mixers(proj, pool_w.astype(BF16), row(pool_scale), row(ln_g), row(ln_b), ws, bs.T)

    qt, ks_aug, vs_aug_t, kw_aug, vw_aug_t, gl = _attention_operands(proj, ngate)
    kv_cmp = proj[:, OFF_KV:OFF_KV + 2 * KV_W].reshape(s, 2, KV_GROUPS, HEAD_DIM).transpose(1, 2, 0, 3)
    xkv = kv_cmp.reshape(2, KV_GROUPS, s // CMP_STRIDE, CMP_STRIDE * HEAD_DIM)
    lane_pad = ((0, 0), (0, 0), (0, 128 - HEAD_DIM))
    kc_aug, vc_aug_t = _compress(xkv, cmp_pos.reshape(2, 1, CMP_BLOCK * HEAD_DIM), cmp_w1.astype(BF16),
                                 cmp_b1.reshape(2, 1, -1), jnp.pad(cmp_w2, lane_pad).astype(BF16),
                                 jnp.pad(cmp_b2.reshape(2, 1, -1), lane_pad))
    yc = _sparse_attention(qt, kc_aug, vc_aug_t, ks_aug, vs_aug_t, kw_aug, vw_aug_t, gl, s)

    return _merge(x, ya, yb, yc, proj, *branch_weights, layer, row(norm_g[1]), row(mod[2]))


def kernel(x, c, norm_g, w_ada, b_ada, w_in, pool_w, pool_scale, gmlp_ln_g, gmlp_ln_b, gmlp_ws, gmlp_bs,
           cmp_pos, cmp_w1, cmp_b1, cmp_w2, cmp_b2, w_br_pool, w_br_gmlp, w_br_nsa, w_out, w_ff1, w_ff2):
    b, s, d = x.shape
    assert b == 1 and d == D_MODEL and s % 1024 == 0
    n_layer = w_ada.shape[0]
    mod_all = _modulation(c, w_ada, b_ada).reshape(n_layer, 6, d)
    xs = x[0]
    row = lambda v: v.reshape(1, -1)
    branch_weights = tuple(w.astype(BF16) for w in (w_br_pool, w_br_gmlp, w_br_nsa, w_out))
    w_ff1_b, w_ff2_b = w_ff1.astype(BF16), w_ff2.astype(BF16)
    for l in range(n_layer):
        mod = mod_all[l]
        xs = _token_mixing(xs, mod, norm_g[l], w_in[l], pool_w[l], pool_scale[l], gmlp_ln_g[l],
                           gmlp_ln_b[l], gmlp_ws[l], gmlp_bs[l], cmp_pos[l], cmp_w1[l], cmp_b1[l],
                           cmp_w2[l], cmp_b2[l], branch_weights, l)
        xs = _ffn(xs, row(norm_g[l, 2]), row(mod[4]), row(mod[3]), w_ff1_b, w_ff2_b, l,
                  row(norm_g[l, 3]), row(mod[5]))
    return xs[None]
```

```python
import jax
import jax.numpy as jnp
from jax import lax
from jax.experimental import pallas as pl
from jax.experimental.pallas import tpu as pltpu

F32 = jnp.float32
BF16 = jnp.bfloat16

D_MODEL = 2048
POOL_WINDOWS = (2, 4, 8, 16)
POOL_HALO = 16
GROUP_CH = 128
MIX_W = 4 * GROUP_CH
GMLP_CHUNK = 128
HEAD_DIM = 64
KV_GROUPS = 4
HEADS_PER_GROUP = 4
Q_W = 16 * HEAD_DIM
KV_W = KV_GROUPS * HEAD_DIM
CMP_BLOCK = 32
CMP_STRIDE = 16
SEL_BLOCK = 64
SEL_TOPN = 16
WINDOW = 512
FORCE_BONUS = 1000.0
LOG2E = 1.4426950408889634
NEG = -1e30
N_GATE = 3 * 16
D_FF = 4 * D_MODEL

Q_BLOCK = 256
QL = HEADS_PER_GROUP * Q_BLOCK
CMP_PER_QB = Q_BLOCK // CMP_STRIDE
WIN_CHUNK = 128
PAD_TILES = WINDOW // Q_BLOCK
SEL_TILE = 512
KS_AUG_W = 128
CMP_AUG_W = 256
V_AUG_ROWS = HEAD_DIM + 16
P_CHUNK = 64
QK_PART = 128

OFF_POOL, OFF_U, OFF_V, OFF_Q, OFF_KV, OFF_BG = 0, 512, 1024, 1536, 2560, 4096
PROJ_W = OFF_BG + 3 * D_MODEL

VMEM_LIMIT = 60 * 1024 * 1024


def _cparams(*sem):
    return pltpu.CompilerParams(dimension_semantics=sem, vmem_limit_bytes=VMEM_LIMIT)


def _dot(a, b):
    return jnp.dot(a, b, preferred_element_type=F32)


def _rms(x, g):
    return x * lax.rsqrt(jnp.mean(x * x, axis=-1, keepdims=True) + 1e-6) * g


def _mod_kernel(c_ref, w_ref, b_ref, o_ref):
    c = c_ref[...]
    act = c * jax.nn.sigmoid(c)
    o_ref[0] = jnp.sum(act * w_ref[0], axis=0, keepdims=True) + b_ref[0]


def _modulation(c, w_ada, b_ada):
    n_layer, d, n_out = w_ada.shape
    tn = 1024
    return pl.pallas_call(
        _mod_kernel,
        grid=(n_layer, n_out // tn),
        in_specs=[
            pl.BlockSpec((d, 1), lambda l, j: (0, 0)),
            pl.BlockSpec((1, d, tn), lambda l, j: (l, 0, j)),
            pl.BlockSpec((1, 1, tn), lambda l, j: (l, 0, j)),
        ],
        out_specs=pl.BlockSpec((1, 1, tn), lambda l, j: (l, 0, j)),
        out_shape=jax.ShapeDtypeStruct((n_layer, 1, n_out), F32),
        compiler_params=_cparams("parallel", "parallel"),
        name="adaln_mod",
    )(c.reshape(d, 1), w_ada, b_ada.reshape(n_layer, 1, n_out))


def _inproj_kernel(x_ref, g_ref, sc_ref, sh_ref, w_ref, wng_ref, cs_ref, o_ref, ng_ref, h_scr):
    @pl.when(pl.program_id(1) == 0)
    def _():
        h = _rms(x_ref[...], g_ref[...]) * (1.0 + sc_ref[...]) + sh_ref[...]
        hb = h.astype(BF16)
        h_scr[...] = hb
        ng_ref[...] = _dot(hb, wng_ref[...])

    o_ref[...] = (_dot(h_scr[...], w_ref[...]) * cs_ref[...]).astype(o_ref.dtype)


def _in_projection(x, g, scale, shift, w_main, w_ng, col_scale):
    s, d = x.shape
    tm, tn = min(1024, s), 2048
    row = lambda i, j: (0, 0)
    return pl.pallas_call(
        _inproj_kernel,
        grid=(s // tm, PROJ_W // tn),
        in_specs=[
            pl.BlockSpec((tm, d), lambda i, j: (i, 0)),
            pl.BlockSpec((1, d), row),
            pl.BlockSpec((1, d), row),
            pl.BlockSpec((1, d), row),
            pl.BlockSpec((d, tn), lambda i, j: (0, j)),
            pl.BlockSpec((d, 128), row),
            pl.BlockSpec((1, tn), lambda i, j: (0, j)),
        ],
        out_specs=[
            pl.BlockSpec((tm, tn), lambda i, j: (i, j)),
            pl.BlockSpec((tm, 128), lambda i, j: (i, 0)),
        ],
        out_shape=[
            jax.ShapeDtypeStruct((s, PROJ_W), BF16),
            jax.ShapeDtypeStruct((s, 128), F32),
        ],
        scratch_shapes=[pltpu.VMEM((tm, d), BF16)],
        compiler_params=_cparams("parallel", "arbitrary"),
        name="in_proj",
    )(x, g, scale, shift, w_main, w_ng, col_scale)


def _mixer_kernel(a_ref, halo_ref, u_ref, v_ref, pw_ref, ps_ref, lg_ref, lb_ref, ws_ref, bs_ref,
                  ya_ref, yb_ref):
    i = pl.program_id(0)
    tm = a_ref.shape[0]
    a = a_ref[...].astype(F32)
    halo = jnp.where(i > 0, halo_ref[...].astype(F32), 0.0)
    ext = jnp.concatenate([halo, a], axis=0)
    p2 = ext[1:] + ext[:-1]
    p4 = p2[2:] + p2[:-2]
    p8 = p4[4:] + p4[:-4]
    p16 = p8[8:] + p8[:-8]
    sums = (p2[15:15 + tm], p4[13:13 + tm], p8[9:9 + tm], p16[1:1 + tm])
    t = i * tm + lax.broadcasted_iota(jnp.int32, (tm, 1), 0)
    for gi, w in enumerate(POOL_WINDOWS):
        cols = slice(gi * GROUP_CH, (gi + 1) * GROUP_CH)
        cnt = jnp.minimum(t + 1, w).astype(F32)
        pooled = sums[gi][:, cols] / cnt - a[:, cols]
        y = _dot(pooled.astype(BF16), pw_ref[gi])
        ya_ref[:, cols] = (y * ps_ref[:, cols]).astype(ya_ref.dtype)

    u = jax.nn.gelu(u_ref[...].astype(F32))
    v = jax.nn.gelu(v_ref[...].astype(F32))
    mu = jnp.mean(v, axis=-1, keepdims=True)
    var = jnp.mean(jnp.square(v - mu), axis=-1, keepdims=True)
    vn = ((v - mu) * lax.rsqrt(var + 1e-5) * lg_ref[...] + lb_ref[...]).astype(BF16)
    r = lax.broadcasted_iota(jnp.int32, (GMLP_CHUNK, GMLP_CHUNK), 0)
    c = lax.broadcasted_iota(jnp.int32, (GMLP_CHUNK, GMLP_CHUNK), 1)
    for gi in range(4):
        cols = slice(gi * GROUP_CH, (gi + 1) * GROUP_CH)
        wsm = jnp.where(r >= c, ws_ref[gi], 0.0).astype(BF16)
        bias = bs_ref[:, gi:gi + 1]
        for ck in range(tm // GMLP_CHUNK):
            rows = slice(ck * GMLP_CHUNK, (ck + 1) * GMLP_CHUNK)
            mixed = _dot(wsm, vn[rows, cols]) + bias
            yb_ref[rows, cols] = (u[rows, cols] * mixed).astype(yb_ref.dtype)


def _mixers(proj, pool_w, pool_scale, ln_g, ln_b, ws, bs_t):
    s = proj.shape[0]
    tm = min(512, s)
    hb = tm // POOL_HALO
    const2 = lambda i: (0, 0)
    const3 = lambda i: (0, 0, 0)
    return pl.pallas_call(
        _mixer_kernel,
        grid=(s // tm,),
        in_specs=[
            pl.BlockSpec((tm, MIX_W), lambda i: (i, OFF_POOL // MIX_W)),
            pl.BlockSpec((POOL_HALO, MIX_W), lambda i: (jnp.maximum(i * hb - 1, 0), OFF_POOL // MIX_W)),
            pl.BlockSpec((tm, MIX_W), lambda i: (i, OFF_U // MIX_W)),
            pl.BlockSpec((tm, MIX_W), lambda i: (i, OFF_V // MIX_W)),
            pl.BlockSpec((4, GROUP_CH, GROUP_CH), const3),
            pl.BlockSpec((1, MIX_W), const2),
            pl.BlockSpec((1, MIX_W), const2),
            pl.BlockSpec((1, MIX_W), const2),
            pl.BlockSpec((4, GMLP_CHUNK, GMLP_CHUNK), const3),
            pl.BlockSpec((GMLP_CHUNK, 4), const2),
        ],
        out_specs=[
            pl.BlockSpec((tm, MIX_W), lambda i: (i, 0)),
            pl.BlockSpec((tm, MIX_W), lambda i: (i, 0)),
        ],
        out_shape=[jax.ShapeDtypeStruct((s, MIX_W), BF16)] * 2,
        compiler_params=_cparams("parallel"),
        name="mixers",
    )(proj, proj, proj, proj, pool_w, pool_scale, ln_g, ln_b, ws, bs_t)


def _compress_kernel(x_ref, pos_ref, w1_ref, b1_ref, w2_ref, b2_ref, kc_ref, vct_ref):
    half = CMP_STRIDE * HEAD_DIM
    x = x_ref[0, 0].astype(F32)
    nc = x.shape[0]
    pos = pos_ref[0]
    first = _dot((x + pos[:, :half]).astype(BF16), w1_ref[0, :half, :])
    second = _dot((x + pos[:, half:]).astype(BF16), w1_ref[0, half:, :])
    hid = jax.nn.gelu(first + pltpu.roll(second, nc - 1, 0) + b1_ref[0])
    out = _dot(hid.astype(BF16), w2_ref[0]) + b2_ref[0]

    @pl.when(pl.program_id(1) == 0)
    def _():
        lane = lax.broadcasted_iota(jnp.int32, (nc, 128), 1)
        grp = lax.broadcasted_iota(jnp.int32, (nc, 128), 0) // 8
        kc_ref[0, :, 0:128] = (out + (lane - HEAD_DIM == grp).astype(F32)).astype(BF16)
        kc_ref[0, :, 128:] = (lane + (128 - HEAD_DIM) == grp).astype(BF16)

    @pl.when(pl.program_id(1) == 1)
    def _():
        ones_row = lax.broadcasted_iota(jnp.int32, (V_AUG_ROWS, nc), 0) == HEAD_DIM
        vct_ref[0] = (out.T[0:V_AUG_ROWS] + ones_row.astype(F32)).astype(BF16)


def _compress(xkv, pos, w1, b1, w2, b2):
    _, n_group, nc, width = xkv.shape
    assert CMP_AUG_W == 256 and HEAD_DIM + nc // 8 <= CMP_AUG_W
    per_kv = lambda g, a: (a, 0, 0)
    per_group = lambda g, a: (g, 0, 0)
    return pl.pallas_call(
        _compress_kernel,
        grid=(n_group, 2),
        in_specs=[
            pl.BlockSpec((1, 1, nc, width), lambda g, a: (a, g, 0, 0)),
            pl.BlockSpec((1, 1, 2 * width), per_kv),
            pl.BlockSpec((1, 2 * width, 128), per_kv),
            pl.BlockSpec((1, 1, 128), per_kv),
            pl.BlockSpec((1, 128, 128), per_kv),
            pl.BlockSpec((1, 1, 128), per_kv),
        ],
        out_specs=[pl.BlockSpec((1, nc, CMP_AUG_W), per_group), pl.BlockSpec((1, V_AUG_ROWS, nc), per_group)],
        out_shape=[jax.ShapeDtypeStruct((n_group, nc, CMP_AUG_W), BF16),
                   jax.ShapeDtypeStruct((n_group, V_AUG_ROWS, nc), BF16)],
        compiler_params=_cparams("parallel", "arbitrary"),
        name="compress_kv",
    )(xkv, pos, w1, b1, w2, b2)


def _operand_kernel(q_lo_ref, q_hi_ref, sel_ref, win_ref, ng_ref, qt_ref, ks_ref, vst_ref, kw_ref, vwt_ref, gl_ref):
    i = pl.program_id(0)
    n_tiles = pl.num_programs(0) - PAD_TILES
    rows = q_lo_ref.shape[0]
    lane = lax.broadcasted_iota(jnp.int32, (rows, 128), 1)
    pos = i * rows + lax.broadcasted_iota(jnp.int32, (rows, 128), 0)
    ones_rows = (lax.broadcasted_iota(jnp.int32, (V_AUG_ROWS - HEAD_DIM, rows), 0) == 0).astype(BF16)

    def key_pair(slab, idx):
        onehot = (lane - HEAD_DIM == idx).astype(F32)
        left = jnp.where(lane < HEAD_DIM, slab, onehot)
        right = jnp.where(lane < HEAD_DIM, pltpu.roll(slab, HEAD_DIM, 1), onehot)
        return left.astype(BF16), right.astype(BF16)

    def transposed_pair(slab):
        t = slab.T.astype(BF16)
        return t[:HEAD_DIM], t[HEAD_DIM:]

    def write_kv(src, keep, idx, k_ref, vt_ref):
        for pair in range(KV_GROUPS // 2):
            lanes = slice(pair * 128, (pair + 1) * 128)
            k_slab = jnp.where(keep, src[:, lanes].astype(F32), 0.0)
            v_slab = jnp.where(keep, src[:, KV_W + pair * 128:KV_W + (pair + 1) * 128].astype(F32), 0.0)
            for g, k, vt in zip((2 * pair, 2 * pair + 1), key_pair(k_slab, idx), transposed_pair(v_slab)):
                k_ref[g] = k
                vt_ref[g, 0:HEAD_DIM, :] = vt
                vt_ref[g, HEAD_DIM:, :] = ones_rows

    @pl.when(i < n_tiles)
    def _():
        for half, ref in enumerate((q_lo_ref, q_hi_ref)):
            for gg in range(2):
                for pp in range(HEADS_PER_GROUP // 2):
                    lanes = slice(gg * 256 + pp * 128, gg * 256 + (pp + 1) * 128)
                    for h, t in zip((2 * pp, 2 * pp + 1), transposed_pair(ref[:, lanes].astype(F32))):
                        qt_ref[2 * half + gg, 0, :, h * Q_BLOCK:(h + 1) * Q_BLOCK] = t
        write_kv(sel_ref, True, (pos // SEL_BLOCK) % (SEL_TILE // SEL_BLOCK), ks_ref, vst_ref)
        logits_t = ng_ref[...].T
        for g in range(KV_GROUPS):
            for h in range(HEADS_PER_GROUP):
                for br in range(3):
                    r = (g * HEADS_PER_GROUP + h) * 3 + br
                    gl_ref[g, 0, br:br + 1, h * Q_BLOCK:(h + 1) * Q_BLOCK] = logits_t[r:r + 1, :]

    write_kv(win_ref, i >= PAD_TILES, (pos // WIN_CHUNK) % 8, kw_ref, vwt_ref)


def _attention_operands(proj, ngate):
    s = proj.shape[0]
    n_tiles = s // Q_BLOCK
    blk = MIX_W
    real = lambda i: jnp.minimum(i, n_tiles - 1)
    src = lambda c: pl.BlockSpec((Q_BLOCK, blk), lambda i: (real(i), c))
    return pl.pallas_call(
        _operand_kernel,
        grid=(n_tiles + PAD_TILES,),
        in_specs=[src(OFF_Q // blk), src(OFF_Q // blk + 1), src(OFF_KV // blk + 1),
                  pl.BlockSpec((Q_BLOCK, blk), lambda i: (jnp.maximum(i - PAD_TILES, 0), OFF_KV // blk + 2)),
                  pl.BlockSpec((Q_BLOCK, 128), lambda i: (real(i), 0))],
        out_specs=[
            pl.BlockSpec((KV_GROUPS, 1, HEAD_DIM, QL), lambda i: (0, real(i), 0, 0)),
            pl.BlockSpec((KV_GROUPS, Q_BLOCK, KS_AUG_W), lambda i: (0, real(i), 0)),
            pl.BlockSpec((KV_GROUPS, V_AUG_ROWS, Q_BLOCK), lambda i: (0, 0, real(i))),
            pl.BlockSpec((KV_GROUPS, Q_BLOCK, KS_AUG_W), lambda i: (0, i, 0)),
            pl.BlockSpec((KV_GROUPS, V_AUG_ROWS, Q_BLOCK), lambda i: (0, 0, i)),
            pl.BlockSpec((KV_GROUPS, 1, 3, QL), lambda i: (0, real(i), 0, 0)),
        ],
        out_shape=[
            jax.ShapeDtypeStruct((KV_GROUPS, n_tiles, HEAD_DIM, QL), BF16),
            jax.ShapeDtypeStruct((KV_GROUPS, s, KS_AUG_W), BF16),
            jax.ShapeDtypeStruct((KV_GROUPS, V_AUG_ROWS, s), BF16),
            jax.ShapeDtypeStruct((KV_GROUPS, s + WINDOW, KS_AUG_W), BF16),
            jax.ShapeDtypeStruct((KV_GROUPS, V_AUG_ROWS, s + WINDOW), BF16),
            jax.ShapeDtypeStruct((KV_GROUPS, n_tiles, 3, QL), F32),
        ],
        compiler_params=_cparams("arbitrary"),
        name="attn_operands",
    )(proj, proj, proj, proj, ngate)


def _nsa_kernel(qt_ref, kc_ref, vct_ref, ks_ref, vst_ref, kw_ref, vwt_ref, gl_ref, o_ref,
                ps_scr, bias_scr, diag_bias_scr, qc_scr, qw_scr, qs_scr, sc_scr, s0_scr, s1_scr, p0_scr, p1_scr):
    qb = pl.program_id(1)
    t0 = qb * Q_BLOCK
    t0a = pl.multiple_of(t0, Q_BLOCK)
    qt = qt_ref[0, 0]
    nc = kc_ref.shape[1]
    n_sel = bias_scr.shape[0]
    q_rel = lax.broadcasted_iota(jnp.int32, (1, QL), 1) % Q_BLOCK
    tq = t0 + q_rel

    def bias_rows(cond, rows):
        return jnp.where(cond, NEG, 0.0).astype(BF16) + jnp.zeros((rows, QL), BF16)

    def flash_update(s, carry, vt):
        m_prev, acc = carry
        m_new = jnp.maximum(m_prev, jnp.max(s, axis=0, keepdims=True))
        p = jnp.exp2((s - m_new).astype(BF16))
        return m_new, jnp.exp2(m_prev - m_new) * acc + _dot(vt, p)

    init = (jnp.full((1, QL), NEG, F32), jnp.zeros((V_AUG_ROWS, QL), F32))

    grp = lax.broadcasted_iota(jnp.int32, (qc_scr.shape[0] - HEAD_DIM, 1), 0)
    visible = CMP_PER_QB * (qb + 1)
    qc_scr[0:HEAD_DIM, :] = qt
    qc_scr[HEAD_DIM:, :] = bias_rows((8 * grp >= visible) & (grp < nc // 8), qc_scr.shape[0] - HEAD_DIM)
    band = CMP_PER_QB + 8
    r0 = pl.multiple_of(jnp.maximum(visible - band, 0), 8)
    band_end = CMP_STRIDE * (r0 + lax.broadcasted_iota(jnp.int32, (band, 1), 0)) + (CMP_BLOCK - 1)

    def compressed(rows):
        def run():
            sc_scr[0:rows, :] = _dot(kc_ref[0, 0:rows, :], qc_scr[...])
            sc_scr[pl.ds(r0, band), :] = jnp.where(band_end <= tq, sc_scr[pl.ds(r0, band), :], NEG)
            s_c = sc_scr[0:rows, :]
            m_c = jnp.maximum(jnp.max(s_c, axis=0, keepdims=True), 0.1 * NEG)
            e_c = jnp.exp2(s_c - m_c)
            o_aug = _dot(vct_ref[0, :, 0:rows], e_c.astype(BF16))
            inv_c = 1.0 / jnp.maximum(o_aug[HEAD_DIM:HEAD_DIM + 1], 1e-30)
            p_grp = None
            for h in range(HEADS_PER_GROUP):
                lanes = slice(h * Q_BLOCK, (h + 1) * Q_BLOCK)
                p_h = e_c[:, lanes] * inv_c[:, lanes]
                p_grp = p_h if p_grp is None else p_grp + p_h
            for c in range(Q_BLOCK // 128):
                ps_scr[c, 8:8 + rows, :] = p_grp[:, c * 128:(c + 1) * 128]
                if rows < nc:
                    ps_scr[c, 8 + rows:, :] = jnp.zeros((nc - rows, 128), F32)
            return o_aug[:HEAD_DIM] * inv_c
        return run

    ps_scr[:, 0:8, :] = jnp.zeros((Q_BLOCK // 128, 8, 128), F32)
    n_part = 8 if nc // 8 >= band else 4
    part = nc // n_part

    def choose(lo, hi):
        if lo == hi:
            return compressed(lo * part)
        mid = (lo + hi) // 2
        return lambda: lax.cond(visible <= mid * part, choose(lo, mid), choose(mid + 1, hi))

    o_c = choose(1, n_part)()
    imp = jnp.concatenate([sum(ps_scr[c, pl.ds(k, n_sel, stride=4), :] for k in range(7, 12))
                           for c in range(Q_BLOCK // 128)], axis=1)

    n_tri = Q_BLOCK // WIN_CHUNK
    n_chunk = WINDOW // WIN_CHUNK + n_tri
    span = n_chunk * WIN_CHUNK
    rho = lax.broadcasted_iota(jnp.int32, (16, 1), 0)
    first_real = WINDOW // WIN_CHUNK - n_tri * qb
    qw_scr[0:HEAD_DIM, :] = qt
    qw_scr[HEAD_DIM:HEAD_DIM + 16, :] = bias_rows((rho < 8) & (((rho - n_tri * qb) & 7) < first_real), 16)
    qw_scr[HEAD_DIM + 16:, :] = jnp.zeros((qw_scr.shape[0] - HEAD_DIM - 16, QL), BF16)
    s_w = _dot(kw_ref[0, pl.ds(t0a, span), :], qw_scr[...])
    i_rel = lax.broadcasted_iota(jnp.int32, (Q_BLOCK, 1), 0)
    s_w = jnp.concatenate([jnp.where(i_rel > q_rel, s_w[:Q_BLOCK], NEG), s_w[Q_BLOCK:WINDOW],
                           jnp.where(i_rel <= q_rel, s_w[WINDOW:], NEG)], axis=0)
    _, acc_w = flash_update(s_w, init, vwt_ref[0, :, pl.ds(t0a, span)])

    s0_scr[...] = _dot(ks_ref[0, 0:SEL_TILE, 0:HEAD_DIM], qt)

    j_idx = lax.broadcasted_iota(jnp.int32, (n_sel, Q_BLOCK), 0)
    cur = (t0 + lax.broadcasted_iota(jnp.int32, (n_sel, Q_BLOCK), 1)) // SEL_BLOCK
    forced = (j_idx == 0) | (j_idx == cur) | (j_idx == cur - 1)
    valid = j_idx <= cur
    score = jnp.where(forced, -2.0, jnp.where(valid, imp, -1.0))
    n_pick = SEL_TOPN - 3

    fast = score
    for _ in range(n_pick):
        fast = jnp.where(fast == jnp.max(fast, axis=0, keepdims=True), -2.0, fast)
    retired = (fast < -1.5) & valid & jnp.logical_not(forced)
    most_retired = jnp.max(jnp.sum(retired.astype(F32), axis=0, keepdims=True))

    def ranked_with_ties():
        def pick_one(_, sc):
            best = jnp.max(sc, axis=0, keepdims=True)
            first = jnp.min(jnp.where(sc == best, j_idx, n_sel), axis=0, keepdims=True)
            return jnp.where(j_idx == first, -2.0, sc)
        return lax.fori_loop(0, n_pick, pick_one, score)

    ranked = lax.cond(most_retired > n_pick, ranked_with_ties, lambda: fast)
    chosen = valid & (ranked < -1.5)
    diag_bias_scr[...] = jnp.where(chosen, 0.0, NEG)
    bias_scr[...] = jnp.where(chosen & (j_idx * SEL_BLOCK < t0), 0.0, NEG)

    blocks_per_tile = SEL_TILE // SEL_BLOCK
    last_tile = ks_ref.shape[1] // SEL_TILE - 1
    qs_scr[0:HEAD_DIM, :] = qt
    qs_scr[HEAD_DIM + 16:, :] = jnp.zeros((qs_scr.shape[0] - HEAD_DIM - 16, QL), BF16)

    def tile_start(kt):
        return pl.multiple_of(jnp.clip(kt, 0, last_tile) * SEL_TILE, SEL_TILE)

    def set_bias_rows(table, kt):
        b0 = pl.multiple_of(jnp.minimum(kt, last_tile) * blocks_per_tile, blocks_per_tile)
        b16 = jnp.concatenate([table[pl.ds(b0, blocks_per_tile), :],
                               jnp.zeros((16 - blocks_per_tile, Q_BLOCK), F32)], axis=0)
        qs_scr[HEAD_DIM:HEAD_DIM + 16, :] = jnp.concatenate([b16] * HEADS_PER_GROUP, axis=1).astype(BF16)

    set_bias_rows(diag_bias_scr, t0 // SEL_TILE)
    s_d = _dot(ks_ref[0, pl.ds(t0a, Q_BLOCK), :], qs_scr[...])
    m_d, acc_d = flash_update(jnp.where(i_rel <= q_rel, s_d, NEG), init, vst_ref[0, :, pl.ds(t0a, Q_BLOCK)])

    def half_step(kt, s_cur, s_nxt, p_cur, p_prev, carry):
        m_prev, alpha_prev, acc, mx_cur = carry
        acc = alpha_prev * acc + _dot(vst_ref[0, :, pl.ds(tile_start(kt - 1), SEL_TILE)], p_prev[...])
        m_new = jnp.maximum(m_prev, jnp.max(mx_cur, axis=0, keepdims=True))
        set_bias_rows(bias_scr, kt + 1)
        k_nxt = tile_start(kt + 1)
        mx_nxt = jnp.full((8, QL), NEG, F32)
        for h in range(SEL_TILE // QK_PART):
            for c in range(QK_PART // P_CHUNK):
                rows = slice(h * QK_PART + c * P_CHUNK, h * QK_PART + (c + 1) * P_CHUNK)
                p_cur[rows, :] = jnp.exp2((s_cur[rows, :] - m_new).astype(BF16))
            part = slice(h * QK_PART, (h + 1) * QK_PART)
            sc = _dot(ks_ref[0, pl.ds(k_nxt + h * QK_PART, QK_PART), :], qs_scr[...])
            s_nxt[part, :] = sc
            mx_nxt = jnp.maximum(mx_nxt, jnp.max(sc.reshape(QK_PART // 8, 8, QL), axis=0))
        return m_new, jnp.exp2(m_prev - m_new), acc, mx_nxt

    def pair_step(i, carry):
        carry = half_step(2 * i, s0_scr, s1_scr, p0_scr, p1_scr, carry)
        return half_step(2 * i + 1, s1_scr, s0_scr, p1_scr, p0_scr, carry)

    n_pairs = ((t0 + SEL_TILE - 1) // SEL_TILE + 1) // 2
    for b in range(blocks_per_tile):
        rows = slice(b * SEL_BLOCK, (b + 1) * SEL_BLOCK)
        s0_scr[rows, :] = s0_scr[rows, :] + jnp.concatenate([bias_scr[b:b + 1, :]] * HEADS_PER_GROUP, axis=1)
    p1_scr[...] = jnp.zeros(p1_scr.shape, BF16)
    mx0 = jnp.max(s0_scr[...].reshape(SEL_TILE // 8, 8, QL), axis=0)
    carry = (m_d, jnp.ones((1, QL), F32), acc_d, mx0)
    _, alpha_last, acc_s, _ = lax.fori_loop(0, n_pairs, pair_step, carry)
    acc_s = alpha_last * acc_s + _dot(vst_ref[0, :, pl.ds(tile_start(2 * n_pairs - 1), SEL_TILE)], p1_scr[...])

    gate = jax.nn.sigmoid(gl_ref[0, 0])
    o = (gate[0:1] * o_c + gate[1:2] * (acc_s[:HEAD_DIM] / acc_s[HEAD_DIM:HEAD_DIM + 1])
         + gate[2:3] * (acc_w[:HEAD_DIM] / acc_w[HEAD_DIM:HEAD_DIM + 1]))
    stacked = jnp.concatenate([o[:, p * Q_BLOCK:(p + 1) * Q_BLOCK] for p in range(HEADS_PER_GROUP)], axis=0)
    o_ref[...] = stacked.T.astype(o_ref.dtype)


def _sparse_attention(qt, kc_aug, vc_aug_t, ks_aug, vs_aug_t, kw_aug, vw_aug_t, gate_logits, s):
    n_group, n_qb = qt.shape[:2]
    nc = kc_aug.shape[1]
    n_sel = s // SEL_BLOCK
    assert HEAD_DIM + nc // 8 <= CMP_AUG_W and nc // 4 >= CMP_PER_QB + 8 and WINDOW // WIN_CHUNK + Q_BLOCK // WIN_CHUNK <= 8
    per_step = lambda g, i: (g, i, 0, 0)
    per_group = lambda shape: pl.BlockSpec((1,) + shape, lambda g, i: (g, 0, 0), pipeline_mode=pl.Buffered(1))
    tile_f32 = pltpu.VMEM((SEL_TILE, QL), F32)
    tile_bf16 = pltpu.VMEM((SEL_TILE, QL), BF16)
    sel_table = pltpu.VMEM((n_sel, Q_BLOCK), F32)
    return pl.pallas_call(
        _nsa_kernel,
        grid=(n_group, n_qb),
        in_specs=[
            pl.BlockSpec((1, 1, HEAD_DIM, QL), per_step),
            per_group((nc, CMP_AUG_W)),
            per_group((V_AUG_ROWS, nc)),
            per_group((s, KS_AUG_W)),
            per_group((V_AUG_ROWS, s)),
            per_group((s + WINDOW, KS_AUG_W)),
            per_group((V_AUG_ROWS, s + WINDOW)),
            pl.BlockSpec((1, 1, 3, QL), per_step),
        ],
        out_specs=pl.BlockSpec((Q_BLOCK, HEADS_PER_GROUP * HEAD_DIM), lambda g, i: (i, g)),
        out_shape=jax.ShapeDtypeStruct((s, Q_W), BF16),
        scratch_shapes=[pltpu.VMEM((Q_BLOCK // 128, 8 + nc, 128), F32), sel_table, sel_table,
                        pltpu.VMEM((CMP_AUG_W, QL), BF16), pltpu.VMEM((KS_AUG_W, QL), BF16),
                        pltpu.VMEM((KS_AUG_W, QL), BF16), pltpu.VMEM((nc, QL), F32),
                        tile_f32, tile_f32, tile_bf16, tile_bf16],
        compiler_params=_cparams("parallel", "arbitrary"),
        name="sparse_attn",
    )(qt, kc_aug, vc_aug_t, ks_aug, vs_aug_t, kw_aug, vw_aug_t, gate_logits)


def _merge_kernel(x_ref, ya_ref, yb_ref, yc_ref, g0_ref, g1_ref, g2_ref, wp_ref, wg_ref, wn_ref,
                  wo_ref, ng_ref, gate_ref, o_ref):
    merged = jax.nn.sigmoid(g0_ref[...].astype(F32)) * _dot(ya_ref[...], wp_ref[0])
    merged += jax.nn.sigmoid(g1_ref[...].astype(F32)) * _dot(yb_ref[...], wg_ref[0])
    merged += jax.nn.sigmoid(g2_ref[...].astype(F32)) * _dot(yc_ref[...], wn_ref[0])
    y = _dot(merged.astype(BF16), wo_ref[0])
    o_ref[...] = x_ref[...] + gate_ref[...] * _rms(y, ng_ref[...])


def _merge(x, ya, yb, yc, proj, wp, wg, wn, wo, layer, norm_g, gate):
    s, d = x.shape
    tm = min(256, s)
    const = lambda i: (0, 0)
    rows = lambda i: (i, 0)
    whole = lambda a: pl.BlockSpec((1,) + a.shape[1:], lambda i: (layer, 0, 0), pipeline_mode=pl.Buffered(1))
    bg = OFF_BG // d
    return pl.pallas_call(
        _merge_kernel,
        grid=(s // tm,),
        in_specs=[
            pl.BlockSpec((tm, d), rows),
            pl.BlockSpec((tm, MIX_W), rows),
            pl.BlockSpec((tm, MIX_W), rows),
            pl.BlockSpec((tm, Q_W), rows),
            pl.BlockSpec((tm, d), lambda i: (i, bg)),
            pl.BlockSpec((tm, d), lambda i: (i, bg + 1)),
            pl.BlockSpec((tm, d), lambda i: (i, bg + 2)),
            whole(wp), whole(wg), whole(wn), whole(wo),
            pl.BlockSpec((1, d), const),
            pl.BlockSpec((1, d), const),
        ],
        out_specs=pl.BlockSpec((tm, d), rows),
        out_shape=jax.ShapeDtypeStruct((s, d), F32),
        compiler_params=_cparams("parallel"),
        name="merge_out",
    )(x, ya, yb, yc, proj, proj, proj, wp, wg, wn, wo, norm_g, gate)


def _ffn_kernel(x_ref, gi_ref, sc_ref, sh_ref, w1_ref, w2_ref, go_ref, gate_ref, o_ref, h_scr):
    j = pl.program_id(1)

    @pl.when(j == 0)
    def _():
        h = _rms(x_ref[...], gi_ref[...]) * (1.0 + sc_ref[...]) + sh_ref[...]
        h_scr[...] = h.astype(BF16)
        o_ref[...] = jnp.zeros_like(o_ref)

    a = jnp.square(jnp.maximum(_dot(h_scr[...], w1_ref[0]), 0.0))
    o_ref[...] += _dot(a.astype(BF16), w2_ref[0])

    @pl.when(j == pl.num_programs(1) - 1)
    def _():
        o_ref[...] = x_ref[...] + gate_ref[...] * _rms(o_ref[...], go_ref[...])


def _ffn(x, g_in, scale, shift, w1, w2, layer, g_out, gate):
    s, d = x.shape
    tm, tf = min(1024, s), 1024
    const = lambda i, j: (0, 0)
    rows = lambda i, j: (i, 0)
    return pl.pallas_call(
        _ffn_kernel,
        grid=(s // tm, D_FF // tf),
        in_specs=[
            pl.BlockSpec((tm, d), rows, pipeline_mode=pl.Buffered(1)),
            pl.BlockSpec((1, d), const),
            pl.BlockSpec((1, d), const),
            pl.BlockSpec((1, d), const),
            pl.BlockSpec((1, d, tf), lambda i, j: (layer, 0, j)),
            pl.BlockSpec((1, tf, d), lambda i, j: (layer, j, 0)),
            pl.BlockSpec((1, d), const),
            pl.BlockSpec((1, d), const),
        ],
        out_specs=pl.BlockSpec((tm, d), rows),
        out_shape=jax.ShapeDtypeStruct((s, d), F32),
        scratch_shapes=[pltpu.VMEM((tm, d), BF16)],
        compiler_params=_cparams("parallel", "arbitrary"),
        name="ffn",
    )(x, g_in, scale, shift, w1, w2, g_out, gate)


def _token_mixing(x, mod, norm_g, w_in, pool_w, pool_scale, ln_g, ln_b, ws, bs, cmp_pos, cmp_w1,
                  cmp_b1, cmp_w2, cmp_b2, branch_weights, layer):
    s, d = x.shape
    ng0 = OFF_BG
    w_main = jnp.concatenate([w_in[:, :ng0], w_in[:, ng0 + N_GATE:]], axis=1).astype(BF16)
    w_ng = jnp.pad(w_in[:, ng0:ng0 + N_GATE], ((0, 0), (0, 128 - N_GATE))).astype(BF16)
    row = lambda v: v.reshape(1, -1)

    col_scale = jnp.ones((1, PROJ_W), F32).at[:, OFF_Q:OFF_KV].set(HEAD_DIM ** -0.5 * LOG2E)
    proj, ngate = _in_projection(x, row(norm_g[0]), row(mod[1]), row(mod[0]), w_main, w_ng, col_scale)

    ya, yb = _mixers(proj, pool_w.astype(BF16), row(pool_scale), row(ln_g), row(ln_b), ws, bs.T)

    qt, ks_aug, vs_aug_t, kw_aug, vw_aug_t, gl = _attention_operands(proj, ngate)
    kv_cmp = proj[:, OFF_KV:OFF_KV + 2 * KV_W].reshape(s, 2, KV_GROUPS, HEAD_DIM).transpose(1, 2, 0, 3)
    xkv = kv_cmp.reshape(2, KV_GROUPS, s // CMP_STRIDE, CMP_STRIDE * HEAD_DIM)
    lane_pad = ((0, 0), (0, 0), (0, 128 - HEAD_DIM))
    kc_aug, vc_aug_t = _compress(xkv, cmp_pos.reshape(2, 1, CMP_BLOCK * HEAD_DIM), cmp_w1.astype(BF16),
                                 cmp_b1.reshape(2, 1, -1), jnp.pad(cmp_w2, lane_pad).astype(BF16),
                                 jnp.pad(cmp_b2.reshape(2, 1, -1), lane_pad))
    yc = _sparse_attention(qt, kc_aug, vc_aug_t, ks_aug, vs_aug_t, kw_aug, vw_aug_t, gl, s)

    return _merge(x, ya, yb, yc, proj, *branch_weights, layer, row(norm_g[1]), row(mod[2]))


def kernel(x, c, norm_g, w_ada, b_ada, w_in, pool_w, pool_scale, gmlp_ln_g, gmlp_ln_b, gmlp_ws, gmlp_bs,
           cmp_pos, cmp_w1, cmp_b1, cmp_w2, cmp_b2, w_br_pool, w_br_gmlp, w_br_nsa, w_out, w_ff1, w_ff2):
    b, s, d = x.shape
    assert b == 1 and d == D_MODEL and s % 1024 == 0
    n_layer = w_ada.shape[0]
    mod_all = _modulation(c, w_ada, b_ada).reshape(n_layer, 6, d)
    xs = x[0]
    row = lambda v: v.reshape(1, -1)
    branch_weights = tuple(w.astype(BF16) for w in (w_br_pool, w_br_gmlp, w_br_nsa, w_out))
    w_ff1_b, w_ff2_b = w_ff1.astype(BF16), w_ff2.astype(BF16)
    for l in range(n_layer):
        mod = mod_all[l]
        xs = _token_mixing(xs, mod, norm_g[l], w_in[l], pool_w[l], pool_scale[l], gmlp_ln_g[l],
                           gmlp_ln_b[l], gmlp_ws[l], gmlp_bs[l], cmp_pos[l], cmp_w1[l], cmp_b1[l],
                           cmp_w2[l], cmp_b2[l], branch_weights, l)
        xs = _ffn(xs, row(norm_g[l, 2]), row(mod[4]), row(mod[3]), w_ff1_b, w_ff2_b, l,
                  row(norm_g[l, 3]), row(mod[5]))
    return xs[None]
```

```python
import jax
import jax.numpy as jnp
from jax import lax
from jax.experimental import pallas as pl
from jax.experimental.pallas import tpu as pltpu

F32 = jnp.float32
BF16 = jnp.bfloat16

D_MODEL = 2048
POOL_WINDOWS = (2, 4, 8, 16)
POOL_HALO = 16
GROUP_CH = 128
MIX_W = 4 * GROUP_CH
GMLP_CHUNK = 128
HEAD_DIM = 64
KV_GROUPS = 4
HEADS_PER_GROUP = 4
Q_W = 16 * HEAD_DIM
KV_W = KV_GROUPS * HEAD_DIM
CMP_BLOCK = 32
CMP_STRIDE = 16
SEL_BLOCK = 64
SEL_TOPN = 16
WINDOW = 512
FORCE_BONUS = 1000.0
LOG2E = 1.4426950408889634
NEG = -1e30
N_GATE = 3 * 16
D_FF = 4 * D_MODEL

Q_BLOCK = 256
QL = HEADS_PER_GROUP * Q_BLOCK
CMP_PER_QB = Q_BLOCK // CMP_STRIDE
WIN_CHUNK = 128
PAD_TILES = WINDOW // Q_BLOCK
SEL_TILE = 512
KS_AUG_W = 128
CMP_AUG_W = 256
V_AUG_ROWS = HEAD_DIM + 16
P_CHUNK = 64
QK_PART = 128

OFF_POOL, OFF_U, OFF_V, OFF_Q, OFF_KV, OFF_BG = 0, 512, 1024, 1536, 2560, 4096
PROJ_W = OFF_BG + 3 * D_MODEL

VMEM_LIMIT = 60 * 1024 * 1024


def _cparams(*sem):
    return pltpu.CompilerParams(dimension_semantics=sem, vmem_limit_bytes=VMEM_LIMIT)


def _dot(a, b):
    return jnp.dot(a, b, preferred_element_type=F32)


def _rms(x, g):
    return x * lax.rsqrt(jnp.mean(x * x, axis=-1, keepdims=True) + 1e-6) * g


def _mod_kernel(c_ref, w_ref, b_ref, o_ref):
    c = c_ref[...]
    act = c * jax.nn.sigmoid(c)
    o_ref[0] = jnp.sum(act * w_ref[0], axis=0, keepdims=True) + b_ref[0]


def _modulation(c, w_ada, b_ada):
    n_layer, d, n_out = w_ada.shape
    tn = 1024
    return pl.pallas_call(
        _mod_kernel,
        grid=(n_layer, n_out // tn),
        in_specs=[
            pl.BlockSpec((d, 1), lambda l, j: (0, 0)),
            pl.BlockSpec((1, d, tn), lambda l, j: (l, 0, j)),
            pl.BlockSpec((1, 1, tn), lambda l, j: (l, 0, j)),
        ],
        out_specs=pl.BlockSpec((1, 1, tn), lambda l, j: (l, 0, j)),
        out_shape=jax.ShapeDtypeStruct((n_layer, 1, n_out), F32),
        compiler_params=_cparams("parallel", "parallel"),
        name="adaln_mod",
    )(c.reshape(d, 1), w_ada, b_ada.reshape(n_layer, 1, n_out))


def _inproj_kernel(x_ref, g_ref, sc_ref, sh_ref, w_ref, wng_ref, cs_ref, o_ref, ng_ref, h_scr):
    @pl.when(pl.program_id(1) == 0)
    def _():
        h = _rms(x_ref[...], g_ref[...]) * (1.0 + sc_ref[...]) + sh_ref[...]
        hb = h.astype(BF16)
        h_scr[...] = hb
        ng_ref[...] = _dot(hb, wng_ref[...])

    o_ref[...] = (_dot(h_scr[...], w_ref[...]) * cs_ref[...]).astype(o_ref.dtype)


def _in_projection(x, g, scale, shift, w_main, w_ng, col_scale):
    s, d = x.shape
    tm, tn = min(1024, s), 2048
    row = lambda i, j: (0, 0)
    return pl.pallas_call(
        _inproj_kernel,
        grid=(s // tm, PROJ_W // tn),
        in_specs=[
            pl.BlockSpec((tm, d), lambda i, j: (i, 0)),
            pl.BlockSpec((1, d), row),
            pl.BlockSpec((1, d), row),
            pl.BlockSpec((1, d), row),
            pl.BlockSpec((d, tn), lambda i, j: (0, j)),
            pl.BlockSpec((d, 128), row),
            pl.BlockSpec((1, tn), lambda i, j: (0, j)),
        ],
        out_specs=[
            pl.BlockSpec((tm, tn), lambda i, j: (i, j)),
            pl.BlockSpec((tm, 128), lambda i, j: (i, 0)),
        ],
        out_shape=[
            jax.ShapeDtypeStruct((s, PROJ_W), BF16),
            jax.ShapeDtypeStruct((s, 128), F32),
        ],
        scratch_shapes=[pltpu.VMEM((tm, d), BF16)],
        compiler_params=_cparams("parallel", "arbitrary"),
        name="in_proj",
    )(x, g, scale, shift, w_main, w_ng, col_scale)


def _mixer_kernel(a_ref, halo_ref, u_ref, v_ref, pw_ref, ps_ref, lg_ref, lb_ref, ws_ref, bs_ref,
                  ya_ref, yb_ref):
    i = pl.program_id(0)
    tm = a_ref.shape[0]
    a = a_ref[...].astype(F32)
    halo = jnp.where(i > 0, halo_ref[...].astype(F32), 0.0)
    ext = jnp.concatenate([halo, a], axis=0)
    p2 = ext[1:] + ext[:-1]
    p4 = p2[2:] + p2[:-2]
    p8 = p4[4:] + p4[:-4]
    p16 = p8[8:] + p8[:-8]
    sums = (p2[15:15 + tm], p4[13:13 + tm], p8[9:9 + tm], p16[1:1 + tm])
    t = i * tm + lax.broadcasted_iota(jnp.int32, (tm, 1), 0)
    for gi, w in enumerate(POOL_WINDOWS):
        cols = slice(gi * GROUP_CH, (gi + 1) * GROUP_CH)
        cnt = jnp.minimum(t + 1, w).astype(F32)
        pooled = sums[gi][:, cols] / cnt - a[:, cols]
        y = _dot(pooled.astype(BF16), pw_ref[gi])
        ya_ref[:, cols] = (y * ps_ref[:, cols]).astype(ya_ref.dtype)

    u = jax.nn.gelu(u_ref[...].astype(F32))
    v = jax.nn.gelu(v_ref[...].astype(F32))
    mu = jnp.mean(v, axis=-1, keepdims=True)
    var = jnp.mean(jnp.square(v - mu), axis=-1, keepdims=True)
    vn = ((v - mu) * lax.rsqrt(var + 1e-5) * lg_ref[...] + lb_ref[...]).astype(BF16)
    r = lax.broadcasted_iota(jnp.int32, (GMLP_CHUNK, GMLP_CHUNK), 0)
    c = lax.broadcasted_iota(jnp.int32, (GMLP_CHUNK, GMLP_CHUNK), 1)
    for gi in range(4):
        cols = slice(gi * GROUP_CH, (gi + 1) * GROUP_CH)
        wsm = jnp.where(r >= c, ws_ref[gi], 0.0).astype(BF16)
        bias = bs_ref[:, gi:gi + 1]
        for ck in range(tm // GMLP_CHUNK):
            rows = slice(ck * GMLP_CHUNK, (ck + 1) * GMLP_CHUNK)
            mixed = _dot(wsm, vn[rows, cols]) + bias
            yb_ref[rows, cols] = (u[rows, cols] * mixed).astype(yb_ref.dtype)


def _mixers(proj, pool_w, pool_scale, ln_g, ln_b, ws, bs_t):
    s = proj.shape[0]
    tm = min(512, s)
    hb = tm // POOL_HALO
    const2 = lambda i: (0, 0)
    const3 = lambda i: (0, 0, 0)
    return pl.pallas_call(
        _mixer_kernel,
        grid=(s // tm,),
        in_specs=[
            pl.BlockSpec((tm, MIX_W), lambda i: (i, OFF_POOL // MIX_W)),
            pl.BlockSpec((POOL_HALO, MIX_W), lambda i: (jnp.maximum(i * hb - 1, 0), OFF_POOL // MIX_W)),
            pl.BlockSpec((tm, MIX_W), lambda i: (i, OFF_U // MIX_W)),
            pl.BlockSpec((tm, MIX_W), lambda i: (i, OFF_V // MIX_W)),
            pl.BlockSpec((4, GROUP_CH, GROUP_CH), const3),
            pl.BlockSpec((1, MIX_W), const2),
            pl.BlockSpec((1, MIX_W), const2),
            pl.BlockSpec((1, MIX_W), const2),
            pl.BlockSpec((4, GMLP_CHUNK, GMLP_CHUNK), const3),
            pl.BlockSpec((GMLP_CHUNK, 4), const2),
        ],
        out_specs=[
            pl.BlockSpec((tm, MIX_W), lambda i: (i, 0)),
            pl.BlockSpec((tm, MIX_W), lambda i: (i, 0)),
        ],
        out_shape=[jax.ShapeDtypeStruct((s, MIX_W), BF16)] * 2,
        compiler_params=_cparams("parallel"),
        name="mixers",
    )(proj, proj, proj, proj, pool_w, pool_scale, ln_g, ln_b, ws, bs_t)


def _compress_kernel(x_ref, pos_ref, w1_ref, b1_ref, w2_ref, b2_ref, kc_ref, vct_ref):
    half = CMP_STRIDE * HEAD_DIM
    x = x_ref[0, 0].astype(F32)
    nc = x.shape[0]
    pos = pos_ref[0]
    first = _dot((x + pos[:, :half]).astype(BF16), w1_ref[0, :half, :])
    second = _dot((x + pos[:, half:]).astype(BF16), w1_ref[0, half:, :])
    hid = jax.nn.gelu(first + pltpu.roll(second, nc - 1, 0) + b1_ref[0])
    out = _dot(hid.astype(BF16), w2_ref[0]) + b2_ref[0]

    @pl.when(pl.program_id(1) == 0)
    def _():
        lane = lax.broadcasted_iota(jnp.int32, (nc, 128), 1)
        grp = lax.broadcasted_iota(jnp.int32, (nc, 128), 0) // 8
        kc_ref[0, :, 0:128] = (out + (lane - HEAD_DIM == grp).astype(F32)).astype(BF16)
        kc_ref[0, :, 128:] = (lane + (128 - HEAD_DIM) == grp).astype(BF16)

    @pl.when(pl.program_id(1) == 1)
    def _():
        ones_row = lax.broadcasted_iota(jnp.int32, (V_AUG_ROWS, nc), 0) == HEAD_DIM
        vct_ref[0] = (out.T[0:V_AUG_ROWS] + ones_row.astype(F32)).astype(BF16)


def _compress(xkv, pos, w1, b1, w2, b2):
    _, n_group, nc, width = xkv.shape
    assert CMP_AUG_W == 256 and HEAD_DIM + nc // 8 <= CMP_AUG_W
    per_kv = lambda g, a: (a, 0, 0)
    per_group = lambda g, a: (g, 0, 0)
    return pl.pallas_call(
        _compress_kernel,
        grid=(n_group, 2),
        in_specs=[
            pl.BlockSpec((1, 1, nc, width), lambda g, a: (a, g, 0, 0)),
            pl.BlockSpec((1, 1, 2 * width), per_kv),
            pl.BlockSpec((1, 2 * width, 128), per_kv),
            pl.BlockSpec((1, 1, 128), per_kv),
            pl.BlockSpec((1, 128, 128), per_kv),
            pl.BlockSpec((1, 1, 128), per_kv),
        ],
        out_specs=[pl.BlockSpec((1, nc, CMP_AUG_W), per_group), pl.BlockSpec((1, V_AUG_ROWS, nc), per_group)],
        out_shape=[jax.ShapeDtypeStruct((n_group, nc, CMP_AUG_W), BF16),
                   jax.ShapeDtypeStruct((n_group, V_AUG_ROWS, nc), BF16)],
        compiler_params=_cparams("parallel", "arbitrary"),
        name="compress_kv",
    )(xkv, pos, w1, b1, w2, b2)


def _operand_kernel(q_lo_ref, q_hi_ref, sel_ref, win_ref, ng_ref, qt_ref, ks_ref, vst_ref, kw_ref, vwt_ref, gl_ref):
    i = pl.program_id(0)
    n_tiles = pl.num_programs(0) - PAD_TILES
    rows = q_lo_ref.shape[0]
    lane = lax.broadcasted_iota(jnp.int32, (rows, 128), 1)
    pos = i * rows + lax.broadcasted_iota(jnp.int32, (rows, 128), 0)
    ones_rows = (lax.broadcasted_iota(jnp.int32, (V_AUG_ROWS - HEAD_DIM, rows), 0) == 0).astype(BF16)

    def key_pair(slab, idx):
        onehot = (lane - HEAD_DIM == idx).astype(F32)
        left = jnp.where(lane < HEAD_DIM, slab, onehot)
        right = jnp.where(lane < HEAD_DIM, pltpu.roll(slab, HEAD_DIM, 1), onehot)
        return left.astype(BF16), right.astype(BF16)

    def transposed_pair(slab):
        t = slab.T.astype(BF16)
        return t[:HEAD_DIM], t[HEAD_DIM:]

    def write_kv(src, keep, idx, k_ref, vt_ref):
        for pair in range(KV_GROUPS // 2):
            lanes = slice(pair * 128, (pair + 1) * 128)
            k_slab = jnp.where(keep, src[:, lanes].astype(F32), 0.0)
            v_slab = jnp.where(keep, src[:, KV_W + pair * 128:KV_W + (pair + 1) * 128].astype(F32), 0.0)
            for g, k, vt in zip((2 * pair, 2 * pair + 1), key_pair(k_slab, idx), transposed_pair(v_slab)):
                k_ref[g] = k
                vt_ref[g, 0:HEAD_DIM, :] = vt
                vt_ref[g, HEAD_DIM:, :] = ones_rows

    @pl.when(i < n_tiles)
    def _():
        for half, ref in enumerate((q_lo_ref, q_hi_ref)):
            for gg in range(2):
                for pp in range(HEADS_PER_GROUP // 2):
                    lanes = slice(gg * 256 + pp * 128, gg * 256 + (pp + 1) * 128)
                    for h, t in zip((2 * pp, 2 * pp + 1), transposed_pair(ref[:, lanes].astype(F32))):
                        qt_ref[2 * half + gg, 0, :, h * Q_BLOCK:(h + 1) * Q_BLOCK] = t
        write_kv(sel_ref, True, (pos // SEL_BLOCK) % (SEL_TILE // SEL_BLOCK), ks_ref, vst_ref)
        logits_t = ng_ref[...].T
        for g in range(KV_GROUPS):
            for h in range(HEADS_PER_GROUP):
                for br in range(3):
                    r = (g * HEADS_PER_GROUP + h) * 3 + br
                    gl_ref[g, 0, br:br + 1, h * Q_BLOCK:(h + 1) * Q_BLOCK] = logits_t[r:r + 1, :]

    write_kv(win_ref, i >= PAD_TILES, (pos // WIN_CHUNK) % 8, kw_ref, vwt_ref)


def _attention_operands(proj, ngate):
    s = proj.shape[0]
    n_tiles = s // Q_BLOCK
    blk = MIX_W
    real = lambda i: jnp.minimum(i, n_tiles - 1)
    src = lambda c: pl.BlockSpec((Q_BLOCK, blk), lambda i: (real(i), c))
    return pl.pallas_call(
        _operand_kernel,
        grid=(n_tiles + PAD_TILES,),
        in_specs=[src(OFF_Q // blk), src(OFF_Q // blk + 1), src(OFF_KV // blk + 1),
                  pl.BlockSpec((Q_BLOCK, blk), lambda i: (jnp.maximum(i - PAD_TILES, 0), OFF_KV // blk + 2)),
                  pl.BlockSpec((Q_BLOCK, 128), lambda i: (real(i), 0))],
        out_specs=[
            pl.BlockSpec((KV_GROUPS, 1, HEAD_DIM, QL), lambda i: (0, real(i), 0, 0)),
            pl.BlockSpec((KV_GROUPS, Q_BLOCK, KS_AUG_W), lambda i: (0, real(i), 0)),
            pl.BlockSpec((KV_GROUPS, V_AUG_ROWS, Q_BLOCK), lambda i: (0, 0, real(i))),
            pl.BlockSpec((KV_GROUPS, Q_BLOCK, KS_AUG_W), lambda i: (0, i, 0)),
            pl.BlockSpec((KV_GROUPS, V_AUG_ROWS, Q_BLOCK), lambda i: (0, 0, i)),
            pl.BlockSpec((KV_GROUPS, 1, 3, QL), lambda i: (0, real(i), 0, 0)),
        ],
        out_shape=[
            jax.ShapeDtypeStruct((KV_GROUPS, n_tiles, HEAD_DIM, QL), BF16),
            jax.ShapeDtypeStruct((KV_GROUPS, s, KS_AUG_W), BF16),
            jax.ShapeDtypeStruct((KV_GROUPS, V_AUG_ROWS, s), BF16),
            jax.ShapeDtypeStruct((KV_GROUPS, s + WINDOW, KS_AUG_W), BF16),
            jax.ShapeDtypeStruct((KV_GROUPS, V_AUG_ROWS, s + WINDOW), BF16),
            jax.ShapeDtypeStruct((KV_GROUPS, n_tiles, 3, QL), F32),
        ],
        compiler_params=_cparams("arbitrary"),
        name="attn_operands",
    )(proj, proj, proj, proj, ngate)


def _nsa_kernel(qt_ref, kc_ref, vct_ref, ks_ref, vst_ref, kw_ref, vwt_ref, gl_ref, o_ref,
                ps_scr, bias_scr, diag_bias_scr, qc_scr, qw_scr, qs_scr, sc_scr, s0_scr, s1_scr, p0_scr, p1_scr):
    qb = pl.program_id(1)
    t0 = qb * Q_BLOCK
    t0a = pl.multiple_of(t0, Q_BLOCK)
    qt = qt_ref[0, 0]
    nc = kc_ref.shape[1]
    n_sel = bias_scr.shape[0]
    q_rel = lax.broadcasted_iota(jnp.int32, (1, QL), 1) % Q_BLOCK
    tq = t0 + q_rel

    def bias_rows(cond, rows):
        return jnp.where(cond, NEG, 0.0).astype(BF16) + jnp.zeros((rows, QL), BF16)

    def flash_update(s, carry, vt):
        m_prev, acc = carry
        m_new = jnp.maximum(m_prev, jnp.max(s, axis=0, keepdims=True))
        p = jnp.exp2((s - m_new).astype(BF16))
        return m_new, jnp.exp2(m_prev - m_new) * acc + _dot(vt, p)

    init = (jnp.full((1, QL), NEG, F32), jnp.zeros((V_AUG_ROWS, QL), F32))

    grp = lax.broadcasted_iota(jnp.int32, (qc_scr.shape[0] - HEAD_DIM, 1), 0)
    visible = CMP_PER_QB * (qb + 1)
    qc_scr[0:HEAD_DIM, :] = qt
    qc_scr[HEAD_DIM:, :] = bias_rows((8 * grp >= visible) & (grp < nc // 8), qc_scr.shape[0] - HEAD_DIM)
    band = CMP_PER_QB + 8
    r0 = pl.multiple_of(jnp.maximum(visible - band, 0), 8)
    band_end = CMP_STRIDE * (r0 + lax.broadcasted_iota(jnp.int32, (band, 1), 0)) + (CMP_BLOCK - 1)

    def compressed(rows):
        def run():
            sc_scr[0:rows, :] = _dot(kc_ref[0, 0:rows, :], qc_scr[...])
            sc_scr[pl.ds(r0, band), :] = jnp.where(band_end <= tq, sc_scr[pl.ds(r0, band), :], NEG)
            s_c = sc_scr[0:rows, :]
            m_c = jnp.maximum(jnp.max(s_c, axis=0, keepdims=True), 0.1 * NEG)
            e_c = jnp.exp2(s_c - m_c)
            o_aug = _dot(vct_ref[0, :, 0:rows], e_c.astype(BF16))
            inv_c = 1.0 / jnp.maximum(o_aug[HEAD_DIM:HEAD_DIM + 1], 1e-30)
            p_grp = None
            for h in range(HEADS_PER_GROUP):
                lanes = slice(h * Q_BLOCK, (h + 1) * Q_BLOCK)
                p_h = e_c[:, lanes] * inv_c[:, lanes]
                p_grp = p_h if p_grp is None else p_grp + p_h
            for c in range(Q_BLOCK // 128):
                ps_scr[c, 8:8 + rows, :] = p_grp[:, c * 128:(c + 1) * 128]
                if rows < nc:
                    ps_scr[c, 8 + rows:, :] = jnp.zeros((nc - rows, 128), F32)
            return o_aug[:HEAD_DIM] * inv_c
        return run

    ps_scr[:, 0:8, :] = jnp.zeros((Q_BLOCK // 128, 8, 128), F32)
    n_part = 8 if nc // 8 >= band else 4
    part = nc // n_part

    def choose(lo, hi):
        if lo == hi:
            return compressed(lo * part)
        mid = (lo + hi) // 2
        return lambda: lax.cond(visible <= mid * part, choose(lo, mid), choose(mid + 1, hi))

    o_c = choose(1, n_part)()
    imp = jnp.concatenate([sum(ps_scr[c, pl.ds(k, n_sel, stride=4), :] for k in range(7, 12))
                           for c in range(Q_BLOCK // 128)], axis=1)

    n_tri = Q_BLOCK // WIN_CHUNK
    n_chunk = WINDOW // WIN_CHUNK + n_tri
    span = n_chunk * WIN_CHUNK
    rho = lax.broadcasted_iota(jnp.int32, (16, 1), 0)
    first_real = WINDOW // WIN_CHUNK - n_tri * qb
    qw_scr[0:HEAD_DIM, :] = qt
    qw_scr[HEAD_DIM:HEAD_DIM + 16, :] = bias_rows((rho < 8) & (((rho - n_tri * qb) & 7) < first_real), 16)
    qw_scr[HEAD_DIM + 16:, :] = jnp.zeros((qw_scr.shape[0] - HEAD_DIM - 16, QL), BF16)
    s_w = _dot(kw_ref[0, pl.ds(t0a, span), :], qw_scr[...])
    i_rel = lax.broadcasted_iota(jnp.int32, (Q_BLOCK, 1), 0)
    s_w = jnp.concatenate([jnp.where(i_rel > q_rel, s_w[:Q_BLOCK], NEG), s_w[Q_BLOCK:WINDOW],
                           jnp.where(i_rel <= q_rel, s_w[WINDOW:], NEG)], axis=0)
    _, acc_w = flash_update(s_w, init, vwt_ref[0, :, pl.ds(t0a, span)])

    j_idx = lax.broadcasted_iota(jnp.int32, (n_sel, Q_BLOCK), 0)
    cur = (t0 + lax.broadcasted_iota(jnp.int32, (n_sel, Q_BLOCK), 1)) // SEL_BLOCK
    forced = (j_idx == 0) | (j_idx == cur) | (j_idx == cur - 1)
    valid = j_idx <= cur
    score = jnp.where(forced, -2.0, jnp.where(valid, imp, -1.0))
    n_pick = SEL_TOPN - 3

    fast = score
    for r in range(n_pick):
        fast = jnp.where(fast == jnp.max(fast, axis=0, keepdims=True), -2.0, fast)
        if r % 3 == 0 and r // 3 < SEL_TILE // QK_PART:
            part = slice((r // 3) * QK_PART, (r // 3 + 1) * QK_PART)
            s0_scr[part, :] = _dot(ks_ref[0, part, 0:HEAD_DIM], qt)
    retired = (fast < -1.5) & valid & jnp.logical_not(forced)
    most_retired = jnp.max(jnp.sum(retired.astype(F32), axis=0, keepdims=True))

    def ranked_with_ties():
        def pick_one(_, sc):
            best = jnp.max(sc, axis=0, keepdims=True)
            first = jnp.min(jnp.where(sc == best, j_idx, n_sel), axis=0, keepdims=True)
            return jnp.where(j_idx == first, -2.0, sc)
        return lax.fori_loop(0, n_pick, pick_one, score)

    ranked = lax.cond(most_retired > n_pick, ranked_with_ties, lambda: fast)
    chosen = valid & (ranked < -1.5)
    diag_bias_scr[...] = jnp.where(chosen, 0.0, NEG)
    bias_scr[...] = jnp.where(chosen & (j_idx * SEL_BLOCK < t0), 0.0, NEG)

    blocks_per_tile = SEL_TILE // SEL_BLOCK
    last_tile = ks_ref.shape[1] // SEL_TILE - 1
    qs_scr[0:HEAD_DIM, :] = qt
    qs_scr[HEAD_DIM + 16:, :] = jnp.zeros((qs_scr.shape[0] - HEAD_DIM - 16, QL), BF16)

    def tile_start(kt):
        return pl.multiple_of(jnp.clip(kt, 0, last_tile) * SEL_TILE, SEL_TILE)

    def set_bias_rows(table, kt):
        b0 = pl.multiple_of(jnp.minimum(kt, last_tile) * blocks_per_tile, blocks_per_tile)
        b16 = jnp.concatenate([table[pl.ds(b0, blocks_per_tile), :],
                               jnp.zeros((16 - blocks_per_tile, Q_BLOCK), F32)], axis=0)
        qs_scr[HEAD_DIM:HEAD_DIM + 16, :] = jnp.concatenate([b16] * HEADS_PER_GROUP, axis=1).astype(BF16)

    set_bias_rows(diag_bias_scr, t0 // SEL_TILE)
    s_d = _dot(ks_ref[0, pl.ds(t0a, Q_BLOCK), :], qs_scr[...])
    m_d, acc_d = flash_update(jnp.where(i_rel <= q_rel, s_d, NEG), init, vst_ref[0, :, pl.ds(t0a, Q_BLOCK)])

    def half_step(kt, s_cur, s_nxt, p_cur, p_prev, carry):
        m_prev, alpha_prev, acc = carry
        acc = alpha_prev * acc + _dot(vst_ref[0, :, pl.ds(tile_start(kt - 1), SEL_TILE)], p_prev[...])
        m8 = jnp.max(s_cur[...].reshape(SEL_TILE // 8, 8, QL), axis=0)
        m_new = jnp.maximum(m_prev, jnp.max(m8, axis=0, keepdims=True))
        set_bias_rows(bias_scr, kt + 1)
        k_nxt = tile_start(kt + 1)
        for h in range(SEL_TILE // QK_PART):
            for c in range(QK_PART // P_CHUNK):
                rows = slice(h * QK_PART + c * P_CHUNK, h * QK_PART + (c + 1) * P_CHUNK)
                p_cur[rows, :] = jnp.exp2((s_cur[rows, :] - m_new).astype(BF16))
            part = slice(h * QK_PART, (h + 1) * QK_PART)
            s_nxt[part, :] = _dot(ks_ref[0, pl.ds(k_nxt + h * QK_PART, QK_PART), :], qs_scr[...])
        return m_new, jnp.exp2(m_prev - m_new), acc

    def pair_step(i, carry):
        carry = half_step(2 * i, s0_scr, s1_scr, p0_scr, p1_scr, carry)
        return half_step(2 * i + 1, s1_scr, s0_scr, p1_scr, p0_scr, carry)

    n_pairs = ((t0 + SEL_TILE - 1) // SEL_TILE + 1) // 2
    for b in range(blocks_per_tile):
        rows = slice(b * SEL_BLOCK, (b + 1) * SEL_BLOCK)
        s0_scr[rows, :] = s0_scr[rows, :] + jnp.concatenate([bias_scr[b:b + 1, :]] * HEADS_PER_GROUP, axis=1)
    p1_scr[...] = jnp.zeros(p1_scr.shape, BF16)
    carry = (m_d, jnp.ones((1, QL), F32), acc_d)
    _, alpha_last, acc_s = lax.fori_loop(0, n_pairs, pair_step, carry)
    acc_s = alpha_last * acc_s + _dot(vst_ref[0, :, pl.ds(tile_start(2 * n_pairs - 1), SEL_TILE)], p1_scr[...])

    gate = jax.nn.sigmoid(gl_ref[0, 0])
    o = (gate[0:1] * o_c + gate[1:2] * (acc_s[:HEAD_DIM] / acc_s[HEAD_DIM:HEAD_DIM + 1])
         + gate[2:3] * (acc_w[:HEAD_DIM] / acc_w[HEAD_DIM:HEAD_DIM + 1]))
    stacked = jnp.concatenate([o[:, p * Q_BLOCK:(p + 1) * Q_BLOCK] for p in range(HEADS_PER_GROUP)], axis=0)
    o_ref[...] = stacked.T.astype(o_ref.dtype)


def _sparse_attention(qt, kc_aug, vc_aug_t, ks_aug, vs_aug_t, kw_aug, vw_aug_t, gate_logits, s):
    n_group, n_qb = qt.shape[:2]
    nc = kc_aug.shape[1]
    n_sel = s // SEL_BLOCK
    assert HEAD_DIM + nc // 8 <= CMP_AUG_W and nc // 4 >= CMP_PER_QB + 8 and WINDOW // WIN_CHUNK + Q_BLOCK // WIN_CHUNK <= 8
    per_step = lambda g, i: (g, i, 0, 0)
    per_group = lambda shape: pl.BlockSpec((1,) + shape, lambda g, i: (g, 0, 0), pipeline_mode=pl.Buffered(1))
    tile_f32 = pltpu.VMEM((SEL_TILE, QL), F32)
    tile_bf16 = pltpu.VMEM((SEL_TILE, QL), BF16)
    sel_table = pltpu.VMEM((n_sel, Q_BLOCK), F32)
    return pl.pallas_call(
        _nsa_kernel,
        grid=(n_group, n_qb),
        in_specs=[
            pl.BlockSpec((1, 1, HEAD_DIM, QL), per_step),
            per_group((nc, CMP_AUG_W)),
            per_group((V_AUG_ROWS, nc)),
            per_group((s, KS_AUG_W)),
            per_group((V_AUG_ROWS, s)),
            per_group((s + WINDOW, KS_AUG_W)),
            per_group((V_AUG_ROWS, s + WINDOW)),
            pl.BlockSpec((1, 1, 3, QL), per_step),
        ],
        out_specs=pl.BlockSpec((Q_BLOCK, HEADS_PER_GROUP * HEAD_DIM), lambda g, i: (i, g)),
        out_shape=jax.ShapeDtypeStruct((s, Q_W), BF16),
        scratch_shapes=[pltpu.VMEM((Q_BLOCK // 128, 8 + nc, 128), F32), sel_table, sel_table,
                        pltpu.VMEM((CMP_AUG_W, QL), BF16), pltpu.VMEM((KS_AUG_W, QL), BF16),
                        pltpu.VMEM((KS_AUG_W, QL), BF16), pltpu.VMEM((nc, QL), F32),
                        tile_f32, tile_f32, tile_bf16, tile_bf16],
        compiler_params=_cparams("parallel", "arbitrary"),
        name="sparse_attn",
    )(qt, kc_aug, vc_aug_t, ks_aug, vs_aug_t, kw_aug, vw_aug_t, gate_logits)


def _merge_kernel(x_ref, ya_ref, yb_ref, yc_ref, g0_ref, g1_ref, g2_ref, wp_ref, wg_ref, wn_ref,
                  wo_ref, ng_ref, gate_ref, o_ref):
    merged = jax.nn.sigmoid(g0_ref[...].astype(F32)) * _dot(ya_ref[...], wp_ref[0])
    merged += jax.nn.sigmoid(g1_ref[...].astype(F32)) * _dot(yb_ref[...], wg_ref[0])
    merged += jax.nn.sigmoid(g2_ref[...].astype(F32)) * _dot(yc_ref[...], wn_ref[0])
    y = _dot(merged.astype(BF16), wo_ref[0])
    o_ref[...] = x_ref[...] + gate_ref[...] * _rms(y, ng_ref[...])


def _merge(x, ya, yb, yc, proj, wp, wg, wn, wo, layer, norm_g, gate):
    s, d = x.shape
    tm = min(256, s)
    const = lambda i: (0, 0)
    rows = lambda i: (i, 0)
    whole = lambda a: pl.BlockSpec((1,) + a.shape[1:], lambda i: (layer, 0, 0), pipeline_mode=pl.Buffered(1))
    bg = OFF_BG // d
    return pl.pallas_call(
        _merge_kernel,
        grid=(s // tm,),
        in_specs=[
            pl.BlockSpec((tm, d), rows),
            pl.BlockSpec((tm, MIX_W), rows),
            pl.BlockSpec((tm, MIX_W), rows),
            pl.BlockSpec((tm, Q_W), rows),
            pl.BlockSpec((tm, d), lambda i: (i, bg)),
            pl.BlockSpec((tm, d), lambda i: (i, bg + 1)),
            pl.BlockSpec((tm, d), lambda i: (i, bg + 2)),
            whole(wp), whole(wg), whole(wn), whole(wo),
            pl.BlockSpec((1, d), const),
            pl.BlockSpec((1, d), const),
        ],
        out_specs=pl.BlockSpec((tm, d), rows),
        out_shape=jax.ShapeDtypeStruct((s, d), F32),
        compiler_params=_cparams("parallel"),
        name="merge_out",
    )(x, ya, yb, yc, proj, proj, proj, wp, wg, wn, wo, norm_g, gate)


def _ffn_kernel(x_ref, gi_ref, sc_ref, sh_ref, w1_ref, w2_ref, go_ref, gate_ref, o_ref, h_scr):
    j = pl.program_id(1)

    @pl.when(j == 0)
    def _():
        h = _rms(x_ref[...], gi_ref[...]) * (1.0 + sc_ref[...]) + sh_ref[...]
        h_scr[...] = h.astype(BF16)
        o_ref[...] = jnp.zeros_like(o_ref)

    a = jnp.square(jnp.maximum(_dot(h_scr[...], w1_ref[0]), 0.0))
    o_ref[...] += _dot(a.astype(BF16), w2_ref[0])

    @pl.when(j == pl.num_programs(1) - 1)
    def _():
        o_ref[...] = x_ref[...] + gate_ref[...] * _rms(o_ref[...], go_ref[...])


def _ffn(x, g_in, scale, shift, w1, w2, layer, g_out, gate):
    s, d = x.shape
    tm, tf = min(1024, s), 1024
    const = lambda i, j: (0, 0)
    rows = lambda i, j: (i, 0)
    return pl.pallas_call(
        _ffn_kernel,
        grid=(s // tm, D_FF // tf),
        in_specs=[
            pl.BlockSpec((tm, d), rows, pipeline_mode=pl.Buffered(1)),
            pl.BlockSpec((1, d), const),
            pl.BlockSpec((1, d), const),
            pl.BlockSpec((1, d), const),
            pl.BlockSpec((1, d, tf), lambda i, j: (layer, 0, j)),
            pl.BlockSpec((1, tf, d), lambda i, j: (layer, j, 0)),
            pl.BlockSpec((1, d), const),
            pl.BlockSpec((1, d), const),
        ],
        out_specs=pl.BlockSpec((tm, d), rows),
        out_shape=jax.ShapeDtypeStruct((s, d), F32),
        scratch_shapes=[pltpu.VMEM((tm, d), BF16)],
        compiler_params=_cparams("parallel", "arbitrary"),
        name="ffn",
    )(x, g_in, scale, shift, w1, w2, g_out, gate)


def _token_mixing(x, mod, norm_g, w_in, pool_w, pool_scale, ln_g, ln_b, ws, bs, cmp_pos, cmp_w1,
                  cmp_b1, cmp_w2, cmp_b2, branch_weights, layer):
    s, d = x.shape
    ng0 = OFF_BG
    w_main = jnp.concatenate([w_in[:, :ng0], w_in[:, ng0 + N_GATE:]], axis=1).astype(BF16)
    w_ng = jnp.pad(w_in[:, ng0:ng0 + N_GATE], ((0, 0), (0, 128 - N_GATE))).astype(BF16)
    row = lambda v: v.reshape(1, -1)

    col_scale = jnp.ones((1, PROJ_W), F32).at[:, OFF_Q:OFF_KV].set(HEAD_DIM ** -0.5 * LOG2E)
    proj, ngate = _in_projection(x, row(norm_g[0]), row(mod[1]), row(mod[0]), w_main, w_ng, col_scale)

    ya, yb = _mixers(proj, pool_w.astype(BF16), row(pool_scale), row(ln_g), row(ln_b), ws, bs.T)

    qt, ks_aug, vs_aug_t, kw_aug, vw_aug_t, gl = _attention_operands(proj, ngate)
    kv_cmp = proj[:, OFF_KV:OFF_KV + 2 * KV_W].reshape(s, 2, KV_GROUPS, HEAD_DIM).transpose(1, 2, 0, 3)
    xkv = kv_cmp.reshape(2, KV_GROUPS, s // CMP_STRIDE, CMP_STRIDE * HEAD_DIM)
    lane_pad = ((0, 0), (0, 0), (0, 128 - HEAD_DIM))
    kc_aug, vc_aug_t = _compress(xkv, cmp_pos.reshape(2, 1, CMP_BLOCK * HEAD_DIM), cmp_w1.astype(BF16),
                                 cmp_b1.reshape(2, 1, -1), jnp.pad(cmp_w2, lane_pad).astype(BF16),
                                 jnp.pad(cmp_b2.reshape(2, 1, -1), lane_pad))
    yc = _sparse_attention(qt, kc_aug, vc_aug_t, ks_aug, vs_aug_t, kw_aug, vw_aug_t, gl, s)

    return _merge(x, ya, yb, yc, proj, *branch_weights, layer, row(norm_g[1]), row(mod[2]))


def kernel(x, c, norm_g, w_ada, b_ada, w_in, pool_w, pool_scale, gmlp_ln_g, gmlp_ln_b, gmlp_ws, gmlp_bs,
           cmp_pos, cmp_w1, cmp_b1, cmp_w2, cmp_b2, w_br_pool, w_br_gmlp, w_br_nsa, w_out, w_ff1, w_ff2):
    b, s, d = x.shape
    assert b == 1 and d == D_MODEL and s % 1024 == 0
    n_layer = w_ada.shape[0]
    mod_all = _modulation(c, w_ada, b_ada).reshape(n_layer, 6, d)
    xs = x[0]
    row = lambda v: v.reshape(1, -1)
    branch_weights = tuple(w.astype(BF16) for w in (w_br_pool, w_br_gmlp, w_br_nsa, w_out))
    w_ff1_b, w_ff2_b = w_ff1.astype(BF16), w_ff2.astype(BF16)
    for l in range(n_layer):
        mod = mod_all[l]
        xs = _token_mixing(xs, mod, norm_g[l], w_in[l], pool_w[l], pool_scale[l], gmlp_ln_g[l],
                           gmlp_ln_b[l], gmlp_ws[l], gmlp_bs[l], cmp_pos[l], cmp_w1[l], cmp_b1[l],
                           cmp_w2[l], cmp_b2[l], branch_weights, l)
        xs = _ffn(xs, row(norm_g[l, 2]), row(mod[4]), row(mod[3]), w_ff1_b, w_ff2_b, l,
                  row(norm_g[l, 3]), row(mod[5]))
    return xs[None]
```
